```python
import jax
import jax.numpy as jnp
from jax import lax
import numpy as np

D_MODEL = 1024
BATCH = 8
SEQ = 2048
DEPTH = 1
DEC_BATCH = 32
DEC_SEQ = 1
PAST_LEN = 16384
PAGE_SIZE = 128

EPS = 1e-6
GDN_HEADS = 8
GDN_DK = 128
GDN_DV = 128
CONV_W = 4
GDN_CHUNK = 64
MLA_HEADS = 8
Q_LORA = 512
KV_LORA = 512
NOPE_DIM = 128
ROPE_DIM = 64
V_DIM = 128
ROPE_THETA = 10000.0
Q_BLOCK = 128
MLA_SCALE = (NOPE_DIM + ROPE_DIM) ** -0.5
D_FF = 4 * D_MODEL
QKV_WIDTH = GDN_HEADS * (2 * GDN_DK + GDN_DV)
Z_WIDTH = GDN_HEADS * GDN_DV
LATENT_WIDTH = KV_LORA + ROPE_DIM
GATE_WIDTH = 2 * D_MODEL
OFF_Z = QKV_WIDTH
OFF_BETA = OFF_Z + Z_WIDTH
OFF_DECAY = OFF_BETA + GDN_HEADS
OFF_QDOWN = OFF_DECAY + GDN_HEADS
OFF_KVDOWN = OFF_QDOWN + Q_LORA
OFF_GATES = OFF_KVDOWN + LATENT_WIDTH
IN_WIDTH = OFF_GATES + GATE_WIDTH
IN_SPLITS = (OFF_Z, OFF_BETA, OFF_DECAY, OFF_QDOWN, OFF_KVDOWN, OFF_GATES)

kernel_name = 'hybrid_gdn_mla_gated_decoder_step'


def rms_norm(x, g):
    xf = x.astype(jnp.float32)
    y = xf * lax.rsqrt(jnp.mean(jnp.square(xf), axis=-1, keepdims=True) + EPS)
    return (y * g.astype(jnp.float32)).astype(x.dtype)


def l2_normalize(x):
    xf = x.astype(jnp.float32)
    return xf * lax.rsqrt(jnp.sum(jnp.square(xf), axis=-1, keepdims=True) + EPS)


def apply_rope(x, pos):
    inv = jnp.power(ROPE_THETA, -jnp.arange(0, ROPE_DIM, 2, dtype=jnp.float32) / ROPE_DIM)
    ang = pos.astype(jnp.float32)[:, None] * inv[None, :]
    cos, sin = jnp.cos(ang), jnp.sin(ang)
    if x.ndim == 4:
        cos, sin = cos[:, None, :], sin[:, None, :]
    xf = x.astype(jnp.float32)
    x1, x2 = xf[..., :ROPE_DIM // 2], xf[..., ROPE_DIM // 2:]
    return jnp.concatenate([x1 * cos - x2 * sin, x1 * sin + x2 * cos], axis=-1).astype(x.dtype)


def causal_conv(buf, x, w):
    xpad = jnp.concatenate([buf.astype(x.dtype), x], axis=1)
    t = x.shape[1]
    y = xpad[:, 0:t] * w[0]
    for i in range(1, CONV_W):
        y = y + xpad[:, i:i + t] * w[i]
    return jax.nn.silu(y), xpad[:, -(CONV_W - 1):]


def gdn_inputs(qkv_raw, b, a, conv_buf, conv_w, a_log, dt_bias):
    bsz, t = qkv_raw.shape[:2]
    qkv, new_buf = causal_conv(conv_buf, qkv_raw, conv_w)
    q, k, v = jnp.split(qkv, [GDN_HEADS * GDN_DK, 2 * GDN_HEADS * GDN_DK], axis=-1)
    q = l2_normalize(q.reshape(bsz, t, GDN_HEADS, GDN_DK)) * (GDN_DK ** -0.5)
    k = l2_normalize(k.reshape(bsz, t, GDN_HEADS, GDN_DK))
    v = v.reshape(bsz, t, GDN_HEADS, GDN_DV).astype(jnp.float32)
    beta = jax.nn.sigmoid(b.astype(jnp.float32))
    g = -jnp.exp(a_log.astype(jnp.float32)) * jax.nn.softplus(
        a.astype(jnp.float32) + dt_bias.astype(jnp.float32))
    return q, k, v, g, beta, new_buf


def gdn_chunked(q, k, v, g, beta, s0):
    bsz, t, h, dk = q.shape
    dv = v.shape[-1]
    n = t // GDN_CHUNK

    def to_chunks(arr):
        return jnp.moveaxis(arr, 1, 2).reshape(bsz, h, n, GDN_CHUNK, *arr.shape[3:])

    q, k, v, g, beta = (to_chunks(arr) for arr in (q, k, v, g, beta))
    G = jnp.cumsum(g, axis=-1)
    idx = jnp.arange(GDN_CHUNK)
    incl = idx[:, None] >= idx[None, :]
    strict = idx[:, None] > idx[None, :]
    diff = G[..., :, None] - G[..., None, :]
    decay = jnp.where(incl, jnp.exp(jnp.where(incl, diff, 0.0)), 0.0)
    kb = k * beta[..., None]
    vb = v * beta[..., None]
    L = jnp.where(strict, jnp.einsum('bhnid,bhnjd->bhnij', kb, k) * decay, 0.0)
    rhs = jnp.concatenate([vb, kb * jnp.exp(G)[..., None]], axis=-1)
    sol = lax.linalg.triangular_solve(L, rhs, left_side=True, lower=True, unit_diagonal=True)
    u0, w = sol[..., :dv], sol[..., dv:]
    aqk = jnp.einsum('bhnid,bhnjd->bhnij', q, k) * decay
    qg = q * jnp.exp(G)[..., None]
    kd = k * jnp.exp(G[..., -1:] - G)[..., None]
    g_end = jnp.exp(G[..., -1])
    xs = tuple(jnp.moveaxis(arr, 2, 0) for arr in (u0, w, qg, aqk, kd, g_end))

    def step(s, inp):
        u0_n, w_n, qg_n, aqk_n, kd_n, g_end_n = inp
        u = u0_n - jnp.einsum('bhik,bhkv->bhiv', w_n, s)
        o = jnp.einsum('bhik,bhkv->bhiv', qg_n, s) + jnp.einsum('bhij,bhjv->bhiv', aqk_n, u)
        s = g_end_n[..., None, None] * s + jnp.einsum('bhik,bhiv->bhkv', kd_n, u)
        return s, o

    s, o = lax.scan(step, s0.astype(jnp.float32), xs)
    o = jnp.moveaxis(o, 0, 2).reshape(bsz, h, t, dv)
    return jnp.moveaxis(o, 1, 2), s


def gdn_recurrent(q, k, v, g, beta, s0):
    xs = tuple(jnp.moveaxis(arr, 1, 0) for arr in (q, k, v, g, beta))

    def step(s, inp):
        q_t, k_t, v_t, g_t, b_t = inp
        s = jnp.exp(g_t)[..., None, None] * s
        u = b_t[..., None] * (v_t - jnp.einsum('bhk,bhkv->bhv', k_t, s))
        s = s + jnp.einsum('bhk,bhv->bhkv', k_t, u)
        return s, jnp.einsum('bhk,bhkv->bhv', q_t, s)

    s, o = lax.scan(step, s0.astype(jnp.float32), xs)
    return jnp.moveaxis(o, 0, 1), s


def gdn_output(o, z, gdn_norm):
    bsz, t = o.shape[:2]
    zf = z.reshape(bsz, t, GDN_HEADS, GDN_DV).astype(jnp.float32)
    y = rms_norm(o, gdn_norm) * jax.nn.silu(zf)
    return y.reshape(bsz, t, Z_WIDTH).astype(z.dtype)


def mla_queries_and_latent(q_down, kv_down, pos, q_a_norm, w_uq, kv_a_norm,
                           q_norm_nope, q_norm_rope, k_norm_rope):
    bsz, t = q_down.shape[:2]
    q = (rms_norm(q_down, q_a_norm) @ w_uq).reshape(bsz, t, MLA_HEADS, NOPE_DIM + ROPE_DIM)
    qn = rms_norm(q[..., :NOPE_DIM], q_norm_nope)
    qr = apply_rope(rms_norm(q[..., NOPE_DIM:], q_norm_rope), pos)
    c = rms_norm(kv_down[..., :KV_LORA], kv_a_norm)
    kr = apply_rope(rms_norm(kv_down[..., KV_LORA:], k_norm_rope), pos)
    return qn, qr, jnp.concatenate([c, kr], axis=-1)


def mla_keys(latent, w_uk, w_uv, k_norm_nope):
    bsz, s = latent.shape[:2]
    c = latent[..., :KV_LORA].astype(w_uk.dtype)
    kr = latent[..., KV_LORA:].astype(w_uk.dtype)
    kn = rms_norm((c @ w_uk).reshape(bsz, s, MLA_HEADS, NOPE_DIM), k_norm_nope)
    v = (c @ w_uv).reshape(bsz, s, MLA_HEADS, V_DIM)
    return kn, kr, v


def mla_scores(qn, qr, kn, kr):
    s = jnp.einsum('bqhd,bkhd->bhqk', qn, kn) + jnp.einsum('bqhr,bkr->bhqk', qr, kr)
    return s.astype(jnp.float32) * MLA_SCALE


def mla_prompt_attention(qn, qr, kn, kr, v):
    bsz, t = qn.shape[:2]
    nb = t // Q_BLOCK
    kpos = jnp.arange(t)

    def blocks(arr):
        return jnp.moveaxis(arr.reshape(bsz, nb, Q_BLOCK, *arr.shape[2:]), 1, 0)

    def attend(inp):
        qn_b, qr_b, i = inp
        s = mla_scores(qn_b, qr_b, kn, kr)
        qpos = i * Q_BLOCK + jnp.arange(Q_BLOCK)
        s = jnp.where(kpos[None, :] <= qpos[:, None], s, -jnp.inf)
        p = jax.nn.softmax(s, axis=-1).astype(v.dtype)
        return jnp.einsum('bhqk,bkhd->bqhd', p, v)

    o = lax.map(attend, (blocks(qn), blocks(qr), jnp.arange(nb)))
    return jnp.moveaxis(o, 0, 1).reshape(bsz, t, MLA_HEADS * V_DIM)


def partial_softmax(s, v):
    m = jnp.max(s, axis=-1)
    p = jnp.exp(s - m[..., None])
    acc = jnp.einsum('bhqk,bkhd->bhqd', p.astype(v.dtype), v).astype(jnp.float32)
    return m, jnp.sum(p, axis=-1), acc


def mla_sample_attention(qn, qr, new_latent, cache_l, page_table, w_uk, w_uv, k_norm_nope):
    bsz, t = qn.shape[:2]

    def page_block(phys):
        kn, kr, v = mla_keys(cache_l[phys], w_uk, w_uv, k_norm_nope)
        return partial_softmax(mla_scores(qn, qr, kn, kr), v)

    m_p, l_p, a_p = lax.map(page_block, page_table.T)
    kn, kr, v = mla_keys(new_latent, w_uk, w_uv, k_norm_nope)
    causal = jnp.tril(jnp.ones((t, t), dtype=bool))
    s = jnp.where(causal, mla_scores(qn, qr, kn, kr), -jnp.inf)
    m_s, l_s, a_s = partial_softmax(s, v)
    m = jnp.concatenate([m_p, m_s[None]], axis=0)
    l = jnp.concatenate([l_p, l_s[None]], axis=0)
    a = jnp.concatenate([a_p, a_s[None]], axis=0)
    wgt = jnp.exp(m - jnp.max(m, axis=0))
    o = jnp.sum(wgt[..., None] * a, axis=0) / jnp.sum(wgt * l, axis=0)[..., None]
    return jnp.moveaxis(o, 1, 2).reshape(bsz, t, MLA_HEADS * V_DIM).astype(qn.dtype)


def merge_and_mlp(x, y_a, y_b, gates, w_ya, w_yb, w_o, norm_mlp, w_up, w_down):
    g_a, g_b = jnp.split(jax.nn.sigmoid(gates), 2, axis=-1)
    x = x + (g_a * (y_a @ w_ya) + g_b * (y_b @ w_yb)) @ w_o
    h = jnp.square(jax.nn.relu(rms_norm(x, norm_mlp) @ w_up))
    return x + h @ w_down


def decoder_layer(x_p, x_s, cache_l, gdn_state_l, conv_state_l, page_table,
                  norm_mix, w_in, conv_w, a_log, dt_bias, gdn_norm,
                  q_a_norm, w_uq, kv_a_norm, w_uk, w_uv,
                  q_norm_nope, q_norm_rope, k_norm_nope, k_norm_rope,
                  w_ya, w_yb, w_o, norm_mlp, w_up, w_down):
    def run_group(x, pos, conv_buf, s0, gdn_core, mla_attend):
        qkv_raw, z, b, a, q_down, kv_down, gates = jnp.split(
            rms_norm(x, norm_mix) @ w_in, IN_SPLITS, axis=-1)
        q, k, v, g, beta, new_buf = gdn_inputs(qkv_raw, b, a, conv_buf, conv_w, a_log, dt_bias)
        o_a, s_new = gdn_core(q, k, v, g, beta, s0)
        y_a = gdn_output(o_a, z, gdn_norm)
        qn, qr, latent = mla_queries_and_latent(q_down, kv_down, pos, q_a_norm, w_uq, kv_a_norm,
                                                q_norm_nope, q_norm_rope, k_norm_rope)
        y_b = mla_attend(qn, qr, latent)
        y = merge_and_mlp(x, y_a, y_b, gates, w_ya, w_yb, w_o, norm_mlp, w_up, w_down)
        return y, latent, s_new, new_buf

    bp, tp = x_p.shape[:2]
    y_p, rows_p, s_p, buf_p = run_group(
        x_p, jnp.arange(tp),
        jnp.zeros((bp, CONV_W - 1, QKV_WIDTH), x_p.dtype),
        jnp.zeros((bp, GDN_HEADS, GDN_DK, GDN_DV), jnp.float32),
        gdn_chunked,
        lambda qn, qr, lat: mla_prompt_attention(qn, qr, *mla_keys(lat, w_uk, w_uv, k_norm_nope)))
    past_len = page_table.shape[1] * PAGE_SIZE
    y_s, rows_s, s_s, buf_s = run_group(
        x_s, past_len + jnp.arange(x_s.shape[1]),
        conv_state_l, gdn_state_l,
        gdn_recurrent,
        lambda qn, qr, lat: mla_sample_attention(qn, qr, lat, cache_l, page_table, w_uk, w_uv, k_norm_nope))
    return y_p, y_s, rows_p, s_p, buf_p, rows_s, s_s, buf_s


def setup_inputs(seed: int = 0) -> dict:
    key = jax.random.key(seed)
    k = jax.random.split(key, 32)
    f32 = jnp.float32
    n_pages = PAST_LEN // PAGE_SIZE
    n_used = DEC_BATCH * n_pages
    n_phys = n_used + n_used // 4

    def dense(kk, fan_in, fan_out):
        return jax.random.normal(kk, (DEPTH, fan_in, fan_out), f32) * fan_in ** -0.5

    def gain(kk, n):
        return 1.0 + 0.02 * jax.random.normal(kk, (DEPTH, n), f32)

    dt = jax.random.uniform(k[8], (DEPTH, GDN_HEADS), f32, 0.001, 0.1)
    return {
        'x_prompt': jax.random.normal(k[0], (BATCH, SEQ, D_MODEL), f32),
        'x_sample': jax.random.normal(k[1], (DEC_BATCH, DEC_SEQ, D_MODEL), f32),
        'cache_mla': jax.random.normal(k[2], (DEPTH, n_phys, PAGE_SIZE, LATENT_WIDTH), f32),
        'state_gdn': jax.random.normal(k[3], (DEPTH, DEC_BATCH, GDN_HEADS, GDN_DK, GDN_DV), f32) * GDN_DK ** -0.5,
        'state_conv': jax.random.normal(k[4], (DEPTH, DEC_BATCH, CONV_W - 1, QKV_WIDTH), f32),
        'page_table': jax.random.permutation(k[5], n_phys)[:n_used].reshape(DEC_BATCH, n_pages).astype(jnp.int32),
        'norm_mix': gain(k[6], D_MODEL),
        'w_in': dense(k[7], D_MODEL, IN_WIDTH),
        'conv_w': jax.random.normal(k[9], (DEPTH, CONV_W, QKV_WIDTH), f32) * CONV_W ** -0.5,
        'a_log': jnp.log(jax.random.uniform(k[10], (DEPTH, GDN_HEADS), f32, 1.0, 16.0)),
        'dt_bias': dt + jnp.log(-jnp.expm1(-dt)),
        'gdn_norm': gain(k[11], GDN_DV),
        'q_a_norm': gain(k[12], Q_LORA),
        'w_uq': dense(k[13], Q_LORA, MLA_HEADS * (NOPE_DIM + ROPE_DIM)),
        'kv_a_norm': gain(k[14], KV_LORA),
        'w_uk': dense(k[15], KV_LORA, MLA_HEADS * NOPE_DIM),
        'w_uv': dense(k[16], KV_LORA, MLA_HEADS * V_DIM),
        'q_norm_nope': gain(k[17], NOPE_DIM),
        'q_norm_rope': gain(k[18], ROPE_DIM),
        'k_norm_nope': gain(k[19], NOPE_DIM),
        'k_norm_rope': gain(k[20], ROPE_DIM),
        'w_ya': dense(k[21], Z_WIDTH, D_MODEL),
        'w_yb': dense(k[22], MLA_HEADS * V_DIM, D_MODEL),
        'w_o': dense(k[23], D_MODEL, D_MODEL),
        'norm_mlp': gain(k[24], D_MODEL),
        'w_up': dense(k[25], D_MODEL, D_FF),
        'w_down': dense(k[26], D_FF, D_MODEL),
    }


def reference(x_prompt, x_sample, cache_mla, state_gdn, state_conv, page_table,
              norm_mix, w_in, conv_w, a_log, dt_bias, gdn_norm,
              q_a_norm, w_uq, kv_a_norm, w_uk, w_uv,
              q_norm_nope, q_norm_rope, k_norm_nope, k_norm_rope,
              w_ya, w_yb, w_o, norm_mlp, w_up, w_down):
    x_p, x_s = x_prompt, x_sample
    rows_p, gdn_p, conv_p, rows_s, gdn_s, conv_s = [], [], [], [], [], []
    for l in range(DEPTH):
        x_p, x_s, rp, sp, bp, rs, ss, bs = decoder_layer(
            x_p, x_s, cache_mla[l], state_gdn[l], state_conv[l], page_table,
            norm_mix[l], w_in[l], conv_w[l], a_log[l], dt_bias[l], gdn_norm[l],
            q_a_norm[l], w_uq[l], kv_a_norm[l], w_uk[l], w_uv[l],
            q_norm_nope[l], q_norm_rope[l], k_norm_nope[l], k_norm_rope[l],
            w_ya[l], w_yb[l], w_o[l], norm_mlp[l], w_up[l], w_down[l])
        rows_p.append(rp)
        gdn_p.append(sp)
        conv_p.append(bp)
        rows_s.append(rs)
        gdn_s.append(ss)
        conv_s.append(bs)
    return (x_p, x_s, jnp.stack(rows_p), jnp.stack(gdn_p), jnp.stack(conv_p),
            jnp.stack(rows_s), jnp.stack(gdn_s), jnp.stack(conv_s))
```

```python
import functools

import jax
import jax.numpy as jnp
from jax import lax
from jax.experimental import pallas as pl
from jax.experimental.pallas import tpu as pltpu

F32 = jnp.float32
BF16 = jnp.bfloat16

EPS = 1e-6
GDN_HEADS = 8
GDN_DK = 128
GDN_DV = 128
CONV_W = 4
GDN_CHUNK = 64
MLA_HEADS = 8
Q_LORA = 512
KV_LORA = 512
NOPE_DIM = 128
ROPE_DIM = 64
V_DIM = 128
ROPE_THETA = 10000.0
MLA_SCALE = (NOPE_DIM + ROPE_DIM) ** -0.5
LATENT_WIDTH = KV_LORA + ROPE_DIM
QKV_WIDTH = GDN_HEADS * (2 * GDN_DK + GDN_DV)
Z_WIDTH = GDN_HEADS * GDN_DV

LANES = 128
QK_PAD = 256
VMEM_LIMIT = 56 * 1024 * 1024
DEC_PAGES_PER_STEP = 16


def _dot(a, b):
    return jnp.dot(a, b, preferred_element_type=F32)


def _dot_nt(a, b):
    return lax.dot_general(a, b, (((1,), (1,)), ((), ())), preferred_element_type=F32)


def _dot_tn(a, b):
    return lax.dot_general(a, b, (((0,), (0,)), ((), ())), preferred_element_type=F32)


def _split2(x):
    hi = x.astype(BF16)
    lo = (x - hi.astype(F32)).astype(BF16)
    return hi, lo


def _split3(x):
    hi = x.astype(BF16)
    r = x - hi.astype(F32)
    mid = r.astype(BF16)
    lo = (r - mid.astype(F32)).astype(BF16)
    return hi, mid, lo


def _rms(x, g, n=None):
    n = x.shape[-1] if n is None else n
    ss = jnp.sum(x * x, axis=-1, keepdims=True) * (1.0 / n)
    return x * lax.rsqrt(ss + EPS) * g


def _sigmoid(x):
    return 1.0 / (1.0 + jnp.exp(-x))


def _silu(x):
    return x * _sigmoid(x)


def _softplus(x):
    return jnp.maximum(x, 0.0) + jnp.log1p(jnp.exp(-jnp.abs(x)))


def _rope(x, cos2, sin2):
    lane = lax.broadcasted_iota(jnp.int32, x.shape, 1)
    fwd = pltpu.roll(x, LANES - ROPE_DIM // 2, axis=1)
    bwd = pltpu.roll(x, ROPE_DIM // 2, axis=1)
    swapped = jnp.where(lane < ROPE_DIM // 2, fwd, bwd)
    return x * cos2 + swapped * sin2


def _full_spec(shape):
    nd = len(shape)
    return pl.BlockSpec(shape, lambda *_: (0,) * nd)


def _params(*sem):
    return pltpu.CompilerParams(dimension_semantics=sem, vmem_limit_bytes=VMEM_LIMIT)


def _gdn_proj_kernel(x_ref, g_ref, wqkv_ref, wz_ref, wba_ref, qkv_ref, z_ref, ba_ref):
    xn = _rms(x_ref[...], g_ref[...]).astype(BF16)
    qkv_ref[...] = _dot(xn, wqkv_ref[...])
    z_ref[...] = _dot(xn, wz_ref[...])
    ba_ref[...] = _dot(xn, wba_ref[...])


def _gdn_proj(x2, g, wqkv, wz, wba):
    m, d = x2.shape
    tm = min(256, m)
    return pl.pallas_call(
        _gdn_proj_kernel,
        grid=(m // tm,),
        in_specs=[pl.BlockSpec((tm, d), lambda i: (i, 0)),
                  _full_spec(g.shape), _full_spec(wqkv.shape), _full_spec(wz.shape),
                  _full_spec(wba.shape)],
        out_specs=[pl.BlockSpec((tm, QKV_WIDTH), lambda i: (i, 0)),
                   pl.BlockSpec((tm, Z_WIDTH), lambda i: (i, 0)),
                   pl.BlockSpec((tm, LANES), lambda i: (i, 0))],
        out_shape=[jax.ShapeDtypeStruct((m, QKV_WIDTH), F32),
                   jax.ShapeDtypeStruct((m, Z_WIDTH), F32),
                   jax.ShapeDtypeStruct((m, LANES), F32)],
        compiler_params=_params("parallel"),
        name="gdn_proj",
    )(x2, g, wqkv, wz, wba)


def _gate_rows(ba, alog_row, dt_row):
    lane = lax.broadcasted_iota(jnp.int32, ba.shape, 1)
    is_a = (lane >= GDN_HEADS) & (lane < 2 * GDN_HEADS)
    g = jnp.where(is_a, -jnp.exp(alog_row) * _softplus(ba + dt_row), 0.0)
    return _sigmoid(ba), g


def _gdn_chunk_kernel(qkv_ref, z_ref, ba_ref, cw_ref, alog_ref, dt_ref, gn_ref,
                      ya_ref, s_ref, cs_ref, xp_ref):
    c = pl.program_id(1)
    nc = pl.num_programs(1)
    C = GDN_CHUNK

    @pl.when(c == 0)
    def _():
        xp_ref[0:8, :] = jnp.zeros((8, QKV_WIDTH), F32)
        s_ref[...] = jnp.zeros(s_ref.shape, F32)

    blk = qkv_ref[0]
    xp_ref[8:8 + C, :] = blk
    cw = cw_ref[...]
    y = blk * cw[CONV_W - 1:CONV_W]
    for i in range(CONV_W - 1):
        y = y + xp_ref[8 - (CONV_W - 1) + i:8 - (CONV_W - 1) + i + C, :] * cw[i:i + 1]
    y = _silu(y)

    @pl.when(c == nc - 1)
    def _():
        cs_ref[0] = xp_ref[8 + C - (CONV_W - 1):8 + C, :]

    xp_ref[0:8, :] = xp_ref[C:C + 8, :]

    beta_full, g_full = _gate_rows(ba_ref[0], alog_ref[...], dt_ref[...])
    row = lax.broadcasted_iota(jnp.int32, (C, C), 0)
    col = lax.broadcasted_iota(jnp.int32, (C, C), 1)
    incl = row >= col
    strict = row > col
    tril = jnp.where(incl, 1.0, 0.0).astype(BF16)
    g_col = sum(_dot(tril, part) for part in _split3(g_full))
    r128 = lax.broadcasted_iota(jnp.int32, (LANES, LANES), 0)
    c128 = lax.broadcasted_iota(jnp.int32, (LANES, LANES), 1)
    eye = jnp.where(r128 == c128, 1.0, 0.0).astype(BF16)
    g_rows = sum(_dot_nt(eye, part) for part in _split3(g_col))

    z = z_ref[0]
    gn = gn_ref[...]
    for h in range(GDN_HEADS):
        q = y[:, h * GDN_DK:(h + 1) * GDN_DK]
        k = y[:, GDN_HEADS * GDN_DK + h * GDN_DK:GDN_HEADS * GDN_DK + (h + 1) * GDN_DK]
        v = y[:, 2 * GDN_HEADS * GDN_DK + h * GDN_DV:2 * GDN_HEADS * GDN_DK + (h + 1) * GDN_DV]
        q = q * lax.rsqrt(jnp.sum(q * q, axis=-1, keepdims=True) + EPS) * (GDN_DK ** -0.5)
        k = k * lax.rsqrt(jnp.sum(k * k, axis=-1, keepdims=True) + EPS)
        beta = beta_full[:, h:h + 1]
        gc = g_col[:, GDN_HEADS + h:GDN_HEADS + h + 1]
        gr = g_rows[GDN_HEADS + h:GDN_HEADS + h + 1, :]
        decay = jnp.where(incl, jnp.exp(jnp.where(incl, gc - gr, 0.0)), 0.0)
        eg = jnp.exp(gc)
        g_last = gc[C - 1:C, :]
        kb = k * beta
        vb = v * beta
        k_bf = k.astype(BF16)
        kq = _dot_nt(jnp.concatenate([kb, q], axis=0).astype(BF16), k_bf)
        lmat = jnp.where(strict, kq[:C] * decay, 0.0)
        aqk = kq[C:] * decay
        n_acc = -lmat
        pw = lmat
        for _ in range(5):
            pw_bf = pw.astype(BF16)
            pw = _dot(pw_bf, pw_bf)
            n_acc = n_acc + pw + _dot(n_acc.astype(BF16), pw.astype(BF16))
        rhs = jnp.concatenate([vb, kb * eg], axis=1)
        n_hi, n_lo = _split2(n_acc)
        rhs_bf = rhs.astype(BF16)
        sol = rhs + _dot(n_hi, rhs_bf) + _dot(n_lo, rhs_bf)
        u0 = sol[:, :GDN_DV]
        w = sol[:, GDN_DV:]
        s_old = s_ref[0, h]
        wq = jnp.concatenate([w, q * eg], axis=0).astype(BF16)
        ws = _dot(wq, s_old.astype(BF16))
        u = u0 - ws[:C]
        u_bf = u.astype(BF16)
        o = ws[C:] + _dot(aqk.astype(BF16), u_bf)
        kd = (k * jnp.exp(g_last - gc)).astype(BF16)
        s_ref[0, h] = jnp.exp(g_last) * s_old + _dot_tn(kd, u_bf)
        zh = z[:, h * GDN_DV:(h + 1) * GDN_DV]
        ya_ref[0, :, h * GDN_DV:(h + 1) * GDN_DV] = (_rms(o, gn) * _silu(zh)).astype(BF16)


def _gdn_chunked(qkv, z, ba, conv_w, alog_row, dt_row, gdn_norm):
    b, t, _ = qkv.shape
    nc = t // GDN_CHUNK
    C = GDN_CHUNK
    return pl.pallas_call(
        _gdn_chunk_kernel,
        grid=(b, nc),
        in_specs=[pl.BlockSpec((1, C, QKV_WIDTH), lambda i, j: (i, j, 0)),
                  pl.BlockSpec((1, C, Z_WIDTH), lambda i, j: (i, j, 0)),
                  pl.BlockSpec((1, C, LANES), lambda i, j: (i, j, 0)),
                  _full_spec(conv_w.shape), _full_spec(alog_row.shape),
                  _full_spec(dt_row.shape), _full_spec(gdn_norm.shape)],
        out_specs=[pl.BlockSpec((1, C, Z_WIDTH), lambda i, j: (i, j, 0)),
                   pl.BlockSpec((1, GDN_HEADS, GDN_DK, GDN_DV), lambda i, j: (i, 0, 0, 0)),
                   pl.BlockSpec((1, CONV_W - 1, QKV_WIDTH), lambda i, j: (i, 0, 0))],
        out_shape=[jax.ShapeDtypeStruct((b, t, Z_WIDTH), BF16),
                   jax.ShapeDtypeStruct((b, GDN_HEADS, GDN_DK, GDN_DV), F32),
                   jax.ShapeDtypeStruct((b, CONV_W - 1, QKV_WIDTH), F32)],
        scratch_shapes=[pltpu.VMEM((8 + C, QKV_WIDTH), F32)],
        compiler_params=_params("parallel", "arbitrary"),
        name="gdn_chunked",
    )(qkv, z, ba, conv_w, alog_row, dt_row, gdn_norm)


def _gdn_step_kernel(qkv_ref, z_ref, ba_ref, cs_ref, s_ref, cw_ref, alog_ref, dt_ref, gn_ref,
                     ya_ref, snew_ref, csnew_ref):
    rowv = qkv_ref[0]
    cs = cs_ref[0]
    cw = cw_ref[...]
    y = rowv * cw[CONV_W - 1:CONV_W]
    for i in range(CONV_W - 1):
        y = y + cs[i:i + 1] * cw[i:i + 1]
    y = _silu(y)
    csnew_ref[0, 0:CONV_W - 2, :] = cs[1:CONV_W - 1]
    csnew_ref[0, CONV_W - 2:CONV_W - 1, :] = rowv

    beta_full, g_full = _gate_rows(ba_ref[0], alog_ref[...], dt_ref[...])
    r128 = lax.broadcasted_iota(jnp.int32, (GDN_DK, GDN_DK), 0)
    c128 = lax.broadcasted_iota(jnp.int32, (GDN_DK, GDN_DK), 1)
    eye = r128 == c128
    z = z_ref[0]
    gn = gn_ref[...]

    def to_col(r):
        return jnp.sum(jnp.where(eye, jnp.broadcast_to(r, (GDN_DK, GDN_DK)), 0.0),
                       axis=1, keepdims=True)

    for h in range(GDN_HEADS):
        q = y[:, h * GDN_DK:(h + 1) * GDN_DK]
        k = y[:, GDN_HEADS * GDN_DK + h * GDN_DK:GDN_HEADS * GDN_DK + (h + 1) * GDN_DK]
        v = y[:, 2 * GDN_HEADS * GDN_DK + h * GDN_DV:2 * GDN_HEADS * GDN_DK + (h + 1) * GDN_DV]
        q = q * lax.rsqrt(jnp.sum(q * q, axis=-1, keepdims=True) + EPS) * (GDN_DK ** -0.5)
        k = k * lax.rsqrt(jnp.sum(k * k, axis=-1, keepdims=True) + EPS)
        beta = beta_full[:, h:h + 1]
        gdec = jnp.exp(g_full[:, GDN_HEADS + h:GDN_HEADS + h + 1])
        k_col = to_col(k)
        q_col = to_col(q)
        s = gdec * s_ref[0, h]
        u = beta * (v - jnp.sum(s * k_col, axis=0, keepdims=True))
        s = s + k_col * u
        snew_ref[0, h] = s
        o = jnp.sum(s * q_col, axis=0, keepdims=True)
        zh = z[:, h * GDN_DV:(h + 1) * GDN_DV]
        ya_ref[0, :, h * GDN_DV:(h + 1) * GDN_DV] = _rms(o, gn) * _silu(zh)


def _gdn_step(qkv, z, ba, conv_state, state, conv_w, alog_row, dt_row, gdn_norm):
    b = qkv.shape[0]
    qkv3 = qkv.reshape(b, 1, QKV_WIDTH)
    z3 = z.reshape(b, 1, Z_WIDTH)
    ba3 = ba.reshape(b, 1, LANES)
    return pl.pallas_call(
        _gdn_step_kernel,
        grid=(b,),
        in_specs=[pl.BlockSpec((1, 1, QKV_WIDTH), lambda i: (i, 0, 0)),
                  pl.BlockSpec((1, 1, Z_WIDTH), lambda i: (i, 0, 0)),
                  pl.BlockSpec((1, 1, LANES), lambda i: (i, 0, 0)),
                  pl.BlockSpec((1, CONV_W - 1, QKV_WIDTH), lambda i: (i, 0, 0)),
                  pl.BlockSpec((1, GDN_HEADS, GDN_DK, GDN_DV), lambda i: (i, 0, 0, 0)),
                  _full_spec(conv_w.shape), _full_spec(alog_row.shape),
                  _full_spec(dt_row.shape), _full_spec(gdn_norm.shape)],
        out_specs=[pl.BlockSpec((1, 1, Z_WIDTH), lambda i: (i, 0, 0)),
                   pl.BlockSpec((1, GDN_HEADS, GDN_DK, GDN_DV), lambda i: (i, 0, 0, 0)),
                   pl.BlockSpec((1, CONV_W - 1, QKV_WIDTH), lambda i: (i, 0, 0))],
        out_shape=[jax.ShapeDtypeStruct((b, 1, Z_WIDTH), F32),
                   jax.ShapeDtypeStruct((b, GDN_HEADS, GDN_DK, GDN_DV), F32),
                   jax.ShapeDtypeStruct((b, CONV_W - 1, QKV_WIDTH), F32)],
        compiler_params=_params("parallel"),
        name="gdn_step",
    )(qkv3, z3, ba3, conv_state, state, conv_w, alog_row, dt_row, gdn_norm)


def _mla_queries_latent(x, cos2, sin2, g_mix, w_qd, w_c, w_kr, g_qa, w_uq, g_kva,
                        g_qn, g_qr, g_kr):
    xn = _rms(x, g_mix).astype(BF16)
    qa = _rms(_dot(xn, w_qd), g_qa).astype(BF16)
    q = _dot(qa, w_uq)
    qn, qr = [], []
    for h in range(MLA_HEADS):
        qn.append(_rms(q[:, h * QK_PAD:h * QK_PAD + NOPE_DIM], g_qn))
        qr.append(_rope(_rms(q[:, h * QK_PAD + NOPE_DIM:(h + 1) * QK_PAD], g_qr, ROPE_DIM),
                        cos2, sin2))
    c = _rms(_dot(xn, w_c), g_kva)
    kr = _rope(_rms(_dot(xn, w_kr), g_kr, ROPE_DIM), cos2, sin2)
    return qn, qr, c, kr


def _mla_pre_kernel(x_ref, cos_ref, sin_ref, gmix_ref, wqd_ref, wc_ref, wkr_ref, gqa_ref,
                    wuq_ref, gkva_ref, gqn_ref, gqr_ref, gkr_ref, wuk_ref, wuv_ref, gkn_ref,
                    q_ref, k_ref, v_ref, lat_ref):
    qn, qr, c, kr = _mla_queries_latent(
        x_ref[0], cos_ref[...], sin_ref[...], gmix_ref[...], wqd_ref[...], wc_ref[...],
        wkr_ref[...], gqa_ref[...], wuq_ref[...], gkva_ref[...], gqn_ref[...], gqr_ref[...],
        gkr_ref[...])
    lat_ref[0, :, 0:KV_LORA] = c
    lat_ref[0, :, KV_LORA:LATENT_WIDTH] = kr[:, 0:ROPE_DIM]
    c_bf = c.astype(BF16)
    kr_bf = kr.astype(BF16)
    kfull = _dot(c_bf, wuk_ref[...])
    v_ref[0] = _dot(c_bf, wuv_ref[...]).astype(BF16)
    gkn = gkn_ref[...]
    for h in range(MLA_HEADS):
        q_ref[0, :, h * QK_PAD:h * QK_PAD + NOPE_DIM] = qn[h].astype(BF16)
        q_ref[0, :, h * QK_PAD + NOPE_DIM:(h + 1) * QK_PAD] = qr[h].astype(BF16)
        kn = _rms(kfull[:, h * NOPE_DIM:(h + 1) * NOPE_DIM], gkn)
        k_ref[0, :, h * QK_PAD:h * QK_PAD + NOPE_DIM] = kn.astype(BF16)
        k_ref[0, :, h * QK_PAD + NOPE_DIM:(h + 1) * QK_PAD] = kr_bf


def _mla_pre(x, cos2, sin2, mw):
    b, t, d = x.shape
    tm = min(256, t)
    consts = [mw["g_mix"], mw["w_qd"], mw["w_c"], mw["w_kr"], mw["g_qa"], mw["w_uq"],
              mw["g_kva"], mw["g_qn"], mw["g_qr"], mw["g_kr"], mw["w_uk"], mw["w_uv"],
              mw["g_kn"]]
    return pl.pallas_call(
        _mla_pre_kernel,
        grid=(b, t // tm),
        in_specs=[pl.BlockSpec((1, tm, d), lambda i, j: (i, j, 0)),
                  pl.BlockSpec((tm, LANES), lambda i, j: (j, 0)),
                  pl.BlockSpec((tm, LANES), lambda i, j: (j, 0))]
                 + [_full_spec(a.shape) for a in consts],
        out_specs=[pl.BlockSpec((1, tm, MLA_HEADS * QK_PAD), lambda i, j: (i, j, 0)),
                   pl.BlockSpec((1, tm, MLA_HEADS * QK_PAD), lambda i, j: (i, j, 0)),
                   pl.BlockSpec((1, tm, MLA_HEADS * V_DIM), lambda i, j: (i, j, 0)),
                   pl.BlockSpec((1, tm, LATENT_WIDTH), lambda i, j: (i, j, 0))],
        out_shape=[jax.ShapeDtypeStruct((b, t, MLA_HEADS * QK_PAD), BF16),
                   jax.ShapeDtypeStruct((b, t, MLA_HEADS * QK_PAD), BF16),
                   jax.ShapeDtypeStruct((b, t, MLA_HEADS * V_DIM), BF16),
                   jax.ShapeDtypeStruct((b, t, LATENT_WIDTH), F32)],
        compiler_params=_params("parallel", "parallel"),
        name="mla_pre",
    )(x, cos2, sin2, *consts)


def _mla_pre_sample_kernel(x_ref, cos_ref, sin_ref, gmix_ref, wqd_ref, wc_ref, wkr_ref, gqa_ref,
                           wuq_ref, gkva_ref, gqn_ref, gqr_ref, gkr_ref, wuk_ref, gkn_ref,
                           qa_ref, qr_ref, lat_ref):
    qn, qr, c, kr = _mla_queries_latent(
        x_ref[...], cos_ref[...], sin_ref[...], gmix_ref[...], wqd_ref[...], wc_ref[...],
        wkr_ref[...], gqa_ref[...], wuq_ref[...], gkva_ref[...], gqn_ref[...], gqr_ref[...],
        gkr_ref[...])
    lat_ref[:, 0:KV_LORA] = c
    lat_ref[:, KV_LORA:LATENT_WIDTH] = kr[:, 0:ROPE_DIM]
    gkn = gkn_ref[...]
    for h in range(MLA_HEADS):
        hi, lo = _split2(qn[h] * gkn)
        w_h = wuk_ref[:, h * NOPE_DIM:(h + 1) * NOPE_DIM]
        qa_ref[h] = _dot_nt(hi, w_h) + _dot_nt(lo, w_h)
        qr_ref[h] = qr[h]


def _mla_pre_sample(x2, cos2, sin2, mw):
    m = x2.shape[0]
    consts = [mw["g_mix"], mw["w_qd"], mw["w_c"], mw["w_kr"], mw["g_qa"], mw["w_uq"],
              mw["g_kva"], mw["g_qn"], mw["g_qr"], mw["g_kr"], mw["w_uk"], mw["g_kn"]]
    args = [x2, cos2, sin2] + consts
    return pl.pallas_call(
        _mla_pre_sample_kernel,
        grid=(1,),
        in_specs=[_full_spec(a.shape) for a in args],
        out_specs=[_full_spec((MLA_HEADS, m, KV_LORA)), _full_spec((MLA_HEADS, m, LANES)),
                   _full_spec((m, LATENT_WIDTH))],
        out_shape=[jax.ShapeDtypeStruct((MLA_HEADS, m, KV_LORA), F32),
                   jax.ShapeDtypeStruct((MLA_HEADS, m, LANES), F32),
                   jax.ShapeDtypeStruct((m, LATENT_WIDTH), F32)],
        compiler_params=_params("arbitrary"),
        name="mla_pre_sample",
    )(*args)


def _attn_kernel(q_ref, k_ref, v_ref, o_ref, *, tq):
    qi = pl.program_id(2)
    q = q_ref[0]
    row = lax.broadcasted_iota(jnp.int32, (tq, tq), 0)
    col = lax.broadcasted_iota(jnp.int32, (tq, tq), 1)

    def body(j, carry):
        m, l, acc = carry
        start = pl.multiple_of(j * tq, tq)
        k = k_ref[0, pl.ds(start, tq), :]
        v = v_ref[0, pl.ds(start, tq), :]
        s = _dot_nt(q, k) * MLA_SCALE
        s = jnp.where((j * tq + col) <= (qi * tq + row), s, -jnp.inf)
        m_new = jnp.maximum(m, jnp.max(s, axis=-1, keepdims=True))
        alpha = jnp.exp(m - m_new)
        p = jnp.exp(s - m_new)
        l = alpha * l + jnp.sum(p, axis=-1, keepdims=True)
        acc = alpha * acc + _dot(p.astype(BF16), v)
        return m_new, l, acc

    init = (jnp.full((tq, 1), -jnp.inf, F32), jnp.zeros((tq, 1), F32),
            jnp.zeros((tq, V_DIM), F32))
    _, l, acc = lax.fori_loop(0, qi + 1, body, init)
    o_ref[0] = (acc / l).astype(BF16)


def _mla_prompt_attention(q_cat, k_cat, v):
    b, t, _ = q_cat.shape
    tq = min(256, t)
    return pl.pallas_call(
        functools.partial(_attn_kernel, tq=tq),
        grid=(b, MLA_HEADS, t // tq),
        in_specs=[pl.BlockSpec((1, tq, QK_PAD), lambda i, h, j: (i, j, h)),
                  pl.BlockSpec((1, t, QK_PAD), lambda i, h, j: (i, 0, h)),
                  pl.BlockSpec((1, t, V_DIM), lambda i, h, j: (i, 0, h))],
        out_specs=pl.BlockSpec((1, tq, V_DIM), lambda i, h, j: (i, j, h)),
        out_shape=jax.ShapeDtypeStruct((b, t, MLA_HEADS * V_DIM), BF16),
        compiler_params=_params("parallel", "parallel", "arbitrary"),
        name="mla_prompt_attention",
    )(q_cat, k_cat, v)


def _dec_attn_kernel(pt_ref, *refs, pps):
    del pt_ref
    page_refs = refs[:pps]
    (wukt_ref, qa_ref, qr_ref, lnew_ref, ctx_ref,
     lhs_ref, qr16_ref, m_ref, l_ref, acc_ref) = refs[pps:]
    b = pl.program_id(0)
    j = pl.program_id(1)
    nj = pl.num_programs(1)
    n_k = MLA_HEADS * NOPE_DIM

    @pl.when((b == 0) & (j == 0))
    def _():
        lhs_ref[0:n_k, :] = wukt_ref[...]

    @pl.when(j == 0)
    def _():
        qa = jnp.concatenate([qa_ref[h, 0] for h in range(MLA_HEADS)]
                             + [jnp.zeros((8, KV_LORA), F32)], axis=0)
        lhs_ref[n_k:n_k + 16, :] = qa.astype(BF16)
        qr = jnp.concatenate([qr_ref[h, 0] for h in range(MLA_HEADS)]
                             + [jnp.zeros((8, LANES), F32)], axis=0)
        qr16_ref[...] = qr.astype(BF16)
        m_ref[...] = jnp.full(m_ref.shape, -jnp.inf, F32)
        l_ref[...] = jnp.zeros(l_ref.shape, F32)
        acc_ref[...] = jnp.zeros(acc_ref.shape, F32)

    def scores(c_bf, kr_bf):
        n = c_bf.shape[0]
        res = _dot_nt(lhs_ref[...], c_bf)
        kt = res[0:n_k]
        ss = jnp.sum((kt * kt).reshape(MLA_HEADS, NOPE_DIM, n), axis=1)
        sn = res[n_k:n_k + MLA_HEADS]
        sr = _dot_nt(qr16_ref[:, 0:ROPE_DIM], kr_bf)[0:MLA_HEADS]
        return (sn * lax.rsqrt(ss * (1.0 / NOPE_DIM) + EPS) + sr) * MLA_SCALE

    for i in range(0, pps, 2):
        tile = jnp.concatenate([page_refs[i][...], page_refs[i + 1][...]], axis=0)
        c_bf = tile[:, 0:KV_LORA].astype(BF16)
        kr_bf = tile[:, KV_LORA:LATENT_WIDTH].astype(BF16)
        s = scores(c_bf, kr_bf)
        m_old = m_ref[...]
        m_new = jnp.maximum(m_old, jnp.max(s, axis=-1, keepdims=True))
        alpha = jnp.exp(m_old - m_new)
        p = jnp.exp(s - m_new)
        l_ref[...] = alpha * l_ref[...] + jnp.sum(p, axis=-1, keepdims=True)
        acc_ref[...] = alpha * acc_ref[...] + _dot(p.astype(BF16), c_bf)
        m_ref[...] = m_new

    @pl.when(j == nj - 1)
    def _():
        ln = lnew_ref[0]
        c_new = ln[:, 0:KV_LORA]
        c8 = jnp.broadcast_to(c_new, (8, KV_LORA)).astype(BF16)
        kr8 = jnp.broadcast_to(ln[:, KV_LORA:LATENT_WIDTH], (8, ROPE_DIM)).astype(BF16)
        s_new = scores(c8, kr8)[:, 0:1]
        m_old = m_ref[...]
        m_new = jnp.maximum(m_old, s_new)
        alpha = jnp.exp(m_old - m_new)
        p_new = jnp.exp(s_new - m_new)
        ctx = (alpha * acc_ref[...] + p_new * c_new) / (alpha * l_ref[...] + p_new)
        for h in range(MLA_HEADS):
            ctx_ref[h, 0] = ctx[h:h + 1]


def _mla_decode_attention(cache, layer, page_table, wuk_t, qa, qr, lat_new):
    bsz, n_pages = page_table.shape
    page = cache.shape[2]
    pps = DEC_PAGES_PER_STEP
    while n_pages % pps:
        pps //= 2
    assert pps >= 2
    qa4 = qa.reshape(MLA_HEADS, bsz, 1, KV_LORA)
    qr4 = qr.reshape(MLA_HEADS, bsz, 1, LANES)
    ln3 = lat_new.reshape(bsz, 1, LATENT_WIDTH)

    def page_spec(i):
        return pl.BlockSpec((pl.Squeezed(), pl.Squeezed(), page, LATENT_WIDTH),
                            lambda b, j, pt: (layer, pt[b, j * pps + i], 0, 0))

    grid_spec = pltpu.PrefetchScalarGridSpec(
        num_scalar_prefetch=1,
        grid=(bsz, n_pages // pps),
        in_specs=[page_spec(i) for i in range(pps)] + [
            pl.BlockSpec(wuk_t.shape, lambda b, j, pt: (0, 0)),
            pl.BlockSpec((MLA_HEADS, 1, 1, KV_LORA), lambda b, j, pt: (0, b, 0, 0)),
            pl.BlockSpec((MLA_HEADS, 1, 1, LANES), lambda b, j, pt: (0, b, 0, 0)),
            pl.BlockSpec((1, 1, LATENT_WIDTH), lambda b, j, pt: (b, 0, 0))],
        out_specs=pl.BlockSpec((MLA_HEADS, 1, 1, KV_LORA), lambda b, j, pt: (0, b, 0, 0)),
        scratch_shapes=[pltpu.VMEM((MLA_HEADS * NOPE_DIM + 16, KV_LORA), BF16),
                        pltpu.VMEM((16, LANES), BF16),
                        pltpu.VMEM((MLA_HEADS, 1), F32),
                        pltpu.VMEM((MLA_HEADS, 1), F32),
                        pltpu.VMEM((MLA_HEADS, KV_LORA), F32)])
    ctx = pl.pallas_call(
        functools.partial(_dec_attn_kernel, pps=pps),
        grid_spec=grid_spec,
        out_shape=jax.ShapeDtypeStruct((MLA_HEADS, bsz, 1, KV_LORA), F32),
        compiler_params=_params("arbitrary", "arbitrary"),
        name="mla_decode_attention",
    )(page_table, *([cache] * pps), wuk_t, qa4, qr4, ln3)
    return ctx.reshape(MLA_HEADS, bsz, KV_LORA)


def _dec_post_kernel(ctx_ref, wuv_ref, y_ref):
    for h in range(MLA_HEADS):
        hi, lo = _split2(ctx_ref[h])
        w_h = wuv_ref[:, h * V_DIM:(h + 1) * V_DIM]
        y_ref[:, h * V_DIM:(h + 1) * V_DIM] = (_dot(hi, w_h) + _dot(lo, w_h)).astype(BF16)


def _dec_post(ctx, w_uv):
    bsz = ctx.shape[1]
    return pl.pallas_call(
        _dec_post_kernel,
        grid=(1,),
        in_specs=[_full_spec(ctx.shape), _full_spec(w_uv.shape)],
        out_specs=_full_spec((bsz, MLA_HEADS * V_DIM)),
        out_shape=jax.ShapeDtypeStruct((bsz, MLA_HEADS * V_DIM), BF16),
        compiler_params=_params("arbitrary"),
        name="mla_decode_values",
    )(ctx, w_uv)


def _merge_mlp_kernel(x_ref, ya_ref, yb_ref, gmix_ref, wg_ref, wya_ref, wyb_ref, wo_ref,
                      gmlp_ref, wup_ref, wdn_ref, y_ref):
    x = x_ref[...]
    d = x.shape[-1]
    xn = _rms(x, gmix_ref[...]).astype(BF16)
    gates = _sigmoid(_dot(xn, wg_ref[...]))
    mix = (gates[:, :d] * _dot(ya_ref[...].astype(BF16), wya_ref[...])
           + gates[:, d:] * _dot(yb_ref[...].astype(BF16), wyb_ref[...]))
    x1 = x + _dot(mix.astype(BF16), wo_ref[...])
    hmid = jnp.maximum(_dot(_rms(x1, gmlp_ref[...]).astype(BF16), wup_ref[...]), 0.0)
    y_ref[...] = x1 + _dot((hmid * hmid).astype(BF16), wdn_ref[...])


def _merge_mlp(x2, ya, yb, ew):
    m, d = x2.shape
    tm = min(256, m)
    consts = [ew["g_mix"], ew["w_gates"], ew["w_ya"], ew["w_yb"], ew["w_o"], ew["g_mlp"],
              ew["w_up"], ew["w_down"]]

    def const_spec(a):
        nd = a.ndim
        return pl.BlockSpec(a.shape, lambda i: (0,) * nd, pipeline_mode=pl.Buffered(1))

    return pl.pallas_call(
        _merge_mlp_kernel,
        grid=(m // tm,),
        in_specs=[pl.BlockSpec((tm, d), lambda i: (i, 0)),
                  pl.BlockSpec((tm, ya.shape[1]), lambda i: (i, 0)),
                  pl.BlockSpec((tm, yb.shape[1]), lambda i: (i, 0))]
                 + [const_spec(a) for a in consts],
        out_specs=pl.BlockSpec((tm, d), lambda i: (i, 0)),
        out_shape=jax.ShapeDtypeStruct((m, d), F32),
        compiler_params=_params("parallel"),
        name="merge_mlp",
    )(x2, ya, yb, *consts)


def _rope_tables(pos):
    inv = jnp.power(ROPE_THETA, -jnp.arange(0, ROPE_DIM, 2, dtype=F32) / ROPE_DIM)
    ang = pos.astype(F32)[:, None] * inv[None, :]
    cos, sin = jnp.cos(ang), jnp.sin(ang)
    zeros = jnp.zeros((pos.shape[0], LANES - ROPE_DIM), F32)
    return (jnp.concatenate([cos, cos, zeros], axis=1),
            jnp.concatenate([-sin, sin, zeros], axis=1))


def _pad_lanes(a, width=LANES, offset=0):
    return jnp.zeros((1, width), F32).at[0, offset:offset + a.shape[0]].set(a.astype(F32))


def _layer_weights(l, norm_mix, w_in, conv_w, a_log, dt_bias, gdn_norm, q_a_norm, w_uq,
                   kv_a_norm, w_uk, w_uv, q_norm_nope, q_norm_rope, k_norm_nope, k_norm_rope,
                   w_ya, w_yb, w_o, norm_mlp, w_up, w_down):
    d = w_in.shape[1]
    off_z = QKV_WIDTH
    off_b = off_z + Z_WIDTH
    off_qd = off_b + 2 * GDN_HEADS
    off_kv = off_qd + Q_LORA
    off_g = off_kv + LATENT_WIDTH
    wi = w_in[l]
    g_mix = norm_mix[l].reshape(1, d)
    gw = {
        "g_mix": g_mix,
        "w_qkv": wi[:, :off_z].astype(BF16),
        "w_z": wi[:, off_z:off_b].astype(BF16),
        "w_ba": jnp.pad(wi[:, off_b:off_qd], ((0, 0), (0, LANES - 2 * GDN_HEADS))).astype(BF16),
        "conv_w": conv_w[l],
        "alog_row": _pad_lanes(a_log[l], offset=GDN_HEADS),
        "dt_row": _pad_lanes(dt_bias[l], offset=GDN_HEADS),
        "gdn_norm": gdn_norm[l].reshape(1, GDN_DV),
    }
    wq = w_uq[l].reshape(Q_LORA, MLA_HEADS, NOPE_DIM + ROPE_DIM)
    wq = jnp.pad(wq, ((0, 0), (0, 0), (0, QK_PAD - NOPE_DIM - ROPE_DIM)))
    mw = {
        "g_mix": g_mix,
        "w_qd": wi[:, off_qd:off_kv].astype(BF16),
        "w_c": wi[:, off_kv:off_kv + KV_LORA].astype(BF16),
        "w_kr": jnp.pad(wi[:, off_kv + KV_LORA:off_g], ((0, 0), (0, LANES - ROPE_DIM))).astype(BF16),
        "g_qa": q_a_norm[l].reshape(1, Q_LORA),
        "w_uq": wq.reshape(Q_LORA, MLA_HEADS * QK_PAD).astype(BF16),
        "g_kva": kv_a_norm[l].reshape(1, KV_LORA),
        "g_qn": q_norm_nope[l].reshape(1, NOPE_DIM),
        "g_qr": _pad_lanes(q_norm_rope[l]),
        "g_kr": _pad_lanes(k_norm_rope[l]),
        "g_kn": k_norm_nope[l].reshape(1, NOPE_DIM),
        "w_uk": w_uk[l].astype(BF16),
        "w_uk_t": w_uk[l].T.astype(BF16),
        "w_uv": w_uv[l].astype(BF16),
    }
    ew = {
        "g_mix": g_mix,
        "w_gates": wi[:, off_g:].astype(BF16),
        "w_ya": w_ya[l].astype(BF16),
        "w_yb": w_yb[l].astype(BF16),
        "w_o": w_o[l].astype(BF16),
        "g_mlp": norm_mlp[l].reshape(1, d),
        "w_up": w_up[l].astype(BF16),
        "w_down": w_down[l].astype(BF16),
    }
    return gw, mw, ew


def kernel(x_prompt, x_sample, cache_mla, state_gdn, state_conv, page_table, norm_mix, w_in,
           conv_w, a_log, dt_bias, gdn_norm, q_a_norm, w_uq, kv_a_norm, w_uk, w_uv,
           q_norm_nope, q_norm_rope, k_norm_nope, k_norm_rope, w_ya, w_yb, w_o, norm_mlp,
           w_up, w_down):
    depth = w_in.shape[0]
    bp, tp, d = x_prompt.shape
    bs, ts, _ = x_sample.shape
    assert ts == 1 and tp % GDN_CHUNK == 0
    past_len = page_table.shape[1] * cache_mla.shape[2]
    cos_p, sin_p = _rope_tables(jnp.arange(tp))
    cos_s, sin_s = _rope_tables(past_len + jnp.zeros((bs,), jnp.int32))

    x_p, x_s = x_prompt, x_sample.reshape(bs, d)
    rows_p, gdn_p, conv_p, rows_s, gdn_s, conv_s = [], [], [], [], [], []
    for l in range(depth):
        gw, mw, ew = _layer_weights(
            l, norm_mix, w_in, conv_w, a_log, dt_bias, gdn_norm, q_a_norm, w_uq, kv_a_norm,
            w_uk, w_uv, q_norm_nope, q_norm_rope, k_norm_nope, k_norm_rope, w_ya, w_yb, w_o,
            norm_mlp, w_up, w_down)
        gdn_consts = (gw["conv_w"], gw["alog_row"], gw["dt_row"], gw["gdn_norm"])

        xp2 = x_p.reshape(bp * tp, d)
        qkv, z, ba = _gdn_proj(xp2, gw["g_mix"], gw["w_qkv"], gw["w_z"], gw["w_ba"])
        ya_p, s_p, cs_p = _gdn_chunked(qkv.reshape(bp, tp, QKV_WIDTH), z.reshape(bp, tp, Z_WIDTH),
                                       ba.reshape(bp, tp, LANES), *gdn_consts)
        q_cat, k_cat, v_p, lat_p = _mla_pre(x_p, cos_p, sin_p, mw)
        yb_p = _mla_prompt_attention(q_cat, k_cat, v_p)
        y_p = _merge_mlp(xp2, ya_p.reshape(bp * tp, Z_WIDTH), yb_p.reshape(bp * tp, -1), ew)

        qkv, z, ba = _gdn_proj(x_s, gw["g_mix"], gw["w_qkv"], gw["w_z"], gw["w_ba"])
        ya_s, s_s, cs_s = _gdn_step(qkv, z, ba, state_conv[l], state_gdn[l], *gdn_consts)
        qa, qr, lat_s = _mla_pre_sample(x_s, cos_s, sin_s, mw)
        ctx = _mla_decode_attention(cache_mla, l, page_table, mw["w_uk_t"], qa, qr, lat_s)
        yb_s = _dec_post(ctx, mw["w_uv"])
        y_s = _merge_mlp(x_s, ya_s.reshape(bs, Z_WIDTH), yb_s, ew)

        x_p, x_s = y_p.reshape(bp, tp, d), y_s
        rows_p.append(lat_p)
        gdn_p.append(s_p)
        conv_p.append(cs_p)
        rows_s.append(lat_s.reshape(bs, 1, LATENT_WIDTH))
        gdn_s.append(s_s)
        conv_s.append(cs_s)
    return (x_p, x_s.reshape(bs, 1, d), jnp.stack(rows_p), jnp.stack(gdn_p), jnp.stack(conv_p),
            jnp.stack(rows_s), jnp.stack(gdn_s), jnp.stack(conv_s))
```

```python
import functools

import jax
import jax.numpy as jnp
from jax import lax
from jax.experimental import pallas as pl
from jax.experimental.pallas import tpu as pltpu

F32 = jnp.float32
BF16 = jnp.bfloat16

EPS = 1e-6
GDN_HEADS = 8
GDN_DK = 128
GDN_DV = 128
CONV_W = 4
GDN_CHUNK = 64
MLA_HEADS = 8
Q_LORA = 512
KV_LORA = 512
NOPE_DIM = 128
ROPE_DIM = 64
V_DIM = 128
ROPE_THETA = 10000.0
MLA_SCALE = (NOPE_DIM + ROPE_DIM) ** -0.5
LATENT_WIDTH = KV_LORA + ROPE_DIM
QKV_WIDTH = GDN_HEADS * (2 * GDN_DK + GDN_DV)
Z_WIDTH = GDN_HEADS * GDN_DV

LANES = 128
QK_PAD = 256
VMEM_LIMIT = 56 * 1024 * 1024
DEC_PAGES_PER_STEP = 16


def _dot(a, b):
    return jnp.dot(a, b, preferred_element_type=F32)


def _dot_nt(a, b):
    return lax.dot_general(a, b, (((1,), (1,)), ((), ())), preferred_element_type=F32)


def _dot_tn(a, b):
    return lax.dot_general(a, b, (((0,), (0,)), ((), ())), preferred_element_type=F32)


def _split2(x):
    hi = x.astype(BF16)
    lo = (x - hi.astype(F32)).astype(BF16)
    return hi, lo


def _split3(x):
    hi = x.astype(BF16)
    r = x - hi.astype(F32)
    mid = r.astype(BF16)
    lo = (r - mid.astype(F32)).astype(BF16)
    return hi, mid, lo


def _rms(x, g, n=None):
    n = x.shape[-1] if n is None else n
    ss = jnp.sum(x * x, axis=-1, keepdims=True) * (1.0 / n)
    return x * lax.rsqrt(ss + EPS) * g


def _sigmoid(x):
    return 1.0 / (1.0 + jnp.exp(-x))


def _silu(x):
    return x * _sigmoid(x)


def _softplus(x):
    return jnp.maximum(x, 0.0) + jnp.log1p(jnp.exp(-jnp.abs(x)))


def _rope(x, cos2, sin2):
    lane = lax.broadcasted_iota(jnp.int32, x.shape, 1)
    fwd = pltpu.roll(x, LANES - ROPE_DIM // 2, axis=1)
    bwd = pltpu.roll(x, ROPE_DIM // 2, axis=1)
    swapped = jnp.where(lane < ROPE_DIM // 2, fwd, bwd)
    return x * cos2 + swapped * sin2


def _full_spec(shape):
    nd = len(shape)
    return pl.BlockSpec(shape, lambda *_: (0,) * nd)


def _params(*sem):
    return pltpu.CompilerParams(dimension_semantics=sem, vmem_limit_bytes=VMEM_LIMIT)


def _gdn_proj_kernel(x_ref, g_ref, wqkv_ref, wz_ref, wba_ref, qkv_ref, z_ref, ba_ref):
    xn = _rms(x_ref[...], g_ref[...]).astype(BF16)
    qkv_ref[...] = _dot(xn, wqkv_ref[...])
    z_ref[...] = _dot(xn, wz_ref[...])
    ba_ref[...] = _dot(xn, wba_ref[...])


def _gdn_proj(x2, g, wqkv, wz, wba):
    m, d = x2.shape
    tm = min(256, m)
    return pl.pallas_call(
        _gdn_proj_kernel,
        grid=(m // tm,),
        in_specs=[pl.BlockSpec((tm, d), lambda i: (i, 0)),
                  _full_spec(g.shape), _full_spec(wqkv.shape), _full_spec(wz.shape),
                  _full_spec(wba.shape)],
        out_specs=[pl.BlockSpec((tm, QKV_WIDTH), lambda i: (i, 0)),
                   pl.BlockSpec((tm, Z_WIDTH), lambda i: (i, 0)),
                   pl.BlockSpec((tm, LANES), lambda i: (i, 0))],
        out_shape=[jax.ShapeDtypeStruct((m, QKV_WIDTH), F32),
                   jax.ShapeDtypeStruct((m, Z_WIDTH), F32),
                   jax.ShapeDtypeStruct((m, LANES), F32)],
        compiler_params=_params("parallel"),
        name="gdn_proj",
    )(x2, g, wqkv, wz, wba)


def _gate_rows(ba, alog_row, dt_row):
    lane = lax.broadcasted_iota(jnp.int32, ba.shape, 1)
    is_a = (lane >= GDN_HEADS) & (lane < 2 * GDN_HEADS)
    g = jnp.where(is_a, -jnp.exp(alog_row) * _softplus(ba + dt_row), 0.0)
    return _sigmoid(ba), g


def _gdn_chunk_kernel(qkv_ref, z_ref, ba_ref, cw_ref, alog_ref, dt_ref, gn_ref,
                      ya_ref, s_ref, cs_ref, xp_ref):
    c = pl.program_id(1)
    nc = pl.num_programs(1)
    C = GDN_CHUNK

    @pl.when(c == 0)
    def _():
        xp_ref[0:8, :] = jnp.zeros((8, QKV_WIDTH), F32)
        s_ref[...] = jnp.zeros(s_ref.shape, F32)

    blk = qkv_ref[0]
    xp_ref[8:8 + C, :] = blk
    cw = cw_ref[...]
    y = blk * cw[CONV_W - 1:CONV_W]
    for i in range(CONV_W - 1):
        y = y + xp_ref[8 - (CONV_W - 1) + i:8 - (CONV_W - 1) + i + C, :] * cw[i:i + 1]
    y = _silu(y)

    @pl.when(c == nc - 1)
    def _():
        cs_ref[0] = xp_ref[8 + C - (CONV_W - 1):8 + C, :]

    xp_ref[0:8, :] = xp_ref[C:C + 8, :]

    beta_full, g_full = _gate_rows(ba_ref[0], alog_ref[...], dt_ref[...])
    row = lax.broadcasted_iota(jnp.int32, (C, C), 0)
    col = lax.broadcasted_iota(jnp.int32, (C, C), 1)
    incl = row >= col
    strict = row > col
    tril = jnp.where(incl, 1.0, 0.0).astype(BF16)
    g_col = sum(_dot(tril, part) for part in _split3(g_full))
    r128 = lax.broadcasted_iota(jnp.int32, (LANES, LANES), 0)
    c128 = lax.broadcasted_iota(jnp.int32, (LANES, LANES), 1)
    eye = jnp.where(r128 == c128, 1.0, 0.0).astype(BF16)
    g_rows = sum(_dot_nt(eye, part) for part in _split3(g_col))

    z = z_ref[0]
    gn = gn_ref[...]
    heads = range(GDN_HEADS)
    q, k, gc, eg, g_last, kb, rhs_bf, rhs, decay, kq = ([None] * GDN_HEADS for _ in range(10))
    for h in heads:
        qh = y[:, h * GDN_DK:(h + 1) * GDN_DK]
        kh = y[:, GDN_HEADS * GDN_DK + h * GDN_DK:GDN_HEADS * GDN_DK + (h + 1) * GDN_DK]
        vh = y[:, 2 * GDN_HEADS * GDN_DK + h * GDN_DV:2 * GDN_HEADS * GDN_DK + (h + 1) * GDN_DV]
        q[h] = qh * lax.rsqrt(jnp.sum(qh * qh, axis=-1, keepdims=True) + EPS) * (GDN_DK ** -0.5)
        k[h] = kh * lax.rsqrt(jnp.sum(kh * kh, axis=-1, keepdims=True) + EPS)
        beta = beta_full[:, h:h + 1]
        gc[h] = g_col[:, GDN_HEADS + h:GDN_HEADS + h + 1]
        gr = g_rows[GDN_HEADS + h:GDN_HEADS + h + 1, :]
        decay[h] = jnp.where(incl, jnp.exp(jnp.where(incl, gc[h] - gr, 0.0)), 0.0)
        eg[h] = jnp.exp(gc[h])
        g_last[h] = gc[h][C - 1:C, :]
        kb[h] = k[h] * beta
        rhs[h] = jnp.concatenate([vh * beta, kb[h] * eg[h]], axis=1)
        rhs_bf[h] = rhs[h].astype(BF16)
    for h in heads:
        kq[h] = _dot_nt(jnp.concatenate([kb[h], q[h]], axis=0).astype(BF16),
                        k[h].astype(BF16))
    pw = [jnp.where(strict, kq[h][:C] * decay[h], 0.0) for h in heads]
    aqk = [(kq[h][C:] * decay[h]).astype(BF16) for h in heads]
    n_acc = [-pw[h] for h in heads]
    for _ in range(5):
        pw_bf = [pw[h].astype(BF16) for h in heads]
        pw = [_dot(pw_bf[h], pw_bf[h]) for h in heads]
        n_acc = [n_acc[h] + pw[h] + _dot(n_acc[h].astype(BF16), pw[h].astype(BF16))
                 for h in heads]
    sol = [None] * GDN_HEADS
    for h in heads:
        n_hi, n_lo = _split2(n_acc[h])
        sol[h] = rhs[h] + _dot(n_hi, rhs_bf[h]) + _dot(n_lo, rhs_bf[h])
    s_old = [s_ref[0, h] for h in heads]
    ws = [_dot(jnp.concatenate([sol[h][:, GDN_DV:], q[h] * eg[h]], axis=0).astype(BF16),
               s_old[h].astype(BF16)) for h in heads]
    u_bf = [(sol[h][:, :GDN_DV] - ws[h][:C]).astype(BF16) for h in heads]
    o = [ws[h][C:] + _dot(aqk[h], u_bf[h]) for h in heads]
    for h in heads:
        kd = (k[h] * jnp.exp(g_last[h] - gc[h])).astype(BF16)
        s_ref[0, h] = jnp.exp(g_last[h]) * s_old[h] + _dot_tn(kd, u_bf[h])
    for h in heads:
        zh = z[:, h * GDN_DV:(h + 1) * GDN_DV]
        ya_ref[0, :, h * GDN_DV:(h + 1) * GDN_DV] = (_rms(o[h], gn) * _silu(zh)).astype(BF16)


def _gdn_chunked(qkv, z, ba, conv_w, alog_row, dt_row, gdn_norm):
    b, t, _ = qkv.shape
    nc = t // GDN_CHUNK
    C = GDN_CHUNK
    return pl.pallas_call(
        _gdn_chunk_kernel,
        grid=(b, nc),
        in_specs=[pl.BlockSpec((1, C, QKV_WIDTH), lambda i, j: (i, j, 0)),
                  pl.BlockSpec((1, C, Z_WIDTH), lambda i, j: (i, j, 0)),
                  pl.BlockSpec((1, C, LANES), lambda i, j: (i, j, 0)),
                  _full_spec(conv_w.shape), _full_spec(alog_row.shape),
                  _full_spec(dt_row.shape), _full_spec(gdn_norm.shape)],
        out_specs=[pl.BlockSpec((1, C, Z_WIDTH), lambda i, j: (i, j, 0)),
                   pl.BlockSpec((1, GDN_HEADS, GDN_DK, GDN_DV), lambda i, j: (i, 0, 0, 0)),
                   pl.BlockSpec((1, CONV_W - 1, QKV_WIDTH), lambda i, j: (i, 0, 0))],
        out_shape=[jax.ShapeDtypeStruct((b, t, Z_WIDTH), BF16),
                   jax.ShapeDtypeStruct((b, GDN_HEADS, GDN_DK, GDN_DV), F32),
                   jax.ShapeDtypeStruct((b, CONV_W - 1, QKV_WIDTH), F32)],
        scratch_shapes=[pltpu.VMEM((8 + C, QKV_WIDTH), F32)],
        compiler_params=_params("parallel", "arbitrary"),
        name="gdn_chunked",
    )(qkv, z, ba, conv_w, alog_row, dt_row, gdn_norm)


def _gdn_step_kernel(qkv_ref, z_ref, ba_ref, cs_ref, s_ref, cw_ref, alog_ref, dt_ref, gn_ref,
                     ya_ref, snew_ref, csnew_ref):
    rowv = qkv_ref[0]
    cs = cs_ref[0]
    cw = cw_ref[...]
    y = rowv * cw[CONV_W - 1:CONV_W]
    for i in range(CONV_W - 1):
        y = y + cs[i:i + 1] * cw[i:i + 1]
    y = _silu(y)
    csnew_ref[0, 0:CONV_W - 2, :] = cs[1:CONV_W - 1]
    csnew_ref[0, CONV_W - 2:CONV_W - 1, :] = rowv

    beta_full, g_full = _gate_rows(ba_ref[0], alog_ref[...], dt_ref[...])
    r128 = lax.broadcasted_iota(jnp.int32, (GDN_DK, GDN_DK), 0)
    c128 = lax.broadcasted_iota(jnp.int32, (GDN_DK, GDN_DK), 1)
    eye = r128 == c128
    z = z_ref[0]
    gn = gn_ref[...]

    def to_col(r):
        return jnp.sum(jnp.where(eye, jnp.broadcast_to(r, (GDN_DK, GDN_DK)), 0.0),
                       axis=1, keepdims=True)

    for h in range(GDN_HEADS):
        q = y[:, h * GDN_DK:(h + 1) * GDN_DK]
        k = y[:, GDN_HEADS * GDN_DK + h * GDN_DK:GDN_HEADS * GDN_DK + (h + 1) * GDN_DK]
        v = y[:, 2 * GDN_HEADS * GDN_DK + h * GDN_DV:2 * GDN_HEADS * GDN_DK + (h + 1) * GDN_DV]
        q = q * lax.rsqrt(jnp.sum(q * q, axis=-1, keepdims=True) + EPS) * (GDN_DK ** -0.5)
        k = k * lax.rsqrt(jnp.sum(k * k, axis=-1, keepdims=True) + EPS)
        beta = beta_full[:, h:h + 1]
        gdec = jnp.exp(g_full[:, GDN_HEADS + h:GDN_HEADS + h + 1])
        k_col = to_col(k)
        q_col = to_col(q)
        s = gdec * s_ref[0, h]
        u = beta * (v - jnp.sum(s * k_col, axis=0, keepdims=True))
        s = s + k_col * u
        snew_ref[0, h] = s
        o = jnp.sum(s * q_col, axis=0, keepdims=True)
        zh = z[:, h * GDN_DV:(h + 1) * GDN_DV]
        ya_ref[0, :, h * GDN_DV:(h + 1) * GDN_DV] = _rms(o, gn) * _silu(zh)


def _gdn_step(qkv, z, ba, conv_state, state, conv_w, alog_row, dt_row, gdn_norm):
    b = qkv.shape[0]
    qkv3 = qkv.reshape(b, 1, QKV_WIDTH)
    z3 = z.reshape(b, 1, Z_WIDTH)
    ba3 = ba.reshape(b, 1, LANES)
    return pl.pallas_call(
        _gdn_step_kernel,
        grid=(b,),
        in_specs=[pl.BlockSpec((1, 1, QKV_WIDTH), lambda i: (i, 0, 0)),
                  pl.BlockSpec((1, 1, Z_WIDTH), lambda i: (i, 0, 0)),
                  pl.BlockSpec((1, 1, LANES), lambda i: (i, 0, 0)),
                  pl.BlockSpec((1, CONV_W - 1, QKV_WIDTH), lambda i: (i, 0, 0)),
                  pl.BlockSpec((1, GDN_HEADS, GDN_DK, GDN_DV), lambda i: (i, 0, 0, 0)),
                  _full_spec(conv_w.shape), _full_spec(alog_row.shape),
                  _full_spec(dt_row.shape), _full_spec(gdn_norm.shape)],
        out_specs=[pl.BlockSpec((1, 1, Z_WIDTH), lambda i: (i, 0, 0)),
                   pl.BlockSpec((1, GDN_HEADS, GDN_DK, GDN_DV), lambda i: (i, 0, 0, 0)),
                   pl.BlockSpec((1, CONV_W - 1, QKV_WIDTH), lambda i: (i, 0, 0))],
        out_shape=[jax.ShapeDtypeStruct((b, 1, Z_WIDTH), F32),
                   jax.ShapeDtypeStruct((b, GDN_HEADS, GDN_DK, GDN_DV), F32),
                   jax.ShapeDtypeStruct((b, CONV_W - 1, QKV_WIDTH), F32)],
        compiler_params=_params("parallel"),
        name="gdn_step",
    )(qkv3, z3, ba3, conv_state, state, conv_w, alog_row, dt_row, gdn_norm)


def _mla_queries_latent(x, cos2, sin2, g_mix, w_qd, w_c, w_kr, g_qa, w_uq, g_kva,
                        g_qn, g_qr, g_kr):
    xn = _rms(x, g_mix).astype(BF16)
    qa = _rms(_dot(xn, w_qd), g_qa).astype(BF16)
    q = _dot(qa, w_uq)
    qn, qr = [], []
    for h in range(MLA_HEADS):
        qn.append(_rms(q[:, h * QK_PAD:h * QK_PAD + NOPE_DIM], g_qn))
        qr.append(_rope(_rms(q[:, h * QK_PAD + NOPE_DIM:(h + 1) * QK_PAD], g_qr, ROPE_DIM),
                        cos2, sin2))
    c = _rms(_dot(xn, w_c), g_kva)
    kr = _rope(_rms(_dot(xn, w_kr), g_kr, ROPE_DIM), cos2, sin2)
    return qn, qr, c, kr


def _mla_pre_kernel(x_ref, cos_ref, sin_ref, gmix_ref, wqd_ref, wc_ref, wkr_ref, gqa_ref,
                    wuq_ref, gkva_ref, gqn_ref, gqr_ref, gkr_ref, wuk_ref, wuv_ref, gkn_ref,
                    q_ref, k_ref, v_ref, lat_ref):
    qn, qr, c, kr = _mla_queries_latent(
        x_ref[0], cos_ref[...], sin_ref[...], gmix_ref[...], wqd_ref[...], wc_ref[...],
        wkr_ref[...], gqa_ref[...], wuq_ref[...], gkva_ref[...], gqn_ref[...], gqr_ref[...],
        gkr_ref[...])
    lat_ref[0, :, 0:KV_LORA] = c
    lat_ref[0, :, KV_LORA:LATENT_WIDTH] = kr[:, 0:ROPE_DIM]
    c_bf = c.astype(BF16)
    kr_bf = kr.astype(BF16)
    kfull = _dot(c_bf, wuk_ref[...])
    v_ref[0] = _dot(c_bf, wuv_ref[...]).astype(BF16)
    gkn = gkn_ref[...]
    for h in range(MLA_HEADS):
        q_ref[0, :, h * QK_PAD:h * QK_PAD + NOPE_DIM] = qn[h].astype(BF16)
        q_ref[0, :, h * QK_PAD + NOPE_DIM:(h + 1) * QK_PAD] = qr[h].astype(BF16)
        kn = _rms(kfull[:, h * NOPE_DIM:(h + 1) * NOPE_DIM], gkn)
        k_ref[0, :, h * QK_PAD:h * QK_PAD + NOPE_DIM] = kn.astype(BF16)
        k_ref[0, :, h * QK_PAD + NOPE_DIM:(h + 1) * QK_PAD] = kr_bf


def _mla_pre(x, cos2, sin2, mw):
    b, t, d = x.shape
    tm = min(256, t)
    consts = [mw["g_mix"], mw["w_qd"], mw["w_c"], mw["w_kr"], mw["g_qa"], mw["w_uq"],
              mw["g_kva"], mw["g_qn"], mw["g_qr"], mw["g_kr"], mw["w_uk"], mw["w_uv"],
              mw["g_kn"]]
    return pl.pallas_call(
        _mla_pre_kernel,
        grid=(b, t // tm),
        in_specs=[pl.BlockSpec((1, tm, d), lambda i, j: (i, j, 0)),
                  pl.BlockSpec((tm, LANES), lambda i, j: (j, 0)),
                  pl.BlockSpec((tm, LANES), lambda i, j: (j, 0))]
                 + [_full_spec(a.shape) for a in consts],
        out_specs=[pl.BlockSpec((1, tm, MLA_HEADS * QK_PAD), lambda i, j: (i, j, 0)),
                   pl.BlockSpec((1, tm, MLA_HEADS * QK_PAD), lambda i, j: (i, j, 0)),
                   pl.BlockSpec((1, tm, MLA_HEADS * V_DIM), lambda i, j: (i, j, 0)),
                   pl.BlockSpec((1, tm, LATENT_WIDTH), lambda i, j: (i, j, 0))],
        out_shape=[jax.ShapeDtypeStruct((b, t, MLA_HEADS * QK_PAD), BF16),
                   jax.ShapeDtypeStruct((b, t, MLA_HEADS * QK_PAD), BF16),
                   jax.ShapeDtypeStruct((b, t, MLA_HEADS * V_DIM), BF16),
                   jax.ShapeDtypeStruct((b, t, LATENT_WIDTH), F32)],
        compiler_params=_params("parallel", "parallel"),
        name="mla_pre",
    )(x, cos2, sin2, *consts)


def _mla_pre_sample_kernel(x_ref, cos_ref, sin_ref, gmix_ref, wqd_ref, wc_ref, wkr_ref, gqa_ref,
                           wuq_ref, gkva_ref, gqn_ref, gqr_ref, gkr_ref, wuk_ref, gkn_ref,
                           qa_ref, qr_ref, lat_ref):
    qn, qr, c, kr = _mla_queries_latent(
        x_ref[...], cos_ref[...], sin_ref[...], gmix_ref[...], wqd_ref[...], wc_ref[...],
        wkr_ref[...], gqa_ref[...], wuq_ref[...], gkva_ref[...], gqn_ref[...], gqr_ref[...],
        gkr_ref[...])
    lat_ref[:, 0:KV_LORA] = c
    lat_ref[:, KV_LORA:LATENT_WIDTH] = kr[:, 0:ROPE_DIM]
    gkn = gkn_ref[...]
    for h in range(MLA_HEADS):
        hi, lo = _split2(qn[h] * gkn)
        w_h = wuk_ref[:, h * NOPE_DIM:(h + 1) * NOPE_DIM]
        qa_ref[h] = _dot_nt(hi, w_h) + _dot_nt(lo, w_h)
        qr_ref[h] = qr[h]


def _mla_pre_sample(x2, cos2, sin2, mw):
    m = x2.shape[0]
    consts = [mw["g_mix"], mw["w_qd"], mw["w_c"], mw["w_kr"], mw["g_qa"], mw["w_uq"],
              mw["g_kva"], mw["g_qn"], mw["g_qr"], mw["g_kr"], mw["w_uk"], mw["g_kn"]]
    args = [x2, cos2, sin2] + consts
    return pl.pallas_call(
        _mla_pre_sample_kernel,
        grid=(1,),
        in_specs=[_full_spec(a.shape) for a in args],
        out_specs=[_full_spec((MLA_HEADS, m, KV_LORA)), _full_spec((MLA_HEADS, m, LANES)),
                   _full_spec((m, LATENT_WIDTH))],
        out_shape=[jax.ShapeDtypeStruct((MLA_HEADS, m, KV_LORA), F32),
                   jax.ShapeDtypeStruct((MLA_HEADS, m, LANES), F32),
                   jax.ShapeDtypeStruct((m, LATENT_WIDTH), F32)],
        compiler_params=_params("arbitrary"),
        name="mla_pre_sample",
    )(*args)


ATTN_HEADS_PER_STEP = 4


def _attn_kernel(q_ref, k_ref, v_ref, o_ref, *, tq):
    qi = pl.program_id(2)
    hs = range(ATTN_HEADS_PER_STEP)
    q = [q_ref[0, :, g * QK_PAD:(g + 1) * QK_PAD] for g in hs]
    row = lax.broadcasted_iota(jnp.int32, (tq, tq), 0)
    col = lax.broadcasted_iota(jnp.int32, (tq, tq), 1)

    def step(j, carry, diagonal):
        start = pl.multiple_of(j * tq, tq)
        k = [k_ref[0, pl.ds(start, tq), g * QK_PAD:(g + 1) * QK_PAD] for g in hs]
        v = [v_ref[0, pl.ds(start, tq), g * V_DIM:(g + 1) * V_DIM] for g in hs]
        s = [_dot_nt(q[g], k[g]) * MLA_SCALE for g in hs]
        if diagonal:
            s = [jnp.where(col <= row, s[g], -jnp.inf) for g in hs]
        m_new = [jnp.maximum(carry[g][0], jnp.max(s[g], axis=-1, keepdims=True)) for g in hs]
        p = [jnp.exp(s[g] - m_new[g]) for g in hs]
        pv = [_dot(p[g].astype(BF16), v[g]) for g in hs]
        out = []
        for g in hs:
            m, l, acc = carry[g]
            alpha = jnp.exp(m - m_new[g])
            out.append((m_new[g], alpha * l + jnp.sum(p[g], axis=-1, keepdims=True),
                        alpha * acc + pv[g]))
        return tuple(out)

    init = tuple((jnp.full((tq, 1), -jnp.inf, F32), jnp.zeros((tq, 1), F32),
                  jnp.zeros((tq, V_DIM), F32)) for _ in hs)
    carry = lax.fori_loop(0, qi, lambda j, c: step(j, c, False), init)
    carry = step(qi, carry, True)
    for g in hs:
        _, l, acc = carry[g]
        o_ref[0, :, g * V_DIM:(g + 1) * V_DIM] = (acc / l).astype(BF16)


def _mla_prompt_attention(q_cat, k_cat, v):
    b, t, _ = q_cat.shape
    tq = min(256, t)
    hps = ATTN_HEADS_PER_STEP
    return pl.pallas_call(
        functools.partial(_attn_kernel, tq=tq),
        grid=(b, MLA_HEADS // hps, t // tq),
        in_specs=[pl.BlockSpec((1, tq, hps * QK_PAD), lambda i, h, j: (i, j, h)),
                  pl.BlockSpec((1, t, hps * QK_PAD), lambda i, h, j: (i, 0, h)),
                  pl.BlockSpec((1, t, hps * V_DIM), lambda i, h, j: (i, 0, h))],
        out_specs=pl.BlockSpec((1, tq, hps * V_DIM), lambda i, h, j: (i, j, h)),
        out_shape=jax.ShapeDtypeStruct((b, t, MLA_HEADS * V_DIM), BF16),
        compiler_params=_params("parallel", "parallel", "arbitrary"),
        name="mla_prompt_attention",
    )(q_cat, k_cat, v)


def _dec_attn_kernel(pt_ref, *refs, pps):
    del pt_ref
    page_refs = refs[:pps]
    (wukt_ref, qa_ref, qr_ref, lnew_ref, ctx_ref,
     lhs_ref, qr16_ref, m_ref, l_ref, acc_ref) = refs[pps:]
    b = pl.program_id(0)
    j = pl.program_id(1)
    nj = pl.num_programs(1)
    n_k = MLA_HEADS * NOPE_DIM

    @pl.when((b == 0) & (j == 0))
    def _():
        lhs_ref[0:n_k, :] = wukt_ref[...]

    @pl.when(j == 0)
    def _():
        qa = jnp.concatenate([qa_ref[h, 0] for h in range(MLA_HEADS)]
                             + [jnp.zeros((8, KV_LORA), F32)], axis=0)
        lhs_ref[n_k:n_k + 16, :] = qa.astype(BF16)
        qr = jnp.concatenate([qr_ref[h, 0] for h in range(MLA_HEADS)]
                             + [jnp.zeros((8, LANES), F32)], axis=0)
        qr16_ref[...] = qr.astype(BF16)
        m_ref[...] = jnp.full(m_ref.shape, -jnp.inf, F32)
        l_ref[...] = jnp.zeros(l_ref.shape, F32)
        acc_ref[...] = jnp.zeros(acc_ref.shape, F32)

    def scores(res, sr):
        kt = res[0:n_k]
        ss = jnp.sum((kt * kt).reshape(MLA_HEADS, NOPE_DIM, res.shape[1]), axis=1)
        sn = res[n_k:n_k + MLA_HEADS]
        return (sn * lax.rsqrt(ss * (1.0 / NOPE_DIM) + EPS) + sr[0:MLA_HEADS]) * MLA_SCALE

    c_bf, s_parts = [], []
    for i in range(0, pps, 2):
        tile = jnp.concatenate([page_refs[i][...], page_refs[i + 1][...]], axis=1)
        c_bf.append(tile[0:KV_LORA].astype(BF16))
        kr_bf = tile[KV_LORA:LATENT_WIDTH].astype(BF16)
        s_parts.append(scores(_dot(lhs_ref[...], c_bf[-1]),
                              _dot(qr16_ref[:, 0:ROPE_DIM], kr_bf)))
    s = jnp.concatenate(s_parts, axis=1)
    m_old = m_ref[...]
    m_new = jnp.maximum(m_old, jnp.max(s, axis=-1, keepdims=True))
    alpha = jnp.exp(m_old - m_new)
    p = jnp.exp(s - m_new)
    l_ref[...] = alpha * l_ref[...] + jnp.sum(p, axis=-1, keepdims=True)
    p_bf = p.astype(BF16)
    n_tok = c_bf[0].shape[1]
    acc = alpha * acc_ref[...]
    for t, c_t in enumerate(c_bf):
        acc = acc + _dot_nt(p_bf[:, t * n_tok:(t + 1) * n_tok], c_t)
    acc_ref[...] = acc
    m_ref[...] = m_new

    @pl.when(j == nj - 1)
    def _():
        ln = lnew_ref[0]
        c_new = ln[:, 0:KV_LORA]
        c8 = jnp.broadcast_to(c_new, (8, KV_LORA)).astype(BF16)
        kr8 = jnp.broadcast_to(ln[:, KV_LORA:LATENT_WIDTH], (8, ROPE_DIM)).astype(BF16)
        s_new = scores(_dot_nt(lhs_ref[...], c8),
                       _dot_nt(qr16_ref[:, 0:ROPE_DIM], kr8))[:, 0:1]
        m_old = m_ref[...]
        m_new = jnp.maximum(m_old, s_new)
        alpha = jnp.exp(m_old - m_new)
        p_new = jnp.exp(s_new - m_new)
        ctx = (alpha * acc_ref[...] + p_new * c_new) / (alpha * l_ref[...] + p_new)
        for h in range(MLA_HEADS):
            ctx_ref[h, 0] = ctx[h:h + 1]


def _mla_decode_attention(cache, layer, page_table, wuk_t, qa, qr, lat_new):
    bsz, n_pages = page_table.shape
    page = cache.shape[3]
    pps = DEC_PAGES_PER_STEP
    while n_pages % pps:
        pps //= 2
    assert pps >= 2
    qa4 = qa.reshape(MLA_HEADS, bsz, 1, KV_LORA)
    qr4 = qr.reshape(MLA_HEADS, bsz, 1, LANES)
    ln3 = lat_new.reshape(bsz, 1, LATENT_WIDTH)

    def page_spec(i):
        return pl.BlockSpec((pl.Squeezed(), pl.Squeezed(), LATENT_WIDTH, page),
                            lambda b, j, pt: (layer, pt[b, j * pps + i], 0, 0))

    grid_spec = pltpu.PrefetchScalarGridSpec(
        num_scalar_prefetch=1,
        grid=(bsz, n_pages // pps),
        in_specs=[page_spec(i) for i in range(pps)] + [
            pl.BlockSpec(wuk_t.shape, lambda b, j, pt: (0, 0)),
            pl.BlockSpec((MLA_HEADS, 1, 1, KV_LORA), lambda b, j, pt: (0, b, 0, 0)),
            pl.BlockSpec((MLA_HEADS, 1, 1, LANES), lambda b, j, pt: (0, b, 0, 0)),
            pl.BlockSpec((1, 1, LATENT_WIDTH), lambda b, j, pt: (b, 0, 0))],
        out_specs=pl.BlockSpec((MLA_HEADS, 1, 1, KV_LORA), lambda b, j, pt: (0, b, 0, 0)),
        scratch_shapes=[pltpu.VMEM((MLA_HEADS * NOPE_DIM + 16, KV_LORA), BF16),
                        pltpu.VMEM((16, LANES), BF16),
                        pltpu.VMEM((MLA_HEADS, 1), F32),
                        pltpu.VMEM((MLA_HEADS, 1), F32),
                        pltpu.VMEM((MLA_HEADS, KV_LORA), F32)])
    ctx = pl.pallas_call(
        functools.partial(_dec_attn_kernel, pps=pps),
        grid_spec=grid_spec,
        out_shape=jax.ShapeDtypeStruct((MLA_HEADS, bsz, 1, KV_LORA), F32),
        compiler_params=_params("arbitrary", "arbitrary"),
        name="mla_decode_attention",
    )(page_table, *([cache] * pps), wuk_t, qa4, qr4, ln3)
    return ctx.reshape(MLA_HEADS, bsz, KV_LORA)


def _dec_post_kernel(ctx_ref, wuv_ref, y_ref):
    for h in range(MLA_HEADS):
        hi, lo = _split2(ctx_ref[h])
        w_h = wuv_ref[:, h * V_DIM:(h + 1) * V_DIM]
        y_ref[:, h * V_DIM:(h + 1) * V_DIM] = (_dot(hi, w_h) + _dot(lo, w_h)).astype(BF16)


def _dec_post(ctx, w_uv):
    bsz = ctx.shape[1]
    return pl.pallas_call(
        _dec_post_kernel,
        grid=(1,),
        in_specs=[_full_spec(ctx.shape), _full_spec(w_uv.shape)],
        out_specs=_full_spec((bsz, MLA_HEADS * V_DIM)),
        out_shape=jax.ShapeDtypeStruct((bsz, MLA_HEADS * V_DIM), BF16),
        compiler_params=_params("arbitrary"),
        name="mla_decode_values",
    )(ctx, w_uv)


def _merge_mlp_kernel(x_ref, ya_ref, yb_ref, gmix_ref, wg_ref, wya_ref, wyb_ref, wo_ref,
                      gmlp_ref, wup_ref, wdn_ref, y_ref):
    x = x_ref[...]
    d = x.shape[-1]
    xn = _rms(x, gmix_ref[...]).astype(BF16)
    gates = _sigmoid(_dot(xn, wg_ref[...]))
    mix = (gates[:, :d] * _dot(ya_ref[...].astype(BF16), wya_ref[...])
           + gates[:, d:] * _dot(yb_ref[...].astype(BF16), wyb_ref[...]))
    x1 = x + _dot(mix.astype(BF16), wo_ref[...])
    hmid = jnp.maximum(_dot(_rms(x1, gmlp_ref[...]).astype(BF16), wup_ref[...]), 0.0)
    y_ref[...] = x1 + _dot((hmid * hmid).astype(BF16), wdn_ref[...])


def _merge_mlp(x2, ya, yb, ew):
    m, d = x2.shape
    tm = min(256, m)
    consts = [ew["g_mix"], ew["w_gates"], ew["w_ya"], ew["w_yb"], ew["w_o"], ew["g_mlp"],
              ew["w_up"], ew["w_down"]]

    def const_spec(a):
        nd = a.ndim
        return pl.BlockSpec(a.shape, lambda i: (0,) * nd, pipeline_mode=pl.Buffered(1))

    return pl.pallas_call(
        _merge_mlp_kernel,
        grid=(m // tm,),
        in_specs=[pl.BlockSpec((tm, d), lambda i: (i, 0)),
                  pl.BlockSpec((tm, ya.shape[1]), lambda i: (i, 0)),
                  pl.BlockSpec((tm, yb.shape[1]), lambda i: (i, 0))]
                 + [const_spec(a) for a in consts],
        out_specs=pl.BlockSpec((tm, d), lambda i: (i, 0)),
        out_shape=jax.ShapeDtypeStruct((m, d), F32),
        compiler_params=_params("parallel"),
        name="merge_mlp",
    )(x2, ya, yb, *consts)


def _rope_tables(pos):
    inv = jnp.power(ROPE_THETA, -jnp.arange(0, ROPE_DIM, 2, dtype=F32) / ROPE_DIM)
    ang = pos.astype(F32)[:, None] * inv[None, :]
    cos, sin = jnp.cos(ang), jnp.sin(ang)
    zeros = jnp.zeros((pos.shape[0], LANES - ROPE_DIM), F32)
    return (jnp.concatenate([cos, cos, zeros], axis=1),
            jnp.concatenate([-sin, sin, zeros], axis=1))


def _pad_lanes(a, width=LANES, offset=0):
    return jnp.zeros((1, width), F32).at[0, offset:offset + a.shape[0]].set(a.astype(F32))


def _layer_weights(l, norm_mix, w_in, conv_w, a_log, dt_bias, gdn_norm, q_a_norm, w_uq,
                   kv_a_norm, w_uk, w_uv, q_norm_nope, q_norm_rope, k_norm_nope, k_norm_rope,
                   w_ya, w_yb, w_o, norm_mlp, w_up, w_down):
    d = w_in.shape[1]
    off_z = QKV_WIDTH
    off_b = off_z + Z_WIDTH
    off_qd = off_b + 2 * GDN_HEADS
    off_kv = off_qd + Q_LORA
    off_g = off_kv + LATENT_WIDTH
    wi = w_in[l]
    g_mix = norm_mix[l].reshape(1, d)
    gw = {
        "g_mix": g_mix,
        "w_qkv": wi[:, :off_z].astype(BF16),
        "w_z": wi[:, off_z:off_b].astype(BF16),
        "w_ba": jnp.pad(wi[:, off_b:off_qd], ((0, 0), (0, LANES - 2 * GDN_HEADS))).astype(BF16),
        "conv_w": conv_w[l],
        "alog_row": _pad_lanes(a_log[l], offset=GDN_HEADS),
        "dt_row": _pad_lanes(dt_bias[l], offset=GDN_HEADS),
        "gdn_norm": gdn_norm[l].reshape(1, GDN_DV),
    }
    wq = w_uq[l].reshape(Q_LORA, MLA_HEADS, NOPE_DIM + ROPE_DIM)
    wq = jnp.pad(wq, ((0, 0), (0, 0), (0, QK_PAD - NOPE_DIM - ROPE_DIM)))
    mw = {
        "g_mix": g_mix,
        "w_qd": wi[:, off_qd:off_kv].astype(BF16),
        "w_c": wi[:, off_kv:off_kv + KV_LORA].astype(BF16),
        "w_kr": jnp.pad(wi[:, off_kv + KV_LORA:off_g], ((0, 0), (0, LANES - ROPE_DIM))).astype(BF16),
        "g_qa": q_a_norm[l].reshape(1, Q_LORA),
        "w_uq": wq.reshape(Q_LORA, MLA_HEADS * QK_PAD).astype(BF16),
        "g_kva": kv_a_norm[l].reshape(1, KV_LORA),
        "g_qn": q_norm_nope[l].reshape(1, NOPE_DIM),
        "g_qr": _pad_lanes(q_norm_rope[l]),
        "g_kr": _pad_lanes(k_norm_rope[l]),
        "g_kn": k_norm_nope[l].reshape(1, NOPE_DIM),
        "w_uk": w_uk[l].astype(BF16),
        "w_uk_t": w_uk[l].T.astype(BF16),
        "w_uv": w_uv[l].astype(BF16),
    }
    ew = {
        "g_mix": g_mix,
        "w_gates": wi[:, off_g:].astype(BF16),
        "w_ya": w_ya[l].astype(BF16),
        "w_yb": w_yb[l].astype(BF16),
        "w_o": w_o[l].astype(BF16),
        "g_mlp": norm_mlp[l].reshape(1, d),
        "w_up": w_up[l].astype(BF16),
        "w_down": w_down[l].astype(BF16),
    }
    return gw, mw, ew


def kernel(x_prompt, x_sample, cache_mla, state_gdn, state_conv, page_table, norm_mix, w_in,
           conv_w, a_log, dt_bias, gdn_norm, q_a_norm, w_uq, kv_a_norm, w_uk, w_uv,
           q_norm_nope, q_norm_rope, k_norm_nope, k_norm_rope, w_ya, w_yb, w_o, norm_mlp,
           w_up, w_down):
    depth = w_in.shape[0]
    bp, tp, d = x_prompt.shape
    bs, ts, _ = x_sample.shape
    assert ts == 1 and tp % GDN_CHUNK == 0
    past_len = page_table.shape[1] * cache_mla.shape[2]
    cache_t = jnp.swapaxes(cache_mla, 2, 3)
    cos_p, sin_p = _rope_tables(jnp.arange(tp))
    cos_s, sin_s = _rope_tables(past_len + jnp.zeros((bs,), jnp.int32))

    x_p, x_s = x_prompt, x_sample.reshape(bs, d)
    rows_p, gdn_p, conv_p, rows_s, gdn_s, conv_s = [], [], [], [], [], []
    for l in range(depth):
        gw, mw, ew = _layer_weights(
            l, norm_mix, w_in, conv_w, a_log, dt_bias, gdn_norm, q_a_norm, w_uq, kv_a_norm,
            w_uk, w_uv, q_norm_nope, q_norm_rope, k_norm_nope, k_norm_rope, w_ya, w_yb, w_o,
            norm_mlp, w_up, w_down)
        gdn_consts = (gw["conv_w"], gw["alog_row"], gw["dt_row"], gw["gdn_norm"])

        xp2 = x_p.reshape(bp * tp, d)
        qkv, z, ba = _gdn_proj(xp2, gw["g_mix"], gw["w_qkv"], gw["w_z"], gw["w_ba"])
        ya_p, s_p, cs_p = _gdn_chunked(qkv.reshape(bp, tp, QKV_WIDTH), z.reshape(bp, tp, Z_WIDTH),
                                       ba.reshape(bp, tp, LANES), *gdn_consts)
        q_cat, k_cat, v_p, lat_p = _mla_pre(x_p, cos_p, sin_p, mw)
        yb_p = _mla_prompt_attention(q_cat, k_cat, v_p)
        y_p = _merge_mlp(xp2, ya_p.reshape(bp * tp, Z_WIDTH), yb_p.reshape(bp * tp, -1), ew)

        qkv, z, ba = _gdn_proj(x_s, gw["g_mix"], gw["w_qkv"], gw["w_z"], gw["w_ba"])
        ya_s, s_s, cs_s = _gdn_step(qkv, z, ba, state_conv[l], state_gdn[l], *gdn_consts)
        qa, qr, lat_s = _mla_pre_sample(x_s, cos_s, sin_s, mw)
        ctx = _mla_decode_attention(cache_t, l, page_table, mw["w_uk_t"], qa, qr, lat_s)
        yb_s = _dec_post(ctx, mw["w_uv"])
        y_s = _merge_mlp(x_s, ya_s.reshape(bs, Z_WIDTH), yb_s, ew)

        x_p, x_s = y_p.reshape(bp, tp, d), y_s
        rows_p.append(lat_p)
        gdn_p.append(s_p)
        conv_p.append(cs_p)
        rows_s.append(lat_s.reshape(bs, 1, LATENT_WIDTH))
        gdn_s.append(s_s)
        conv_s.append(cs_s)
    return (x_p, x_s.reshape(bs, 1, d), jnp.stack(rows_p), jnp.stack(gdn_p), jnp.stack(conv_p),
            jnp.stack(rows_s), jnp.stack(gdn_s), jnp.stack(conv_s))
```

```python
import functools

import jax
import jax.numpy as jnp
from jax import lax
from jax.experimental import pallas as pl
from jax.experimental.pallas import tpu as pltpu

F32 = jnp.float32
BF16 = jnp.bfloat16

EPS = 1e-6
GDN_HEADS = 8
GDN_DK = 128
GDN_DV = 128
CONV_W = 4
GDN_CHUNK = 64
MLA_HEADS = 8
Q_LORA = 512
KV_LORA = 512
NOPE_DIM = 128
ROPE_DIM = 64
V_DIM = 128
ROPE_THETA = 10000.0
MLA_SCALE = (NOPE_DIM + ROPE_DIM) ** -0.5
LOG2_E = 1.4426950408889634
LATENT_WIDTH = KV_LORA + ROPE_DIM
QKV_WIDTH = GDN_HEADS * (2 * GDN_DK + GDN_DV)
Z_WIDTH = GDN_HEADS * GDN_DV

LANES = 128
QK_PAD = 256
VMEM_LIMIT = 56 * 1024 * 1024
DEC_PAGES_PER_STEP = 32
ATTN_TILE = 256


def _dot(a, b):
    return jnp.dot(a, b, preferred_element_type=F32)


def _dot_nt(a, b):
    return lax.dot_general(a, b, (((1,), (1,)), ((), ())), preferred_element_type=F32)


def _dot_tn(a, b):
    return lax.dot_general(a, b, (((0,), (0,)), ((), ())), preferred_element_type=F32)


def _split2(x):
    hi = x.astype(BF16)
    lo = (x - hi.astype(F32)).astype(BF16)
    return hi, lo


def _split3(x):
    hi = x.astype(BF16)
    r = x - hi.astype(F32)
    mid = r.astype(BF16)
    lo = (r - mid.astype(F32)).astype(BF16)
    return hi, mid, lo


def _rms(x, g, n=None):
    n = x.shape[-1] if n is None else n
    ss = jnp.sum(x * x, axis=-1, keepdims=True) * (1.0 / n)
    return x * lax.rsqrt(ss + EPS) * g


def _sigmoid(x):
    return 1.0 / (1.0 + jnp.exp(-x))


def _silu(x):
    return x * _sigmoid(x)


def _softplus(x):
    return jnp.maximum(x, 0.0) + jnp.log1p(jnp.exp(-jnp.abs(x)))


def _rope(x, cos2, sin2):
    lane = lax.broadcasted_iota(jnp.int32, x.shape, 1)
    fwd = pltpu.roll(x, LANES - ROPE_DIM // 2, axis=1)
    bwd = pltpu.roll(x, ROPE_DIM // 2, axis=1)
    swapped = jnp.where(lane < ROPE_DIM // 2, fwd, bwd)
    return x * cos2 + swapped * sin2


def _full_spec(shape):
    nd = len(shape)
    return pl.BlockSpec(shape, lambda *_: (0,) * nd)


def _params(*sem):
    return pltpu.CompilerParams(dimension_semantics=sem, vmem_limit_bytes=VMEM_LIMIT)


def _gdn_proj_kernel(x_ref, g_ref, wqkv_ref, wz_ref, wba_ref, qkv_ref, z_ref, ba_ref):
    xn = _rms(x_ref[...], g_ref[...]).astype(BF16)
    qkv_ref[...] = _dot(xn, wqkv_ref[...])
    z_ref[...] = _dot(xn, wz_ref[...])
    ba_ref[...] = _dot(xn, wba_ref[...])


def _gdn_proj(x2, g, wqkv, wz, wba):
    m, d = x2.shape
    tm = min(256, m)
    return pl.pallas_call(
        _gdn_proj_kernel,
        grid=(m // tm,),
        in_specs=[pl.BlockSpec((tm, d), lambda i: (i, 0)),
                  _full_spec(g.shape), _full_spec(wqkv.shape), _full_spec(wz.shape),
                  _full_spec(wba.shape)],
        out_specs=[pl.BlockSpec((tm, QKV_WIDTH), lambda i: (i, 0)),
                   pl.BlockSpec((tm, Z_WIDTH), lambda i: (i, 0)),
                   pl.BlockSpec((tm, LANES), lambda i: (i, 0))],
        out_shape=[jax.ShapeDtypeStruct((m, QKV_WIDTH), F32),
                   jax.ShapeDtypeStruct((m, Z_WIDTH), F32),
                   jax.ShapeDtypeStruct((m, LANES), F32)],
        compiler_params=_params("parallel"),
        name="gdn_proj",
    )(x2, g, wqkv, wz, wba)


def _gate_rows(ba, alog_row, dt_row):
    lane = lax.broadcasted_iota(jnp.int32, ba.shape, 1)
    is_a = (lane >= GDN_HEADS) & (lane < 2 * GDN_HEADS)
    g = jnp.where(is_a, -jnp.exp(alog_row) * _softplus(ba + dt_row), 0.0)
    return _sigmoid(ba), g


def _gdn_chunk_kernel(qkv_ref, z_ref, ba_ref, cw_ref, alog_ref, dt_ref, gn_ref,
                      ya_ref, s_ref, cs_ref, xp_ref):
    c = pl.program_id(1)
    nc = pl.num_programs(1)
    C = GDN_CHUNK
    TB = qkv_ref.shape[1]
    n_sub = TB // C

    @pl.when(c == 0)
    def _():
        xp_ref[0:8, :] = jnp.zeros((8, QKV_WIDTH), F32)
        s_ref[...] = jnp.zeros(s_ref.shape, F32)

    blk = qkv_ref[0]
    xp_ref[8:8 + TB, :] = blk
    cw = cw_ref[...]
    y = blk * cw[CONV_W - 1:CONV_W]
    for i in range(CONV_W - 1):
        y = y + xp_ref[8 - (CONV_W - 1) + i:8 - (CONV_W - 1) + i + TB, :] * cw[i:i + 1]
    y = _silu(y)

    @pl.when(c == nc - 1)
    def _():
        cs_ref[0] = xp_ref[8 + TB - (CONV_W - 1):8 + TB, :]

    xp_ref[0:8, :] = xp_ref[TB:TB + 8, :]

    beta_full, g_full = _gate_rows(ba_ref[0], alog_ref[...], dt_ref[...])
    row = lax.broadcasted_iota(jnp.int32, (C, C), 0)
    col = lax.broadcasted_iota(jnp.int32, (C, C), 1)
    incl = row >= col
    strict = row > col
    rtb = lax.broadcasted_iota(jnp.int32, (TB, TB), 0)
    ctb = lax.broadcasted_iota(jnp.int32, (TB, TB), 1)
    tril = jnp.where((rtb >= ctb) & (rtb // C == ctb // C), 1.0, 0.0).astype(BF16)
    g_col = sum(_dot(tril, part) for part in _split3(g_full))
    r128 = lax.broadcasted_iota(jnp.int32, (LANES, LANES), 0)
    c128 = lax.broadcasted_iota(jnp.int32, (LANES, LANES), 1)
    eye = jnp.where(r128 == c128, 1.0, 0.0).astype(BF16)
    g_rows = sum(_dot_nt(eye, part) for part in _split3(g_col))

    z = z_ref[0]
    gn = gn_ref[...]
    heads = range(GDN_HEADS)
    units = [(n, h) for n in range(n_sub) for h in heads]
    q, k, gc, eg, g_last, kb, rhs_bf, rhs, decay, kq = ({} for _ in range(10))
    for n, h in units:
        r0 = n * C
        qh = y[r0:r0 + C, h * GDN_DK:(h + 1) * GDN_DK]
        kh = y[r0:r0 + C, GDN_HEADS * GDN_DK + h * GDN_DK:GDN_HEADS * GDN_DK + (h + 1) * GDN_DK]
        vh = y[r0:r0 + C,
               2 * GDN_HEADS * GDN_DK + h * GDN_DV:2 * GDN_HEADS * GDN_DK + (h + 1) * GDN_DV]
        u = (n, h)
        q[u] = qh * lax.rsqrt(jnp.sum(qh * qh, axis=-1, keepdims=True) + EPS) * (GDN_DK ** -0.5)
        k[u] = kh * lax.rsqrt(jnp.sum(kh * kh, axis=-1, keepdims=True) + EPS)
        beta = beta_full[r0:r0 + C, h:h + 1]
        gc[u] = g_col[r0:r0 + C, GDN_HEADS + h:GDN_HEADS + h + 1]
        gr = g_rows[GDN_HEADS + h:GDN_HEADS + h + 1, r0:r0 + C]
        decay[u] = jnp.where(incl, jnp.exp(jnp.where(incl, gc[u] - gr, 0.0)), 0.0)
        eg[u] = jnp.exp(gc[u])
        g_last[u] = gc[u][C - 1:C, :]
        kb[u] = k[u] * beta
        rhs[u] = jnp.concatenate([vh * beta, kb[u] * eg[u]], axis=1)
        rhs_bf[u] = rhs[u].astype(BF16)
    for u in units:
        kq[u] = _dot_nt(jnp.concatenate([kb[u], q[u]], axis=0).astype(BF16),
                        k[u].astype(BF16))
    pw = {u: jnp.where(strict, kq[u][:C] * decay[u], 0.0) for u in units}
    aqk = {u: (kq[u][C:] * decay[u]).astype(BF16) for u in units}
    n_acc = {u: -pw[u] for u in units}
    for _ in range(5):
        pw_bf = {u: pw[u].astype(BF16) for u in units}
        pw = {u: _dot(pw_bf[u], pw_bf[u]) for u in units}
        n_acc = {u: n_acc[u] + pw[u] + _dot(n_acc[u].astype(BF16), pw[u].astype(BF16))
                 for u in units}
    sol = {}
    for u in units:
        n_hi, n_lo = _split2(n_acc[u])
        sol[u] = rhs[u] + _dot(n_hi, rhs_bf[u]) + _dot(n_lo, rhs_bf[u])
    wq = {u: jnp.concatenate([sol[u][:, GDN_DV:], q[u] * eg[u]], axis=0).astype(BF16)
          for u in units}
    kd = {u: (k[u] * jnp.exp(g_last[u] - gc[u])).astype(BF16) for u in units}
    s_cur = [s_ref[0, h] for h in heads]
    for n in range(n_sub):
        ws = [_dot(wq[(n, h)], s_cur[h].astype(BF16)) for h in heads]
        u_bf = [(sol[(n, h)][:, :GDN_DV] - ws[h][:C]).astype(BF16) for h in heads]
        o = [ws[h][C:] + _dot(aqk[(n, h)], u_bf[h]) for h in heads]
        s_cur = [jnp.exp(g_last[(n, h)]) * s_cur[h] + _dot_tn(kd[(n, h)], u_bf[h])
                 for h in heads]
        for h in heads:
            zh = z[n * C:(n + 1) * C, h * GDN_DV:(h + 1) * GDN_DV]
            ya_ref[0, n * C:(n + 1) * C, h * GDN_DV:(h + 1) * GDN_DV] = (
                _rms(o[h], gn) * _silu(zh)).astype(BF16)
    for h in heads:
        s_ref[0, h] = s_cur[h]


GDN_CHUNKS_PER_STEP = 2


def _gdn_chunked(qkv, z, ba, conv_w, alog_row, dt_row, gdn_norm):
    b, t, _ = qkv.shape
    C = GDN_CHUNK * GDN_CHUNKS_PER_STEP
    assert t % C == 0
    nc = t // C
    return pl.pallas_call(
        _gdn_chunk_kernel,
        grid=(b, nc),
        in_specs=[pl.BlockSpec((1, C, QKV_WIDTH), lambda i, j: (i, j, 0)),
                  pl.BlockSpec((1, C, Z_WIDTH), lambda i, j: (i, j, 0)),
                  pl.BlockSpec((1, C, LANES), lambda i, j: (i, j, 0)),
                  _full_spec(conv_w.shape), _full_spec(alog_row.shape),
                  _full_spec(dt_row.shape), _full_spec(gdn_norm.shape)],
        out_specs=[pl.BlockSpec((1, C, Z_WIDTH), lambda i, j: (i, j, 0)),
                   pl.BlockSpec((1, GDN_HEADS, GDN_DK, GDN_DV), lambda i, j: (i, 0, 0, 0)),
                   pl.BlockSpec((1, CONV_W - 1, QKV_WIDTH), lambda i, j: (i, 0, 0))],
        out_shape=[jax.ShapeDtypeStruct((b, t, Z_WIDTH), BF16),
                   jax.ShapeDtypeStruct((b, GDN_HEADS, GDN_DK, GDN_DV), F32),
                   jax.ShapeDtypeStruct((b, CONV_W - 1, QKV_WIDTH), F32)],
        scratch_shapes=[pltpu.VMEM((8 + C, QKV_WIDTH), F32)],
        compiler_params=_params("parallel", "arbitrary"),
        name="gdn_chunked",
    )(qkv, z, ba, conv_w, alog_row, dt_row, gdn_norm)


def _gdn_step_kernel(qkv_ref, z_ref, ba_ref, cs_ref, s_ref, cw_ref, alog_ref, dt_ref, gn_ref,
                     ya_ref, snew_ref, csnew_ref):
    rowv = qkv_ref[0]
    cs = cs_ref[0]
    cw = cw_ref[...]
    y = rowv * cw[CONV_W - 1:CONV_W]
    for i in range(CONV_W - 1):
        y = y + cs[i:i + 1] * cw[i:i + 1]
    y = _silu(y)
    csnew_ref[0, 0:CONV_W - 2, :] = cs[1:CONV_W - 1]
    csnew_ref[0, CONV_W - 2:CONV_W - 1, :] = rowv

    beta_full, g_full = _gate_rows(ba_ref[0], alog_ref[...], dt_ref[...])
    r128 = lax.broadcasted_iota(jnp.int32, (GDN_DK, GDN_DK), 0)
    c128 = lax.broadcasted_iota(jnp.int32, (GDN_DK, GDN_DK), 1)
    eye = r128 == c128
    z = z_ref[0]
    gn = gn_ref[...]

    def to_col(r):
        return jnp.sum(jnp.where(eye, jnp.broadcast_to(r, (GDN_DK, GDN_DK)), 0.0),
                       axis=1, keepdims=True)

    for h in range(GDN_HEADS):
        q = y[:, h * GDN_DK:(h + 1) * GDN_DK]
        k = y[:, GDN_HEADS * GDN_DK + h * GDN_DK:GDN_HEADS * GDN_DK + (h + 1) * GDN_DK]
        v = y[:, 2 * GDN_HEADS * GDN_DK + h * GDN_DV:2 * GDN_HEADS * GDN_DK + (h + 1) * GDN_DV]
        q = q * lax.rsqrt(jnp.sum(q * q, axis=-1, keepdims=True) + EPS) * (GDN_DK ** -0.5)
        k = k * lax.rsqrt(jnp.sum(k * k, axis=-1, keepdims=True) + EPS)
        beta = beta_full[:, h:h + 1]
        gdec = jnp.exp(g_full[:, GDN_HEADS + h:GDN_HEADS + h + 1])
        k_col = to_col(k)
        q_col = to_col(q)
        s = gdec * s_ref[0, h]
        u = beta * (v - jnp.sum(s * k_col, axis=0, keepdims=True))
        s = s + k_col * u
        snew_ref[0, h] = s
        o = jnp.sum(s * q_col, axis=0, keepdims=True)
        zh = z[:, h * GDN_DV:(h + 1) * GDN_DV]
        ya_ref[0, :, h * GDN_DV:(h + 1) * GDN_DV] = _rms(o, gn) * _silu(zh)


def _gdn_step(qkv, z, ba, conv_state, state, conv_w, alog_row, dt_row, gdn_norm):
    b = qkv.shape[0]
    qkv3 = qkv.reshape(b, 1, QKV_WIDTH)
    z3 = z.reshape(b, 1, Z_WIDTH)
    ba3 = ba.reshape(b, 1, LANES)
    return pl.pallas_call(
        _gdn_step_kernel,
        grid=(b,),
        in_specs=[pl.BlockSpec((1, 1, QKV_WIDTH), lambda i: (i, 0, 0)),
                  pl.BlockSpec((1, 1, Z_WIDTH), lambda i: (i, 0, 0)),
                  pl.BlockSpec((1, 1, LANES), lambda i: (i, 0, 0)),
                  pl.BlockSpec((1, CONV_W - 1, QKV_WIDTH), lambda i: (i, 0, 0)),
                  pl.BlockSpec((1, GDN_HEADS, GDN_DK, GDN_DV), lambda i: (i, 0, 0, 0)),
                  _full_spec(conv_w.shape), _full_spec(alog_row.shape),
                  _full_spec(dt_row.shape), _full_spec(gdn_norm.shape)],
        out_specs=[pl.BlockSpec((1, 1, Z_WIDTH), lambda i: (i, 0, 0)),
                   pl.BlockSpec((1, GDN_HEADS, GDN_DK, GDN_DV), lambda i: (i, 0, 0, 0)),
                   pl.BlockSpec((1, CONV_W - 1, QKV_WIDTH), lambda i: (i, 0, 0))],
        out_shape=[jax.ShapeDtypeStruct((b, 1, Z_WIDTH), F32),
                   jax.ShapeDtypeStruct((b, GDN_HEADS, GDN_DK, GDN_DV), F32),
                   jax.ShapeDtypeStruct((b, CONV_W - 1, QKV_WIDTH), F32)],
        compiler_params=_params("parallel"),
        name="gdn_step",
    )(qkv3, z3, ba3, conv_state, state, conv_w, alog_row, dt_row, gdn_norm)


def _mla_queries_latent(x, cos2, sin2, g_mix, w_qd, w_c, w_kr, g_qa, w_uq, g_kva,
                        g_qn, g_qr, g_kr):
    xn = _rms(x, g_mix).astype(BF16)
    qa = _rms(_dot(xn, w_qd), g_qa).astype(BF16)
    q = _dot(qa, w_uq)
    qn, qr = [], []
    for h in range(MLA_HEADS):
        qn.append(_rms(q[:, h * QK_PAD:h * QK_PAD + NOPE_DIM], g_qn))
        qr.append(_rope(_rms(q[:, h * QK_PAD + NOPE_DIM:(h + 1) * QK_PAD], g_qr, ROPE_DIM),
                        cos2, sin2))
    c = _rms(_dot(xn, w_c), g_kva)
    kr = _rope(_rms(_dot(xn, w_kr), g_kr, ROPE_DIM), cos2, sin2)
    return qn, qr, c, kr


def _mla_pre_kernel(x_ref, cos_ref, sin_ref, gmix_ref, wqd_ref, wc_ref, wkr_ref, gqa_ref,
                    wuq_ref, gkva_ref, gqn_ref, gqr_ref, gkr_ref, wuk_ref, wuv_ref, gkn_ref,
                    q_ref, k_ref, v_ref, lat_ref):
    qn, qr, c, kr = _mla_queries_latent(
        x_ref[0], cos_ref[...], sin_ref[...], gmix_ref[...], wqd_ref[...], wc_ref[...],
        wkr_ref[...], gqa_ref[...], wuq_ref[...], gkva_ref[...], gqn_ref[...], gqr_ref[...],
        gkr_ref[...])
    lat_ref[0, :, 0:KV_LORA] = c
    lat_ref[0, :, KV_LORA:LATENT_WIDTH] = kr[:, 0:ROPE_DIM]
    c_bf = c.astype(BF16)
    kr_bf = kr.astype(BF16)
    kfull = _dot(c_bf, wuk_ref[...])
    v_ref[0, 0] = _dot_nt(wuv_ref[...], c_bf).astype(BF16)
    gkn = gkn_ref[...]
    for h in range(MLA_HEADS):
        q_ref[0, :, h * QK_PAD:h * QK_PAD + NOPE_DIM] = qn[h].astype(BF16)
        q_ref[0, :, h * QK_PAD + NOPE_DIM:(h + 1) * QK_PAD] = qr[h].astype(BF16)
        kn = _rms(kfull[:, h * NOPE_DIM:(h + 1) * NOPE_DIM], gkn)
        k_ref[0, :, h * QK_PAD:h * QK_PAD + NOPE_DIM] = kn.astype(BF16)
        k_ref[0, :, h * QK_PAD + NOPE_DIM:(h + 1) * QK_PAD] = kr_bf


def _mla_pre(x, cos2, sin2, mw):
    b, t, d = x.shape
    tm = min(ATTN_TILE, t)
    consts = [mw["g_mix"], mw["w_qd"], mw["w_c"], mw["w_kr"], mw["g_qa"], mw["w_uq"],
              mw["g_kva"], mw["g_qn"], mw["g_qr"], mw["g_kr"], mw["w_uk"], mw["w_uv_t"],
              mw["g_kn"]]
    return pl.pallas_call(
        _mla_pre_kernel,
        grid=(b, t // tm),
        in_specs=[pl.BlockSpec((1, tm, d), lambda i, j: (i, j, 0)),
                  pl.BlockSpec((tm, LANES), lambda i, j: (j, 0)),
                  pl.BlockSpec((tm, LANES), lambda i, j: (j, 0))]
                 + [_full_spec(a.shape) for a in consts],
        out_specs=[pl.BlockSpec((1, tm, MLA_HEADS * QK_PAD), lambda i, j: (i, j, 0)),
                   pl.BlockSpec((1, tm, MLA_HEADS * QK_PAD), lambda i, j: (i, j, 0)),
                   pl.BlockSpec((1, 1, MLA_HEADS * V_DIM, tm), lambda i, j: (i, j, 0, 0)),
                   pl.BlockSpec((1, tm, LATENT_WIDTH), lambda i, j: (i, j, 0))],
        out_shape=[jax.ShapeDtypeStruct((b, t, MLA_HEADS * QK_PAD), BF16),
                   jax.ShapeDtypeStruct((b, t, MLA_HEADS * QK_PAD), BF16),
                   jax.ShapeDtypeStruct((b, t // tm, MLA_HEADS * V_DIM, tm), BF16),
                   jax.ShapeDtypeStruct((b, t, LATENT_WIDTH), F32)],
        compiler_params=_params("parallel", "parallel"),
        name="mla_pre",
    )(x, cos2, sin2, *consts)


def _mla_pre_sample_kernel(x_ref, cos_ref, sin_ref, gmix_ref, wqd_ref, wc_ref, wkr_ref, gqa_ref,
                           wuq_ref, gkva_ref, gqn_ref, gqr_ref, gkr_ref, wuk_ref, gkn_ref,
                           qa_ref, qr_ref, lat_ref):
    qn, qr, c, kr = _mla_queries_latent(
        x_ref[...], cos_ref[...], sin_ref[...], gmix_ref[...], wqd_ref[...], wc_ref[...],
        wkr_ref[...], gqa_ref[...], wuq_ref[...], gkva_ref[...], gqn_ref[...], gqr_ref[...],
        gkr_ref[...])
    lat_ref[:, 0:KV_LORA] = c
    lat_ref[:, KV_LORA:LATENT_WIDTH] = kr[:, 0:ROPE_DIM]
    gkn = gkn_ref[...]
    for h in range(MLA_HEADS):
        hi, lo = _split2(qn[h] * gkn)
        w_h = wuk_ref[:, h * NOPE_DIM:(h + 1) * NOPE_DIM]
        qa_ref[h] = _dot_nt(hi, w_h) + _dot_nt(lo, w_h)
        qr_ref[h] = qr[h]


def _mla_pre_sample(x2, cos2, sin2, mw):
    m = x2.shape[0]
    consts = [mw["g_mix"], mw["w_qd"], mw["w_c"], mw["w_kr"], mw["g_qa"], mw["w_uq"],
              mw["g_kva"], mw["g_qn"], mw["g_qr"], mw["g_kr"], mw["w_uk"], mw["g_kn"]]
    args = [x2, cos2, sin2] + consts
    return pl.pallas_call(
        _mla_pre_sample_kernel,
        grid=(1,),
        in_specs=[_full_spec(a.shape) for a in args],
        out_specs=[_full_spec((MLA_HEADS, m, KV_LORA)), _full_spec((MLA_HEADS, m, LANES)),
                   _full_spec((m, LATENT_WIDTH))],
        out_shape=[jax.ShapeDtypeStruct((MLA_HEADS, m, KV_LORA), F32),
                   jax.ShapeDtypeStruct((MLA_HEADS, m, LANES), F32),
                   jax.ShapeDtypeStruct((m, LATENT_WIDTH), F32)],
        compiler_params=_params("arbitrary"),
        name="mla_pre_sample",
    )(*args)


ATTN_HEADS_PER_STEP = 8


def _attn_kernel(q_ref, k_ref, vt_ref, o_ref, *, tq):
    qi = pl.program_id(2)
    hs = range(ATTN_HEADS_PER_STEP)
    q = [q_ref[0, :, g * QK_PAD:(g + 1) * QK_PAD] for g in hs]
    key = lax.broadcasted_iota(jnp.int32, (tq, tq), 0)
    qry = lax.broadcasted_iota(jnp.int32, (tq, tq), 1)

    def raw_scores(j):
        start = pl.multiple_of(j * tq, tq)
        return tuple(_dot_nt(k_ref[0, pl.ds(start, tq), g * QK_PAD:(g + 1) * QK_PAD], q[g])
                     for g in hs)

    def consume(j, s_raw, stats, diagonal):
        vt = [vt_ref[0, j, g * V_DIM:(g + 1) * V_DIM, :] for g in hs]
        s = [s_raw[g] * (MLA_SCALE * LOG2_E) for g in hs]
        if diagonal:
            s = [jnp.where(key <= qry, s[g], -jnp.inf) for g in hs]
        m_new = [jnp.maximum(stats[g][0], jnp.max(s[g], axis=0, keepdims=True)) for g in hs]
        p = [jnp.exp2(s[g] - m_new[g]) for g in hs]
        pv = [_dot(vt[g], p[g].astype(BF16)) for g in hs]
        out = []
        for g in hs:
            m, l, acc = stats[g]
            alpha = jnp.exp2(m - m_new[g])
            out.append((m_new[g], alpha * l + jnp.sum(p[g], axis=0, keepdims=True),
                        alpha * acc + pv[g]))
        return tuple(out)

    init = tuple((jnp.full((1, tq), -jnp.inf, F32), jnp.zeros((1, tq), F32),
                  jnp.zeros((V_DIM, tq), F32)) for _ in hs)
    stats = lax.fori_loop(0, qi, lambda j, st: consume(j, raw_scores(j), st, False), init)
    stats = consume(qi, raw_scores(qi), stats, True)
    for g in hs:
        _, l, acc = stats[g]
        o_ref[0, :, g * V_DIM:(g + 1) * V_DIM] = (acc / l).T.astype(BF16)


def _mla_prompt_attention(q_cat, k_cat, v_t):
    b, t, _ = q_cat.shape
    tq = min(ATTN_TILE, t)
    hps = ATTN_HEADS_PER_STEP
    return pl.pallas_call(
        functools.partial(_attn_kernel, tq=tq),
        grid=(b, MLA_HEADS // hps, t // tq),
        in_specs=[pl.BlockSpec((1, tq, hps * QK_PAD), lambda i, h, j: (i, j, h)),
                  pl.BlockSpec((1, t, hps * QK_PAD), lambda i, h, j: (i, 0, h)),
                  pl.BlockSpec((1, t // tq, hps * V_DIM, tq), lambda i, h, j: (i, 0, h, 0))],
        out_specs=pl.BlockSpec((1, tq, hps * V_DIM), lambda i, h, j: (i, j, h)),
        out_shape=jax.ShapeDtypeStruct((b, t, MLA_HEADS * V_DIM), BF16),
        compiler_params=_params("parallel", "parallel", "arbitrary"),
        name="mla_prompt_attention",
    )(q_cat, k_cat, v_t)


def _dec_attn_kernel(pt_ref, *refs, pps):
    del pt_ref
    page_refs = refs[:pps]
    (wukt_ref, qa_ref, qr_ref, lnew_ref, ctx_ref,
     lhs_ref, qr16_ref, m_ref, l_ref, acc_ref) = refs[pps:]
    b = pl.program_id(0)
    j = pl.program_id(1)
    nj = pl.num_programs(1)
    n_k = MLA_HEADS * NOPE_DIM

    @pl.when((b == 0) & (j == 0))
    def _():
        lhs_ref[0:n_k, :] = wukt_ref[...]

    @pl.when(j == 0)
    def _():
        qa = jnp.concatenate([qa_ref[h, 0] for h in range(MLA_HEADS)]
                             + [jnp.zeros((8, KV_LORA), F32)], axis=0)
        lhs_ref[n_k:n_k + 16, :] = qa.astype(BF16)
        qr = jnp.concatenate([qr_ref[h, 0] for h in range(MLA_HEADS)]
                             + [jnp.zeros((8, LANES), F32)], axis=0)
        qr16_ref[...] = qr.astype(BF16)
        m_ref[...] = jnp.full(m_ref.shape, -jnp.inf, F32)
        l_ref[...] = jnp.zeros(l_ref.shape, F32)
        acc_ref[...] = jnp.zeros(acc_ref.shape, F32)

    def scores(res, sr):
        kt = res[0:n_k]
        ss = jnp.sum((kt * kt).reshape(MLA_HEADS, NOPE_DIM, res.shape[1]), axis=1)
        sn = res[n_k:n_k + MLA_HEADS]
        return (sn * lax.rsqrt(ss * (1.0 / NOPE_DIM) + EPS) + sr[0:MLA_HEADS]) * MLA_SCALE

    c_bf, s_parts = [], []
    for i in range(0, pps, 2):
        tile = jnp.concatenate([page_refs[i][...], page_refs[i + 1][...]], axis=1)
        c_bf.append(tile[0:KV_LORA].astype(BF16))
        kr_bf = tile[KV_LORA:LATENT_WIDTH].astype(BF16)
        s_parts.append(scores(_dot(lhs_ref[...], c_bf[-1]),
                              _dot(qr16_ref[:, 0:ROPE_DIM], kr_bf)))
    s = jnp.concatenate(s_parts, axis=1)
    m_old = m_ref[...]
    m_new = jnp.maximum(m_old, jnp.max(s, axis=-1, keepdims=True))
    alpha = jnp.exp(m_old - m_new)
    p = jnp.exp(s - m_new)
    l_ref[...] = alpha * l_ref[...] + jnp.sum(p, axis=-1, keepdims=True)
    p_bf = p.astype(BF16)
    n_tok = c_bf[0].shape[1]
    acc = alpha * acc_ref[...]
    for t, c_t in enumerate(c_bf):
        acc = acc + _dot_nt(p_bf[:, t * n_tok:(t + 1) * n_tok], c_t)
    acc_ref[...] = acc
    m_ref[...] = m_new

    @pl.when(j == nj - 1)
    def _():
        ln = lnew_ref[0]
        c_new = ln[:, 0:KV_LORA]
        c8 = jnp.broadcast_to(c_new, (8, KV_LORA)).astype(BF16)
        kr8 = jnp.broadcast_to(ln[:, KV_LORA:LATENT_WIDTH], (8, ROPE_DIM)).astype(BF16)
        s_new = scores(_dot_nt(lhs_ref[...], c8),
                       _dot_nt(qr16_ref[:, 0:ROPE_DIM], kr8))[:, 0:1]
        m_old = m_ref[...]
        m_new = jnp.maximum(m_old, s_new)
        alpha = jnp.exp(m_old - m_new)
        p_new = jnp.exp(s_new - m_new)
        ctx = (alpha * acc_ref[...] + p_new * c_new) / (alpha * l_ref[...] + p_new)
        for h in range(MLA_HEADS):
            ctx_ref[h, 0] = ctx[h:h + 1]


def _mla_decode_attention(cache, layer, page_table, wuk_t, qa, qr, lat_new):
    bsz, n_pages = page_table.shape
    page = cache.shape[3]
    pps = DEC_PAGES_PER_STEP
    while n_pages % pps:
        pps //= 2
    assert pps >= 2
    qa4 = qa.reshape(MLA_HEADS, bsz, 1, KV_LORA)
    qr4 = qr.reshape(MLA_HEADS, bsz, 1, LANES)
    ln3 = lat_new.reshape(bsz, 1, LATENT_WIDTH)

    def page_spec(i):
        return pl.BlockSpec((pl.Squeezed(), pl.Squeezed(), LATENT_WIDTH, page),
                            lambda b, j, pt: (layer, pt[b, j * pps + i], 0, 0))

    grid_spec = pltpu.PrefetchScalarGridSpec(
        num_scalar_prefetch=1,
        grid=(bsz, n_pages // pps),
        in_specs=[page_spec(i) for i in range(pps)] + [
            pl.BlockSpec(wuk_t.shape, lambda b, j, pt: (0, 0)),
            pl.BlockSpec((MLA_HEADS, 1, 1, KV_LORA), lambda b, j, pt: (0, b, 0, 0)),
            pl.BlockSpec((MLA_HEADS, 1, 1, LANES), lambda b, j, pt: (0, b, 0, 0)),
            pl.BlockSpec((1, 1, LATENT_WIDTH), lambda b, j, pt: (b, 0, 0))],
        out_specs=pl.BlockSpec((MLA_HEADS, 1, 1, KV_LORA), lambda b, j, pt: (0, b, 0, 0)),
        scratch_shapes=[pltpu.VMEM((MLA_HEADS * NOPE_DIM + 16, KV_LORA), BF16),
                        pltpu.VMEM((16, LANES), BF16),
                        pltpu.VMEM((MLA_HEADS, 1), F32),
                        pltpu.VMEM((MLA_HEADS, 1), F32),
                        pltpu.VMEM((MLA_HEADS, KV_LORA), F32)])
    ctx = pl.pallas_call(
        functools.partial(_dec_attn_kernel, pps=pps),
        grid_spec=grid_spec,
        out_shape=jax.ShapeDtypeStruct((MLA_HEADS, bsz, 1, KV_LORA), F32),
        compiler_params=_params("arbitrary", "arbitrary"),
        name="mla_decode_attention",
    )(page_table, *([cache] * pps), wuk_t, qa4, qr4, ln3)
    return ctx.reshape(MLA_HEADS, bsz, KV_LORA)


def _dec_post_kernel(ctx_ref, wuv_ref, y_ref):
    for h in range(MLA_HEADS):
        hi, lo = _split2(ctx_ref[h])
        w_h = wuv_ref[:, h * V_DIM:(h + 1) * V_DIM]
        y_ref[:, h * V_DIM:(h + 1) * V_DIM] = (_dot(hi, w_h) + _dot(lo, w_h)).astype(BF16)


def _dec_post(ctx, w_uv):
    bsz = ctx.shape[1]
    return pl.pallas_call(
        _dec_post_kernel,
        grid=(1,),
        in_specs=[_full_spec(ctx.shape), _full_spec(w_uv.shape)],
        out_specs=_full_spec((bsz, MLA_HEADS * V_DIM)),
        out_shape=jax.ShapeDtypeStruct((bsz, MLA_HEADS * V_DIM), BF16),
        compiler_params=_params("arbitrary"),
        name="mla_decode_values",
    )(ctx, w_uv)


def _merge_mlp_kernel(x_ref, ya_ref, yb_ref, gmix_ref, wg_ref, wya_ref, wyb_ref, wo_ref,
                      gmlp_ref, wup_ref, wdn_ref, y_ref):
    tm, d = x_ref.shape
    d_ff = wup_ref.shape[1]
    n_grp = max(1, tm // MERGE_ROWS_PER_GROUP)
    rows = tm // n_grp
    grp = range(n_grp)
    sl = [slice(r * rows, (r + 1) * rows) for r in grp]
    x = [x_ref[sl[r], :] for r in grp]
    xn = [_rms(x[r], gmix_ref[...]).astype(BF16) for r in grp]
    gates = [_sigmoid(_dot(xn[r], wg_ref[...])) for r in grp]
    pa = [_dot(ya_ref[sl[r], :].astype(BF16), wya_ref[...]) for r in grp]
    pb = [_dot(yb_ref[sl[r], :].astype(BF16), wyb_ref[...]) for r in grp]
    mix = [(gates[r][:, :d] * pa[r] + gates[r][:, d:] * pb[r]).astype(BF16) for r in grp]
    x1 = [x[r] + _dot(mix[r], wo_ref[...]) for r in grp]
    hin = [_rms(x1[r], gmlp_ref[...]).astype(BF16) for r in grp]
    y = x1
    for f in range(0, d_ff, MLP_FF_BLOCK):
        hmid = [jnp.maximum(_dot(hin[r], wup_ref[:, f:f + MLP_FF_BLOCK]), 0.0) for r in grp]
        y = [y[r] + _dot((hmid[r] * hmid[r]).astype(BF16), wdn_ref[f:f + MLP_FF_BLOCK, :])
             for r in grp]
    for r in grp:
        y_ref[sl[r], :] = y[r]


MERGE_ROWS_PER_GROUP = 256
MLP_FF_BLOCK = 1024


def _merge_mlp(x2, ya, yb, ew):
    m, d = x2.shape
    tm = min(512, m)
    consts = [ew["g_mix"], ew["w_gates"], ew["w_ya"], ew["w_yb"], ew["w_o"], ew["g_mlp"],
              ew["w_up"], ew["w_down"]]

    def const_spec(a):
        nd = a.ndim
        return pl.BlockSpec(a.shape, lambda i: (0,) * nd, pipeline_mode=pl.Buffered(1))

    return pl.pallas_call(
        _merge_mlp_kernel,
        grid=(m // tm,),
        in_specs=[pl.BlockSpec((tm, d), lambda i: (i, 0)),
                  pl.BlockSpec((tm, ya.shape[1]), lambda i: (i, 0)),
                  pl.BlockSpec((tm, yb.shape[1]), lambda i: (i, 0))]
                 + [const_spec(a) for a in consts],
        out_specs=pl.BlockSpec((tm, d), lambda i: (i, 0)),
        out_shape=jax.ShapeDtypeStruct((m, d), F32),
        compiler_params=_params("parallel"),
        name="merge_mlp",
    )(x2, ya, yb, *consts)


def _rope_tables(pos):
    inv = jnp.power(ROPE_THETA, -jnp.arange(0, ROPE_DIM, 2, dtype=F32) / ROPE_DIM)
    ang = pos.astype(F32)[:, None] * inv[None, :]
    cos, sin = jnp.cos(ang), jnp.sin(ang)
    zeros = jnp.zeros((pos.shape[0], LANES - ROPE_DIM), F32)
    return (jnp.concatenate([cos, cos, zeros], axis=1),
            jnp.concatenate([-sin, sin, zeros], axis=1))


def _pad_lanes(a, width=LANES, offset=0):
    return jnp.zeros((1, width), F32).at[0, offset:offset + a.shape[0]].set(a.astype(F32))


def _layer_weights(l, norm_mix, w_in, conv_w, a_log, dt_bias, gdn_norm, q_a_norm, w_uq,
                   kv_a_norm, w_uk, w_uv, q_norm_nope, q_norm_rope, k_norm_nope, k_norm_rope,
                   w_ya, w_yb, w_o, norm_mlp, w_up, w_down):
    d = w_in.shape[1]
    off_z = QKV_WIDTH
    off_b = off_z + Z_WIDTH
    off_qd = off_b + 2 * GDN_HEADS
    off_kv = off_qd + Q_LORA
    off_g = off_kv + LATENT_WIDTH
    wi = w_in[l]
    g_mix = norm_mix[l].reshape(1, d)
    gw = {
        "g_mix": g_mix,
        "w_qkv": wi[:, :off_z].astype(BF16),
        "w_z": wi[:, off_z:off_b].astype(BF16),
        "w_ba": jnp.pad(wi[:, off_b:off_qd], ((0, 0), (0, LANES - 2 * GDN_HEADS))).astype(BF16),
        "conv_w": conv_w[l],
        "alog_row": _pad_lanes(a_log[l], offset=GDN_HEADS),
        "dt_row": _pad_lanes(dt_bias[l], offset=GDN_HEADS),
        "gdn_norm": gdn_norm[l].reshape(1, GDN_DV),
    }
    wq = w_uq[l].reshape(Q_LORA, MLA_HEADS, NOPE_DIM + ROPE_DIM)
    wq = jnp.pad(wq, ((0, 0), (0, 0), (0, QK_PAD - NOPE_DIM - ROPE_DIM)))
    mw = {
        "g_mix": g_mix,
        "w_qd": wi[:, off_qd:off_kv].astype(BF16),
        "w_c": wi[:, off_kv:off_kv + KV_LORA].astype(BF16),
        "w_kr": jnp.pad(wi[:, off_kv + KV_LORA:off_g], ((0, 0), (0, LANES - ROPE_DIM))).astype(BF16),
        "g_qa": q_a_norm[l].reshape(1, Q_LORA),
        "w_uq": wq.reshape(Q_LORA, MLA_HEADS * QK_PAD).astype(BF16),
        "g_kva": kv_a_norm[l].reshape(1, KV_LORA),
        "g_qn": q_norm_nope[l].reshape(1, NOPE_DIM),
        "g_qr": _pad_lanes(q_norm_rope[l]),
        "g_kr": _pad_lanes(k_norm_rope[l]),
        "g_kn": k_norm_nope[l].reshape(1, NOPE_DIM),
        "w_uk": w_uk[l].astype(BF16),
        "w_uk_t": w_uk[l].T.astype(BF16),
        "w_uv": w_uv[l].astype(BF16),
        "w_uv_t": w_uv[l].T.astype(BF16),
    }
    ew = {
        "g_mix": g_mix,
        "w_gates": wi[:, off_g:].astype(BF16),
        "w_ya": w_ya[l].astype(BF16),
        "w_yb": w_yb[l].astype(BF16),
        "w_o": w_o[l].astype(BF16),
        "g_mlp": norm_mlp[l].reshape(1, d),
        "w_up": w_up[l].astype(BF16),
        "w_down": w_down[l].astype(BF16),
    }
    return gw, mw, ew


def kernel(x_prompt, x_sample, cache_mla, state_gdn, state_conv, page_table, norm_mix, w_in,
           conv_w, a_log, dt_bias, gdn_norm, q_a_norm, w_uq, kv_a_norm, w_uk, w_uv,
           q_norm_nope, q_norm_rope, k_norm_nope, k_norm_rope, w_ya, w_yb, w_o, norm_mlp,
           w_up, w_down):
    depth = w_in.shape[0]
    bp, tp, d = x_prompt.shape
    bs, ts, _ = x_sample.shape
    assert ts == 1 and tp % GDN_CHUNK == 0
    past_len = page_table.shape[1] * cache_mla.shape[2]
    cache_t = jnp.swapaxes(cache_mla, 2, 3)
    cos_p, sin_p = _rope_tables(jnp.arange(tp))
    cos_s, sin_s = _rope_tables(past_len + jnp.zeros((bs,), jnp.int32))

    x_p, x_s = x_prompt, x_sample.reshape(bs, d)
    rows_p, gdn_p, conv_p, rows_s, gdn_s, conv_s = [], [], [], [], [], []
    for l in range(depth):
        gw, mw, ew = _layer_weights(
            l, norm_mix, w_in, conv_w, a_log, dt_bias, gdn_norm, q_a_norm, w_uq, kv_a_norm,
            w_uk, w_uv, q_norm_nope, q_norm_rope, k_norm_nope, k_norm_rope, w_ya, w_yb, w_o,
            norm_mlp, w_up, w_down)
        gdn_consts = (gw["conv_w"], gw["alog_row"], gw["dt_row"], gw["gdn_norm"])

        xp2 = x_p.reshape(bp * tp, d)
        qkv, z, ba = _gdn_proj(xp2, gw["g_mix"], gw["w_qkv"], gw["w_z"], gw["w_ba"])
        ya_p, s_p, cs_p = _gdn_chunked(qkv.reshape(bp, tp, QKV_WIDTH), z.reshape(bp, tp, Z_WIDTH),
                                       ba.reshape(bp, tp, LANES), *gdn_consts)
        q_cat, k_cat, v_p, lat_p = _mla_pre(x_p, cos_p, sin_p, mw)
        yb_p = _mla_prompt_attention(q_cat, k_cat, v_p)
        y_p = _merge_mlp(xp2, ya_p.reshape(bp * tp, Z_WIDTH), yb_p.reshape(bp * tp, -1), ew)

        qkv, z, ba = _gdn_proj(x_s, gw["g_mix"], gw["w_qkv"], gw["w_z"], gw["w_ba"])
        ya_s, s_s, cs_s = _gdn_step(qkv, z, ba, state_conv[l], state_gdn[l], *gdn_consts)
        qa, qr, lat_s = _mla_pre_sample(x_s, cos_s, sin_s, mw)
        ctx = _mla_decode_attention(cache_t, l, page_table, mw["w_uk_t"], qa, qr, lat_s)
        yb_s = _dec_post(ctx, mw["w_uv"])
        y_s = _merge_mlp(x_s, ya_s.reshape(bs, Z_WIDTH), yb_s, ew)

        x_p, x_s = y_p.reshape(bp, tp, d), y_s
        rows_p.append(lat_p)
        gdn_p.append(s_p)
        conv_p.append(cs_p)
        rows_s.append(lat_s.reshape(bs, 1, LATENT_WIDTH))
        gdn_s.append(s_s)
        conv_s.append(cs_s)
    return (x_p, x_s.reshape(bs, 1, d), jnp.stack(rows_p), jnp.stack(gdn_p), jnp.stack(conv_p),
            jnp.stack(rows_s), jnp.stack(gdn_s), jnp.stack(conv_s))
```

```python
import functools

import jax
import jax.numpy as jnp
from jax import lax
from jax.experimental import pallas as pl
from jax.experimental.pallas import tpu as pltpu

F32 = jnp.float32
BF16 = jnp.bfloat16

EPS = 1e-6
GDN_HEADS = 8
GDN_DK = 128
GDN_DV = 128
CONV_W = 4
GDN_CHUNK = 64
MLA_HEADS = 8
Q_LORA = 512
KV_LORA = 512
NOPE_DIM = 128
ROPE_DIM = 64
V_DIM = 128
ROPE_THETA = 10000.0
MLA_SCALE = (NOPE_DIM + ROPE_DIM) ** -0.5
LOG2_E = 1.4426950408889634
LATENT_WIDTH = KV_LORA + ROPE_DIM
QKV_WIDTH = GDN_HEADS * (2 * GDN_DK + GDN_DV)
Z_WIDTH = GDN_HEADS * GDN_DV

LANES = 128
QK_PAD = 256
VMEM_LIMIT = 56 * 1024 * 1024
DEC_PAGES_PER_STEP = 32
ATTN_TILE = 256


def _dot(a, b):
    return jnp.dot(a, b, preferred_element_type=F32)


def _dot_nt(a, b):
    return lax.dot_general(a, b, (((1,), (1,)), ((), ())), preferred_element_type=F32)


def _dot_tn(a, b):
    return lax.dot_general(a, b, (((0,), (0,)), ((), ())), preferred_element_type=F32)


def _split2(x):
    hi = x.astype(BF16)
    lo = (x - hi.astype(F32)).astype(BF16)
    return hi, lo


def _split3(x):
    hi = x.astype(BF16)
    r = x - hi.astype(F32)
    mid = r.astype(BF16)
    lo = (r - mid.astype(F32)).astype(BF16)
    return hi, mid, lo


def _rms(x, g, n=None):
    n = x.shape[-1] if n is None else n
    ss = jnp.sum(x * x, axis=-1, keepdims=True) * (1.0 / n)
    return x * lax.rsqrt(ss + EPS) * g


def _sigmoid(x):
    return 1.0 / (1.0 + jnp.exp(-x))


def _silu(x):
    return x * _sigmoid(x)


def _softplus(x):
    return jnp.maximum(x, 0.0) + jnp.log1p(jnp.exp(-jnp.abs(x)))


def _rope(x, cos2, sin2):
    lane = lax.broadcasted_iota(jnp.int32, x.shape, 1)
    fwd = pltpu.roll(x, LANES - ROPE_DIM // 2, axis=1)
    bwd = pltpu.roll(x, ROPE_DIM // 2, axis=1)
    swapped = jnp.where(lane < ROPE_DIM // 2, fwd, bwd)
    return x * cos2 + swapped * sin2


def _full_spec(shape):
    nd = len(shape)
    return pl.BlockSpec(shape, lambda *_: (0,) * nd)


def _params(*sem):
    return pltpu.CompilerParams(dimension_semantics=sem, vmem_limit_bytes=VMEM_LIMIT)


def _gdn_proj_kernel(x_ref, g_ref, wqkv_ref, wz_ref, wba_ref, qkv_ref, z_ref, ba_ref):
    xn = _rms(x_ref[...], g_ref[...]).astype(BF16)
    qkv_ref[...] = _dot(xn, wqkv_ref[...])
    z_ref[...] = _dot(xn, wz_ref[...])
    ba_ref[...] = _dot(xn, wba_ref[...])


def _gdn_proj(x2, g, wqkv, wz, wba):
    m, d = x2.shape
    tm = min(256, m)
    return pl.pallas_call(
        _gdn_proj_kernel,
        grid=(m // tm,),
        in_specs=[pl.BlockSpec((tm, d), lambda i: (i, 0)),
                  _full_spec(g.shape), _full_spec(wqkv.shape), _full_spec(wz.shape),
                  _full_spec(wba.shape)],
        out_specs=[pl.BlockSpec((tm, QKV_WIDTH), lambda i: (i, 0)),
                   pl.BlockSpec((tm, Z_WIDTH), lambda i: (i, 0)),
                   pl.BlockSpec((tm, LANES), lambda i: (i, 0))],
        out_shape=[jax.ShapeDtypeStruct((m, QKV_WIDTH), F32),
                   jax.ShapeDtypeStruct((m, Z_WIDTH), F32),
                   jax.ShapeDtypeStruct((m, LANES), F32)],
        compiler_params=_params("parallel"),
        name="gdn_proj",
    )(x2, g, wqkv, wz, wba)


PROJ_CONV_COLS = 512


def _gdn_proj_conv_kernel(x_ref, g_ref, wqkv_ref, wz_ref, wba_ref, cw_ref,
                          y_ref, z_ref, ba_ref, cs_ref, xp_ref, *, tiles_per_seq):
    i = pl.program_id(0)
    tm = x_ref.shape[0]

    @pl.when(i % tiles_per_seq == 0)
    def _():
        xp_ref[...] = jnp.zeros((8, QKV_WIDTH), F32)

    xn = _rms(x_ref[...], g_ref[...]).astype(BF16)
    row8 = lax.broadcasted_iota(jnp.int32, (8, PROJ_CONV_COLS), 0)
    for c0 in range(0, QKV_WIDTH, PROJ_CONV_COLS):
        cb = slice(c0, c0 + PROJ_CONV_COLS)
        raw = _dot(xn, wqkv_ref[:, cb])
        halo = xp_ref[:, cb]
        cw = cw_ref[:, cb]
        y = raw * cw[CONV_W - 1:CONV_W]
        for t in range(CONV_W - 1):
            k = CONV_W - 1 - t
            back = pltpu.roll(raw, k, axis=0)
            head = jnp.where(row8 < k, pltpu.roll(halo, k, axis=0), back[0:8])
            y = y + jnp.concatenate([head, back[8:]], axis=0) * cw[t:t + 1]
        y_ref[:, cb] = _silu(y)
        xp_ref[:, cb] = raw[tm - 8:tm]
    z_ref[...] = _dot(xn, wz_ref[...])
    ba_ref[...] = _dot(xn, wba_ref[...])

    @pl.when(i % tiles_per_seq == tiles_per_seq - 1)
    def _():
        cs_ref[0] = xp_ref[8 - (CONV_W - 1):8, :]


def _gdn_proj_conv(x, g, wqkv, wz, wba, conv_w):
    b, t, d = x.shape
    tm = min(256, t)
    assert t % tm == 0
    tiles_per_seq = t // tm
    m = b * t
    return pl.pallas_call(
        functools.partial(_gdn_proj_conv_kernel, tiles_per_seq=tiles_per_seq),
        grid=(m // tm,),
        in_specs=[pl.BlockSpec((tm, d), lambda i: (i, 0)),
                  _full_spec(g.shape), _full_spec(wqkv.shape), _full_spec(wz.shape),
                  _full_spec(wba.shape), _full_spec(conv_w.shape)],
        out_specs=[pl.BlockSpec((tm, QKV_WIDTH), lambda i: (i, 0)),
                   pl.BlockSpec((tm, Z_WIDTH), lambda i: (i, 0)),
                   pl.BlockSpec((tm, LANES), lambda i: (i, 0)),
                   pl.BlockSpec((1, CONV_W - 1, QKV_WIDTH), lambda i: (i // tiles_per_seq, 0, 0))],
        out_shape=[jax.ShapeDtypeStruct((m, QKV_WIDTH), F32),
                   jax.ShapeDtypeStruct((m, Z_WIDTH), F32),
                   jax.ShapeDtypeStruct((m, LANES), F32),
                   jax.ShapeDtypeStruct((b, CONV_W - 1, QKV_WIDTH), F32)],
        scratch_shapes=[pltpu.VMEM((8, QKV_WIDTH), F32)],
        compiler_params=_params("arbitrary"),
        name="gdn_proj_conv",
    )(x.reshape(m, d), g, wqkv, wz, wba, conv_w)


def _gate_rows(ba, alog_row, dt_row):
    lane = lax.broadcasted_iota(jnp.int32, ba.shape, 1)
    is_a = (lane >= GDN_HEADS) & (lane < 2 * GDN_HEADS)
    g = jnp.where(is_a, -jnp.exp(alog_row) * _softplus(ba + dt_row), 0.0)
    return _sigmoid(ba), g


def _gdn_chunk_kernel(y_ref, z_ref, ba_ref, alog_ref, dt_ref, gn_ref, ya_ref, s_ref):
    c = pl.program_id(1)
    C = GDN_CHUNK
    TB = y_ref.shape[1]
    n_sub = TB // C

    @pl.when(c == 0)
    def _():
        s_ref[...] = jnp.zeros(s_ref.shape, F32)

    y = y_ref[0]

    beta_full, g_full = _gate_rows(ba_ref[0], alog_ref[...], dt_ref[...])
    row = lax.broadcasted_iota(jnp.int32, (C, C), 0)
    col = lax.broadcasted_iota(jnp.int32, (C, C), 1)
    incl = row >= col
    strict = row > col
    rtb = lax.broadcasted_iota(jnp.int32, (TB, TB), 0)
    ctb = lax.broadcasted_iota(jnp.int32, (TB, TB), 1)
    tril = jnp.where((rtb >= ctb) & (rtb // C == ctb // C), 1.0, 0.0).astype(BF16)
    g_col = sum(_dot(tril, part) for part in _split3(g_full))
    r128 = lax.broadcasted_iota(jnp.int32, (LANES, LANES), 0)
    c128 = lax.broadcasted_iota(jnp.int32, (LANES, LANES), 1)
    eye = jnp.where(r128 == c128, 1.0, 0.0).astype(BF16)
    g_rows = sum(_dot_nt(eye, part) for part in _split3(g_col))

    z = z_ref[0]
    gn = gn_ref[...]
    heads = range(GDN_HEADS)
    units = [(n, h) for n in range(n_sub) for h in heads]
    q, k, gc, eg, g_last, kb, rhs_bf, rhs, decay, kq = ({} for _ in range(10))
    for n, h in units:
        r0 = n * C
        qh = y[r0:r0 + C, h * GDN_DK:(h + 1) * GDN_DK]
        kh = y[r0:r0 + C, GDN_HEADS * GDN_DK + h * GDN_DK:GDN_HEADS * GDN_DK + (h + 1) * GDN_DK]
        vh = y[r0:r0 + C,
               2 * GDN_HEADS * GDN_DK + h * GDN_DV:2 * GDN_HEADS * GDN_DK + (h + 1) * GDN_DV]
        u = (n, h)
        q[u] = qh * lax.rsqrt(jnp.sum(qh * qh, axis=-1, keepdims=True) + EPS) * (GDN_DK ** -0.5)
        k[u] = kh * lax.rsqrt(jnp.sum(kh * kh, axis=-1, keepdims=True) + EPS)
        beta = beta_full[r0:r0 + C, h:h + 1]
        gc[u] = g_col[r0:r0 + C, GDN_HEADS + h:GDN_HEADS + h + 1]
        gr = g_rows[GDN_HEADS + h:GDN_HEADS + h + 1, r0:r0 + C]
        decay[u] = jnp.where(incl, jnp.exp(jnp.where(incl, gc[u] - gr, 0.0)), 0.0)
        eg[u] = jnp.exp(gc[u])
        g_last[u] = gc[u][C - 1:C, :]
        kb[u] = k[u] * beta
        rhs[u] = jnp.concatenate([vh * beta, kb[u] * eg[u]], axis=1)
        rhs_bf[u] = rhs[u].astype(BF16)
    for u in units:
        kq[u] = _dot_nt(jnp.concatenate([kb[u], q[u]], axis=0).astype(BF16),
                        k[u].astype(BF16))
    pw = {u: jnp.where(strict, kq[u][:C] * decay[u], 0.0) for u in units}
    aqk = {u: (kq[u][C:] * decay[u]).astype(BF16) for u in units}
    n_acc = {u: -pw[u] for u in units}
    for _ in range(5):
        pw_bf = {u: pw[u].astype(BF16) for u in units}
        pw = {u: _dot(pw_bf[u], pw_bf[u]) for u in units}
        n_acc = {u: n_acc[u] + pw[u] + _dot(n_acc[u].astype(BF16), pw[u].astype(BF16))
                 for u in units}
    sol = {}
    for u in units:
        n_hi, n_lo = _split2(n_acc[u])
        sol[u] = rhs[u] + _dot(n_hi, rhs_bf[u]) + _dot(n_lo, rhs_bf[u])
    wq = {u: jnp.concatenate([sol[u][:, GDN_DV:], q[u] * eg[u]], axis=0).astype(BF16)
          for u in units}
    kd = {u: (k[u] * jnp.exp(g_last[u] - gc[u])).astype(BF16) for u in units}
    s_cur = [s_ref[0, h] for h in heads]
    for n in range(n_sub):
        ws = [_dot(wq[(n, h)], s_cur[h].astype(BF16)) for h in heads]
        u_bf = [(sol[(n, h)][:, :GDN_DV] - ws[h][:C]).astype(BF16) for h in heads]
        o = [ws[h][C:] + _dot(aqk[(n, h)], u_bf[h]) for h in heads]
        s_cur = [jnp.exp(g_last[(n, h)]) * s_cur[h] + _dot_tn(kd[(n, h)], u_bf[h])
                 for h in heads]
        for h in heads:
            zh = z[n * C:(n + 1) * C, h * GDN_DV:(h + 1) * GDN_DV]
            ya_ref[0, n * C:(n + 1) * C, h * GDN_DV:(h + 1) * GDN_DV] = (
                _rms(o[h], gn) * _silu(zh)).astype(BF16)
    for h in heads:
        s_ref[0, h] = s_cur[h]


GDN_CHUNKS_PER_STEP = 2


def _gdn_chunked(y, z, ba, alog_row, dt_row, gdn_norm):
    b, t, _ = y.shape
    C = GDN_CHUNK * GDN_CHUNKS_PER_STEP
    assert t % C == 0
    nc = t // C
    return pl.pallas_call(
        _gdn_chunk_kernel,
        grid=(b, nc),
        in_specs=[pl.BlockSpec((1, C, QKV_WIDTH), lambda i, j: (i, j, 0)),
                  pl.BlockSpec((1, C, Z_WIDTH), lambda i, j: (i, j, 0)),
                  pl.BlockSpec((1, C, LANES), lambda i, j: (i, j, 0)),
                  _full_spec(alog_row.shape), _full_spec(dt_row.shape),
                  _full_spec(gdn_norm.shape)],
        out_specs=[pl.BlockSpec((1, C, Z_WIDTH), lambda i, j: (i, j, 0)),
                   pl.BlockSpec((1, GDN_HEADS, GDN_DK, GDN_DV), lambda i, j: (i, 0, 0, 0))],
        out_shape=[jax.ShapeDtypeStruct((b, t, Z_WIDTH), BF16),
                   jax.ShapeDtypeStruct((b, GDN_HEADS, GDN_DK, GDN_DV), F32)],
        compiler_params=_params("parallel", "arbitrary"),
        name="gdn_chunked",
    )(y, z, ba, alog_row, dt_row, gdn_norm)


def _gdn_step_kernel(qkv_ref, z_ref, ba_ref, cs_ref, s_ref, cw_ref, alog_ref, dt_ref, gn_ref,
                     ya_ref, snew_ref, csnew_ref):
    rowv = qkv_ref[0]
    cs = cs_ref[0]
    cw = cw_ref[...]
    y = rowv * cw[CONV_W - 1:CONV_W]
    for i in range(CONV_W - 1):
        y = y + cs[i:i + 1] * cw[i:i + 1]
    y = _silu(y)
    csnew_ref[0, 0:CONV_W - 2, :] = cs[1:CONV_W - 1]
    csnew_ref[0, CONV_W - 2:CONV_W - 1, :] = rowv

    beta_full, g_full = _gate_rows(ba_ref[0], alog_ref[...], dt_ref[...])
    r128 = lax.broadcasted_iota(jnp.int32, (GDN_DK, GDN_DK), 0)
    c128 = lax.broadcasted_iota(jnp.int32, (GDN_DK, GDN_DK), 1)
    eye = r128 == c128
    z = z_ref[0]
    gn = gn_ref[...]

    def to_col(r):
        return jnp.sum(jnp.where(eye, jnp.broadcast_to(r, (GDN_DK, GDN_DK)), 0.0),
                       axis=1, keepdims=True)

    for h in range(GDN_HEADS):
        q = y[:, h * GDN_DK:(h + 1) * GDN_DK]
        k = y[:, GDN_HEADS * GDN_DK + h * GDN_DK:GDN_HEADS * GDN_DK + (h + 1) * GDN_DK]
        v = y[:, 2 * GDN_HEADS * GDN_DK + h * GDN_DV:2 * GDN_HEADS * GDN_DK + (h + 1) * GDN_DV]
        q = q * lax.rsqrt(jnp.sum(q * q, axis=-1, keepdims=True) + EPS) * (GDN_DK ** -0.5)
        k = k * lax.rsqrt(jnp.sum(k * k, axis=-1, keepdims=True) + EPS)
        beta = beta_full[:, h:h + 1]
        gdec = jnp.exp(g_full[:, GDN_HEADS + h:GDN_HEADS + h + 1])
        k_col = to_col(k)
        q_col = to_col(q)
        s = gdec * s_ref[0, h]
        u = beta * (v - jnp.sum(s * k_col, axis=0, keepdims=True))
        s = s + k_col * u
        snew_ref[0, h] = s
        o = jnp.sum(s * q_col, axis=0, keepdims=True)
        zh = z[:, h * GDN_DV:(h + 1) * GDN_DV]
        ya_ref[0, :, h * GDN_DV:(h + 1) * GDN_DV] = _rms(o, gn) * _silu(zh)


def _gdn_step(qkv, z, ba, conv_state, state, conv_w, alog_row, dt_row, gdn_norm):
    b = qkv.shape[0]
    qkv3 = qkv.reshape(b, 1, QKV_WIDTH)
    z3 = z.reshape(b, 1, Z_WIDTH)
    ba3 = ba.reshape(b, 1, LANES)
    return pl.pallas_call(
        _gdn_step_kernel,
        grid=(b,),
        in_specs=[pl.BlockSpec((1, 1, QKV_WIDTH), lambda i: (i, 0, 0)),
                  pl.BlockSpec((1, 1, Z_WIDTH), lambda i: (i, 0, 0)),
                  pl.BlockSpec((1, 1, LANES), lambda i: (i, 0, 0)),
                  pl.BlockSpec((1, CONV_W - 1, QKV_WIDTH), lambda i: (i, 0, 0)),
                  pl.BlockSpec((1, GDN_HEADS, GDN_DK, GDN_DV), lambda i: (i, 0, 0, 0)),
                  _full_spec(conv_w.shape), _full_spec(alog_row.shape),
                  _full_spec(dt_row.shape), _full_spec(gdn_norm.shape)],
        out_specs=[pl.BlockSpec((1, 1, Z_WIDTH), lambda i: (i, 0, 0)),
                   pl.BlockSpec((1, GDN_HEADS, GDN_DK, GDN_DV), lambda i: (i, 0, 0, 0)),
                   pl.BlockSpec((1, CONV_W - 1, QKV_WIDTH), lambda i: (i, 0, 0))],
        out_shape=[jax.ShapeDtypeStruct((b, 1, Z_WIDTH), F32),
                   jax.ShapeDtypeStruct((b, GDN_HEADS, GDN_DK, GDN_DV), F32),
                   jax.ShapeDtypeStruct((b, CONV_W - 1, QKV_WIDTH), F32)],
        compiler_params=_params("parallel"),
        name="gdn_step",
    )(qkv3, z3, ba3, conv_state, state, conv_w, alog_row, dt_row, gdn_norm)


def _mla_queries_latent(x, cos2, sin2, g_mix, w_qd, w_c, w_kr, g_qa, w_uq, g_kva,
                        g_qn, g_qr, g_kr):
    xn = _rms(x, g_mix).astype(BF16)
    qa = _rms(_dot(xn, w_qd), g_qa).astype(BF16)
    q = _dot(qa, w_uq)
    qn, qr = [], []
    for h in range(MLA_HEADS):
        qn.append(_rms(q[:, h * QK_PAD:h * QK_PAD + NOPE_DIM], g_qn))
        qr.append(_rope(_rms(q[:, h * QK_PAD + NOPE_DIM:(h + 1) * QK_PAD], g_qr, ROPE_DIM),
                        cos2, sin2))
    c = _rms(_dot(xn, w_c), g_kva)
    kr = _rope(_rms(_dot(xn, w_kr), g_kr, ROPE_DIM), cos2, sin2)
    return qn, qr, c, kr


def _mla_pre_kernel(x_ref, cos_ref, sin_ref, gmix_ref, wqd_ref, wc_ref, wkr_ref, gqa_ref,
                    wuq_ref, gkva_ref, gqn_ref, gqr_ref, gkr_ref, wuk_ref, wuv_ref, gkn_ref,
                    q_ref, k_ref, v_ref, lat_ref):
    qn, qr, c, kr = _mla_queries_latent(
        x_ref[0], cos_ref[...], sin_ref[...], gmix_ref[...], wqd_ref[...], wc_ref[...],
        wkr_ref[...], gqa_ref[...], wuq_ref[...], gkva_ref[...], gqn_ref[...], gqr_ref[...],
        gkr_ref[...])
    lat_ref[0, :, 0:KV_LORA] = c
    lat_ref[0, :, KV_LORA:LATENT_WIDTH] = kr[:, 0:ROPE_DIM]
    c_bf = c.astype(BF16)
    kr_bf = kr.astype(BF16)
    kfull = _dot(c_bf, wuk_ref[...])
    v_ref[0, 0] = _dot_nt(wuv_ref[...], c_bf).astype(BF16)
    gkn = gkn_ref[...]
    for h in range(MLA_HEADS):
        q_ref[0, :, h * QK_PAD:h * QK_PAD + NOPE_DIM] = qn[h].astype(BF16)
        q_ref[0, :, h * QK_PAD + NOPE_DIM:(h + 1) * QK_PAD] = qr[h].astype(BF16)
        kn = _rms(kfull[:, h * NOPE_DIM:(h + 1) * NOPE_DIM], gkn)
        k_ref[0, :, h * QK_PAD:h * QK_PAD + NOPE_DIM] = kn.astype(BF16)
        k_ref[0, :, h * QK_PAD + NOPE_DIM:(h + 1) * QK_PAD] = kr_bf


def _mla_pre(x, cos2, sin2, mw):
    b, t, d = x.shape
    tm = min(ATTN_TILE, t)
    consts = [mw["g_mix"], mw["w_qd"], mw["w_c"], mw["w_kr"], mw["g_qa"], mw["w_uq"],
              mw["g_kva"], mw["g_qn"], mw["g_qr"], mw["g_kr"], mw["w_uk"], mw["w_uv_t"],
              mw["g_kn"]]
    return pl.pallas_call(
        _mla_pre_kernel,
        grid=(b, t // tm),
        in_specs=[pl.BlockSpec((1, tm, d), lambda i, j: (i, j, 0)),
                  pl.BlockSpec((tm, LANES), lambda i, j: (j, 0)),
                  pl.BlockSpec((tm, LANES), lambda i, j: (j, 0))]
                 + [_full_spec(a.shape) for a in consts],
        out_specs=[pl.BlockSpec((1, tm, MLA_HEADS * QK_PAD), lambda i, j: (i, j, 0)),
                   pl.BlockSpec((1, tm, MLA_HEADS * QK_PAD), lambda i, j: (i, j, 0)),
                   pl.BlockSpec((1, 1, MLA_HEADS * V_DIM, tm), lambda i, j: (i, j, 0, 0)),
                   pl.BlockSpec((1, tm, LATENT_WIDTH), lambda i, j: (i, j, 0))],
        out_shape=[jax.ShapeDtypeStruct((b, t, MLA_HEADS * QK_PAD), BF16),
                   jax.ShapeDtypeStruct((b, t, MLA_HEADS * QK_PAD), BF16),
                   jax.ShapeDtypeStruct((b, t // tm, MLA_HEADS * V_DIM, tm), BF16),
                   jax.ShapeDtypeStruct((b, t, LATENT_WIDTH), F32)],
        compiler_params=_params("parallel", "parallel"),
        name="mla_pre",
    )(x, cos2, sin2, *consts)


def _mla_pre_sample_kernel(x_ref, cos_ref, sin_ref, gmix_ref, wqd_ref, wc_ref, wkr_ref, gqa_ref,
                           wuq_ref, gkva_ref, gqn_ref, gqr_ref, gkr_ref, wuk_ref, gkn_ref,
                           qa_ref, qr_ref, lat_ref):
    qn, qr, c, kr = _mla_queries_latent(
        x_ref[...], cos_ref[...], sin_ref[...], gmix_ref[...], wqd_ref[...], wc_ref[...],
        wkr_ref[...], gqa_ref[...], wuq_ref[...], gkva_ref[...], gqn_ref[...], gqr_ref[...],
        gkr_ref[...])
    lat_ref[:, 0:KV_LORA] = c
    lat_ref[:, KV_LORA:LATENT_WIDTH] = kr[:, 0:ROPE_DIM]
    gkn = gkn_ref[...]
    for h in range(MLA_HEADS):
        hi, lo = _split2(qn[h] * gkn)
        w_h = wuk_ref[:, h * NOPE_DIM:(h + 1) * NOPE_DIM]
        qa_ref[h] = _dot_nt(hi, w_h) + _dot_nt(lo, w_h)
        qr_ref[h] = qr[h]


def _mla_pre_sample(x2, cos2, sin2, mw):
    m = x2.shape[0]
    consts = [mw["g_mix"], mw["w_qd"], mw["w_c"], mw["w_kr"], mw["g_qa"], mw["w_uq"],
              mw["g_kva"], mw["g_qn"], mw["g_qr"], mw["g_kr"], mw["w_uk"], mw["g_kn"]]
    args = [x2, cos2, sin2] + consts
    return pl.pallas_call(
        _mla_pre_sample_kernel,
        grid=(1,),
        in_specs=[_full_spec(a.shape) for a in args],
        out_specs=[_full_spec((MLA_HEADS, m, KV_LORA)), _full_spec((MLA_HEADS, m, LANES)),
                   _full_spec((m, LATENT_WIDTH))],
        out_shape=[jax.ShapeDtypeStruct((MLA_HEADS, m, KV_LORA), F32),
                   jax.ShapeDtypeStruct((MLA_HEADS, m, LANES), F32),
                   jax.ShapeDtypeStruct((m, LATENT_WIDTH), F32)],
        compiler_params=_params("arbitrary"),
        name="mla_pre_sample",
    )(*args)


ATTN_HEADS_PER_STEP = 8


def _attn_kernel(q_ref, k_ref, vt_ref, o_ref, *, tq):
    qi = pl.program_id(2)
    hs = range(ATTN_HEADS_PER_STEP)
    q = [q_ref[0, :, g * QK_PAD:(g + 1) * QK_PAD] for g in hs]
    key = lax.broadcasted_iota(jnp.int32, (tq, tq), 0)
    qry = lax.broadcasted_iota(jnp.int32, (tq, tq), 1)

    def raw_scores(j):
        start = pl.multiple_of(j * tq, tq)
        return tuple(_dot_nt(k_ref[0, pl.ds(start, tq), g * QK_PAD:(g + 1) * QK_PAD], q[g])
                     for g in hs)

    def consume(j, s_raw, stats, diagonal):
        vt = [vt_ref[0, j, g * V_DIM:(g + 1) * V_DIM, :] for g in hs]
        s = [s_raw[g] * (MLA_SCALE * LOG2_E) for g in hs]
        if diagonal:
            s = [jnp.where(key <= qry, s[g], -jnp.inf) for g in hs]
        m_new = [jnp.maximum(stats[g][0], jnp.max(s[g], axis=0, keepdims=True)) for g in hs]
        p = [jnp.exp2(s[g] - m_new[g]) for g in hs]
        pv = [_dot(vt[g], p[g].astype(BF16)) for g in hs]
        out = []
        for g in hs:
            m, l, acc = stats[g]
            alpha = jnp.exp2(m - m_new[g])
            out.append((m_new[g], alpha * l + jnp.sum(p[g], axis=0, keepdims=True),
                        alpha * acc + pv[g]))
        return tuple(out)

    init = tuple((jnp.full((1, tq), -jnp.inf, F32), jnp.zeros((1, tq), F32),
                  jnp.zeros((V_DIM, tq), F32)) for _ in hs)
    stats = lax.fori_loop(0, qi, lambda j, st: consume(j, raw_scores(j), st, False), init)
    stats = consume(qi, raw_scores(qi), stats, True)
    for g in hs:
        _, l, acc = stats[g]
        o_ref[0, :, g * V_DIM:(g + 1) * V_DIM] = (acc / l).T.astype(BF16)


def _mla_prompt_attention(q_cat, k_cat, v_t):
    b, t, _ = q_cat.shape
    tq = min(ATTN_TILE, t)
    hps = ATTN_HEADS_PER_STEP
    return pl.pallas_call(
        functools.partial(_attn_kernel, tq=tq),
        grid=(b, MLA_HEADS // hps, t // tq),
        in_specs=[pl.BlockSpec((1, tq, hps * QK_PAD), lambda i, h, j: (i, j, h)),
                  pl.BlockSpec((1, t, hps * QK_PAD), lambda i, h, j: (i, 0, h)),
                  pl.BlockSpec((1, t // tq, hps * V_DIM, tq), lambda i, h, j: (i, 0, h, 0))],
        out_specs=pl.BlockSpec((1, tq, hps * V_DIM), lambda i, h, j: (i, j, h)),
        out_shape=jax.ShapeDtypeStruct((b, t, MLA_HEADS * V_DIM), BF16),
        compiler_params=_params("parallel", "parallel", "arbitrary"),
        name="mla_prompt_attention",
    )(q_cat, k_cat, v_t)


def _dec_attn_kernel(pt_ref, *refs, pps):
    del pt_ref
    page_refs = refs[:pps]
    (wukt_ref, qa_ref, qr_ref, lnew_ref, ctx_ref,
     lhs_ref, qr16_ref, m_ref, l_ref, acc_ref, cprev_ref, pprev_ref) = refs[pps:]
    b = pl.program_id(0)
    j = pl.program_id(1)
    nj = pl.num_programs(1)
    n_k = MLA_HEADS * NOPE_DIM

    @pl.when((b == 0) & (j == 0))
    def _():
        lhs_ref[0:n_k, :] = wukt_ref[...]
        cprev_ref[...] = jnp.zeros(cprev_ref.shape, BF16)

    @pl.when(j == 0)
    def _():
        qa = jnp.concatenate([qa_ref[h, 0] for h in range(MLA_HEADS)]
                             + [jnp.zeros((8, KV_LORA), F32)], axis=0)
        lhs_ref[n_k:n_k + 16, :] = qa.astype(BF16)
        qr = jnp.concatenate([qr_ref[h, 0] for h in range(MLA_HEADS)]
                             + [jnp.zeros((8, LANES), F32)], axis=0)
        qr16_ref[...] = qr.astype(BF16)
        m_ref[...] = jnp.full(m_ref.shape, -jnp.inf, F32)
        l_ref[...] = jnp.zeros(l_ref.shape, F32)
        acc_ref[...] = jnp.zeros(acc_ref.shape, F32)
        pprev_ref[...] = jnp.zeros(pprev_ref.shape, F32)

    def scores(res, sr):
        kt = res[0:n_k]
        ss = jnp.sum((kt * kt).reshape(MLA_HEADS, NOPE_DIM, res.shape[1]), axis=1)
        sn = res[n_k:n_k + MLA_HEADS]
        return (sn * lax.rsqrt(ss * (1.0 / NOPE_DIM) + EPS) + sr[0:MLA_HEADS]) * MLA_SCALE

    n_tile = pps // 2
    n_tok = 2 * page_refs[0].shape[1]
    slot = j % 2
    p_prev = pprev_ref[...].astype(BF16)
    pv = jnp.zeros((MLA_HEADS, KV_LORA), F32)
    s_parts = []
    for t in range(n_tile):
        tile = jnp.concatenate([page_refs[2 * t][...], page_refs[2 * t + 1][...]], axis=1)
        c_bf = tile[0:KV_LORA].astype(BF16)
        kr_bf = tile[KV_LORA:LATENT_WIDTH].astype(BF16)
        res = _dot(lhs_ref[...], c_bf)
        pv = pv + _dot_nt(p_prev[:, t * n_tok:(t + 1) * n_tok], cprev_ref[1 - slot, t])
        cprev_ref[slot, t] = c_bf
        s_parts.append(scores(res, _dot(qr16_ref[:, 0:ROPE_DIM], kr_bf)))
    s = jnp.concatenate(s_parts, axis=1)
    m_old = m_ref[...]
    m_new = jnp.maximum(m_old, jnp.max(s, axis=-1, keepdims=True))
    alpha = jnp.exp(m_old - m_new)
    p = jnp.exp(s - m_new)
    l_ref[...] = alpha * l_ref[...] + jnp.sum(p, axis=-1, keepdims=True)
    acc_ref[...] = alpha * (acc_ref[...] + pv)
    pprev_ref[...] = p
    m_ref[...] = m_new

    @pl.when(j == nj - 1)
    def _():
        p_bf = p.astype(BF16)
        acc = acc_ref[...]
        for t in range(n_tile):
            acc = acc + _dot_nt(p_bf[:, t * n_tok:(t + 1) * n_tok], cprev_ref[slot, t])
        acc_ref[...] = acc
        ln = lnew_ref[0]
        c_new = ln[:, 0:KV_LORA]
        c8 = jnp.broadcast_to(c_new, (8, KV_LORA)).astype(BF16)
        kr8 = jnp.broadcast_to(ln[:, KV_LORA:LATENT_WIDTH], (8, ROPE_DIM)).astype(BF16)
        s_new = scores(_dot_nt(lhs_ref[...], c8),
                       _dot_nt(qr16_ref[:, 0:ROPE_DIM], kr8))[:, 0:1]
        m_old = m_ref[...]
        m_new = jnp.maximum(m_old, s_new)
        alpha = jnp.exp(m_old - m_new)
        p_new = jnp.exp(s_new - m_new)
        ctx = (alpha * acc_ref[...] + p_new * c_new) / (alpha * l_ref[...] + p_new)
        for h in range(MLA_HEADS):
            ctx_ref[h, 0] = ctx[h:h + 1]


def _mla_decode_attention(cache, layer, page_table, wuk_t, qa, qr, lat_new):
    bsz, n_pages = page_table.shape
    page = cache.shape[3]
    pps = DEC_PAGES_PER_STEP
    while n_pages % pps:
        pps //= 2
    assert pps >= 2
    qa4 = qa.reshape(MLA_HEADS, bsz, 1, KV_LORA)
    qr4 = qr.reshape(MLA_HEADS, bsz, 1, LANES)
    ln3 = lat_new.reshape(bsz, 1, LATENT_WIDTH)

    def page_spec(i):
        return pl.BlockSpec((pl.Squeezed(), pl.Squeezed(), LATENT_WIDTH, page),
                            lambda b, j, pt: (layer, pt[b, j * pps + i], 0, 0))

    grid_spec = pltpu.PrefetchScalarGridSpec(
        num_scalar_prefetch=1,
        grid=(bsz, n_pages // pps),
        in_specs=[page_spec(i) for i in range(pps)] + [
            pl.BlockSpec(wuk_t.shape, lambda b, j, pt: (0, 0)),
            pl.BlockSpec((MLA_HEADS, 1, 1, KV_LORA), lambda b, j, pt: (0, b, 0, 0)),
            pl.BlockSpec((MLA_HEADS, 1, 1, LANES), lambda b, j, pt: (0, b, 0, 0)),
            pl.BlockSpec((1, 1, LATENT_WIDTH), lambda b, j, pt: (b, 0, 0))],
        out_specs=pl.BlockSpec((MLA_HEADS, 1, 1, KV_LORA), lambda b, j, pt: (0, b, 0, 0)),
        scratch_shapes=[pltpu.VMEM((MLA_HEADS * NOPE_DIM + 16, KV_LORA), BF16),
                        pltpu.VMEM((16, LANES), BF16),
                        pltpu.VMEM((MLA_HEADS, 1), F32),
                        pltpu.VMEM((MLA_HEADS, 1), F32),
                        pltpu.VMEM((MLA_HEADS, KV_LORA), F32),
                        pltpu.VMEM((2, pps // 2, KV_LORA, 2 * page), BF16),
                        pltpu.VMEM((MLA_HEADS, pps * page), F32)])
    ctx = pl.pallas_call(
        functools.partial(_dec_attn_kernel, pps=pps),
        grid_spec=grid_spec,
        out_shape=jax.ShapeDtypeStruct((MLA_HEADS, bsz, 1, KV_LORA), F32),
        compiler_params=_params("arbitrary", "arbitrary"),
        name="mla_decode_attention",
    )(page_table, *([cache] * pps), wuk_t, qa4, qr4, ln3)
    return ctx.reshape(MLA_HEADS, bsz, KV_LORA)


def _dec_post_kernel(ctx_ref, wuv_ref, y_ref):
    for h in range(MLA_HEADS):
        hi, lo = _split2(ctx_ref[h])
        w_h = wuv_ref[:, h * V_DIM:(h + 1) * V_DIM]
        y_ref[:, h * V_DIM:(h + 1) * V_DIM] = (_dot(hi, w_h) + _dot(lo, w_h)).astype(BF16)


def _dec_post(ctx, w_uv):
    bsz = ctx.shape[1]
    return pl.pallas_call(
        _dec_post_kernel,
        grid=(1,),
        in_specs=[_full_spec(ctx.shape), _full_spec(w_uv.shape)],
        out_specs=_full_spec((bsz, MLA_HEADS * V_DIM)),
        out_shape=jax.ShapeDtypeStruct((bsz, MLA_HEADS * V_DIM), BF16),
        compiler_params=_params("arbitrary"),
        name="mla_decode_values",
    )(ctx, w_uv)


def _merge_mlp_kernel(x_ref, ya_ref, yb_ref, gmix_ref, wg_ref, wya_ref, wyb_ref, wo_ref,
                      gmlp_ref, wup_ref, wdn_ref, y_ref):
    tm, d = x_ref.shape
    d_ff = wup_ref.shape[1]
    n_grp = max(1, tm // MERGE_ROWS_PER_GROUP)
    rows = tm // n_grp
    grp = range(n_grp)
    sl = [slice(r * rows, (r + 1) * rows) for r in grp]
    x = [x_ref[sl[r], :] for r in grp]
    xn = [_rms(x[r], gmix_ref[...]).astype(BF16) for r in grp]
    gates = [_sigmoid(_dot(xn[r], wg_ref[...])) for r in grp]
    pa = [_dot(ya_ref[sl[r], :].astype(BF16), wya_ref[...]) for r in grp]
    pb = [_dot(yb_ref[sl[r], :].astype(BF16), wyb_ref[...]) for r in grp]
    mix = [(gates[r][:, :d] * pa[r] + gates[r][:, d:] * pb[r]).astype(BF16) for r in grp]
    x1 = [x[r] + _dot(mix[r], wo_ref[...]) for r in grp]
    hin = [_rms(x1[r], gmlp_ref[...]).astype(BF16) for r in grp]
    y = x1
    for f in range(0, d_ff, MLP_FF_BLOCK):
        hmid = [jnp.maximum(_dot(hin[r], wup_ref[:, f:f + MLP_FF_BLOCK]), 0.0) for r in grp]
        y = [y[r] + _dot((hmid[r] * hmid[r]).astype(BF16), wdn_ref[f:f + MLP_FF_BLOCK, :])
             for r in grp]
    for r in grp:
        y_ref[sl[r], :] = y[r]


MERGE_ROWS_PER_GROUP = 256
MLP_FF_BLOCK = 1024


def _merge_mlp(x2, ya, yb, ew):
    m, d = x2.shape
    tm = min(512, m)
    consts = [ew["g_mix"], ew["w_gates"], ew["w_ya"], ew["w_yb"], ew["w_o"], ew["g_mlp"],
              ew["w_up"], ew["w_down"]]

    def const_spec(a):
        nd = a.ndim
        return pl.BlockSpec(a.shape, lambda i: (0,) * nd, pipeline_mode=pl.Buffered(1))

    return pl.pallas_call(
        _merge_mlp_kernel,
        grid=(m // tm,),
        in_specs=[pl.BlockSpec((tm, d), lambda i: (i, 0)),
                  pl.BlockSpec((tm, ya.shape[1]), lambda i: (i, 0)),
                  pl.BlockSpec((tm, yb.shape[1]), lambda i: (i, 0))]
                 + [const_spec(a) for a in consts],
        out_specs=pl.BlockSpec((tm, d), lambda i: (i, 0)),
        out_shape=jax.ShapeDtypeStruct((m, d), F32),
        compiler_params=_params("parallel"),
        name="merge_mlp",
    )(x2, ya, yb, *consts)


def _rope_tables(pos):
    inv = jnp.power(ROPE_THETA, -jnp.arange(0, ROPE_DIM, 2, dtype=F32) / ROPE_DIM)
    ang = pos.astype(F32)[:, None] * inv[None, :]
    cos, sin = jnp.cos(ang), jnp.sin(ang)
    zeros = jnp.zeros((pos.shape[0], LANES - ROPE_DIM), F32)
    return (jnp.concatenate([cos, cos, zeros], axis=1),
            jnp.concatenate([-sin, sin, zeros], axis=1))


def _pad_lanes(a, width=LANES, offset=0):
    return jnp.zeros((1, width), F32).at[0, offset:offset + a.shape[0]].set(a.astype(F32))


def _layer_weights(l, norm_mix, w_in, conv_w, a_log, dt_bias, gdn_norm, q_a_norm, w_uq,
                   kv_a_norm, w_uk, w_uv, q_norm_nope, q_norm_rope, k_norm_nope, k_norm_rope,
                   w_ya, w_yb, w_o, norm_mlp, w_up, w_down):
    d = w_in.shape[1]
    off_z = QKV_WIDTH
    off_b = off_z + Z_WIDTH
    off_qd = off_b + 2 * GDN_HEADS
    off_kv = off_qd + Q_LORA
    off_g = off_kv + LATENT_WIDTH
    wi = w_in[l]
    g_mix = norm_mix[l].reshape(1, d)
    gw = {
        "g_mix": g_mix,
        "w_qkv": wi[:, :off_z].astype(BF16),
        "w_z": wi[:, off_z:off_b].astype(BF16),
        "w_ba": jnp.pad(wi[:, off_b:off_qd], ((0, 0), (0, LANES - 2 * GDN_HEADS))).astype(BF16),
        "conv_w": conv_w[l],
        "alog_row": _pad_lanes(a_log[l], offset=GDN_HEADS),
        "dt_row": _pad_lanes(dt_bias[l], offset=GDN_HEADS),
        "gdn_norm": gdn_norm[l].reshape(1, GDN_DV),
    }
    wq = w_uq[l].reshape(Q_LORA, MLA_HEADS, NOPE_DIM + ROPE_DIM)
    wq = jnp.pad(wq, ((0, 0), (0, 0), (0, QK_PAD - NOPE_DIM - ROPE_DIM)))
    mw = {
        "g_mix": g_mix,
        "w_qd": wi[:, off_qd:off_kv].astype(BF16),
        "w_c": wi[:, off_kv:off_kv + KV_LORA].astype(BF16),
        "w_kr": jnp.pad(wi[:, off_kv + KV_LORA:off_g], ((0, 0), (0, LANES - ROPE_DIM))).astype(BF16),
        "g_qa": q_a_norm[l].reshape(1, Q_LORA),
        "w_uq": wq.reshape(Q_LORA, MLA_HEADS * QK_PAD).astype(BF16),
        "g_kva": kv_a_norm[l].reshape(1, KV_LORA),
        "g_qn": q_norm_nope[l].reshape(1, NOPE_DIM),
        "g_qr": _pad_lanes(q_norm_rope[l]),
        "g_kr": _pad_lanes(k_norm_rope[l]),
        "g_kn": k_norm_nope[l].reshape(1, NOPE_DIM),
        "w_uk": w_uk[l].astype(BF16),
        "w_uk_t": w_uk[l].T.astype(BF16),
        "w_uv": w_uv[l].astype(BF16),
        "w_uv_t": w_uv[l].T.astype(BF16),
    }
    ew = {
        "g_mix": g_mix,
        "w_gates": wi[:, off_g:].astype(BF16),
        "w_ya": w_ya[l].astype(BF16),
        "w_yb": w_yb[l].astype(BF16),
        "w_o": w_o[l].astype(BF16),
        "g_mlp": norm_mlp[l].reshape(1, d),
        "w_up": w_up[l].astype(BF16),
        "w_down": w_down[l].astype(BF16),
    }
    return gw, mw, ew


def kernel(x_prompt, x_sample, cache_mla, state_gdn, state_conv, page_table, norm_mix, w_in,
           conv_w, a_log, dt_bias, gdn_norm, q_a_norm, w_uq, kv_a_norm, w_uk, w_uv,
           q_norm_nope, q_norm_rope, k_norm_nope, k_norm_rope, w_ya, w_yb, w_o, norm_mlp,
           w_up, w_down):
    depth = w_in.shape[0]
    bp, tp, d = x_prompt.shape
    bs, ts, _ = x_sample.shape
    assert ts == 1 and tp % GDN_CHUNK == 0
    past_len = page_table.shape[1] * cache_mla.shape[2]
    cache_t = jnp.swapaxes(cache_mla, 2, 3)
    cos_p, sin_p = _rope_tables(jnp.arange(tp))
    cos_s, sin_s = _rope_tables(past_len + jnp.zeros((bs,), jnp.int32))

    x_p, x_s = x_prompt, x_sample.reshape(bs, d)
    rows_p, gdn_p, conv_p, rows_s, gdn_s, conv_s = [], [], [], [], [], []
    for l in range(depth):
        gw, mw, ew = _layer_weights(
            l, norm_mix, w_in, conv_w, a_log, dt_bias, gdn_norm, q_a_norm, w_uq, kv_a_norm,
            w_uk, w_uv, q_norm_nope, q_norm_rope, k_norm_nope, k_norm_rope, w_ya, w_yb, w_o,
            norm_mlp, w_up, w_down)
        gdn_consts = (gw["conv_w"], gw["alog_row"], gw["dt_row"], gw["gdn_norm"])

        xp2 = x_p.reshape(bp * tp, d)
        yc, z, ba, cs_p = _gdn_proj_conv(x_p, gw["g_mix"], gw["w_qkv"], gw["w_z"], gw["w_ba"],
                                         gw["conv_w"])
        ya_p, s_p = _gdn_chunked(yc.reshape(bp, tp, QKV_WIDTH), z.reshape(bp, tp, Z_WIDTH),
                                 ba.reshape(bp, tp, LANES), *gdn_consts[1:])
        q_cat, k_cat, v_p, lat_p = _mla_pre(x_p, cos_p, sin_p, mw)
        yb_p = _mla_prompt_attention(q_cat, k_cat, v_p)
        y_p = _merge_mlp(xp2, ya_p.reshape(bp * tp, Z_WIDTH), yb_p.reshape(bp * tp, -1), ew)

        qkv, z, ba = _gdn_proj(x_s, gw["g_mix"], gw["w_qkv"], gw["w_z"], gw["w_ba"])
        ya_s, s_s, cs_s = _gdn_step(qkv, z, ba, state_conv[l], state_gdn[l], *gdn_consts)
        qa, qr, lat_s = _mla_pre_sample(x_s, cos_s, sin_s, mw)
        ctx = _mla_decode_attention(cache_t, l, page_table, mw["w_uk_t"], qa, qr, lat_s)
        yb_s = _dec_post(ctx, mw["w_uv"])
        y_s = _merge_mlp(x_s, ya_s.reshape(bs, Z_WIDTH), yb_s, ew)

        x_p, x_s = y_p.reshape(bp, tp, d), y_s
        rows_p.append(lat_p)
        gdn_p.append(s_p)
        conv_p.append(cs_p)
        rows_s.append(lat_s.reshape(bs, 1, LATENT_WIDTH))
        gdn_s.append(s_s)
        conv_s.append(cs_s)
    return (x_p, x_s.reshape(bs, 1, d), jnp.stack(rows_p), jnp.stack(gdn_p), jnp.stack(conv_p),
            jnp.stack(rows_s), jnp.stack(gdn_s), jnp.stack(conv_s))
```

```python
import functools

import jax
import jax.numpy as jnp
from jax import lax
from jax.experimental import pallas as pl
from jax.experimental.pallas import tpu as pltpu

F32 = jnp.float32
BF16 = jnp.bfloat16

EPS = 1e-6
GDN_HEADS = 8
GDN_DK = 128
GDN_DV = 128
CONV_W = 4
GDN_CHUNK = 64
MLA_HEADS = 8
Q_LORA = 512
KV_LORA = 512
NOPE_DIM = 128
ROPE_DIM = 64
V_DIM = 128
ROPE_THETA = 10000.0
MLA_SCALE = (NOPE_DIM + ROPE_DIM) ** -0.5
LOG2_E = 1.4426950408889634
ATTN_Q_SCALE = MLA_SCALE * LOG2_E
LATENT_WIDTH = KV_LORA + ROPE_DIM
QKV_WIDTH = GDN_HEADS * (2 * GDN_DK + GDN_DV)
Z_WIDTH = GDN_HEADS * GDN_DV

LANES = 128
QK_PAD = 256
VMEM_LIMIT = 56 * 1024 * 1024
DEC_PAGES_PER_STEP = 32
ATTN_TILE = 256


def _dot(a, b):
    return jnp.dot(a, b, preferred_element_type=F32)


def _dot_nt(a, b):
    return lax.dot_general(a, b, (((1,), (1,)), ((), ())), preferred_element_type=F32)


def _dot_tn(a, b):
    return lax.dot_general(a, b, (((0,), (0,)), ((), ())), preferred_element_type=F32)


def _split2(x):
    hi = x.astype(BF16)
    lo = (x - hi.astype(F32)).astype(BF16)
    return hi, lo


def _split3(x):
    hi = x.astype(BF16)
    r = x - hi.astype(F32)
    mid = r.astype(BF16)
    lo = (r - mid.astype(F32)).astype(BF16)
    return hi, mid, lo


def _rms(x, g, n=None):
    n = x.shape[-1] if n is None else n
    ss = jnp.sum(x * x, axis=-1, keepdims=True) * (1.0 / n)
    return x * lax.rsqrt(ss + EPS) * g


def _sigmoid(x):
    return 1.0 / (1.0 + jnp.exp(-x))


def _silu(x):
    h = 0.5 * x
    return h + h * jnp.tanh(h)


def _softplus(x):
    return jnp.maximum(x, 0.0) + jnp.log1p(jnp.exp(-jnp.abs(x)))


def _rope(x, cos2, sin2):
    lane = lax.broadcasted_iota(jnp.int32, x.shape, 1)
    fwd = pltpu.roll(x, LANES - ROPE_DIM // 2, axis=1)
    bwd = pltpu.roll(x, ROPE_DIM // 2, axis=1)
    swapped = jnp.where(lane < ROPE_DIM // 2, fwd, bwd)
    return x * cos2 + swapped * sin2


def _full_spec(shape):
    nd = len(shape)
    return pl.BlockSpec(shape, lambda *_: (0,) * nd)


def _params(*sem):
    return pltpu.CompilerParams(dimension_semantics=sem, vmem_limit_bytes=VMEM_LIMIT)


def _gdn_proj_kernel(x_ref, g_ref, wqkv_ref, wz_ref, wba_ref, qkv_ref, z_ref, ba_ref):
    xn = _rms(x_ref[...], g_ref[...]).astype(BF16)
    qkv_ref[...] = _dot(xn, wqkv_ref[...])
    z_ref[...] = _dot(xn, wz_ref[...])
    ba_ref[...] = _dot(xn, wba_ref[...])


def _gdn_proj(x2, g, wqkv, wz, wba):
    m, d = x2.shape
    tm = min(256, m)
    return pl.pallas_call(
        _gdn_proj_kernel,
        grid=(m // tm,),
        in_specs=[pl.BlockSpec((tm, d), lambda i: (i, 0)),
                  _full_spec(g.shape), _full_spec(wqkv.shape), _full_spec(wz.shape),
                  _full_spec(wba.shape)],
        out_specs=[pl.BlockSpec((tm, QKV_WIDTH), lambda i: (i, 0)),
                   pl.BlockSpec((tm, Z_WIDTH), lambda i: (i, 0)),
                   pl.BlockSpec((tm, LANES), lambda i: (i, 0))],
        out_shape=[jax.ShapeDtypeStruct((m, QKV_WIDTH), F32),
                   jax.ShapeDtypeStruct((m, Z_WIDTH), F32),
                   jax.ShapeDtypeStruct((m, LANES), F32)],
        compiler_params=_params("parallel"),
        name="gdn_proj",
    )(x2, g, wqkv, wz, wba)


PROJ_CONV_COLS = 512


def _gdn_proj_conv_kernel(x_ref, g_ref, wqkv_ref, wz_ref, wba_ref, cw_ref,
                          y_ref, z_ref, ba_ref, cs_ref, xp_ref, *, tiles_per_seq):
    i = pl.program_id(0)
    tm = x_ref.shape[0]

    @pl.when(i % tiles_per_seq == 0)
    def _():
        xp_ref[...] = jnp.zeros((8, QKV_WIDTH), F32)

    xn = _rms(x_ref[...], g_ref[...]).astype(BF16)
    row8 = lax.broadcasted_iota(jnp.int32, (8, PROJ_CONV_COLS), 0)
    for c0 in range(0, QKV_WIDTH, PROJ_CONV_COLS):
        cb = slice(c0, c0 + PROJ_CONV_COLS)
        raw = _dot(xn, wqkv_ref[:, cb])
        halo = xp_ref[:, cb]
        cw = cw_ref[:, cb]
        y = raw * cw[CONV_W - 1:CONV_W]
        for t in range(CONV_W - 1):
            k = CONV_W - 1 - t
            back = pltpu.roll(raw, k, axis=0)
            head = jnp.where(row8 < k, pltpu.roll(halo, k, axis=0), back[0:8])
            y = y + jnp.concatenate([head, back[8:]], axis=0) * cw[t:t + 1]
        y_ref[:, cb] = _silu(y)
        xp_ref[:, cb] = raw[tm - 8:tm]
    z_ref[...] = _dot(xn, wz_ref[...])
    ba_ref[...] = _dot(xn, wba_ref[...])

    @pl.when(i % tiles_per_seq == tiles_per_seq - 1)
    def _():
        cs_ref[0] = xp_ref[8 - (CONV_W - 1):8, :]


def _gdn_proj_conv(x, g, wqkv, wz, wba, conv_w):
    b, t, d = x.shape
    tm = min(256, t)
    assert t % tm == 0
    tiles_per_seq = t // tm
    m = b * t
    return pl.pallas_call(
        functools.partial(_gdn_proj_conv_kernel, tiles_per_seq=tiles_per_seq),
        grid=(m // tm,),
        in_specs=[pl.BlockSpec((tm, d), lambda i: (i, 0)),
                  _full_spec(g.shape), _full_spec(wqkv.shape), _full_spec(wz.shape),
                  _full_spec(wba.shape), _full_spec(conv_w.shape)],
        out_specs=[pl.BlockSpec((tm, QKV_WIDTH), lambda i: (i, 0)),
                   pl.BlockSpec((tm, Z_WIDTH), lambda i: (i, 0)),
                   pl.BlockSpec((tm, LANES), lambda i: (i, 0)),
                   pl.BlockSpec((1, CONV_W - 1, QKV_WIDTH), lambda i: (i // tiles_per_seq, 0, 0))],
        out_shape=[jax.ShapeDtypeStruct((m, QKV_WIDTH), F32),
                   jax.ShapeDtypeStruct((m, Z_WIDTH), F32),
                   jax.ShapeDtypeStruct((m, LANES), F32),
                   jax.ShapeDtypeStruct((b, CONV_W - 1, QKV_WIDTH), F32)],
        scratch_shapes=[pltpu.VMEM((8, QKV_WIDTH), F32)],
        compiler_params=_params("arbitrary"),
        name="gdn_proj_conv",
    )(x.reshape(m, d), g, wqkv, wz, wba, conv_w)


def _gate_rows(ba, alog_row, dt_row):
    lane = lax.broadcasted_iota(jnp.int32, ba.shape, 1)
    is_a = (lane >= GDN_HEADS) & (lane < 2 * GDN_HEADS)
    g = jnp.where(is_a, -jnp.exp(alog_row) * _softplus(ba + dt_row), 0.0)
    return _sigmoid(ba), g


def _gdn_chunk_kernel(y_ref, z_ref, ba_ref, alog_ref, dt_ref, gn_ref, ya_ref, s_ref):
    c = pl.program_id(1)
    C = GDN_CHUNK
    TB = y_ref.shape[1]
    n_sub = TB // C

    @pl.when(c == 0)
    def _():
        s_ref[...] = jnp.zeros(s_ref.shape, F32)

    y = y_ref[0]

    beta_full, g_full = _gate_rows(ba_ref[0], alog_ref[...], dt_ref[...])
    row = lax.broadcasted_iota(jnp.int32, (C, C), 0)
    col = lax.broadcasted_iota(jnp.int32, (C, C), 1)
    incl = row >= col
    strict = row > col
    rtb = lax.broadcasted_iota(jnp.int32, (TB, TB), 0)
    ctb = lax.broadcasted_iota(jnp.int32, (TB, TB), 1)
    tril = jnp.where((rtb >= ctb) & (rtb // C == ctb // C), 1.0, 0.0).astype(BF16)
    g_col = sum(_dot(tril, part) for part in _split3(g_full))
    r128 = lax.broadcasted_iota(jnp.int32, (LANES, LANES), 0)
    c128 = lax.broadcasted_iota(jnp.int32, (LANES, LANES), 1)
    eye = jnp.where(r128 == c128, 1.0, 0.0).astype(BF16)
    g_rows = sum(_dot_nt(eye, part) for part in _split3(g_col))

    z = z_ref[0]
    gn = gn_ref[...]
    heads = range(GDN_HEADS)
    units = [(n, h) for n in range(n_sub) for h in heads]
    q, k, gc, eg, g_last, kb, rhs_bf, rhs, decay, kq = ({} for _ in range(10))
    for n, h in units:
        r0 = n * C
        qh = y[r0:r0 + C, h * GDN_DK:(h + 1) * GDN_DK]
        kh = y[r0:r0 + C, GDN_HEADS * GDN_DK + h * GDN_DK:GDN_HEADS * GDN_DK + (h + 1) * GDN_DK]
        vh = y[r0:r0 + C,
               2 * GDN_HEADS * GDN_DK + h * GDN_DV:2 * GDN_HEADS * GDN_DK + (h + 1) * GDN_DV]
        u = (n, h)
        q[u] = qh * lax.rsqrt(jnp.sum(qh * qh, axis=-1, keepdims=True) + EPS) * (GDN_DK ** -0.5)
        k[u] = kh * lax.rsqrt(jnp.sum(kh * kh, axis=-1, keepdims=True) + EPS)
        beta = beta_full[r0:r0 + C, h:h + 1]
        gc[u] = g_col[r0:r0 + C, GDN_HEADS + h:GDN_HEADS + h + 1]
        gr = g_rows[GDN_HEADS + h:GDN_HEADS + h + 1, r0:r0 + C]
        decay[u] = jnp.where(incl, jnp.exp(jnp.where(incl, gc[u] - gr, 0.0)), 0.0)
        eg[u] = jnp.exp(gc[u])
        g_last[u] = gc[u][C - 1:C, :]
        kb[u] = k[u] * beta
        rhs[u] = jnp.concatenate([vh * beta, kb[u] * eg[u]], axis=1)
        rhs_bf[u] = rhs[u].astype(BF16)
    for u in units:
        kq[u] = _dot_nt(jnp.concatenate([kb[u], q[u]], axis=0).astype(BF16),
                        k[u].astype(BF16))
    pw = {u: jnp.where(strict, kq[u][:C] * decay[u], 0.0) for u in units}
    aqk = {u: (kq[u][C:] * decay[u]).astype(BF16) for u in units}
    n_acc = {u: -pw[u] for u in units}
    for _ in range(C.bit_length() - 2):
        pw_bf = {u: pw[u].astype(BF16) for u in units}
        pw = {u: _dot(pw_bf[u], pw_bf[u]) for u in units}
        n_acc = {u: n_acc[u] + pw[u] + _dot(n_acc[u].astype(BF16), pw[u].astype(BF16))
                 for u in units}
    sol = {}
    for u in units:
        n_hi, n_lo = _split2(n_acc[u])
        sol[u] = rhs[u] + _dot(n_hi, rhs_bf[u]) + _dot(n_lo, rhs_bf[u])
    wq = {u: jnp.concatenate([sol[u][:, GDN_DV:], q[u] * eg[u]], axis=0).astype(BF16)
          for u in units}
    kd = {u: (k[u] * jnp.exp(g_last[u] - gc[u])).astype(BF16) for u in units}
    s_cur = [s_ref[0, h] for h in heads]
    for n in range(n_sub):
        ws = [_dot(wq[(n, h)], s_cur[h].astype(BF16)) for h in heads]
        u_bf = [(sol[(n, h)][:, :GDN_DV] - ws[h][:C]).astype(BF16) for h in heads]
        o = [ws[h][C:] + _dot(aqk[(n, h)], u_bf[h]) for h in heads]
        s_cur = [jnp.exp(g_last[(n, h)]) * s_cur[h] + _dot_tn(kd[(n, h)], u_bf[h])
                 for h in heads]
        for h in heads:
            zh = z[n * C:(n + 1) * C, h * GDN_DV:(h + 1) * GDN_DV]
            ya_ref[0, n * C:(n + 1) * C, h * GDN_DV:(h + 1) * GDN_DV] = (
                _rms(o[h], gn) * _silu(zh)).astype(BF16)
    for h in heads:
        s_ref[0, h] = s_cur[h]


GDN_CHUNKS_PER_STEP = 2


def _gdn_chunked(y, z, ba, alog_row, dt_row, gdn_norm):
    b, t, _ = y.shape
    C = GDN_CHUNK * GDN_CHUNKS_PER_STEP
    assert t % C == 0
    nc = t // C
    return pl.pallas_call(
        _gdn_chunk_kernel,
        grid=(b, nc),
        in_specs=[pl.BlockSpec((1, C, QKV_WIDTH), lambda i, j: (i, j, 0)),
                  pl.BlockSpec((1, C, Z_WIDTH), lambda i, j: (i, j, 0)),
                  pl.BlockSpec((1, C, LANES), lambda i, j: (i, j, 0)),
                  _full_spec(alog_row.shape), _full_spec(dt_row.shape),
                  _full_spec(gdn_norm.shape)],
        out_specs=[pl.BlockSpec((1, C, Z_WIDTH), lambda i, j: (i, j, 0)),
                   pl.BlockSpec((1, GDN_HEADS, GDN_DK, GDN_DV), lambda i, j: (i, 0, 0, 0))],
        out_shape=[jax.ShapeDtypeStruct((b, t, Z_WIDTH), BF16),
                   jax.ShapeDtypeStruct((b, GDN_HEADS, GDN_DK, GDN_DV), F32)],
        compiler_params=_params("parallel", "arbitrary"),
        name="gdn_chunked",
    )(y, z, ba, alog_row, dt_row, gdn_norm)


def _gdn_step_kernel(qkv_ref, z_ref, ba_ref, cs_ref, s_ref, cw_ref, alog_ref, dt_ref, gn_ref,
                     ya_ref, snew_ref, csnew_ref):
    rowv = qkv_ref[0]
    cs = cs_ref[0]
    cw = cw_ref[...]
    y = rowv * cw[CONV_W - 1:CONV_W]
    for i in range(CONV_W - 1):
        y = y + cs[i:i + 1] * cw[i:i + 1]
    y = _silu(y)
    csnew_ref[0, 0:CONV_W - 2, :] = cs[1:CONV_W - 1]
    csnew_ref[0, CONV_W - 2:CONV_W - 1, :] = rowv

    beta_full, g_full = _gate_rows(ba_ref[0], alog_ref[...], dt_ref[...])
    r128 = lax.broadcasted_iota(jnp.int32, (GDN_DK, GDN_DK), 0)
    c128 = lax.broadcasted_iota(jnp.int32, (GDN_DK, GDN_DK), 1)
    eye = r128 == c128
    z = z_ref[0]
    gn = gn_ref[...]

    def to_col(r):
        return jnp.sum(jnp.where(eye, jnp.broadcast_to(r, (GDN_DK, GDN_DK)), 0.0),
                       axis=1, keepdims=True)

    heads = range(GDN_HEADS)
    q, k, v = [], [], []
    for h in heads:
        qh = y[:, h * GDN_DK:(h + 1) * GDN_DK]
        kh = y[:, GDN_HEADS * GDN_DK + h * GDN_DK:GDN_HEADS * GDN_DK + (h + 1) * GDN_DK]
        v.append(y[:, 2 * GDN_HEADS * GDN_DK + h * GDN_DV:2 * GDN_HEADS * GDN_DK + (h + 1) * GDN_DV])
        q.append(qh * lax.rsqrt(jnp.sum(qh * qh, axis=-1, keepdims=True) + EPS) * (GDN_DK ** -0.5))
        k.append(kh * lax.rsqrt(jnp.sum(kh * kh, axis=-1, keepdims=True) + EPS))
    k_col = [to_col(k[h]) for h in heads]
    q_col = [to_col(q[h]) for h in heads]
    s = [jnp.exp(g_full[:, GDN_HEADS + h:GDN_HEADS + h + 1]) * s_ref[0, h] for h in heads]
    u = [beta_full[:, h:h + 1] * (v[h] - jnp.sum(s[h] * k_col[h], axis=0, keepdims=True))
         for h in heads]
    s = [s[h] + k_col[h] * u[h] for h in heads]
    o = [jnp.sum(s[h] * q_col[h], axis=0, keepdims=True) for h in heads]
    for h in heads:
        snew_ref[0, h] = s[h]
        zh = z[:, h * GDN_DV:(h + 1) * GDN_DV]
        ya_ref[0, :, h * GDN_DV:(h + 1) * GDN_DV] = _rms(o[h], gn) * _silu(zh)


def _gdn_step(qkv, z, ba, conv_state, state, conv_w, alog_row, dt_row, gdn_norm):
    b = qkv.shape[0]
    qkv3 = qkv.reshape(b, 1, QKV_WIDTH)
    z3 = z.reshape(b, 1, Z_WIDTH)
    ba3 = ba.reshape(b, 1, LANES)
    return pl.pallas_call(
        _gdn_step_kernel,
        grid=(b,),
        in_specs=[pl.BlockSpec((1, 1, QKV_WIDTH), lambda i: (i, 0, 0)),
                  pl.BlockSpec((1, 1, Z_WIDTH), lambda i: (i, 0, 0)),
                  pl.BlockSpec((1, 1, LANES), lambda i: (i, 0, 0)),
                  pl.BlockSpec((1, CONV_W - 1, QKV_WIDTH), lambda i: (i, 0, 0)),
                  pl.BlockSpec((1, GDN_HEADS, GDN_DK, GDN_DV), lambda i: (i, 0, 0, 0)),
                  _full_spec(conv_w.shape), _full_spec(alog_row.shape),
                  _full_spec(dt_row.shape), _full_spec(gdn_norm.shape)],
        out_specs=[pl.BlockSpec((1, 1, Z_WIDTH), lambda i: (i, 0, 0)),
                   pl.BlockSpec((1, GDN_HEADS, GDN_DK, GDN_DV), lambda i: (i, 0, 0, 0)),
                   pl.BlockSpec((1, CONV_W - 1, QKV_WIDTH), lambda i: (i, 0, 0))],
        out_shape=[jax.ShapeDtypeStruct((b, 1, Z_WIDTH), F32),
                   jax.ShapeDtypeStruct((b, GDN_HEADS, GDN_DK, GDN_DV), F32),
                   jax.ShapeDtypeStruct((b, CONV_W - 1, QKV_WIDTH), F32)],
        compiler_params=_params("parallel"),
        name="gdn_step",
    )(qkv3, z3, ba3, conv_state, state, conv_w, alog_row, dt_row, gdn_norm)


def _mla_queries_latent(x, cos2, sin2, g_mix, w_qd, w_c, w_kr, g_qa, w_uq, g_kva,
                        g_qn, g_qr, g_kr):
    xn = _rms(x, g_mix).astype(BF16)
    qa = _rms(_dot(xn, w_qd), g_qa).astype(BF16)
    q = _dot(qa, w_uq)
    qn, qr = [], []
    for h in range(MLA_HEADS):
        qn.append(_rms(q[:, h * QK_PAD:h * QK_PAD + NOPE_DIM], g_qn))
        qr.append(_rope(_rms(q[:, h * QK_PAD + NOPE_DIM:(h + 1) * QK_PAD], g_qr, ROPE_DIM),
                        cos2, sin2))
    c = _rms(_dot(xn, w_c), g_kva)
    kr = _rope(_rms(_dot(xn, w_kr), g_kr, ROPE_DIM), cos2, sin2)
    return qn, qr, c, kr


def _mla_pre_kernel(x_ref, cos_ref, sin_ref, gmix_ref, wqd_ref, wc_ref, wkr_ref, gqa_ref,
                    wuq_ref, gkva_ref, gqn_ref, gqr_ref, gkr_ref, wuk_ref, wuv_ref, gkn_ref,
                    q_ref, k_ref, v_ref, lat_ref):
    qn, qr, c, kr = _mla_queries_latent(
        x_ref[0], cos_ref[...], sin_ref[...], gmix_ref[...], wqd_ref[...], wc_ref[...],
        wkr_ref[...], gqa_ref[...], wuq_ref[...], gkva_ref[...], gqn_ref[...], gqr_ref[...],
        gkr_ref[...])
    lat_ref[0, :, 0:KV_LORA] = c
    lat_ref[0, :, KV_LORA:LATENT_WIDTH] = kr[:, 0:ROPE_DIM]
    c_bf = c.astype(BF16)
    kr_bf = kr.astype(BF16)
    kfull = _dot(c_bf, wuk_ref[...])
    v_ref[0, 0] = _dot_nt(wuv_ref[...], c_bf).astype(BF16)
    gkn = gkn_ref[...]
    for h in range(MLA_HEADS):
        q_ref[0, :, h * QK_PAD:h * QK_PAD + NOPE_DIM] = (qn[h] * ATTN_Q_SCALE).astype(BF16)
        q_ref[0, :, h * QK_PAD + NOPE_DIM:(h + 1) * QK_PAD] = (qr[h] * ATTN_Q_SCALE).astype(BF16)
        kn = _rms(kfull[:, h * NOPE_DIM:(h + 1) * NOPE_DIM], gkn)
        k_ref[0, :, h * QK_PAD:h * QK_PAD + NOPE_DIM] = kn.astype(BF16)
        k_ref[0, :, h * QK_PAD + NOPE_DIM:(h + 1) * QK_PAD] = kr_bf


def _mla_pre(x, cos2, sin2, mw):
    b, t, d = x.shape
    tm = min(ATTN_TILE, t)
    consts = [mw["g_mix"], mw["w_qd"], mw["w_c"], mw["w_kr"], mw["g_qa"], mw["w_uq"],
              mw["g_kva"], mw["g_qn"], mw["g_qr"], mw["g_kr"], mw["w_uk"], mw["w_uv_t"],
              mw["g_kn"]]
    return pl.pallas_call(
        _mla_pre_kernel,
        grid=(b, t // tm),
        in_specs=[pl.BlockSpec((1, tm, d), lambda i, j: (i, j, 0)),
                  pl.BlockSpec((tm, LANES), lambda i, j: (j, 0)),
                  pl.BlockSpec((tm, LANES), lambda i, j: (j, 0))]
                 + [_full_spec(a.shape) for a in consts],
        out_specs=[pl.BlockSpec((1, tm, MLA_HEADS * QK_PAD), lambda i, j: (i, j, 0)),
                   pl.BlockSpec((1, tm, MLA_HEADS * QK_PAD), lambda i, j: (i, j, 0)),
                   pl.BlockSpec((1, 1, MLA_HEADS * V_DIM, tm), lambda i, j: (i, j, 0, 0)),
                   pl.BlockSpec((1, tm, LATENT_WIDTH), lambda i, j: (i, j, 0))],
        out_shape=[jax.ShapeDtypeStruct((b, t, MLA_HEADS * QK_PAD), BF16),
                   jax.ShapeDtypeStruct((b, t, MLA_HEADS * QK_PAD), BF16),
                   jax.ShapeDtypeStruct((b, t // tm, MLA_HEADS * V_DIM, tm), BF16),
                   jax.ShapeDtypeStruct((b, t, LATENT_WIDTH), F32)],
        compiler_params=_params("parallel", "parallel"),
        name="mla_pre",
    )(x, cos2, sin2, *consts)


def _mla_pre_sample_kernel(x_ref, cos_ref, sin_ref, gmix_ref, wqd_ref, wc_ref, wkr_ref, gqa_ref,
                           wuq_ref, gkva_ref, gqn_ref, gqr_ref, gkr_ref, wuk_ref, gkn_ref,
                           qa_ref, qr_ref, lat_ref):
    qn, qr, c, kr = _mla_queries_latent(
        x_ref[...], cos_ref[...], sin_ref[...], gmix_ref[...], wqd_ref[...], wc_ref[...],
        wkr_ref[...], gqa_ref[...], wuq_ref[...], gkva_ref[...], gqn_ref[...], gqr_ref[...],
        gkr_ref[...])
    lat_ref[:, 0:KV_LORA] = c
    lat_ref[:, KV_LORA:LATENT_WIDTH] = kr[:, 0:ROPE_DIM]
    gkn = gkn_ref[...]
    for h in range(MLA_HEADS):
        hi, lo = _split2(qn[h] * gkn)
        w_h = wuk_ref[:, h * NOPE_DIM:(h + 1) * NOPE_DIM]
        qa_ref[h] = _dot_nt(hi, w_h) + _dot_nt(lo, w_h)
        qr_ref[h] = qr[h]


def _mla_pre_sample(x2, cos2, sin2, mw):
    m = x2.shape[0]
    consts = [mw["g_mix"], mw["w_qd"], mw["w_c"], mw["w_kr"], mw["g_qa"], mw["w_uq"],
              mw["g_kva"], mw["g_qn"], mw["g_qr"], mw["g_kr"], mw["w_uk"], mw["g_kn"]]
    args = [x2, cos2, sin2] + consts
    return pl.pallas_call(
        _mla_pre_sample_kernel,
        grid=(1,),
        in_specs=[_full_spec(a.shape) for a in args],
        out_specs=[_full_spec((MLA_HEADS, m, KV_LORA)), _full_spec((MLA_HEADS, m, LANES)),
                   _full_spec((m, LATENT_WIDTH))],
        out_shape=[jax.ShapeDtypeStruct((MLA_HEADS, m, KV_LORA), F32),
                   jax.ShapeDtypeStruct((MLA_HEADS, m, LANES), F32),
                   jax.ShapeDtypeStruct((m, LATENT_WIDTH), F32)],
        compiler_params=_params("arbitrary"),
        name="mla_pre_sample",
    )(*args)


ATTN_HEADS_PER_STEP = 8


def _attn_kernel(q_ref, k_ref, vt_ref, o_ref, *, tq):
    qi = pl.program_id(2)
    hs = range(ATTN_HEADS_PER_STEP)
    q = [q_ref[0, :, g * QK_PAD:(g + 1) * QK_PAD] for g in hs]
    key = lax.broadcasted_iota(jnp.int32, (tq, tq), 0)
    qry = lax.broadcasted_iota(jnp.int32, (tq, tq), 1)

    def raw_scores(j):
        start = pl.multiple_of(j * tq, tq)
        return tuple(_dot_nt(k_ref[0, pl.ds(start, tq), g * QK_PAD:(g + 1) * QK_PAD], q[g])
                     for g in hs)

    def consume(j, s_raw, stats, diagonal):
        vt = [vt_ref[0, j, g * V_DIM:(g + 1) * V_DIM, :] for g in hs]
        s = list(s_raw)
        if diagonal:
            s = [jnp.where(key <= qry, s[g], -jnp.inf) for g in hs]
        m_new = [jnp.maximum(stats[g][0], jnp.max(s[g], axis=0, keepdims=True)) for g in hs]
        p = [jnp.exp2(s[g] - m_new[g]) for g in hs]
        pv = [_dot(vt[g], p[g].astype(BF16)) for g in hs]
        out = []
        for g in hs:
            m, l, acc = stats[g]
            alpha = jnp.exp2(m - m_new[g])
            out.append((m_new[g], alpha * l + jnp.sum(p[g], axis=0, keepdims=True),
                        alpha * acc + pv[g]))
        return tuple(out)

    init = tuple((jnp.full((1, tq), -jnp.inf, F32), jnp.zeros((1, tq), F32),
                  jnp.zeros((V_DIM, tq), F32)) for _ in hs)
    stats = lax.fori_loop(0, qi, lambda j, st: consume(j, raw_scores(j), st, False), init)
    stats = consume(qi, raw_scores(qi), stats, True)
    for g in hs:
        _, l, acc = stats[g]
        o_ref[0, :, g * V_DIM:(g + 1) * V_DIM] = (acc / l).T.astype(BF16)


def _mla_prompt_attention(q_cat, k_cat, v_t):
    b, t, _ = q_cat.shape
    tq = min(ATTN_TILE, t)
    hps = ATTN_HEADS_PER_STEP
    return pl.pallas_call(
        functools.partial(_attn_kernel, tq=tq),
        grid=(b, MLA_HEADS // hps, t // tq),
        in_specs=[pl.BlockSpec((1, tq, hps * QK_PAD), lambda i, h, j: (i, j, h)),
                  pl.BlockSpec((1, t, hps * QK_PAD), lambda i, h, j: (i, 0, h)),
                  pl.BlockSpec((1, t // tq, hps * V_DIM, tq), lambda i, h, j: (i, 0, h, 0))],
        out_specs=pl.BlockSpec((1, tq, hps * V_DIM), lambda i, h, j: (i, j, h)),
        out_shape=jax.ShapeDtypeStruct((b, t, MLA_HEADS * V_DIM), BF16),
        compiler_params=_params("parallel", "parallel", "arbitrary"),
        name="mla_prompt_attention",
    )(q_cat, k_cat, v_t)


def _dec_attn_kernel(pt_ref, *refs, pps):
    del pt_ref
    page_refs = refs[:pps]
    (wukt_ref, qa_ref, qr_ref, lnew_ref, ctx_ref,
     lhs_ref, qr16_ref, m_ref, l_ref, acc_ref) = refs[pps:]
    b = pl.program_id(0)
    j = pl.program_id(1)
    nj = pl.num_programs(1)
    n_k = MLA_HEADS * NOPE_DIM

    @pl.when((b == 0) & (j == 0))
    def _():
        lhs_ref[0:n_k, :] = wukt_ref[...]

    @pl.when(j == 0)
    def _():
        qa = jnp.concatenate([qa_ref[h, 0] for h in range(MLA_HEADS)]
                             + [jnp.zeros((8, KV_LORA), F32)], axis=0)
        lhs_ref[n_k:n_k + 16, :] = qa.astype(BF16)
        qr = jnp.concatenate([qr_ref[h, 0] for h in range(MLA_HEADS)]
                             + [jnp.zeros((8, LANES), F32)], axis=0)
        qr16_ref[...] = qr.astype(BF16)
        m_ref[...] = jnp.full(m_ref.shape, -jnp.inf, F32)
        l_ref[...] = jnp.zeros(l_ref.shape, F32)
        acc_ref[...] = jnp.zeros(acc_ref.shape, F32)

    def scores(res, sr):
        kt = res[0:n_k]
        ss = jnp.sum((kt * kt).reshape(MLA_HEADS, NOPE_DIM, res.shape[1]), axis=1)
        sn = res[n_k:n_k + MLA_HEADS]
        return (sn * lax.rsqrt(ss * (1.0 / NOPE_DIM) + EPS) + sr[0:MLA_HEADS]) * MLA_SCALE

    c_bf, s_parts = [], []
    for i in range(0, pps, 2):
        tile = jnp.concatenate([page_refs[i][...], page_refs[i + 1][...]], axis=1)
        c_bf.append(tile[0:KV_LORA].astype(BF16))
        kr_bf = tile[KV_LORA:LATENT_WIDTH].astype(BF16)
        s_parts.append(scores(_dot(lhs_ref[...], c_bf[-1]),
                              _dot(qr16_ref[:, 0:ROPE_DIM], kr_bf)))
    s = jnp.concatenate(s_parts, axis=1)
    m_old = m_ref[...]
    m_new = jnp.maximum(m_old, jnp.max(s, axis=-1, keepdims=True))
    alpha = jnp.exp(m_old - m_new)
    p = jnp.exp(s - m_new)
    l_ref[...] = alpha * l_ref[...] + jnp.sum(p, axis=-1, keepdims=True)
    p_bf = p.astype(BF16)
    n_tok = c_bf[0].shape[1]
    acc = alpha * acc_ref[...]
    for t, c_t in enumerate(c_bf):
        acc = acc + _dot_nt(p_bf[:, t * n_tok:(t + 1) * n_tok], c_t)
    acc_ref[...] = acc
    m_ref[...] = m_new

    @pl.when(j == nj - 1)
    def _():
        ln = lnew_ref[0]
        c_new = ln[:, 0:KV_LORA]
        c8 = jnp.broadcast_to(c_new, (8, KV_LORA)).astype(BF16)
        kr8 = jnp.broadcast_to(ln[:, KV_LORA:LATENT_WIDTH], (8, ROPE_DIM)).astype(BF16)
        s_new = scores(_dot_nt(lhs_ref[...], c8),
                       _dot_nt(qr16_ref[:, 0:ROPE_DIM], kr8))[:, 0:1]
        m_old = m_ref[...]
        m_new = jnp.maximum(m_old, s_new)
        alpha = jnp.exp(m_old - m_new)
        p_new = jnp.exp(s_new - m_new)
        ctx = (alpha * acc_ref[...] + p_new * c_new) / (alpha * l_ref[...] + p_new)
        for h in range(MLA_HEADS):
            ctx_ref[h, 0] = ctx[h:h + 1]


def _mla_decode_attention(cache, layer, page_table, wuk_t, qa, qr, lat_new):
    bsz, n_pages = page_table.shape
    page = cache.shape[3]
    pps = DEC_PAGES_PER_STEP
    while n_pages % pps:
        pps //= 2
    assert pps >= 2
    qa4 = qa.reshape(MLA_HEADS, bsz, 1, KV_LORA)
    qr4 = qr.reshape(MLA_HEADS, bsz, 1, LANES)
    ln3 = lat_new.reshape(bsz, 1, LATENT_WIDTH)

    def page_spec(i):
        return pl.BlockSpec((pl.Squeezed(), pl.Squeezed(), LATENT_WIDTH, page),
                            lambda b, j, pt: (layer, pt[b, j * pps + i], 0, 0))

    grid_spec = pltpu.PrefetchScalarGridSpec(
        num_scalar_prefetch=1,
        grid=(bsz, n_pages // pps),
        in_specs=[page_spec(i) for i in range(pps)] + [
            pl.BlockSpec(wuk_t.shape, lambda b, j, pt: (0, 0)),
            pl.BlockSpec((MLA_HEADS, 1, 1, KV_LORA), lambda b, j, pt: (0, b, 0, 0)),
            pl.BlockSpec((MLA_HEADS, 1, 1, LANES), lambda b, j, pt: (0, b, 0, 0)),
            pl.BlockSpec((1, 1, LATENT_WIDTH), lambda b, j, pt: (b, 0, 0))],
        out_specs=pl.BlockSpec((MLA_HEADS, 1, 1, KV_LORA), lambda b, j, pt: (0, b, 0, 0)),
        scratch_shapes=[pltpu.VMEM((MLA_HEADS * NOPE_DIM + 16, KV_LORA), BF16),
                        pltpu.VMEM((16, LANES), BF16),
                        pltpu.VMEM((MLA_HEADS, 1), F32),
                        pltpu.VMEM((MLA_HEADS, 1), F32),
                        pltpu.VMEM((MLA_HEADS, KV_LORA), F32)])
    ctx = pl.pallas_call(
        functools.partial(_dec_attn_kernel, pps=pps),
        grid_spec=grid_spec,
        out_shape=jax.ShapeDtypeStruct((MLA_HEADS, bsz, 1, KV_LORA), F32),
        compiler_params=_params("arbitrary", "arbitrary"),
        name="mla_decode_attention",
    )(page_table, *([cache] * pps), wuk_t, qa4, qr4, ln3)
    return ctx.reshape(MLA_HEADS, bsz, KV_LORA)


def _dec_post_kernel(ctx_ref, wuv_ref, y_ref):
    for h in range(MLA_HEADS):
        hi, lo = _split2(ctx_ref[h])
        w_h = wuv_ref[:, h * V_DIM:(h + 1) * V_DIM]
        y_ref[:, h * V_DIM:(h + 1) * V_DIM] = (_dot(hi, w_h) + _dot(lo, w_h)).astype(BF16)


def _dec_post(ctx, w_uv):
    bsz = ctx.shape[1]
    return pl.pallas_call(
        _dec_post_kernel,
        grid=(1,),
        in_specs=[_full_spec(ctx.shape), _full_spec(w_uv.shape)],
        out_specs=_full_spec((bsz, MLA_HEADS * V_DIM)),
        out_shape=jax.ShapeDtypeStruct((bsz, MLA_HEADS * V_DIM), BF16),
        compiler_params=_params("arbitrary"),
        name="mla_decode_values",
    )(ctx, w_uv)


def _merge_mlp_kernel(x_ref, ya_ref, yb_ref, gmix_ref, wg_ref, wya_ref, wyb_ref, wo_ref,
                      gmlp_ref, wup_ref, wdn_ref, y_ref):
    tm, d = x_ref.shape
    d_ff = wup_ref.shape[1]
    n_grp = max(1, tm // MERGE_ROWS_PER_GROUP)
    rows = tm // n_grp
    grp = range(n_grp)
    sl = [slice(r * rows, (r + 1) * rows) for r in grp]
    x = [x_ref[sl[r], :] for r in grp]
    xn = [_rms(x[r], gmix_ref[...]).astype(BF16) for r in grp]
    gates = [_sigmoid(_dot(xn[r], wg_ref[...])) for r in grp]
    pa = [_dot(ya_ref[sl[r], :].astype(BF16), wya_ref[...]) for r in grp]
    pb = [_dot(yb_ref[sl[r], :].astype(BF16), wyb_ref[...]) for r in grp]
    mix = [(gates[r][:, :d] * pa[r] + gates[r][:, d:] * pb[r]).astype(BF16) for r in grp]
    x1 = [x[r] + _dot(mix[r], wo_ref[...]) for r in grp]
    hin = [_rms(x1[r], gmlp_ref[...]).astype(BF16) for r in grp]
    y = x1
    for f in range(0, d_ff, MLP_FF_BLOCK):
        hmid = [jnp.maximum(_dot(hin[r], wup_ref[:, f:f + MLP_FF_BLOCK]), 0.0) for r in grp]
        y = [y[r] + _dot((hmid[r] * hmid[r]).astype(BF16), wdn_ref[f:f + MLP_FF_BLOCK, :])
             for r in grp]
    for r in grp:
        y_ref[sl[r], :] = y[r]


MERGE_ROWS_PER_GROUP = 256
MLP_FF_BLOCK = 1024


def _merge_mlp(x2, ya, yb, ew):
    m, d = x2.shape
    tm = min(512, m)
    consts = [ew["g_mix"], ew["w_gates"], ew["w_ya"], ew["w_yb"], ew["w_o"], ew["g_mlp"],
              ew["w_up"], ew["w_down"]]

    def const_spec(a):
        nd = a.ndim
        return pl.BlockSpec(a.shape, lambda i: (0,) * nd, pipeline_mode=pl.Buffered(1))

    return pl.pallas_call(
        _merge_mlp_kernel,
        grid=(m // tm,),
        in_specs=[pl.BlockSpec((tm, d), lambda i: (i, 0)),
                  pl.BlockSpec((tm, ya.shape[1]), lambda i: (i, 0)),
                  pl.BlockSpec((tm, yb.shape[1]), lambda i: (i, 0))]
                 + [const_spec(a) for a in consts],
        out_specs=pl.BlockSpec((tm, d), lambda i: (i, 0)),
        out_shape=jax.ShapeDtypeStruct((m, d), F32),
        compiler_params=_params("parallel"),
        name="merge_mlp",
    )(x2, ya, yb, *consts)


def _rope_tables(pos):
    inv = jnp.power(ROPE_THETA, -jnp.arange(0, ROPE_DIM, 2, dtype=F32) / ROPE_DIM)
    ang = pos.astype(F32)[:, None] * inv[None, :]
    cos, sin = jnp.cos(ang), jnp.sin(ang)
    zeros = jnp.zeros((pos.shape[0], LANES - ROPE_DIM), F32)
    return (jnp.concatenate([cos, cos, zeros], axis=1),
            jnp.concatenate([-sin, sin, zeros], axis=1))


def _pad_lanes(a, width=LANES, offset=0):
    return jnp.zeros((1, width), F32).at[0, offset:offset + a.shape[0]].set(a.astype(F32))


def _layer_weights(l, norm_mix, w_in, conv_w, a_log, dt_bias, gdn_norm, q_a_norm, w_uq,
                   kv_a_norm, w_uk, w_uv, q_norm_nope, q_norm_rope, k_norm_nope, k_norm_rope,
                   w_ya, w_yb, w_o, norm_mlp, w_up, w_down):
    d = w_in.shape[1]
    off_z = QKV_WIDTH
    off_b = off_z + Z_WIDTH
    off_qd = off_b + 2 * GDN_HEADS
    off_kv = off_qd + Q_LORA
    off_g = off_kv + LATENT_WIDTH
    wi = w_in[l]
    g_mix = norm_mix[l].reshape(1, d)
    gw = {
        "g_mix": g_mix,
        "w_qkv": wi[:, :off_z].astype(BF16),
        "w_z": wi[:, off_z:off_b].astype(BF16),
        "w_ba": jnp.pad(wi[:, off_b:off_qd], ((0, 0), (0, LANES - 2 * GDN_HEADS))).astype(BF16),
        "conv_w": conv_w[l],
        "alog_row": _pad_lanes(a_log[l], offset=GDN_HEADS),
        "dt_row": _pad_lanes(dt_bias[l], offset=GDN_HEADS),
        "gdn_norm": gdn_norm[l].reshape(1, GDN_DV),
    }
    wq = w_uq[l].reshape(Q_LORA, MLA_HEADS, NOPE_DIM + ROPE_DIM)
    wq = jnp.pad(wq, ((0, 0), (0, 0), (0, QK_PAD - NOPE_DIM - ROPE_DIM)))
    mw = {
        "g_mix": g_mix,
        "w_qd": wi[:, off_qd:off_kv].astype(BF16),
        "w_c": wi[:, off_kv:off_kv + KV_LORA].astype(BF16),
        "w_kr": jnp.pad(wi[:, off_kv + KV_LORA:off_g], ((0, 0), (0, LANES - ROPE_DIM))).astype(BF16),
        "g_qa": q_a_norm[l].reshape(1, Q_LORA),
        "w_uq": wq.reshape(Q_LORA, MLA_HEADS * QK_PAD).astype(BF16),
        "g_kva": kv_a_norm[l].reshape(1, KV_LORA),
        "g_qn": q_norm_nope[l].reshape(1, NOPE_DIM),
        "g_qr": _pad_lanes(q_norm_rope[l]),
        "g_kr": _pad_lanes(k_norm_rope[l]),
        "g_kn": k_norm_nope[l].reshape(1, NOPE_DIM),
        "w_uk": w_uk[l].astype(BF16),
        "w_uk_t": w_uk[l].T.astype(BF16),
        "w_uv": w_uv[l].astype(BF16),
        "w_uv_t": w_uv[l].T.astype(BF16),
    }
    ew = {
        "g_mix": g_mix,
        "w_gates": wi[:, off_g:].astype(BF16),
        "w_ya": w_ya[l].astype(BF16),
        "w_yb": w_yb[l].astype(BF16),
        "w_o": w_o[l].astype(BF16),
        "g_mlp": norm_mlp[l].reshape(1, d),
        "w_up": w_up[l].astype(BF16),
        "w_down": w_down[l].astype(BF16),
    }
    return gw, mw, ew


def kernel(x_prompt, x_sample, cache_mla, state_gdn, state_conv, page_table, norm_mix, w_in,
           conv_w, a_log, dt_bias, gdn_norm, q_a_norm, w_uq, kv_a_norm, w_uk, w_uv,
           q_norm_nope, q_norm_rope, k_norm_nope, k_norm_rope, w_ya, w_yb, w_o, norm_mlp,
           w_up, w_down):
    depth = w_in.shape[0]
    bp, tp, d = x_prompt.shape
    bs, ts, _ = x_sample.shape
    assert ts == 1 and tp % GDN_CHUNK == 0
    past_len = page_table.shape[1] * cache_mla.shape[2]
    cache_t = jnp.swapaxes(cache_mla, 2, 3)
    cos_p, sin_p = _rope_tables(jnp.arange(tp))
    cos_s, sin_s = _rope_tables(past_len + jnp.zeros((bs,), jnp.int32))

    x_p, x_s = x_prompt, x_sample.reshape(bs, d)
    rows_p, gdn_p, conv_p, rows_s, gdn_s, conv_s = [], [], [], [], [], []
    for l in range(depth):
        gw, mw, ew = _layer_weights(
            l, norm_mix, w_in, conv_w, a_log, dt_bias, gdn_norm, q_a_norm, w_uq, kv_a_norm,
            w_uk, w_uv, q_norm_nope, q_norm_rope, k_norm_nope, k_norm_rope, w_ya, w_yb, w_o,
            norm_mlp, w_up, w_down)
        gdn_consts = (gw["conv_w"], gw["alog_row"], gw["dt_row"], gw["gdn_norm"])

        xp2 = x_p.reshape(bp * tp, d)
        yc, z, ba, cs_p = _gdn_proj_conv(x_p, gw["g_mix"], gw["w_qkv"], gw["w_z"], gw["w_ba"],
                                         gw["conv_w"])
        ya_p, s_p = _gdn_chunked(yc.reshape(bp, tp, QKV_WIDTH), z.reshape(bp, tp, Z_WIDTH),
                                 ba.reshape(bp, tp, LANES), *gdn_consts[1:])
        q_cat, k_cat, v_p, lat_p = _mla_pre(x_p, cos_p, sin_p, mw)
        yb_p = _mla_prompt_attention(q_cat, k_cat, v_p)
        y_p = _merge_mlp(xp2, ya_p.reshape(bp * tp, Z_WIDTH), yb_p.reshape(bp * tp, -1), ew)

        qkv, z, ba = _gdn_proj(x_s, gw["g_mix"], gw["w_qkv"], gw["w_z"], gw["w_ba"])
        ya_s, s_s, cs_s = _gdn_step(qkv, z, ba, state_conv[l], state_gdn[l], *gdn_consts)
        qa, qr, lat_s = _mla_pre_sample(x_s, cos_s, sin_s, mw)
        ctx = _mla_decode_attention(cache_t, l, page_table, mw["w_uk_t"], qa, qr, lat_s)
        yb_s = _dec_post(ctx, mw["w_uv"])
        y_s = _merge_mlp(x_s, ya_s.reshape(bs, Z_WIDTH), yb_s, ew)

        x_p, x_s = y_p.reshape(bp, tp, d), y_s
        rows_p.append(lat_p)
        gdn_p.append(s_p)
        conv_p.append(cs_p)
        rows_s.append(lat_s.reshape(bs, 1, LATENT_WIDTH))
        gdn_s.append(s_s)
        conv_s.append(cs_s)
    return (x_p, x_s.reshape(bs, 1, d), jnp.stack(rows_p), jnp.stack(gdn_p), jnp.stack(conv_p),
            jnp.stack(rows_s), jnp.stack(gdn_s), jnp.stack(conv_s))
```

```python
import functools

import jax
import jax.numpy as jnp
from jax import lax
from jax.experimental import pallas as pl
from jax.experimental.pallas import tpu as pltpu

F32 = jnp.float32
BF16 = jnp.bfloat16

EPS = 1e-6
GDN_HEADS = 8
GDN_DK = 128
GDN_DV = 128
CONV_W = 4
GDN_CHUNK = 64
MLA_HEADS = 8
Q_LORA = 512
KV_LORA = 512
NOPE_DIM = 128
ROPE_DIM = 64
V_DIM = 128
ROPE_THETA = 10000.0
MLA_SCALE = (NOPE_DIM + ROPE_DIM) ** -0.5
LOG2_E = 1.4426950408889634
ATTN_Q_SCALE = MLA_SCALE * LOG2_E
LATENT_WIDTH = KV_LORA + ROPE_DIM
QKV_WIDTH = GDN_HEADS * (2 * GDN_DK + GDN_DV)
Z_WIDTH = GDN_HEADS * GDN_DV

LANES = 128
QK_PAD = 256
VMEM_LIMIT = 56 * 1024 * 1024
DEC_PAGES_PER_STEP = 64
ATTN_TILE = 256


def _dot(a, b):
    return jnp.dot(a, b, preferred_element_type=F32)


def _dot_nt(a, b):
    return lax.dot_general(a, b, (((1,), (1,)), ((), ())), preferred_element_type=F32)


def _dot_tn(a, b):
    return lax.dot_general(a, b, (((0,), (0,)), ((), ())), preferred_element_type=F32)


def _split2(x):
    hi = x.astype(BF16)
    lo = (x - hi.astype(F32)).astype(BF16)
    return hi, lo


def _split3(x):
    hi = x.astype(BF16)
    r = x - hi.astype(F32)
    mid = r.astype(BF16)
    lo = (r - mid.astype(F32)).astype(BF16)
    return hi, mid, lo


def _rms(x, g, n=None):
    n = x.shape[-1] if n is None else n
    ss = jnp.sum(x * x, axis=-1, keepdims=True) * (1.0 / n)
    return x * lax.rsqrt(ss + EPS) * g


def _sigmoid(x):
    return 1.0 / (1.0 + jnp.exp(-x))


def _silu(x):
    h = 0.5 * x
    return h + h * jnp.tanh(h)


def _softplus(x):
    return jnp.maximum(x, 0.0) + jnp.log1p(jnp.exp(-jnp.abs(x)))


def _rope(x, cos2, sin2):
    lane = lax.broadcasted_iota(jnp.int32, x.shape, 1)
    fwd = pltpu.roll(x, LANES - ROPE_DIM // 2, axis=1)
    bwd = pltpu.roll(x, ROPE_DIM // 2, axis=1)
    swapped = jnp.where(lane < ROPE_DIM // 2, fwd, bwd)
    return x * cos2 + swapped * sin2


def _full_spec(shape):
    nd = len(shape)
    return pl.BlockSpec(shape, lambda *_: (0,) * nd)


def _params(*sem):
    return pltpu.CompilerParams(dimension_semantics=sem, vmem_limit_bytes=VMEM_LIMIT)


def _gdn_proj_kernel(x_ref, g_ref, wqkv_ref, wz_ref, wba_ref, qkv_ref, z_ref, ba_ref):
    xn = _rms(x_ref[...], g_ref[...]).astype(BF16)
    qkv_ref[...] = _dot(xn, wqkv_ref[...])
    z_ref[...] = _dot(xn, wz_ref[...])
    ba_ref[...] = _dot(xn, wba_ref[...])


def _gdn_proj(x2, g, wqkv, wz, wba):
    m, d = x2.shape
    tm = min(256, m)
    return pl.pallas_call(
        _gdn_proj_kernel,
        grid=(m // tm,),
        in_specs=[pl.BlockSpec((tm, d), lambda i: (i, 0)),
                  _full_spec(g.shape), _full_spec(wqkv.shape), _full_spec(wz.shape),
                  _full_spec(wba.shape)],
        out_specs=[pl.BlockSpec((tm, QKV_WIDTH), lambda i: (i, 0)),
                   pl.BlockSpec((tm, Z_WIDTH), lambda i: (i, 0)),
                   pl.BlockSpec((tm, LANES), lambda i: (i, 0))],
        out_shape=[jax.ShapeDtypeStruct((m, QKV_WIDTH), F32),
                   jax.ShapeDtypeStruct((m, Z_WIDTH), F32),
                   jax.ShapeDtypeStruct((m, LANES), F32)],
        compiler_params=_params("parallel"),
        name="gdn_proj",
    )(x2, g, wqkv, wz, wba)


PROJ_CONV_COLS = 512
assert CONV_W == 4


def _gdn_proj_conv_kernel(x_ref, g_ref, wqkv_ref, wz_ref, wba_ref, cw_ref,
                          y_ref, z_ref, ba_ref, cs_ref, xp_ref, *, tiles_per_seq):
    i = pl.program_id(0)
    tm = x_ref.shape[0]

    @pl.when(i % tiles_per_seq == 0)
    def _():
        xp_ref[...] = jnp.zeros((8, QKV_WIDTH), F32)

    xn = _rms(x_ref[...], g_ref[...]).astype(BF16)
    row8 = lax.broadcasted_iota(jnp.int32, (8, PROJ_CONV_COLS), 0)
    for c0 in range(0, QKV_WIDTH, PROJ_CONV_COLS):
        cb = slice(c0, c0 + PROJ_CONV_COLS)
        raw = _dot(xn, wqkv_ref[:, cb])
        halo = xp_ref[:, cb]
        cw = cw_ref[:, cb]
        back2 = pltpu.roll(raw, 2, axis=0)
        x2 = jnp.concatenate([jnp.where(row8 < 2, pltpu.roll(halo, 2, axis=0), back2[0:8]),
                              back2[8:]], axis=0)
        u = raw * cw[2:3] + x2 * cw[0:1]
        u_prev = halo[7:8] * cw[2:3] + halo[5:6] * cw[0:1]
        back1 = pltpu.roll(u, 1, axis=0)
        su = jnp.concatenate([jnp.where(row8 < 1, u_prev, back1[0:8]), back1[8:]], axis=0)
        y_ref[:, cb] = _silu(raw * cw[3:4] + x2 * cw[1:2] + su)
        xp_ref[:, cb] = raw[tm - 8:tm]
    z_ref[...] = _dot(xn, wz_ref[...])
    ba_ref[...] = _dot(xn, wba_ref[...])

    @pl.when(i % tiles_per_seq == tiles_per_seq - 1)
    def _():
        cs_ref[0] = xp_ref[8 - (CONV_W - 1):8, :]


def _gdn_proj_conv(x, g, wqkv, wz, wba, conv_w):
    b, t, d = x.shape
    tm = min(256, t)
    assert t % tm == 0
    tiles_per_seq = t // tm
    m = b * t
    return pl.pallas_call(
        functools.partial(_gdn_proj_conv_kernel, tiles_per_seq=tiles_per_seq),
        grid=(m // tm,),
        in_specs=[pl.BlockSpec((tm, d), lambda i: (i, 0)),
                  _full_spec(g.shape), _full_spec(wqkv.shape), _full_spec(wz.shape),
                  _full_spec(wba.shape), _full_spec(conv_w.shape)],
        out_specs=[pl.BlockSpec((tm, QKV_WIDTH), lambda i: (i, 0)),
                   pl.BlockSpec((tm, Z_WIDTH), lambda i: (i, 0)),
                   pl.BlockSpec((tm, LANES), lambda i: (i, 0)),
                   pl.BlockSpec((1, CONV_W - 1, QKV_WIDTH), lambda i: (i // tiles_per_seq, 0, 0))],
        out_shape=[jax.ShapeDtypeStruct((m, QKV_WIDTH), F32),
                   jax.ShapeDtypeStruct((m, Z_WIDTH), F32),
                   jax.ShapeDtypeStruct((m, LANES), F32),
                   jax.ShapeDtypeStruct((b, CONV_W - 1, QKV_WIDTH), F32)],
        scratch_shapes=[pltpu.VMEM((8, QKV_WIDTH), F32)],
        compiler_params=_params("arbitrary"),
        name="gdn_proj_conv",
    )(x.reshape(m, d), g, wqkv, wz, wba, conv_w)


def _gate_rows(ba, alog_row, dt_row):
    lane = lax.broadcasted_iota(jnp.int32, ba.shape, 1)
    is_a = (lane >= GDN_HEADS) & (lane < 2 * GDN_HEADS)
    g = jnp.where(is_a, -jnp.exp(alog_row) * _softplus(ba + dt_row), 0.0)
    return _sigmoid(ba), g


def _gdn_chunk_kernel(y_ref, z_ref, ba_ref, alog_ref, dt_ref, gn_ref, ya_ref, s_ref):
    c = pl.program_id(1)
    C = GDN_CHUNK
    TB = y_ref.shape[1]
    n_sub = TB // C

    @pl.when(c == 0)
    def _():
        s_ref[...] = jnp.zeros(s_ref.shape, F32)

    y = y_ref[0]

    beta_full, g_full = _gate_rows(ba_ref[0], alog_ref[...], dt_ref[...])
    row = lax.broadcasted_iota(jnp.int32, (C, C), 0)
    col = lax.broadcasted_iota(jnp.int32, (C, C), 1)
    incl = row >= col
    strict = row > col
    rtb = lax.broadcasted_iota(jnp.int32, (TB, TB), 0)
    ctb = lax.broadcasted_iota(jnp.int32, (TB, TB), 1)
    tril = jnp.where((rtb >= ctb) & (rtb // C == ctb // C), 1.0, 0.0).astype(BF16)
    g_col = sum(_dot(tril, part) for part in _split3(g_full))
    r128 = lax.broadcasted_iota(jnp.int32, (LANES, LANES), 0)
    c128 = lax.broadcasted_iota(jnp.int32, (LANES, LANES), 1)
    eye = jnp.where(r128 == c128, 1.0, 0.0).astype(BF16)
    g_rows = sum(_dot_nt(eye, part) for part in _split3(g_col))

    z = z_ref[0]
    gn = gn_ref[...]
    heads = range(GDN_HEADS)
    units = [(n, h) for n in range(n_sub) for h in heads]
    q, k, gc, eg, g_last, kb, rhs_bf, rhs, decay, kq = ({} for _ in range(10))
    for n, h in units:
        r0 = n * C
        qh = y[r0:r0 + C, h * GDN_DK:(h + 1) * GDN_DK]
        kh = y[r0:r0 + C, GDN_HEADS * GDN_DK + h * GDN_DK:GDN_HEADS * GDN_DK + (h + 1) * GDN_DK]
        vh = y[r0:r0 + C,
               2 * GDN_HEADS * GDN_DK + h * GDN_DV:2 * GDN_HEADS * GDN_DK + (h + 1) * GDN_DV]
        u = (n, h)
        q[u] = qh * lax.rsqrt(jnp.sum(qh * qh, axis=-1, keepdims=True) + EPS) * (GDN_DK ** -0.5)
        k[u] = kh * lax.rsqrt(jnp.sum(kh * kh, axis=-1, keepdims=True) + EPS)
        beta = beta_full[r0:r0 + C, h:h + 1]
        gc[u] = g_col[r0:r0 + C, GDN_HEADS + h:GDN_HEADS + h + 1]
        gr = g_rows[GDN_HEADS + h:GDN_HEADS + h + 1, r0:r0 + C]
        decay[u] = jnp.where(incl, jnp.exp(jnp.where(incl, gc[u] - gr, 0.0)), 0.0)
        eg[u] = jnp.exp(gc[u])
        g_last[u] = gc[u][C - 1:C, :]
        kb[u] = k[u] * beta
        rhs[u] = jnp.concatenate([vh * beta, kb[u] * eg[u]], axis=1)
        rhs_bf[u] = rhs[u].astype(BF16)
    for u in units:
        kq[u] = _dot_nt(jnp.concatenate([kb[u], q[u]], axis=0).astype(BF16),
                        k[u].astype(BF16))
    pw = {u: jnp.where(strict, kq[u][:C] * decay[u], 0.0) for u in units}
    aqk = {u: (kq[u][C:] * decay[u]).astype(BF16) for u in units}
    n_acc = {u: -pw[u] for u in units}
    for _ in range(C.bit_length() - 2):
        pw_bf = {u: pw[u].astype(BF16) for u in units}
        pw = {u: _dot(pw_bf[u], pw_bf[u]) for u in units}
        n_acc = {u: n_acc[u] + pw[u] + _dot(n_acc[u].astype(BF16), pw[u].astype(BF16))
                 for u in units}
    sol = {}
    for u in units:
        n_hi, n_lo = _split2(n_acc[u])
        sol[u] = rhs[u] + _dot(n_hi, rhs_bf[u]) + _dot(n_lo, rhs_bf[u])
    wq = {u: jnp.concatenate([sol[u][:, GDN_DV:], q[u] * eg[u]], axis=0).astype(BF16)
          for u in units}
    kd = {u: (k[u] * jnp.exp(g_last[u] - gc[u])).astype(BF16) for u in units}
    s_cur = [s_ref[0, h] for h in heads]
    for n in range(n_sub):
        ws = [_dot(wq[(n, h)], s_cur[h].astype(BF16)) for h in heads]
        u_bf = [(sol[(n, h)][:, :GDN_DV] - ws[h][:C]).astype(BF16) for h in heads]
        o = [ws[h][C:] + _dot(aqk[(n, h)], u_bf[h]) for h in heads]
        s_cur = [jnp.exp(g_last[(n, h)]) * s_cur[h] + _dot_tn(kd[(n, h)], u_bf[h])
                 for h in heads]
        for h in heads:
            zh = z[n * C:(n + 1) * C, h * GDN_DV:(h + 1) * GDN_DV]
            ya_ref[0, n * C:(n + 1) * C, h * GDN_DV:(h + 1) * GDN_DV] = (
                _rms(o[h], gn) * _silu(zh)).astype(BF16)
    for h in heads:
        s_ref[0, h] = s_cur[h]


GDN_CHUNKS_PER_STEP = 2


def _gdn_chunked(y, z, ba, alog_row, dt_row, gdn_norm):
    b, t, _ = y.shape
    C = GDN_CHUNK * GDN_CHUNKS_PER_STEP
    assert t % C == 0
    nc = t // C
    return pl.pallas_call(
        _gdn_chunk_kernel,
        grid=(b, nc),
        in_specs=[pl.BlockSpec((1, C, QKV_WIDTH), lambda i, j: (i, j, 0)),
                  pl.BlockSpec((1, C, Z_WIDTH), lambda i, j: (i, j, 0)),
                  pl.BlockSpec((1, C, LANES), lambda i, j: (i, j, 0)),
                  _full_spec(alog_row.shape), _full_spec(dt_row.shape),
                  _full_spec(gdn_norm.shape)],
        out_specs=[pl.BlockSpec((1, C, Z_WIDTH), lambda i, j: (i, j, 0)),
                   pl.BlockSpec((1, GDN_HEADS, GDN_DK, GDN_DV), lambda i, j: (i, 0, 0, 0))],
        out_shape=[jax.ShapeDtypeStruct((b, t, Z_WIDTH), BF16),
                   jax.ShapeDtypeStruct((b, GDN_HEADS, GDN_DK, GDN_DV), F32)],
        compiler_params=_params("parallel", "arbitrary"),
        name="gdn_chunked",
    )(y, z, ba, alog_row, dt_row, gdn_norm)


def _gdn_step_kernel(qkv_ref, z_ref, ba_ref, cs_ref, s_ref, cw_ref, alog_ref, dt_ref, gn_ref,
                     ya_ref, snew_ref, csnew_ref):
    rowv = qkv_ref[0]
    cs = cs_ref[0]
    cw = cw_ref[...]
    y = rowv * cw[CONV_W - 1:CONV_W]
    for i in range(CONV_W - 1):
        y = y + cs[i:i + 1] * cw[i:i + 1]
    y = _silu(y)
    csnew_ref[0, 0:CONV_W - 2, :] = cs[1:CONV_W - 1]
    csnew_ref[0, CONV_W - 2:CONV_W - 1, :] = rowv

    beta_full, g_full = _gate_rows(ba_ref[0], alog_ref[...], dt_ref[...])
    r128 = lax.broadcasted_iota(jnp.int32, (GDN_DK, GDN_DK), 0)
    c128 = lax.broadcasted_iota(jnp.int32, (GDN_DK, GDN_DK), 1)
    eye = r128 == c128
    z = z_ref[0]
    gn = gn_ref[...]

    def to_col(r):
        return jnp.sum(jnp.where(eye, jnp.broadcast_to(r, (GDN_DK, GDN_DK)), 0.0),
                       axis=1, keepdims=True)

    heads = range(GDN_HEADS)
    q, k, v = [], [], []
    for h in heads:
        qh = y[:, h * GDN_DK:(h + 1) * GDN_DK]
        kh = y[:, GDN_HEADS * GDN_DK + h * GDN_DK:GDN_HEADS * GDN_DK + (h + 1) * GDN_DK]
        v.append(y[:, 2 * GDN_HEADS * GDN_DK + h * GDN_DV:2 * GDN_HEADS * GDN_DK + (h + 1) * GDN_DV])
        q.append(qh * lax.rsqrt(jnp.sum(qh * qh, axis=-1, keepdims=True) + EPS) * (GDN_DK ** -0.5))
        k.append(kh * lax.rsqrt(jnp.sum(kh * kh, axis=-1, keepdims=True) + EPS))
    k_col = [to_col(k[h]) for h in heads]
    q_col = [to_col(q[h]) for h in heads]
    s = [jnp.exp(g_full[:, GDN_HEADS + h:GDN_HEADS + h + 1]) * s_ref[0, h] for h in heads]
    u = [beta_full[:, h:h + 1] * (v[h] - jnp.sum(s[h] * k_col[h], axis=0, keepdims=True))
         for h in heads]
    s = [s[h] + k_col[h] * u[h] for h in heads]
    o = [jnp.sum(s[h] * q_col[h], axis=0, keepdims=True) for h in heads]
    for h in heads:
        snew_ref[0, h] = s[h]
        zh = z[:, h * GDN_DV:(h + 1) * GDN_DV]
        ya_ref[0, :, h * GDN_DV:(h + 1) * GDN_DV] = _rms(o[h], gn) * _silu(zh)


def _gdn_step(qkv, z, ba, conv_state, state, conv_w, alog_row, dt_row, gdn_norm):
    b = qkv.shape[0]
    qkv3 = qkv.reshape(b, 1, QKV_WIDTH)
    z3 = z.reshape(b, 1, Z_WIDTH)
    ba3 = ba.reshape(b, 1, LANES)
    return pl.pallas_call(
        _gdn_step_kernel,
        grid=(b,),
        in_specs=[pl.BlockSpec((1, 1, QKV_WIDTH), lambda i: (i, 0, 0)),
                  pl.BlockSpec((1, 1, Z_WIDTH), lambda i: (i, 0, 0)),
                  pl.BlockSpec((1, 1, LANES), lambda i: (i, 0, 0)),
                  pl.BlockSpec((1, CONV_W - 1, QKV_WIDTH), lambda i: (i, 0, 0)),
                  pl.BlockSpec((1, GDN_HEADS, GDN_DK, GDN_DV), lambda i: (i, 0, 0, 0)),
                  _full_spec(conv_w.shape), _full_spec(alog_row.shape),
                  _full_spec(dt_row.shape), _full_spec(gdn_norm.shape)],
        out_specs=[pl.BlockSpec((1, 1, Z_WIDTH), lambda i: (i, 0, 0)),
                   pl.BlockSpec((1, GDN_HEADS, GDN_DK, GDN_DV), lambda i: (i, 0, 0, 0)),
                   pl.BlockSpec((1, CONV_W - 1, QKV_WIDTH), lambda i: (i, 0, 0))],
        out_shape=[jax.ShapeDtypeStruct((b, 1, Z_WIDTH), F32),
                   jax.ShapeDtypeStruct((b, GDN_HEADS, GDN_DK, GDN_DV), F32),
                   jax.ShapeDtypeStruct((b, CONV_W - 1, QKV_WIDTH), F32)],
        compiler_params=_params("parallel"),
        name="gdn_step",
    )(qkv3, z3, ba3, conv_state, state, conv_w, alog_row, dt_row, gdn_norm)


def _mla_queries_latent(x, cos2, sin2, g_mix, w_qd, w_c, w_kr, g_qa, w_uq, g_kva,
                        g_qn, g_qr, g_kr):
    xn = _rms(x, g_mix).astype(BF16)
    qa = _rms(_dot(xn, w_qd), g_qa).astype(BF16)
    q = _dot(qa, w_uq)
    qn, qr = [], []
    for h in range(MLA_HEADS):
        qn.append(_rms(q[:, h * QK_PAD:h * QK_PAD + NOPE_DIM], g_qn))
        qr.append(_rope(_rms(q[:, h * QK_PAD + NOPE_DIM:(h + 1) * QK_PAD], g_qr, ROPE_DIM),
                        cos2, sin2))
    c = _rms(_dot(xn, w_c), g_kva)
    kr = _rope(_rms(_dot(xn, w_kr), g_kr, ROPE_DIM), cos2, sin2)
    return qn, qr, c, kr


def _mla_pre_kernel(x_ref, cos_ref, sin_ref, gmix_ref, wqd_ref, wc_ref, wkr_ref, gqa_ref,
                    wuq_ref, gkva_ref, gqn_ref, gqr_ref, gkr_ref, wuk_ref, wuv_ref, gkn_ref,
                    q_ref, k_ref, v_ref, lat_ref):
    qn, qr, c, kr = _mla_queries_latent(
        x_ref[0], cos_ref[...], sin_ref[...], gmix_ref[...], wqd_ref[...], wc_ref[...],
        wkr_ref[...], gqa_ref[...], wuq_ref[...], gkva_ref[...], gqn_ref[...] * ATTN_Q_SCALE,
        gqr_ref[...] * ATTN_Q_SCALE, gkr_ref[...])
    lat_ref[0, :, 0:KV_LORA] = c
    lat_ref[0, :, KV_LORA:LATENT_WIDTH] = kr[:, 0:ROPE_DIM]
    c_bf = c.astype(BF16)
    kr_bf = kr.astype(BF16)
    kfull = _dot(c_bf, wuk_ref[...])
    v_ref[0, 0] = _dot_nt(wuv_ref[...], c_bf).astype(BF16)
    gkn = gkn_ref[...]
    for h in range(MLA_HEADS):
        q_ref[0, :, h * QK_PAD:h * QK_PAD + NOPE_DIM] = qn[h].astype(BF16)
        q_ref[0, :, h * QK_PAD + NOPE_DIM:(h + 1) * QK_PAD] = qr[h].astype(BF16)
        kn = _rms(kfull[:, h * NOPE_DIM:(h + 1) * NOPE_DIM], gkn)
        k_ref[0, :, h * QK_PAD:h * QK_PAD + NOPE_DIM] = kn.astype(BF16)
        k_ref[0, :, h * QK_PAD + NOPE_DIM:(h + 1) * QK_PAD] = kr_bf


def _mla_pre(x, cos2, sin2, mw):
    b, t, d = x.shape
    tm = min(ATTN_TILE, t)
    consts = [mw["g_mix"], mw["w_qd"], mw["w_c"], mw["w_kr"], mw["g_qa"], mw["w_uq"],
              mw["g_kva"], mw["g_qn"], mw["g_qr"], mw["g_kr"], mw["w_uk"], mw["w_uv_t"],
              mw["g_kn"]]
    return pl.pallas_call(
        _mla_pre_kernel,
        grid=(b, t // tm),
        in_specs=[pl.BlockSpec((1, tm, d), lambda i, j: (i, j, 0)),
                  pl.BlockSpec((tm, LANES), lambda i, j: (j, 0)),
                  pl.BlockSpec((tm, LANES), lambda i, j: (j, 0))]
                 + [_full_spec(a.shape) for a in consts],
        out_specs=[pl.BlockSpec((1, tm, MLA_HEADS * QK_PAD), lambda i, j: (i, j, 0)),
                   pl.BlockSpec((1, tm, MLA_HEADS * QK_PAD), lambda i, j: (i, j, 0)),
                   pl.BlockSpec((1, 1, MLA_HEADS * V_DIM, tm), lambda i, j: (i, j, 0, 0)),
                   pl.BlockSpec((1, tm, LATENT_WIDTH), lambda i, j: (i, j, 0))],
        out_shape=[jax.ShapeDtypeStruct((b, t, MLA_HEADS * QK_PAD), BF16),
                   jax.ShapeDtypeStruct((b, t, MLA_HEADS * QK_PAD), BF16),
                   jax.ShapeDtypeStruct((b, t // tm, MLA_HEADS * V_DIM, tm), BF16),
                   jax.ShapeDtypeStruct((b, t, LATENT_WIDTH), F32)],
        compiler_params=_params("parallel", "parallel"),
        name="mla_pre",
    )(x, cos2, sin2, *consts)


def _mla_pre_sample_kernel(x_ref, cos_ref, sin_ref, gmix_ref, wqd_ref, wc_ref, wkr_ref, gqa_ref,
                           wuq_ref, gkva_ref, gqn_ref, gqr_ref, gkr_ref, wuk_ref, gkn_ref,
                           qa_ref, qr_ref, lat_ref):
    qn, qr, c, kr = _mla_queries_latent(
        x_ref[...], cos_ref[...], sin_ref[...], gmix_ref[...], wqd_ref[...], wc_ref[...],
        wkr_ref[...], gqa_ref[...], wuq_ref[...], gkva_ref[...], gqn_ref[...], gqr_ref[...],
        gkr_ref[...])
    lat_ref[:, 0:KV_LORA] = c
    lat_ref[:, KV_LORA:LATENT_WIDTH] = kr[:, 0:ROPE_DIM]
    gkn = gkn_ref[...]
    for h in range(MLA_HEADS):
        hi, lo = _split2(qn[h] * gkn)
        w_h = wuk_ref[:, h * NOPE_DIM:(h + 1) * NOPE_DIM]
        qa_ref[h] = _dot_nt(hi, w_h) + _dot_nt(lo, w_h)
        qr_ref[h] = qr[h]


def _mla_pre_sample(x2, cos2, sin2, mw):
    m = x2.shape[0]
    consts = [mw["g_mix"], mw["w_qd"], mw["w_c"], mw["w_kr"], mw["g_qa"], mw["w_uq"],
              mw["g_kva"], mw["g_qn"], mw["g_qr"], mw["g_kr"], mw["w_uk"], mw["g_kn"]]
    args = [x2, cos2, sin2] + consts
    return pl.pallas_call(
        _mla_pre_sample_kernel,
        grid=(1,),
        in_specs=[_full_spec(a.shape) for a in args],
        out_specs=[_full_spec((MLA_HEADS, m, KV_LORA)), _full_spec((MLA_HEADS, m, LANES)),
                   _full_spec((m, LATENT_WIDTH))],
        out_shape=[jax.ShapeDtypeStruct((MLA_HEADS, m, KV_LORA), F32),
                   jax.ShapeDtypeStruct((MLA_HEADS, m, LANES), F32),
                   jax.ShapeDtypeStruct((m, LATENT_WIDTH), F32)],
        compiler_params=_params("arbitrary"),
        name="mla_pre_sample",
    )(*args)


ATTN_HEADS_PER_STEP = 8


def _attn_kernel(q_ref, k_ref, vt_ref, o_ref, *, tq):
    qi = pl.program_id(2)
    hs = range(ATTN_HEADS_PER_STEP)
    q = [q_ref[0, :, g * QK_PAD:(g + 1) * QK_PAD] for g in hs]
    key = lax.broadcasted_iota(jnp.int32, (tq, tq), 0)
    qry = lax.broadcasted_iota(jnp.int32, (tq, tq), 1)

    def raw_scores(j):
        start = pl.multiple_of(j * tq, tq)
        return tuple(_dot_nt(k_ref[0, pl.ds(start, tq), g * QK_PAD:(g + 1) * QK_PAD], q[g])
                     for g in hs)

    def consume(j, s_raw, stats, diagonal):
        vt = [vt_ref[0, j, g * V_DIM:(g + 1) * V_DIM, :] for g in hs]
        s = list(s_raw)
        if diagonal:
            s = [jnp.where(key <= qry, s[g], -jnp.inf) for g in hs]
        m_new = [jnp.maximum(stats[g][0], jnp.max(s[g], axis=0, keepdims=True)) for g in hs]
        p = [jnp.exp2(s[g] - m_new[g]) for g in hs]
        pv = [_dot(vt[g], p[g].astype(BF16)) for g in hs]
        out = []
        for g in hs:
            m, l, acc = stats[g]
            alpha = jnp.exp2(m - m_new[g])
            out.append((m_new[g], alpha * l + jnp.sum(p[g], axis=0, keepdims=True),
                        alpha * acc + pv[g]))
        return tuple(out)

    init = tuple((jnp.full((1, tq), -jnp.inf, F32), jnp.zeros((1, tq), F32),
                  jnp.zeros((V_DIM, tq), F32)) for _ in hs)
    stats = lax.fori_loop(0, qi, lambda j, st: consume(j, raw_scores(j), st, False), init)
    stats = consume(qi, raw_scores(qi), stats, True)
    for g in hs:
        _, l, acc = stats[g]
        o_ref[0, :, g * V_DIM:(g + 1) * V_DIM] = (acc / l).T.astype(BF16)


def _mla_prompt_attention(q_cat, k_cat, v_t):
    b, t, _ = q_cat.shape
    tq = min(ATTN_TILE, t)
    hps = ATTN_HEADS_PER_STEP
    return pl.pallas_call(
        functools.partial(_attn_kernel, tq=tq),
        grid=(b, MLA_HEADS // hps, t // tq),
        in_specs=[pl.BlockSpec((1, tq, hps * QK_PAD), lambda i, h, j: (i, j, h)),
                  pl.BlockSpec((1, t, hps * QK_PAD), lambda i, h, j: (i, 0, h)),
                  pl.BlockSpec((1, t // tq, hps * V_DIM, tq), lambda i, h, j: (i, 0, h, 0))],
        out_specs=pl.BlockSpec((1, tq, hps * V_DIM), lambda i, h, j: (i, j, h)),
        out_shape=jax.ShapeDtypeStruct((b, t, MLA_HEADS * V_DIM), BF16),
        compiler_params=_params("parallel", "parallel", "arbitrary"),
        name="mla_prompt_attention",
    )(q_cat, k_cat, v_t)


def _dec_attn_kernel(pt_ref, *refs, pps):
    del pt_ref
    page_refs = refs[:pps]
    (wukt_ref, qa_ref, qr_ref, lnew_ref, ctx_ref,
     lhs_ref, qr16_ref, m_ref, l_ref, acc_ref) = refs[pps:]
    b = pl.program_id(0)
    j = pl.program_id(1)
    nj = pl.num_programs(1)
    n_k = MLA_HEADS * NOPE_DIM

    @pl.when((b == 0) & (j == 0))
    def _():
        lhs_ref[0:n_k, :] = wukt_ref[...]

    @pl.when(j == 0)
    def _():
        qa = jnp.concatenate([qa_ref[h, 0] for h in range(MLA_HEADS)]
                             + [jnp.zeros((8, KV_LORA), F32)], axis=0)
        lhs_ref[n_k:n_k + 16, :] = qa.astype(BF16)
        qr = jnp.concatenate([qr_ref[h, 0] for h in range(MLA_HEADS)]
                             + [jnp.zeros((8, LANES), F32)], axis=0)
        qr16_ref[...] = qr.astype(BF16)
        m_ref[...] = jnp.full(m_ref.shape, -jnp.inf, F32)
        l_ref[...] = jnp.zeros(l_ref.shape, F32)
        acc_ref[...] = jnp.zeros(acc_ref.shape, F32)

    def scores(res, sr):
        kt = res[0:n_k]
        ss = jnp.sum((kt * kt).reshape(MLA_HEADS, NOPE_DIM, res.shape[1]), axis=1)
        sn = res[n_k:n_k + MLA_HEADS]
        return (sn * lax.rsqrt(ss * (1.0 / NOPE_DIM) + EPS) + sr[0:MLA_HEADS]) * MLA_SCALE

    c_bf, s_parts = [], []
    for i in range(0, pps, 2):
        tile = jnp.concatenate([page_refs[i][...], page_refs[i + 1][...]], axis=1)
        c_bf.append(tile[0:KV_LORA].astype(BF16))
        kr_bf = tile[KV_LORA:LATENT_WIDTH].astype(BF16)
        s_parts.append(scores(_dot(lhs_ref[...], c_bf[-1]),
                              _dot(qr16_ref[:, 0:ROPE_DIM], kr_bf)))
    s = jnp.concatenate(s_parts, axis=1)
    m_old = m_ref[...]
    m_new = jnp.maximum(m_old, jnp.max(s, axis=-1, keepdims=True))
    alpha = jnp.exp(m_old - m_new)
    p = jnp.exp(s - m_new)
    l_ref[...] = alpha * l_ref[...] + jnp.sum(p, axis=-1, keepdims=True)
    p_bf = p.astype(BF16)
    n_tok = c_bf[0].shape[1]
    acc = alpha * acc_ref[...]
    for t, c_t in enumerate(c_bf):
        acc = acc + _dot_nt(p_bf[:, t * n_tok:(t + 1) * n_tok], c_t)
    acc_ref[...] = acc
    m_ref[...] = m_new

    @pl.when(j == nj - 1)
    def _():
        ln = lnew_ref[0]
        c_new = ln[:, 0:KV_LORA]
        c8 = jnp.broadcast_to(c_new, (8, KV_LORA)).astype(BF16)
        kr8 = jnp.broadcast_to(ln[:, KV_LORA:LATENT_WIDTH], (8, ROPE_DIM)).astype(BF16)
        s_new = scores(_dot_nt(lhs_ref[...], c8),
                       _dot_nt(qr16_ref[:, 0:ROPE_DIM], kr8))[:, 0:1]
        m_old = m_ref[...]
        m_new = jnp.maximum(m_old, s_new)
        alpha = jnp.exp(m_old - m_new)
        p_new = jnp.exp(s_new - m_new)
        ctx = (alpha * acc_ref[...] + p_new * c_new) / (alpha * l_ref[...] + p_new)
        for h in range(MLA_HEADS):
            ctx_ref[h, 0] = ctx[h:h + 1]


def _dec_pages_per_step(n_pages):
    pps = DEC_PAGES_PER_STEP
    while n_pages % pps:
        pps //= 2
    assert pps >= 2
    return pps


def _dec_scratch():
    return [pltpu.VMEM((MLA_HEADS * NOPE_DIM + 16, KV_LORA), BF16),
            pltpu.VMEM((16, LANES), BF16),
            pltpu.VMEM((MLA_HEADS, 1), F32),
            pltpu.VMEM((MLA_HEADS, 1), F32),
            pltpu.VMEM((MLA_HEADS, KV_LORA), F32)]


def _mla_decode_attention(cache, layer, page_table, wuk_t, qa, qr, lat_new):
    bsz, n_pages = page_table.shape
    page = cache.shape[3]
    pps = _dec_pages_per_step(n_pages)
    qa4 = qa.reshape(MLA_HEADS, bsz, 1, KV_LORA)
    qr4 = qr.reshape(MLA_HEADS, bsz, 1, LANES)
    ln3 = lat_new.reshape(bsz, 1, LATENT_WIDTH)

    def page_spec(i):
        return pl.BlockSpec((pl.Squeezed(), pl.Squeezed(), LATENT_WIDTH, page),
                            lambda b, j, pt: (layer, pt[b, j * pps + i], 0, 0))

    grid_spec = pltpu.PrefetchScalarGridSpec(
        num_scalar_prefetch=1,
        grid=(bsz, n_pages // pps),
        in_specs=[page_spec(i) for i in range(pps)] + [
            pl.BlockSpec(wuk_t.shape, lambda b, j, pt: (0, 0)),
            pl.BlockSpec((MLA_HEADS, 1, 1, KV_LORA), lambda b, j, pt: (0, b, 0, 0)),
            pl.BlockSpec((MLA_HEADS, 1, 1, LANES), lambda b, j, pt: (0, b, 0, 0)),
            pl.BlockSpec((1, 1, LATENT_WIDTH), lambda b, j, pt: (b, 0, 0))],
        out_specs=pl.BlockSpec((MLA_HEADS, 1, 1, KV_LORA), lambda b, j, pt: (0, b, 0, 0)),
        scratch_shapes=_dec_scratch())
    ctx = pl.pallas_call(
        functools.partial(_dec_attn_kernel, pps=pps),
        grid_spec=grid_spec,
        out_shape=jax.ShapeDtypeStruct((MLA_HEADS, bsz, 1, KV_LORA), F32),
        compiler_params=_params("arbitrary", "arbitrary"),
        name="mla_decode_attention",
    )(page_table, *([cache] * pps), wuk_t, qa4, qr4, ln3)
    return ctx.reshape(MLA_HEADS, bsz, KV_LORA)


def _dec_post_kernel(ctx_ref, wuv_ref, y_ref):
    for h in range(MLA_HEADS):
        hi, lo = _split2(ctx_ref[h])
        w_h = wuv_ref[:, h * V_DIM:(h + 1) * V_DIM]
        y_ref[:, h * V_DIM:(h + 1) * V_DIM] = (_dot(hi, w_h) + _dot(lo, w_h)).astype(BF16)


def _dec_post(ctx, w_uv):
    bsz = ctx.shape[1]
    return pl.pallas_call(
        _dec_post_kernel,
        grid=(1,),
        in_specs=[_full_spec(ctx.shape), _full_spec(w_uv.shape)],
        out_specs=_full_spec((bsz, MLA_HEADS * V_DIM)),
        out_shape=jax.ShapeDtypeStruct((bsz, MLA_HEADS * V_DIM), BF16),
        compiler_params=_params("arbitrary"),
        name="mla_decode_values",
    )(ctx, w_uv)


def _merge_mlp_kernel(x_ref, ya_ref, yb_ref, gmix_ref, wg_ref, wya_ref, wyb_ref, wo_ref,
                      gmlp_ref, wup_ref, wdn_ref, y_ref):
    tm, d = x_ref.shape
    d_ff = wup_ref.shape[1]
    n_grp = max(1, tm // MERGE_ROWS_PER_GROUP)
    rows = tm // n_grp
    grp = range(n_grp)
    sl = [slice(r * rows, (r + 1) * rows) for r in grp]
    x = [x_ref[sl[r], :] for r in grp]
    xn = [_rms(x[r], gmix_ref[...]).astype(BF16) for r in grp]
    gates = [_sigmoid(_dot(xn[r], wg_ref[...])) for r in grp]
    pa = [_dot(ya_ref[sl[r], :].astype(BF16), wya_ref[...]) for r in grp]
    pb = [_dot(yb_ref[sl[r], :].astype(BF16), wyb_ref[...]) for r in grp]
    mix = [(gates[r][:, :d] * pa[r] + gates[r][:, d:] * pb[r]).astype(BF16) for r in grp]
    x1 = [x[r] + _dot(mix[r], wo_ref[...]) for r in grp]
    hin = [_rms(x1[r], gmlp_ref[...]).astype(BF16) for r in grp]
    y = x1
    for f in range(0, d_ff, MLP_FF_BLOCK):
        hmid = [jnp.maximum(_dot(hin[r], wup_ref[:, f:f + MLP_FF_BLOCK]), 0.0) for r in grp]
        y = [y[r] + _dot((hmid[r] * hmid[r]).astype(BF16), wdn_ref[f:f + MLP_FF_BLOCK, :])
             for r in grp]
    for r in grp:
        y_ref[sl[r], :] = y[r]


MERGE_ROWS_PER_GROUP = 256
MLP_FF_BLOCK = 1024


def _merge_mlp(x2, ya, yb, ew):
    m, d = x2.shape
    tm = min(512, m)
    consts = [ew["g_mix"], ew["w_gates"], ew["w_ya"], ew["w_yb"], ew["w_o"], ew["g_mlp"],
              ew["w_up"], ew["w_down"]]

    def const_spec(a):
        nd = a.ndim
        return pl.BlockSpec(a.shape, lambda i: (0,) * nd, pipeline_mode=pl.Buffered(1))

    return pl.pallas_call(
        _merge_mlp_kernel,
        grid=(m // tm,),
        in_specs=[pl.BlockSpec((tm, d), lambda i: (i, 0)),
                  pl.BlockSpec((tm, ya.shape[1]), lambda i: (i, 0)),
                  pl.BlockSpec((tm, yb.shape[1]), lambda i: (i, 0))]
                 + [const_spec(a) for a in consts],
        out_specs=pl.BlockSpec((tm, d), lambda i: (i, 0)),
        out_shape=jax.ShapeDtypeStruct((m, d), F32),
        compiler_params=_params("parallel"),
        name="merge_mlp",
    )(x2, ya, yb, *consts)


def _rope_tables(pos):
    inv = jnp.power(ROPE_THETA, -jnp.arange(0, ROPE_DIM, 2, dtype=F32) / ROPE_DIM)
    ang = pos.astype(F32)[:, None] * inv[None, :]
    cos, sin = jnp.cos(ang), jnp.sin(ang)
    zeros = jnp.zeros((pos.shape[0], LANES - ROPE_DIM), F32)
    return (jnp.concatenate([cos, cos, zeros], axis=1),
            jnp.concatenate([-sin, sin, zeros], axis=1))


def _pad_lanes(a, width=LANES, offset=0):
    return jnp.zeros((1, width), F32).at[0, offset:offset + a.shape[0]].set(a.astype(F32))


def _layer_weights(l, norm_mix, w_in, conv_w, a_log, dt_bias, gdn_norm, q_a_norm, w_uq,
                   kv_a_norm, w_uk, w_uv, q_norm_nope, q_norm_rope, k_norm_nope, k_norm_rope,
                   w_ya, w_yb, w_o, norm_mlp, w_up, w_down):
    d = w_in.shape[1]
    off_z = QKV_WIDTH
    off_b = off_z + Z_WIDTH
    off_qd = off_b + 2 * GDN_HEADS
    off_kv = off_qd + Q_LORA
    off_g = off_kv + LATENT_WIDTH
    wi = w_in[l]
    g_mix = norm_mix[l].reshape(1, d)
    gw = {
        "g_mix": g_mix,
        "w_qkv": wi[:, :off_z].astype(BF16),
        "w_z": wi[:, off_z:off_b].astype(BF16),
        "w_ba": jnp.pad(wi[:, off_b:off_qd], ((0, 0), (0, LANES - 2 * GDN_HEADS))).astype(BF16),
        "conv_w": conv_w[l],
        "alog_row": _pad_lanes(a_log[l], offset=GDN_HEADS),
        "dt_row": _pad_lanes(dt_bias[l], offset=GDN_HEADS),
        "gdn_norm": gdn_norm[l].reshape(1, GDN_DV),
    }
    wq = w_uq[l].reshape(Q_LORA, MLA_HEADS, NOPE_DIM + ROPE_DIM)
    wq = jnp.pad(wq, ((0, 0), (0, 0), (0, QK_PAD - NOPE_DIM - ROPE_DIM)))
    mw = {
        "g_mix": g_mix,
        "w_qd": wi[:, off_qd:off_kv].astype(BF16),
        "w_c": wi[:, off_kv:off_kv + KV_LORA].astype(BF16),
        "w_kr": jnp.pad(wi[:, off_kv + KV_LORA:off_g], ((0, 0), (0, LANES - ROPE_DIM))).astype(BF16),
        "g_qa": q_a_norm[l].reshape(1, Q_LORA),
        "w_uq": wq.reshape(Q_LORA, MLA_HEADS * QK_PAD).astype(BF16),
        "g_kva": kv_a_norm[l].reshape(1, KV_LORA),
        "g_qn": q_norm_nope[l].reshape(1, NOPE_DIM),
        "g_qr": _pad_lanes(q_norm_rope[l]),
        "g_kr": _pad_lanes(k_norm_rope[l]),
        "g_kn": k_norm_nope[l].reshape(1, NOPE_DIM),
        "w_uk": w_uk[l].astype(BF16),
        "w_uk_t": w_uk[l].T.astype(BF16),
        "w_uv": w_uv[l].astype(BF16),
        "w_uv_t": w_uv[l].T.astype(BF16),
    }
    ew = {
        "g_mix": g_mix,
        "w_gates": wi[:, off_g:].astype(BF16),
        "w_ya": w_ya[l].astype(BF16),
        "w_yb": w_yb[l].astype(BF16),
        "w_o": w_o[l].astype(BF16),
        "g_mlp": norm_mlp[l].reshape(1, d),
        "w_up": w_up[l].astype(BF16),
        "w_down": w_down[l].astype(BF16),
    }
    return gw, mw, ew


def kernel(x_prompt, x_sample, cache_mla, state_gdn, state_conv, page_table, norm_mix, w_in,
           conv_w, a_log, dt_bias, gdn_norm, q_a_norm, w_uq, kv_a_norm, w_uk, w_uv,
           q_norm_nope, q_norm_rope, k_norm_nope, k_norm_rope, w_ya, w_yb, w_o, norm_mlp,
           w_up, w_down):
    depth = w_in.shape[0]
    bp, tp, d = x_prompt.shape
    bs, ts, _ = x_sample.shape
    assert ts == 1 and tp % GDN_CHUNK == 0
    past_len = page_table.shape[1] * cache_mla.shape[2]
    cache_t = jnp.swapaxes(cache_mla, 2, 3)
    cos_p, sin_p = _rope_tables(jnp.arange(tp))
    cos_s, sin_s = _rope_tables(past_len + jnp.zeros((bs,), jnp.int32))

    x_p, x_s = x_prompt, x_sample.reshape(bs, d)
    rows_p, gdn_p, conv_p, rows_s, gdn_s, conv_s = [], [], [], [], [], []
    for l in range(depth):
        gw, mw, ew = _layer_weights(
            l, norm_mix, w_in, conv_w, a_log, dt_bias, gdn_norm, q_a_norm, w_uq, kv_a_norm,
            w_uk, w_uv, q_norm_nope, q_norm_rope, k_norm_nope, k_norm_rope, w_ya, w_yb, w_o,
            norm_mlp, w_up, w_down)
        gdn_consts = (gw["conv_w"], gw["alog_row"], gw["dt_row"], gw["gdn_norm"])

        xp2 = x_p.reshape(bp * tp, d)
        yc, z, ba, cs_p = _gdn_proj_conv(x_p, gw["g_mix"], gw["w_qkv"], gw["w_z"], gw["w_ba"],
                                         gw["conv_w"])
        ya_p, s_p = _gdn_chunked(yc.reshape(bp, tp, QKV_WIDTH), z.reshape(bp, tp, Z_WIDTH),
                                 ba.reshape(bp, tp, LANES), *gdn_consts[1:])
        q_cat, k_cat, v_p, lat_p = _mla_pre(x_p, cos_p, sin_p, mw)
        yb_p = _mla_prompt_attention(q_cat, k_cat, v_p)
        y_p = _merge_mlp(xp2, ya_p.reshape(bp * tp, Z_WIDTH), yb_p.reshape(bp * tp, -1), ew)

        qkv, z, ba = _gdn_proj(x_s, gw["g_mix"], gw["w_qkv"], gw["w_z"], gw["w_ba"])
        ya_s, s_s, cs_s = _gdn_step(qkv, z, ba, state_conv[l], state_gdn[l], *gdn_consts)
        qa, qr, lat_s = _mla_pre_sample(x_s, cos_s, sin_s, mw)
        ctx = _mla_decode_attention(cache_t, l, page_table, mw["w_uk_t"], qa, qr, lat_s)
        yb_s = _dec_post(ctx, mw["w_uv"])
        y_s = _merge_mlp(x_s, ya_s.reshape(bs, Z_WIDTH), yb_s, ew)

        x_p, x_s = y_p.reshape(bp, tp, d), y_s
        rows_p.append(lat_p)
        gdn_p.append(s_p)
        conv_p.append(cs_p)
        rows_s.append(lat_s.reshape(bs, 1, LATENT_WIDTH))
        gdn_s.append(s_s)
        conv_s.append(cs_s)
    return (x_p, x_s.reshape(bs, 1, d), jnp.stack(rows_p), jnp.stack(gdn_p), jnp.stack(conv_p),
            jnp.stack(rows_s), jnp.stack(gdn_s), jnp.stack(conv_s))
```

```python
import functools

import jax
import jax.numpy as jnp
from jax import lax
from jax.experimental import pallas as pl
from jax.experimental.pallas import tpu as pltpu

F32 = jnp.float32
BF16 = jnp.bfloat16

EPS = 1e-6
GDN_HEADS = 8
GDN_DK = 128
GDN_DV = 128
CONV_W = 4
GDN_CHUNK = 64
MLA_HEADS = 8
Q_LORA = 512
KV_LORA = 512
NOPE_DIM = 128
ROPE_DIM = 64
V_DIM = 128
ROPE_THETA = 10000.0
MLA_SCALE = (NOPE_DIM + ROPE_DIM) ** -0.5
LOG2_E = 1.4426950408889634
ATTN_Q_SCALE = MLA_SCALE * LOG2_E
LATENT_WIDTH = KV_LORA + ROPE_DIM
QKV_WIDTH = GDN_HEADS * (2 * GDN_DK + GDN_DV)
Z_WIDTH = GDN_HEADS * GDN_DV

LANES = 128
QK_PAD = 256
VMEM_LIMIT = 56 * 1024 * 1024
DEC_PAGES_PER_STEP = 64
ATTN_TILE = 256


def _dot(a, b):
    return jnp.dot(a, b, preferred_element_type=F32)


def _dot_nt(a, b):
    return lax.dot_general(a, b, (((1,), (1,)), ((), ())), preferred_element_type=F32)


def _dot_tn(a, b):
    return lax.dot_general(a, b, (((0,), (0,)), ((), ())), preferred_element_type=F32)


def _split2(x):
    hi = x.astype(BF16)
    lo = (x - hi.astype(F32)).astype(BF16)
    return hi, lo


def _split3(x):
    hi = x.astype(BF16)
    r = x - hi.astype(F32)
    mid = r.astype(BF16)
    lo = (r - mid.astype(F32)).astype(BF16)
    return hi, mid, lo


def _rms(x, g, n=None):
    n = x.shape[-1] if n is None else n
    ss = jnp.sum(x * x, axis=-1, keepdims=True) * (1.0 / n)
    return x * lax.rsqrt(ss + EPS) * g


def _sigmoid(x):
    return 1.0 / (1.0 + jnp.exp(-x))


def _silu(x):
    h = 0.5 * x
    return h + h * jnp.tanh(h)


def _softplus(x):
    return jnp.maximum(x, 0.0) + jnp.log1p(jnp.exp(-jnp.abs(x)))


def _rope(x, cos2, sin2):
    lane = lax.broadcasted_iota(jnp.int32, x.shape, 1)
    fwd = pltpu.roll(x, LANES - ROPE_DIM // 2, axis=1)
    bwd = pltpu.roll(x, ROPE_DIM // 2, axis=1)
    swapped = jnp.where(lane < ROPE_DIM // 2, fwd, bwd)
    return x * cos2 + swapped * sin2


def _full_spec(shape):
    nd = len(shape)
    return pl.BlockSpec(shape, lambda *_: (0,) * nd)


def _params(*sem):
    return pltpu.CompilerParams(dimension_semantics=sem, vmem_limit_bytes=VMEM_LIMIT)


def _gdn_proj_kernel(x_ref, g_ref, wqkv_ref, wz_ref, wba_ref, qkv_ref, z_ref, ba_ref):
    xn = _rms(x_ref[...], g_ref[...]).astype(BF16)
    qkv_ref[...] = _dot(xn, wqkv_ref[...])
    z_ref[...] = _dot(xn, wz_ref[...])
    ba_ref[...] = _dot(xn, wba_ref[...])


def _gdn_proj(x2, g, wqkv, wz, wba):
    m, d = x2.shape
    tm = min(256, m)
    return pl.pallas_call(
        _gdn_proj_kernel,
        grid=(m // tm,),
        in_specs=[pl.BlockSpec((tm, d), lambda i: (i, 0)),
                  _full_spec(g.shape), _full_spec(wqkv.shape), _full_spec(wz.shape),
                  _full_spec(wba.shape)],
        out_specs=[pl.BlockSpec((tm, QKV_WIDTH), lambda i: (i, 0)),
                   pl.BlockSpec((tm, Z_WIDTH), lambda i: (i, 0)),
                   pl.BlockSpec((tm, LANES), lambda i: (i, 0))],
        out_shape=[jax.ShapeDtypeStruct((m, QKV_WIDTH), F32),
                   jax.ShapeDtypeStruct((m, Z_WIDTH), F32),
                   jax.ShapeDtypeStruct((m, LANES), F32)],
        compiler_params=_params("parallel"),
        name="gdn_proj",
    )(x2, g, wqkv, wz, wba)


PROJ_CONV_COLS = 512
assert CONV_W == 4


def _gdn_proj_conv_kernel(x_ref, g_ref, wqkv_ref, wz_ref, wba_ref, cw_ref,
                          y_ref, z_ref, ba_ref, cs_ref, xp_ref, *, tiles_per_seq):
    i = pl.program_id(0)
    tm = x_ref.shape[0]

    @pl.when(i % tiles_per_seq == 0)
    def _():
        xp_ref[...] = jnp.zeros((8, QKV_WIDTH), F32)

    xn = _rms(x_ref[...], g_ref[...]).astype(BF16)
    row8 = lax.broadcasted_iota(jnp.int32, (8, PROJ_CONV_COLS), 0)
    for c0 in range(0, QKV_WIDTH, PROJ_CONV_COLS):
        cb = slice(c0, c0 + PROJ_CONV_COLS)
        raw = _dot(xn, wqkv_ref[:, cb])
        halo = xp_ref[:, cb]
        cw = cw_ref[:, cb]
        back2 = pltpu.roll(raw, 2, axis=0)
        x2 = jnp.concatenate([jnp.where(row8 < 2, pltpu.roll(halo, 2, axis=0), back2[0:8]),
                              back2[8:]], axis=0)
        u = raw * cw[2:3] + x2 * cw[0:1]
        u_prev = halo[7:8] * cw[2:3] + halo[5:6] * cw[0:1]
        back1 = pltpu.roll(u, 1, axis=0)
        su = jnp.concatenate([jnp.where(row8 < 1, u_prev, back1[0:8]), back1[8:]], axis=0)
        y_ref[:, cb] = _silu(raw * cw[3:4] + x2 * cw[1:2] + su)
        xp_ref[:, cb] = raw[tm - 8:tm]
    z_ref[...] = _dot(xn, wz_ref[...])
    ba_ref[...] = _dot(xn, wba_ref[...])

    @pl.when(i % tiles_per_seq == tiles_per_seq - 1)
    def _():
        cs_ref[0] = xp_ref[8 - (CONV_W - 1):8, :]


def _gdn_proj_conv(x, g, wqkv, wz, wba, conv_w):
    b, t, d = x.shape
    tm = min(256, t)
    assert t % tm == 0
    tiles_per_seq = t // tm
    m = b * t
    return pl.pallas_call(
        functools.partial(_gdn_proj_conv_kernel, tiles_per_seq=tiles_per_seq),
        grid=(m // tm,),
        in_specs=[pl.BlockSpec((tm, d), lambda i: (i, 0)),
                  _full_spec(g.shape), _full_spec(wqkv.shape), _full_spec(wz.shape),
                  _full_spec(wba.shape), _full_spec(conv_w.shape)],
        out_specs=[pl.BlockSpec((tm, QKV_WIDTH), lambda i: (i, 0)),
                   pl.BlockSpec((tm, Z_WIDTH), lambda i: (i, 0)),
                   pl.BlockSpec((tm, LANES), lambda i: (i, 0)),
                   pl.BlockSpec((1, CONV_W - 1, QKV_WIDTH), lambda i: (i // tiles_per_seq, 0, 0))],
        out_shape=[jax.ShapeDtypeStruct((m, QKV_WIDTH), F32),
                   jax.ShapeDtypeStruct((m, Z_WIDTH), F32),
                   jax.ShapeDtypeStruct((m, LANES), F32),
                   jax.ShapeDtypeStruct((b, CONV_W - 1, QKV_WIDTH), F32)],
        scratch_shapes=[pltpu.VMEM((8, QKV_WIDTH), F32)],
        compiler_params=_params("arbitrary"),
        name="gdn_proj_conv",
    )(x.reshape(m, d), g, wqkv, wz, wba, conv_w)


def _gate_rows(ba, alog_row, dt_row):
    lane = lax.broadcasted_iota(jnp.int32, ba.shape, 1)
    is_a = (lane >= GDN_HEADS) & (lane < 2 * GDN_HEADS)
    g = jnp.where(is_a, -jnp.exp(alog_row) * _softplus(ba + dt_row), 0.0)
    return _sigmoid(ba), g


def _gdn_chunk_kernel(y_ref, z_ref, ba_ref, alog_ref, dt_ref, gn_ref, ya_ref, s_ref):
    c = pl.program_id(1)
    C = GDN_CHUNK
    TB = y_ref.shape[1]
    n_sub = TB // C

    @pl.when(c == 0)
    def _():
        s_ref[...] = jnp.zeros(s_ref.shape, F32)

    y = y_ref[0]

    beta_full, g_full = _gate_rows(ba_ref[0], alog_ref[...], dt_ref[...])
    row = lax.broadcasted_iota(jnp.int32, (C, C), 0)
    col = lax.broadcasted_iota(jnp.int32, (C, C), 1)
    incl = row >= col
    strict = row > col
    rtb = lax.broadcasted_iota(jnp.int32, (TB, TB), 0)
    ctb = lax.broadcasted_iota(jnp.int32, (TB, TB), 1)
    tril = jnp.where((rtb >= ctb) & (rtb // C == ctb // C), 1.0, 0.0).astype(BF16)
    g_col = sum(_dot(tril, part) for part in _split3(g_full))
    r128 = lax.broadcasted_iota(jnp.int32, (LANES, LANES), 0)
    c128 = lax.broadcasted_iota(jnp.int32, (LANES, LANES), 1)
    eye = jnp.where(r128 == c128, 1.0, 0.0).astype(BF16)
    g_rows = sum(_dot_nt(eye, part) for part in _split3(g_col))

    z = z_ref[0]
    gn = gn_ref[...]
    heads = range(GDN_HEADS)
    units = [(n, h) for n in range(n_sub) for h in heads]
    q, k, gc, eg, g_last, kb, rhs_bf, rhs, decay, kq = ({} for _ in range(10))
    for n, h in units:
        r0 = n * C
        qh = y[r0:r0 + C, h * GDN_DK:(h + 1) * GDN_DK]
        kh = y[r0:r0 + C, GDN_HEADS * GDN_DK + h * GDN_DK:GDN_HEADS * GDN_DK + (h + 1) * GDN_DK]
        vh = y[r0:r0 + C,
               2 * GDN_HEADS * GDN_DK + h * GDN_DV:2 * GDN_HEADS * GDN_DK + (h + 1) * GDN_DV]
        u = (n, h)
        q[u] = qh * lax.rsqrt(jnp.sum(qh * qh, axis=-1, keepdims=True) + EPS) * (GDN_DK ** -0.5)
        k[u] = kh * lax.rsqrt(jnp.sum(kh * kh, axis=-1, keepdims=True) + EPS)
        beta = beta_full[r0:r0 + C, h:h + 1]
        gc[u] = g_col[r0:r0 + C, GDN_HEADS + h:GDN_HEADS + h + 1]
        gr = g_rows[GDN_HEADS + h:GDN_HEADS + h + 1, r0:r0 + C]
        decay[u] = jnp.where(incl, jnp.exp(jnp.where(incl, gc[u] - gr, 0.0)), 0.0)
        eg[u] = jnp.exp(gc[u])
        g_last[u] = gc[u][C - 1:C, :]
        kb[u] = k[u] * beta
        rhs[u] = jnp.concatenate([vh * beta, kb[u] * eg[u]], axis=1)
        rhs_bf[u] = rhs[u].astype(BF16)
    for u in units:
        kq[u] = _dot_nt(jnp.concatenate([kb[u], q[u]], axis=0).astype(BF16),
                        k[u].astype(BF16))
    pw = {u: jnp.where(strict, kq[u][:C] * decay[u], 0.0) for u in units}
    aqk = {u: (kq[u][C:] * decay[u]).astype(BF16) for u in units}
    n_acc = {u: -pw[u] for u in units}
    for _ in range(C.bit_length() - 2):
        pw_bf = {u: pw[u].astype(BF16) for u in units}
        pw = {u: _dot(pw_bf[u], pw_bf[u]) for u in units}
        n_acc = {u: n_acc[u] + pw[u] + _dot(n_acc[u].astype(BF16), pw[u].astype(BF16))
                 for u in units}
    sol = {}
    for u in units:
        n_hi, n_lo = _split2(n_acc[u])
        sol[u] = rhs[u] + _dot(n_hi, rhs_bf[u]) + _dot(n_lo, rhs_bf[u])
    wq = {u: jnp.concatenate([sol[u][:, GDN_DV:], q[u] * eg[u]], axis=0).astype(BF16)
          for u in units}
    kd = {u: (k[u] * jnp.exp(g_last[u] - gc[u])).astype(BF16) for u in units}
    s_cur = [s_ref[0, h] for h in heads]
    for n in range(n_sub):
        ws = [_dot(wq[(n, h)], s_cur[h].astype(BF16)) for h in heads]
        u_bf = [(sol[(n, h)][:, :GDN_DV] - ws[h][:C]).astype(BF16) for h in heads]
        o = [ws[h][C:] + _dot(aqk[(n, h)], u_bf[h]) for h in heads]
        s_cur = [jnp.exp(g_last[(n, h)]) * s_cur[h] + _dot_tn(kd[(n, h)], u_bf[h])
                 for h in heads]
        for h in heads:
            zh = z[n * C:(n + 1) * C, h * GDN_DV:(h + 1) * GDN_DV]
            ya_ref[0, n * C:(n + 1) * C, h * GDN_DV:(h + 1) * GDN_DV] = (
                _rms(o[h], gn) * _silu(zh)).astype(BF16)
    for h in heads:
        s_ref[0, h] = s_cur[h]


GDN_CHUNKS_PER_STEP = 2


def _gdn_chunked(y, z, ba, alog_row, dt_row, gdn_norm):
    b, t, _ = y.shape
    C = GDN_CHUNK * GDN_CHUNKS_PER_STEP
    assert t % C == 0
    nc = t // C
    return pl.pallas_call(
        _gdn_chunk_kernel,
        grid=(b, nc),
        in_specs=[pl.BlockSpec((1, C, QKV_WIDTH), lambda i, j: (i, j, 0)),
                  pl.BlockSpec((1, C, Z_WIDTH), lambda i, j: (i, j, 0)),
                  pl.BlockSpec((1, C, LANES), lambda i, j: (i, j, 0)),
                  _full_spec(alog_row.shape), _full_spec(dt_row.shape),
                  _full_spec(gdn_norm.shape)],
        out_specs=[pl.BlockSpec((1, C, Z_WIDTH), lambda i, j: (i, j, 0)),
                   pl.BlockSpec((1, GDN_HEADS, GDN_DK, GDN_DV), lambda i, j: (i, 0, 0, 0))],
        out_shape=[jax.ShapeDtypeStruct((b, t, Z_WIDTH), BF16),
                   jax.ShapeDtypeStruct((b, GDN_HEADS, GDN_DK, GDN_DV), F32)],
        compiler_params=_params("parallel", "arbitrary"),
        name="gdn_chunked",
    )(y, z, ba, alog_row, dt_row, gdn_norm)


def _gdn_step_kernel(qkv_ref, z_ref, ba_ref, cs_ref, s_ref, cw_ref, alog_ref, dt_ref, gn_ref,
                     ya_ref, snew_ref, csnew_ref):
    rowv = qkv_ref[0]
    cs = cs_ref[0]
    cw = cw_ref[...]
    y = rowv * cw[CONV_W - 1:CONV_W]
    for i in range(CONV_W - 1):
        y = y + cs[i:i + 1] * cw[i:i + 1]
    y = _silu(y)
    csnew_ref[0, 0:CONV_W - 2, :] = cs[1:CONV_W - 1]
    csnew_ref[0, CONV_W - 2:CONV_W - 1, :] = rowv

    beta_full, g_full = _gate_rows(ba_ref[0], alog_ref[...], dt_ref[...])
    r128 = lax.broadcasted_iota(jnp.int32, (GDN_DK, GDN_DK), 0)
    c128 = lax.broadcasted_iota(jnp.int32, (GDN_DK, GDN_DK), 1)
    eye = r128 == c128
    z = z_ref[0]
    gn = gn_ref[...]

    def to_col(r):
        return jnp.sum(jnp.where(eye, jnp.broadcast_to(r, (GDN_DK, GDN_DK)), 0.0),
                       axis=1, keepdims=True)

    heads = range(GDN_HEADS)
    q, k, v = [], [], []
    for h in heads:
        qh = y[:, h * GDN_DK:(h + 1) * GDN_DK]
        kh = y[:, GDN_HEADS * GDN_DK + h * GDN_DK:GDN_HEADS * GDN_DK + (h + 1) * GDN_DK]
        v.append(y[:, 2 * GDN_HEADS * GDN_DK + h * GDN_DV:2 * GDN_HEADS * GDN_DK + (h + 1) * GDN_DV])
        q.append(qh * lax.rsqrt(jnp.sum(qh * qh, axis=-1, keepdims=True) + EPS) * (GDN_DK ** -0.5))
        k.append(kh * lax.rsqrt(jnp.sum(kh * kh, axis=-1, keepdims=True) + EPS))
    k_col = [to_col(k[h]) for h in heads]
    q_col = [to_col(q[h]) for h in heads]
    s = [jnp.exp(g_full[:, GDN_HEADS + h:GDN_HEADS + h + 1]) * s_ref[0, h] for h in heads]
    u = [beta_full[:, h:h + 1] * (v[h] - jnp.sum(s[h] * k_col[h], axis=0, keepdims=True))
         for h in heads]
    s = [s[h] + k_col[h] * u[h] for h in heads]
    o = [jnp.sum(s[h] * q_col[h], axis=0, keepdims=True) for h in heads]
    for h in heads:
        snew_ref[0, h] = s[h]
        zh = z[:, h * GDN_DV:(h + 1) * GDN_DV]
        ya_ref[0, :, h * GDN_DV:(h + 1) * GDN_DV] = _rms(o[h], gn) * _silu(zh)


def _gdn_step(qkv, z, ba, conv_state, state, conv_w, alog_row, dt_row, gdn_norm):
    b = qkv.shape[0]
    qkv3 = qkv.reshape(b, 1, QKV_WIDTH)
    z3 = z.reshape(b, 1, Z_WIDTH)
    ba3 = ba.reshape(b, 1, LANES)
    return pl.pallas_call(
        _gdn_step_kernel,
        grid=(b,),
        in_specs=[pl.BlockSpec((1, 1, QKV_WIDTH), lambda i: (i, 0, 0)),
                  pl.BlockSpec((1, 1, Z_WIDTH), lambda i: (i, 0, 0)),
                  pl.BlockSpec((1, 1, LANES), lambda i: (i, 0, 0)),
                  pl.BlockSpec((1, CONV_W - 1, QKV_WIDTH), lambda i: (i, 0, 0)),
                  pl.BlockSpec((1, GDN_HEADS, GDN_DK, GDN_DV), lambda i: (i, 0, 0, 0)),
                  _full_spec(conv_w.shape), _full_spec(alog_row.shape),
                  _full_spec(dt_row.shape), _full_spec(gdn_norm.shape)],
        out_specs=[pl.BlockSpec((1, 1, Z_WIDTH), lambda i: (i, 0, 0)),
                   pl.BlockSpec((1, GDN_HEADS, GDN_DK, GDN_DV), lambda i: (i, 0, 0, 0)),
                   pl.BlockSpec((1, CONV_W - 1, QKV_WIDTH), lambda i: (i, 0, 0))],
        out_shape=[jax.ShapeDtypeStruct((b, 1, Z_WIDTH), F32),
                   jax.ShapeDtypeStruct((b, GDN_HEADS, GDN_DK, GDN_DV), F32),
                   jax.ShapeDtypeStruct((b, CONV_W - 1, QKV_WIDTH), F32)],
        compiler_params=_params("parallel"),
        name="gdn_step",
    )(qkv3, z3, ba3, conv_state, state, conv_w, alog_row, dt_row, gdn_norm)


def _mla_queries_latent(x, cos2, sin2, g_mix, w_qd, w_c, w_kr, g_qa, w_uq, g_kva,
                        g_qn, g_qr, g_kr):
    xn = _rms(x, g_mix).astype(BF16)
    qa = _rms(_dot(xn, w_qd), g_qa).astype(BF16)
    q = _dot(qa, w_uq)
    qn, qr = [], []
    for h in range(MLA_HEADS):
        qn.append(_rms(q[:, h * QK_PAD:h * QK_PAD + NOPE_DIM], g_qn))
        qr.append(_rope(_rms(q[:, h * QK_PAD + NOPE_DIM:(h + 1) * QK_PAD], g_qr, ROPE_DIM),
                        cos2, sin2))
    c = _rms(_dot(xn, w_c), g_kva)
    kr = _rope(_rms(_dot(xn, w_kr), g_kr, ROPE_DIM), cos2, sin2)
    return qn, qr, c, kr


def _mla_pre_kernel(x_ref, cos_ref, sin_ref, gmix_ref, wqd_ref, wc_ref, wkr_ref, gqa_ref,
                    wuq_ref, gkva_ref, gqn_ref, gqr_ref, gkr_ref, wuk_ref, wuv_ref, gkn_ref,
                    q_ref, k_ref, v_ref, lat_ref):
    qn, qr, c, kr = _mla_queries_latent(
        x_ref[0], cos_ref[...], sin_ref[...], gmix_ref[...], wqd_ref[...], wc_ref[...],
        wkr_ref[...], gqa_ref[...], wuq_ref[...], gkva_ref[...], gqn_ref[...] * ATTN_Q_SCALE,
        gqr_ref[...] * ATTN_Q_SCALE, gkr_ref[...])
    lat_ref[0, :, 0:KV_LORA] = c
    lat_ref[0, :, KV_LORA:LATENT_WIDTH] = kr[:, 0:ROPE_DIM]
    c_bf = c.astype(BF16)
    kr_bf = kr.astype(BF16)
    kfull = _dot(c_bf, wuk_ref[...])
    v_ref[0, 0] = _dot_nt(wuv_ref[...], c_bf).astype(BF16)
    gkn = gkn_ref[...]
    for h in range(MLA_HEADS):
        q_ref[0, :, h * QK_PAD:h * QK_PAD + NOPE_DIM] = qn[h].astype(BF16)
        q_ref[0, :, h * QK_PAD + NOPE_DIM:(h + 1) * QK_PAD] = qr[h].astype(BF16)
        kn = _rms(kfull[:, h * NOPE_DIM:(h + 1) * NOPE_DIM], gkn)
        k_ref[0, :, h * QK_PAD:h * QK_PAD + NOPE_DIM] = kn.astype(BF16)
        k_ref[0, :, h * QK_PAD + NOPE_DIM:(h + 1) * QK_PAD] = kr_bf


def _mla_pre(x, cos2, sin2, mw):
    b, t, d = x.shape
    tm = min(ATTN_TILE, t)
    consts = [mw["g_mix"], mw["w_qd"], mw["w_c"], mw["w_kr"], mw["g_qa"], mw["w_uq"],
              mw["g_kva"], mw["g_qn"], mw["g_qr"], mw["g_kr"], mw["w_uk"], mw["w_uv_t"],
              mw["g_kn"]]
    return pl.pallas_call(
        _mla_pre_kernel,
        grid=(b, t // tm),
        in_specs=[pl.BlockSpec((1, tm, d), lambda i, j: (i, j, 0)),
                  pl.BlockSpec((tm, LANES), lambda i, j: (j, 0)),
                  pl.BlockSpec((tm, LANES), lambda i, j: (j, 0))]
                 + [_full_spec(a.shape) for a in consts],
        out_specs=[pl.BlockSpec((1, tm, MLA_HEADS * QK_PAD), lambda i, j: (i, j, 0)),
                   pl.BlockSpec((1, tm, MLA_HEADS * QK_PAD), lambda i, j: (i, j, 0)),
                   pl.BlockSpec((1, 1, MLA_HEADS * V_DIM, tm), lambda i, j: (i, j, 0, 0)),
                   pl.BlockSpec((1, tm, LATENT_WIDTH), lambda i, j: (i, j, 0))],
        out_shape=[jax.ShapeDtypeStruct((b, t, MLA_HEADS * QK_PAD), BF16),
                   jax.ShapeDtypeStruct((b, t, MLA_HEADS * QK_PAD), BF16),
                   jax.ShapeDtypeStruct((b, t // tm, MLA_HEADS * V_DIM, tm), BF16),
                   jax.ShapeDtypeStruct((b, t, LATENT_WIDTH), F32)],
        compiler_params=_params("parallel", "parallel"),
        name="mla_pre",
    )(x, cos2, sin2, *consts)


def _mla_pre_sample_kernel(x_ref, cos_ref, sin_ref, gmix_ref, wqd_ref, wc_ref, wkr_ref, gqa_ref,
                           wuq_ref, gkva_ref, gqn_ref, gqr_ref, gkr_ref, wuk_ref, gkn_ref,
                           qa_ref, qr_ref, lat_ref):
    qn, qr, c, kr = _mla_queries_latent(
        x_ref[...], cos_ref[...], sin_ref[...], gmix_ref[...], wqd_ref[...], wc_ref[...],
        wkr_ref[...], gqa_ref[...], wuq_ref[...], gkva_ref[...], gqn_ref[...], gqr_ref[...],
        gkr_ref[...])
    lat_ref[:, 0:KV_LORA] = c
    lat_ref[:, KV_LORA:LATENT_WIDTH] = kr[:, 0:ROPE_DIM]
    gkn = gkn_ref[...]
    for h in range(MLA_HEADS):
        hi, lo = _split2(qn[h] * gkn)
        w_h = wuk_ref[:, h * NOPE_DIM:(h + 1) * NOPE_DIM]
        qa_ref[h] = _dot_nt(hi, w_h) + _dot_nt(lo, w_h)
        qr_ref[h] = qr[h]


def _mla_pre_sample(x2, cos2, sin2, mw):
    m = x2.shape[0]
    consts = [mw["g_mix"], mw["w_qd"], mw["w_c"], mw["w_kr"], mw["g_qa"], mw["w_uq"],
              mw["g_kva"], mw["g_qn"], mw["g_qr"], mw["g_kr"], mw["w_uk"], mw["g_kn"]]
    args = [x2, cos2, sin2] + consts
    return pl.pallas_call(
        _mla_pre_sample_kernel,
        grid=(1,),
        in_specs=[_full_spec(a.shape) for a in args],
        out_specs=[_full_spec((MLA_HEADS, m, KV_LORA)), _full_spec((MLA_HEADS, m, LANES)),
                   _full_spec((m, LATENT_WIDTH))],
        out_shape=[jax.ShapeDtypeStruct((MLA_HEADS, m, KV_LORA), F32),
                   jax.ShapeDtypeStruct((MLA_HEADS, m, LANES), F32),
                   jax.ShapeDtypeStruct((m, LATENT_WIDTH), F32)],
        compiler_params=_params("arbitrary"),
        name="mla_pre_sample",
    )(*args)


ATTN_HEADS_PER_STEP = 8


def _attn_kernel(q_ref, k_ref, vt_ref, o_ref, *, tq):
    qi = pl.program_id(2)
    hs = range(ATTN_HEADS_PER_STEP)
    q = [q_ref[0, :, g * QK_PAD:(g + 1) * QK_PAD] for g in hs]
    key = lax.broadcasted_iota(jnp.int32, (tq, tq), 0)
    qry = lax.broadcasted_iota(jnp.int32, (tq, tq), 1)

    def raw_scores(j):
        start = pl.multiple_of(j * tq, tq)
        return tuple(_dot_nt(k_ref[0, pl.ds(start, tq), g * QK_PAD:(g + 1) * QK_PAD], q[g])
                     for g in hs)

    def consume(j, s_raw, stats, diagonal):
        vt = [vt_ref[0, j, g * V_DIM:(g + 1) * V_DIM, :] for g in hs]
        s = list(s_raw)
        if diagonal:
            s = [jnp.where(key <= qry, s[g], -jnp.inf) for g in hs]
        m_new = [jnp.maximum(stats[g][0], jnp.max(s[g], axis=0, keepdims=True)) for g in hs]
        p = [jnp.exp2(s[g] - m_new[g]) for g in hs]
        pv = [_dot(vt[g], p[g].astype(BF16)) for g in hs]
        out = []
        for g in hs:
            m, l, acc = stats[g]
            alpha = jnp.exp2(m - m_new[g])
            out.append((m_new[g], alpha * l + jnp.sum(p[g], axis=0, keepdims=True),
                        alpha * acc + pv[g]))
        return tuple(out)

    init = tuple((jnp.full((1, tq), -jnp.inf, F32), jnp.zeros((1, tq), F32),
                  jnp.zeros((V_DIM, tq), F32)) for _ in hs)
    stats = lax.fori_loop(0, qi, lambda j, st: consume(j, raw_scores(j), st, False), init)
    stats = consume(qi, raw_scores(qi), stats, True)
    for g in hs:
        _, l, acc = stats[g]
        o_ref[0, :, g * V_DIM:(g + 1) * V_DIM] = (acc / l).T.astype(BF16)


def _mla_prompt_attention(q_cat, k_cat, v_t):
    b, t, _ = q_cat.shape
    tq = min(ATTN_TILE, t)
    hps = ATTN_HEADS_PER_STEP
    return pl.pallas_call(
        functools.partial(_attn_kernel, tq=tq),
        grid=(b, MLA_HEADS // hps, t // tq),
        in_specs=[pl.BlockSpec((1, tq, hps * QK_PAD), lambda i, h, j: (i, j, h)),
                  pl.BlockSpec((1, t, hps * QK_PAD), lambda i, h, j: (i, 0, h)),
                  pl.BlockSpec((1, t // tq, hps * V_DIM, tq), lambda i, h, j: (i, 0, h, 0))],
        out_specs=pl.BlockSpec((1, tq, hps * V_DIM), lambda i, h, j: (i, j, h)),
        out_shape=jax.ShapeDtypeStruct((b, t, MLA_HEADS * V_DIM), BF16),
        compiler_params=_params("parallel", "parallel", "arbitrary"),
        name="mla_prompt_attention",
    )(q_cat, k_cat, v_t)


def _dec_attn_kernel(pt_ref, *refs, pps):
    del pt_ref
    page_refs = refs[:pps]
    (wukt_ref, qa_ref, qr_ref, lnew_ref, ctx_ref,
     lhs_ref, qr16_ref, m_ref, l_ref, acc_ref) = refs[pps:]
    b = pl.program_id(0)
    j = pl.program_id(1)
    nj = pl.num_programs(1)
    n_k = MLA_HEADS * NOPE_DIM

    @pl.when((b == 0) & (j == 0))
    def _():
        lhs_ref[0:n_k, :] = wukt_ref[...]

    @pl.when(j == 0)
    def _():
        qa = jnp.concatenate([qa_ref[h, 0] for h in range(MLA_HEADS)]
                             + [jnp.zeros((8, KV_LORA), F32)], axis=0)
        lhs_ref[n_k:n_k + 16, :] = qa.astype(BF16)
        qr = jnp.concatenate([qr_ref[h, 0] for h in range(MLA_HEADS)]
                             + [jnp.zeros((8, LANES), F32)], axis=0)
        qr16_ref[...] = qr.astype(BF16)
        m_ref[...] = jnp.full(m_ref.shape, -jnp.inf, F32)
        l_ref[...] = jnp.zeros(l_ref.shape, F32)
        acc_ref[...] = jnp.zeros(acc_ref.shape, F32)

    def scores(res, sr):
        kt = res[0:n_k]
        ss = jnp.sum((kt * kt).reshape(MLA_HEADS, NOPE_DIM, res.shape[1]), axis=1)
        sn = res[n_k:n_k + MLA_HEADS]
        return (sn * lax.rsqrt(ss * (1.0 / NOPE_DIM) + EPS) + sr[0:MLA_HEADS]) * MLA_SCALE

    c_tok, s_parts = [], []
    for i in range(0, pps, 2):
        tile = jnp.concatenate([page_refs[i][...], page_refs[i + 1][...]], axis=1)
        c_bf = tile[0:KV_LORA].astype(BF16)
        kr_bf = tile[KV_LORA:LATENT_WIDTH].astype(BF16)
        s_parts.append(scores(_dot(lhs_ref[...], c_bf),
                              _dot(qr16_ref[:, 0:ROPE_DIM], kr_bf)))
        c_tok.append(tile[0:KV_LORA].T.astype(BF16))
    s = jnp.concatenate(s_parts, axis=1)
    m_old = m_ref[...]
    m_new = jnp.maximum(m_old, jnp.max(s, axis=-1, keepdims=True))
    alpha = jnp.exp(m_old - m_new)
    p = jnp.exp(s - m_new)
    l_ref[...] = alpha * l_ref[...] + jnp.sum(p, axis=-1, keepdims=True)
    p_bf = p.astype(BF16)
    n_tok = c_tok[0].shape[0]
    acc = alpha * acc_ref[...]
    for t, c_t in enumerate(c_tok):
        acc = acc + _dot(p_bf[:, t * n_tok:(t + 1) * n_tok], c_t)
    acc_ref[...] = acc
    m_ref[...] = m_new

    @pl.when(j == nj - 1)
    def _():
        ln = lnew_ref[0]
        c_new = ln[:, 0:KV_LORA]
        c8 = jnp.broadcast_to(c_new, (8, KV_LORA)).astype(BF16)
        kr8 = jnp.broadcast_to(ln[:, KV_LORA:LATENT_WIDTH], (8, ROPE_DIM)).astype(BF16)
        s_new = scores(_dot_nt(lhs_ref[...], c8),
                       _dot_nt(qr16_ref[:, 0:ROPE_DIM], kr8))[:, 0:1]
        m_old = m_ref[...]
        m_new = jnp.maximum(m_old, s_new)
        alpha = jnp.exp(m_old - m_new)
        p_new = jnp.exp(s_new - m_new)
        ctx = (alpha * acc_ref[...] + p_new * c_new) / (alpha * l_ref[...] + p_new)
        for h in range(MLA_HEADS):
            ctx_ref[h, 0] = ctx[h:h + 1]


def _dec_pages_per_step(n_pages):
    pps = DEC_PAGES_PER_STEP
    while n_pages % pps:
        pps //= 2
    assert pps >= 2
    return pps


def _dec_scratch():
    return [pltpu.VMEM((MLA_HEADS * NOPE_DIM + 16, KV_LORA), BF16),
            pltpu.VMEM((16, LANES), BF16),
            pltpu.VMEM((MLA_HEADS, 1), F32),
            pltpu.VMEM((MLA_HEADS, 1), F32),
            pltpu.VMEM((MLA_HEADS, KV_LORA), F32)]


def _mla_decode_attention(cache, layer, page_table, wuk_t, qa, qr, lat_new):
    bsz, n_pages = page_table.shape
    page = cache.shape[3]
    pps = _dec_pages_per_step(n_pages)
    qa4 = qa.reshape(MLA_HEADS, bsz, 1, KV_LORA)
    qr4 = qr.reshape(MLA_HEADS, bsz, 1, LANES)
    ln3 = lat_new.reshape(bsz, 1, LATENT_WIDTH)

    def page_spec(i):
        return pl.BlockSpec((pl.Squeezed(), pl.Squeezed(), LATENT_WIDTH, page),
                            lambda b, j, pt: (layer, pt[b, j * pps + i], 0, 0))

    grid_spec = pltpu.PrefetchScalarGridSpec(
        num_scalar_prefetch=1,
        grid=(bsz, n_pages // pps),
        in_specs=[page_spec(i) for i in range(pps)] + [
            pl.BlockSpec(wuk_t.shape, lambda b, j, pt: (0, 0)),
            pl.BlockSpec((MLA_HEADS, 1, 1, KV_LORA), lambda b, j, pt: (0, b, 0, 0)),
            pl.BlockSpec((MLA_HEADS, 1, 1, LANES), lambda b, j, pt: (0, b, 0, 0)),
            pl.BlockSpec((1, 1, LATENT_WIDTH), lambda b, j, pt: (b, 0, 0))],
        out_specs=pl.BlockSpec((MLA_HEADS, 1, 1, KV_LORA), lambda b, j, pt: (0, b, 0, 0)),
        scratch_shapes=_dec_scratch())
    ctx = pl.pallas_call(
        functools.partial(_dec_attn_kernel, pps=pps),
        grid_spec=grid_spec,
        out_shape=jax.ShapeDtypeStruct((MLA_HEADS, bsz, 1, KV_LORA), F32),
        compiler_params=_params("arbitrary", "arbitrary"),
        name="mla_decode_attention",
    )(page_table, *([cache] * pps), wuk_t, qa4, qr4, ln3)
    return ctx.reshape(MLA_HEADS, bsz, KV_LORA)


def _dec_post_kernel(ctx_ref, wuv_ref, y_ref):
    for h in range(MLA_HEADS):
        hi, lo = _split2(ctx_ref[h])
        w_h = wuv_ref[:, h * V_DIM:(h + 1) * V_DIM]
        y_ref[:, h * V_DIM:(h + 1) * V_DIM] = (_dot(hi, w_h) + _dot(lo, w_h)).astype(BF16)


def _dec_post(ctx, w_uv):
    bsz = ctx.shape[1]
    return pl.pallas_call(
        _dec_post_kernel,
        grid=(1,),
        in_specs=[_full_spec(ctx.shape), _full_spec(w_uv.shape)],
        out_specs=_full_spec((bsz, MLA_HEADS * V_DIM)),
        out_shape=jax.ShapeDtypeStruct((bsz, MLA_HEADS * V_DIM), BF16),
        compiler_params=_params("arbitrary"),
        name="mla_decode_values",
    )(ctx, w_uv)


def _merge_mlp_kernel(x_ref, ya_ref, yb_ref, gmix_ref, wg_ref, wya_ref, wyb_ref, wo_ref,
                      gmlp_ref, wup_ref, wdn_ref, y_ref):
    tm, d = x_ref.shape
    d_ff = wup_ref.shape[1]
    n_grp = max(1, tm // MERGE_ROWS_PER_GROUP)
    rows = tm // n_grp
    grp = range(n_grp)
    sl = [slice(r * rows, (r + 1) * rows) for r in grp]
    x = [x_ref[sl[r], :] for r in grp]
    xn = [_rms(x[r], gmix_ref[...]).astype(BF16) for r in grp]
    gates = [_sigmoid(_dot(xn[r], wg_ref[...])) for r in grp]
    pa = [_dot(ya_ref[sl[r], :].astype(BF16), wya_ref[...]) for r in grp]
    pb = [_dot(yb_ref[sl[r], :].astype(BF16), wyb_ref[...]) for r in grp]
    mix = [(gates[r][:, :d] * pa[r] + gates[r][:, d:] * pb[r]).astype(BF16) for r in grp]
    x1 = [x[r] + _dot(mix[r], wo_ref[...]) for r in grp]
    hin = [_rms(x1[r], gmlp_ref[...]).astype(BF16) for r in grp]
    y = x1
    for f in range(0, d_ff, MLP_FF_BLOCK):
        hmid = [jnp.maximum(_dot(hin[r], wup_ref[:, f:f + MLP_FF_BLOCK]), 0.0) for r in grp]
        y = [y[r] + _dot((hmid[r] * hmid[r]).astype(BF16), wdn_ref[f:f + MLP_FF_BLOCK, :])
             for r in grp]
    for r in grp:
        y_ref[sl[r], :] = y[r]


MERGE_ROWS_PER_GROUP = 256
MLP_FF_BLOCK = 1024


def _merge_mlp(x2, ya, yb, ew):
    m, d = x2.shape
    tm = min(512, m)
    consts = [ew["g_mix"], ew["w_gates"], ew["w_ya"], ew["w_yb"], ew["w_o"], ew["g_mlp"],
              ew["w_up"], ew["w_down"]]

    def const_spec(a):
        nd = a.ndim
        return pl.BlockSpec(a.shape, lambda i: (0,) * nd, pipeline_mode=pl.Buffered(1))

    return pl.pallas_call(
        _merge_mlp_kernel,
        grid=(m // tm,),
        in_specs=[pl.BlockSpec((tm, d), lambda i: (i, 0)),
                  pl.BlockSpec((tm, ya.shape[1]), lambda i: (i, 0)),
                  pl.BlockSpec((tm, yb.shape[1]), lambda i: (i, 0))]
                 + [const_spec(a) for a in consts],
        out_specs=pl.BlockSpec((tm, d), lambda i: (i, 0)),
        out_shape=jax.ShapeDtypeStruct((m, d), F32),
        compiler_params=_params("parallel"),
        name="merge_mlp",
    )(x2, ya, yb, *consts)


def _rope_tables(pos):
    inv = jnp.power(ROPE_THETA, -jnp.arange(0, ROPE_DIM, 2, dtype=F32) / ROPE_DIM)
    ang = pos.astype(F32)[:, None] * inv[None, :]
    cos, sin = jnp.cos(ang), jnp.sin(ang)
    zeros = jnp.zeros((pos.shape[0], LANES - ROPE_DIM), F32)
    return (jnp.concatenate([cos, cos, zeros], axis=1),
            jnp.concatenate([-sin, sin, zeros], axis=1))


def _pad_lanes(a, width=LANES, offset=0):
    return jnp.zeros((1, width), F32).at[0, offset:offset + a.shape[0]].set(a.astype(F32))


def _layer_weights(l, norm_mix, w_in, conv_w, a_log, dt_bias, gdn_norm, q_a_norm, w_uq,
                   kv_a_norm, w_uk, w_uv, q_norm_nope, q_norm_rope, k_norm_nope, k_norm_rope,
                   w_ya, w_yb, w_o, norm_mlp, w_up, w_down):
    d = w_in.shape[1]
    off_z = QKV_WIDTH
    off_b = off_z + Z_WIDTH
    off_qd = off_b + 2 * GDN_HEADS
    off_kv = off_qd + Q_LORA
    off_g = off_kv + LATENT_WIDTH
    wi = w_in[l]
    g_mix = norm_mix[l].reshape(1, d)
    gw = {
        "g_mix": g_mix,
        "w_qkv": wi[:, :off_z].astype(BF16),
        "w_z": wi[:, off_z:off_b].astype(BF16),
        "w_ba": jnp.pad(wi[:, off_b:off_qd], ((0, 0), (0, LANES - 2 * GDN_HEADS))).astype(BF16),
        "conv_w": conv_w[l],
        "alog_row": _pad_lanes(a_log[l], offset=GDN_HEADS),
        "dt_row": _pad_lanes(dt_bias[l], offset=GDN_HEADS),
        "gdn_norm": gdn_norm[l].reshape(1, GDN_DV),
    }
    wq = w_uq[l].reshape(Q_LORA, MLA_HEADS, NOPE_DIM + ROPE_DIM)
    wq = jnp.pad(wq, ((0, 0), (0, 0), (0, QK_PAD - NOPE_DIM - ROPE_DIM)))
    mw = {
        "g_mix": g_mix,
        "w_qd": wi[:, off_qd:off_kv].astype(BF16),
        "w_c": wi[:, off_kv:off_kv + KV_LORA].astype(BF16),
        "w_kr": jnp.pad(wi[:, off_kv + KV_LORA:off_g], ((0, 0), (0, LANES - ROPE_DIM))).astype(BF16),
        "g_qa": q_a_norm[l].reshape(1, Q_LORA),
        "w_uq": wq.reshape(Q_LORA, MLA_HEADS * QK_PAD).astype(BF16),
        "g_kva": kv_a_norm[l].reshape(1, KV_LORA),
        "g_qn": q_norm_nope[l].reshape(1, NOPE_DIM),
        "g_qr": _pad_lanes(q_norm_rope[l]),
        "g_kr": _pad_lanes(k_norm_rope[l]),
        "g_kn": k_norm_nope[l].reshape(1, NOPE_DIM),
        "w_uk": w_uk[l].astype(BF16),
        "w_uk_t": w_uk[l].T.astype(BF16),
        "w_uv": w_uv[l].astype(BF16),
        "w_uv_t": w_uv[l].T.astype(BF16),
    }
    ew = {
        "g_mix": g_mix,
        "w_gates": wi[:, off_g:].astype(BF16),
        "w_ya": w_ya[l].astype(BF16),
        "w_yb": w_yb[l].astype(BF16),
        "w_o": w_o[l].astype(BF16),
        "g_mlp": norm_mlp[l].reshape(1, d),
        "w_up": w_up[l].astype(BF16),
        "w_down": w_down[l].astype(BF16),
    }
    return gw, mw, ew


def kernel(x_prompt, x_sample, cache_mla, state_gdn, state_conv, page_table, norm_mix, w_in,
           conv_w, a_log, dt_bias, gdn_norm, q_a_norm, w_uq, kv_a_norm, w_uk, w_uv,
           q_norm_nope, q_norm_rope, k_norm_nope, k_norm_rope, w_ya, w_yb, w_o, norm_mlp,
           w_up, w_down):
    depth = w_in.shape[0]
    bp, tp, d = x_prompt.shape
    bs, ts, _ = x_sample.shape
    assert ts == 1 and tp % GDN_CHUNK == 0
    past_len = page_table.shape[1] * cache_mla.shape[2]
    cache_t = jnp.swapaxes(cache_mla, 2, 3)
    cos_p, sin_p = _rope_tables(jnp.arange(tp))
    cos_s, sin_s = _rope_tables(past_len + jnp.zeros((bs,), jnp.int32))

    x_p, x_s = x_prompt, x_sample.reshape(bs, d)
    rows_p, gdn_p, conv_p, rows_s, gdn_s, conv_s = [], [], [], [], [], []
    for l in range(depth):
        gw, mw, ew = _layer_weights(
            l, norm_mix, w_in, conv_w, a_log, dt_bias, gdn_norm, q_a_norm, w_uq, kv_a_norm,
            w_uk, w_uv, q_norm_nope, q_norm_rope, k_norm_nope, k_norm_rope, w_ya, w_yb, w_o,
            norm_mlp, w_up, w_down)
        gdn_consts = (gw["conv_w"], gw["alog_row"], gw["dt_row"], gw["gdn_norm"])

        xp2 = x_p.reshape(bp * tp, d)
        yc, z, ba, cs_p = _gdn_proj_conv(x_p, gw["g_mix"], gw["w_qkv"], gw["w_z"], gw["w_ba"],
                                         gw["conv_w"])
        ya_p, s_p = _gdn_chunked(yc.reshape(bp, tp, QKV_WIDTH), z.reshape(bp, tp, Z_WIDTH),
                                 ba.reshape(bp, tp, LANES), *gdn_consts[1:])
        q_cat, k_cat, v_p, lat_p = _mla_pre(x_p, cos_p, sin_p, mw)
        yb_p = _mla_prompt_attention(q_cat, k_cat, v_p)
        y_p = _merge_mlp(xp2, ya_p.reshape(bp * tp, Z_WIDTH), yb_p.reshape(bp * tp, -1), ew)

        qkv, z, ba = _gdn_proj(x_s, gw["g_mix"], gw["w_qkv"], gw["w_z"], gw["w_ba"])
        ya_s, s_s, cs_s = _gdn_step(qkv, z, ba, state_conv[l], state_gdn[l], *gdn_consts)
        qa, qr, lat_s = _mla_pre_sample(x_s, cos_s, sin_s, mw)
        ctx = _mla_decode_attention(cache_t, l, page_table, mw["w_uk_t"], qa, qr, lat_s)
        yb_s = _dec_post(ctx, mw["w_uv"])
        y_s = _merge_mlp(x_s, ya_s.reshape(bs, Z_WIDTH), yb_s, ew)

        x_p, x_s = y_p.reshape(bp, tp, d), y_s
        rows_p.append(lat_p)
        gdn_p.append(s_p)
        conv_p.append(cs_p)
        rows_s.append(lat_s.reshape(bs, 1, LATENT_WIDTH))
        gdn_s.append(s_s)
        conv_s.append(cs_s)
    return (x_p, x_s.reshape(bs, 1, d), jnp.stack(rows_p), jnp.stack(gdn_p), jnp.stack(conv_p),
            jnp.stack(rows_s), jnp.stack(gdn_s), jnp.stack(conv_s))
```

```python
import functools

import jax
import jax.numpy as jnp
from jax import lax
from jax.experimental import pallas as pl
from jax.experimental.pallas import tpu as pltpu

F32 = jnp.float32
BF16 = jnp.bfloat16

EPS = 1e-6
GDN_HEADS = 8
GDN_DK = 128
GDN_DV = 128
CONV_W = 4
GDN_CHUNK = 64
MLA_HEADS = 8
Q_LORA = 512
KV_LORA = 512
NOPE_DIM = 128
ROPE_DIM = 64
V_DIM = 128
ROPE_THETA = 10000.0
MLA_SCALE = (NOPE_DIM + ROPE_DIM) ** -0.5
LOG2_E = 1.4426950408889634
ATTN_Q_SCALE = MLA_SCALE * LOG2_E
LATENT_WIDTH = KV_LORA + ROPE_DIM
QKV_WIDTH = GDN_HEADS * (2 * GDN_DK + GDN_DV)
Z_WIDTH = GDN_HEADS * GDN_DV

LANES = 128
QK_PAD = 256
VMEM_LIMIT = 56 * 1024 * 1024
DEC_PAGES_PER_STEP = 64
ATTN_TILE = 256


def _dot(a, b):
    return jnp.dot(a, b, preferred_element_type=F32)


def _dot_nt(a, b):
    return lax.dot_general(a, b, (((1,), (1,)), ((), ())), preferred_element_type=F32)


def _dot_tn(a, b):
    return lax.dot_general(a, b, (((0,), (0,)), ((), ())), preferred_element_type=F32)


def _split2(x):
    hi = x.astype(BF16)
    lo = (x - hi.astype(F32)).astype(BF16)
    return hi, lo


def _split3(x):
    hi = x.astype(BF16)
    r = x - hi.astype(F32)
    mid = r.astype(BF16)
    lo = (r - mid.astype(F32)).astype(BF16)
    return hi, mid, lo


def _rms(x, g, n=None):
    n = x.shape[-1] if n is None else n
    ss = jnp.sum(x * x, axis=-1, keepdims=True) * (1.0 / n)
    return x * lax.rsqrt(ss + EPS) * g


def _sigmoid(x):
    return 1.0 / (1.0 + jnp.exp(-x))


def _silu(x):
    h = 0.5 * x
    return h + h * jnp.tanh(h)


def _softplus(x):
    return jnp.maximum(x, 0.0) + jnp.log1p(jnp.exp(-jnp.abs(x)))


def _rope(x, cos2, sin2):
    lane = lax.broadcasted_iota(jnp.int32, x.shape, 1)
    fwd = pltpu.roll(x, LANES - ROPE_DIM // 2, axis=1)
    bwd = pltpu.roll(x, ROPE_DIM // 2, axis=1)
    swapped = jnp.where(lane < ROPE_DIM // 2, fwd, bwd)
    return x * cos2 + swapped * sin2


def _full_spec(shape):
    nd = len(shape)
    return pl.BlockSpec(shape, lambda *_: (0,) * nd)


def _params(*sem):
    return pltpu.CompilerParams(dimension_semantics=sem, vmem_limit_bytes=VMEM_LIMIT)


def _gdn_proj_kernel(x_ref, g_ref, wqkv_ref, wz_ref, wba_ref, qkv_ref, z_ref, ba_ref):
    xn = _rms(x_ref[...], g_ref[...]).astype(BF16)
    qkv_ref[...] = _dot(xn, wqkv_ref[...])
    z_ref[...] = _dot(xn, wz_ref[...])
    ba_ref[...] = _dot(xn, wba_ref[...])


def _gdn_proj(x2, g, wqkv, wz, wba):
    m, d = x2.shape
    tm = min(256, m)
    return pl.pallas_call(
        _gdn_proj_kernel,
        grid=(m // tm,),
        in_specs=[pl.BlockSpec((tm, d), lambda i: (i, 0)),
                  _full_spec(g.shape), _full_spec(wqkv.shape), _full_spec(wz.shape),
                  _full_spec(wba.shape)],
        out_specs=[pl.BlockSpec((tm, QKV_WIDTH), lambda i: (i, 0)),
                   pl.BlockSpec((tm, Z_WIDTH), lambda i: (i, 0)),
                   pl.BlockSpec((tm, LANES), lambda i: (i, 0))],
        out_shape=[jax.ShapeDtypeStruct((m, QKV_WIDTH), F32),
                   jax.ShapeDtypeStruct((m, Z_WIDTH), F32),
                   jax.ShapeDtypeStruct((m, LANES), F32)],
        compiler_params=_params("parallel"),
        name="gdn_proj",
    )(x2, g, wqkv, wz, wba)


PROJ_CONV_COLS = 512
assert CONV_W == 4


def _gdn_proj_conv_kernel(x_ref, g_ref, wqkv_ref, wz_ref, wba_ref, cw_ref,
                          y_ref, z_ref, ba_ref, cs_ref, xp_ref, *, tiles_per_seq):
    i = pl.program_id(0)
    tm = x_ref.shape[0]

    @pl.when(i % tiles_per_seq == 0)
    def _():
        xp_ref[...] = jnp.zeros((8, QKV_WIDTH), F32)

    xn = _rms(x_ref[...], g_ref[...]).astype(BF16)
    row8 = lax.broadcasted_iota(jnp.int32, (8, PROJ_CONV_COLS), 0)
    for c0 in range(0, QKV_WIDTH, PROJ_CONV_COLS):
        cb = slice(c0, c0 + PROJ_CONV_COLS)
        raw = _dot(xn, wqkv_ref[:, cb])
        halo = xp_ref[:, cb]
        cw = cw_ref[:, cb]
        back2 = pltpu.roll(raw, 2, axis=0)
        x2 = jnp.concatenate([jnp.where(row8 < 2, pltpu.roll(halo, 2, axis=0), back2[0:8]),
                              back2[8:]], axis=0)
        u = raw * cw[2:3] + x2 * cw[0:1]
        u_prev = halo[7:8] * cw[2:3] + halo[5:6] * cw[0:1]
        back1 = pltpu.roll(u, 1, axis=0)
        su = jnp.concatenate([jnp.where(row8 < 1, u_prev, back1[0:8]), back1[8:]], axis=0)
        y_ref[:, cb] = _silu(raw * cw[3:4] + x2 * cw[1:2] + su)
        xp_ref[:, cb] = raw[tm - 8:tm]
    z_ref[...] = _dot(xn, wz_ref[...])
    ba_ref[...] = _dot(xn, wba_ref[...])

    @pl.when(i % tiles_per_seq == tiles_per_seq - 1)
    def _():
        cs_ref[0] = xp_ref[8 - (CONV_W - 1):8, :]


def _gdn_proj_conv(x, g, wqkv, wz, wba, conv_w):
    b, t, d = x.shape
    tm = min(256, t)
    assert t % tm == 0
    tiles_per_seq = t // tm
    m = b * t
    return pl.pallas_call(
        functools.partial(_gdn_proj_conv_kernel, tiles_per_seq=tiles_per_seq),
        grid=(m // tm,),
        in_specs=[pl.BlockSpec((tm, d), lambda i: (i, 0)),
                  _full_spec(g.shape), _full_spec(wqkv.shape), _full_spec(wz.shape),
                  _full_spec(wba.shape), _full_spec(conv_w.shape)],
        out_specs=[pl.BlockSpec((tm, QKV_WIDTH), lambda i: (i, 0)),
                   pl.BlockSpec((tm, Z_WIDTH), lambda i: (i, 0)),
                   pl.BlockSpec((tm, LANES), lambda i: (i, 0)),
                   pl.BlockSpec((1, CONV_W - 1, QKV_WIDTH), lambda i: (i // tiles_per_seq, 0, 0))],
        out_shape=[jax.ShapeDtypeStruct((m, QKV_WIDTH), F32),
                   jax.ShapeDtypeStruct((m, Z_WIDTH), F32),
                   jax.ShapeDtypeStruct((m, LANES), F32),
                   jax.ShapeDtypeStruct((b, CONV_W - 1, QKV_WIDTH), F32)],
        scratch_shapes=[pltpu.VMEM((8, QKV_WIDTH), F32)],
        compiler_params=_params("arbitrary"),
        name="gdn_proj_conv",
    )(x.reshape(m, d), g, wqkv, wz, wba, conv_w)


def _gate_rows(ba, alog_row, dt_row):
    lane = lax.broadcasted_iota(jnp.int32, ba.shape, 1)
    is_a = (lane >= GDN_HEADS) & (lane < 2 * GDN_HEADS)
    g = jnp.where(is_a, -jnp.exp(alog_row) * _softplus(ba + dt_row), 0.0)
    return _sigmoid(ba), g


def _gdn_chunk_kernel(y_ref, z_ref, ba_ref, alog_ref, dt_ref, gn_ref, ya_ref, s_ref):
    c = pl.program_id(1)
    C = GDN_CHUNK
    TB = y_ref.shape[1]
    n_sub = TB // C

    @pl.when(c == 0)
    def _():
        s_ref[...] = jnp.zeros(s_ref.shape, F32)

    y = y_ref[0]

    beta_full, g_full = _gate_rows(ba_ref[0], alog_ref[...], dt_ref[...])
    row = lax.broadcasted_iota(jnp.int32, (C, C), 0)
    col = lax.broadcasted_iota(jnp.int32, (C, C), 1)
    incl = row >= col
    strict = row > col
    rtb = lax.broadcasted_iota(jnp.int32, (TB, TB), 0)
    ctb = lax.broadcasted_iota(jnp.int32, (TB, TB), 1)
    tril = jnp.where((rtb >= ctb) & (rtb // C == ctb // C), 1.0, 0.0).astype(BF16)
    g_col = sum(_dot(tril, part) for part in _split3(g_full))
    r128 = lax.broadcasted_iota(jnp.int32, (LANES, LANES), 0)
    c128 = lax.broadcasted_iota(jnp.int32, (LANES, LANES), 1)
    eye = jnp.where(r128 == c128, 1.0, 0.0).astype(BF16)
    g_rows = sum(_dot_nt(eye, part) for part in _split3(g_col))

    z = z_ref[0]
    gn = gn_ref[...]
    heads = range(GDN_HEADS)
    units = [(n, h) for n in range(n_sub) for h in heads]
    q, k, gc, eg, g_last, kb, rhs_bf, rhs, decay, kq = ({} for _ in range(10))
    for n, h in units:
        r0 = n * C
        qh = y[r0:r0 + C, h * GDN_DK:(h + 1) * GDN_DK]
        kh = y[r0:r0 + C, GDN_HEADS * GDN_DK + h * GDN_DK:GDN_HEADS * GDN_DK + (h + 1) * GDN_DK]
        vh = y[r0:r0 + C,
               2 * GDN_HEADS * GDN_DK + h * GDN_DV:2 * GDN_HEADS * GDN_DK + (h + 1) * GDN_DV]
        u = (n, h)
        q[u] = qh * lax.rsqrt(jnp.sum(qh * qh, axis=-1, keepdims=True) + EPS) * (GDN_DK ** -0.5)
        k[u] = kh * lax.rsqrt(jnp.sum(kh * kh, axis=-1, keepdims=True) + EPS)
        beta = beta_full[r0:r0 + C, h:h + 1]
        gc[u] = g_col[r0:r0 + C, GDN_HEADS + h:GDN_HEADS + h + 1]
        gr = g_rows[GDN_HEADS + h:GDN_HEADS + h + 1, r0:r0 + C]
        decay[u] = jnp.where(incl, jnp.exp(jnp.where(incl, gc[u] - gr, 0.0)), 0.0)
        eg[u] = jnp.exp(gc[u])
        g_last[u] = gc[u][C - 1:C, :]
        kb[u] = k[u] * beta
        rhs[u] = jnp.concatenate([vh * beta, kb[u] * eg[u]], axis=1)
        rhs_bf[u] = rhs[u].astype(BF16)
    for u in units:
        kq[u] = _dot_nt(jnp.concatenate([kb[u], q[u]], axis=0).astype(BF16),
                        k[u].astype(BF16))
    pw = {u: jnp.where(strict, kq[u][:C] * decay[u], 0.0) for u in units}
    aqk = {u: (kq[u][C:] * decay[u]).astype(BF16) for u in units}
    n_acc = {u: -pw[u] for u in units}
    pw_bf = {u: pw[u].astype(BF16) for u in units}
    for _ in range(C.bit_length() - 2):
        pw = {u: _dot(pw_bf[u], pw_bf[u]) for u in units}
        pw_bf = {u: pw[u].astype(BF16) for u in units}
        n_acc = {u: n_acc[u] + pw[u] + _dot(n_acc[u].astype(BF16), pw_bf[u]) for u in units}
    sol = {}
    for u in units:
        sol[u] = rhs[u] + _dot(n_acc[u].astype(BF16), rhs_bf[u])
    wq = {u: jnp.concatenate([sol[u][:, GDN_DV:], q[u] * eg[u]], axis=0).astype(BF16)
          for u in units}
    kd = {u: (k[u] * jnp.exp(g_last[u] - gc[u])).astype(BF16) for u in units}
    s_cur = [s_ref[0, h] for h in heads]
    for n in range(n_sub):
        ws = [_dot(wq[(n, h)], s_cur[h].astype(BF16)) for h in heads]
        u_bf = [(sol[(n, h)][:, :GDN_DV] - ws[h][:C]).astype(BF16) for h in heads]
        o = [ws[h][C:] + _dot(aqk[(n, h)], u_bf[h]) for h in heads]
        s_cur = [jnp.exp(g_last[(n, h)]) * s_cur[h] + _dot_tn(kd[(n, h)], u_bf[h])
                 for h in heads]
        for h in heads:
            zh = z[n * C:(n + 1) * C, h * GDN_DV:(h + 1) * GDN_DV]
            ya_ref[0, n * C:(n + 1) * C, h * GDN_DV:(h + 1) * GDN_DV] = (
                _rms(o[h], gn) * _silu(zh)).astype(BF16)
    for h in heads:
        s_ref[0, h] = s_cur[h]


GDN_CHUNKS_PER_STEP = 2


def _gdn_chunked(y, z, ba, alog_row, dt_row, gdn_norm):
    b, t, _ = y.shape
    C = GDN_CHUNK * GDN_CHUNKS_PER_STEP
    assert t % C == 0
    nc = t // C
    return pl.pallas_call(
        _gdn_chunk_kernel,
        grid=(b, nc),
        in_specs=[pl.BlockSpec((1, C, QKV_WIDTH), lambda i, j: (i, j, 0)),
                  pl.BlockSpec((1, C, Z_WIDTH), lambda i, j: (i, j, 0)),
                  pl.BlockSpec((1, C, LANES), lambda i, j: (i, j, 0)),
                  _full_spec(alog_row.shape), _full_spec(dt_row.shape),
                  _full_spec(gdn_norm.shape)],
        out_specs=[pl.BlockSpec((1, C, Z_WIDTH), lambda i, j: (i, j, 0)),
                   pl.BlockSpec((1, GDN_HEADS, GDN_DK, GDN_DV), lambda i, j: (i, 0, 0, 0))],
        out_shape=[jax.ShapeDtypeStruct((b, t, Z_WIDTH), BF16),
                   jax.ShapeDtypeStruct((b, GDN_HEADS, GDN_DK, GDN_DV), F32)],
        compiler_params=_params("parallel", "arbitrary"),
        name="gdn_chunked",
    )(y, z, ba, alog_row, dt_row, gdn_norm)


def _gdn_step_kernel(qkv_ref, z_ref, ba_ref, cs_ref, s_ref, cw_ref, alog_ref, dt_ref, gn_ref,
                     ya_ref, snew_ref, csnew_ref):
    rowv = qkv_ref[0]
    cs = cs_ref[0]
    cw = cw_ref[...]
    y = rowv * cw[CONV_W - 1:CONV_W]
    for i in range(CONV_W - 1):
        y = y + cs[i:i + 1] * cw[i:i + 1]
    y = _silu(y)
    csnew_ref[0, 0:CONV_W - 2, :] = cs[1:CONV_W - 1]
    csnew_ref[0, CONV_W - 2:CONV_W - 1, :] = rowv

    beta_full, g_full = _gate_rows(ba_ref[0], alog_ref[...], dt_ref[...])
    r128 = lax.broadcasted_iota(jnp.int32, (GDN_DK, GDN_DK), 0)
    c128 = lax.broadcasted_iota(jnp.int32, (GDN_DK, GDN_DK), 1)
    eye = r128 == c128
    z = z_ref[0]
    gn = gn_ref[...]

    def to_col(r):
        return jnp.sum(jnp.where(eye, jnp.broadcast_to(r, (GDN_DK, GDN_DK)), 0.0),
                       axis=1, keepdims=True)

    heads = range(GDN_HEADS)
    q, k, v = [], [], []
    for h in heads:
        qh = y[:, h * GDN_DK:(h + 1) * GDN_DK]
        kh = y[:, GDN_HEADS * GDN_DK + h * GDN_DK:GDN_HEADS * GDN_DK + (h + 1) * GDN_DK]
        v.append(y[:, 2 * GDN_HEADS * GDN_DK + h * GDN_DV:2 * GDN_HEADS * GDN_DK + (h + 1) * GDN_DV])
        q.append(qh * lax.rsqrt(jnp.sum(qh * qh, axis=-1, keepdims=True) + EPS) * (GDN_DK ** -0.5))
        k.append(kh * lax.rsqrt(jnp.sum(kh * kh, axis=-1, keepdims=True) + EPS))
    k_col = [to_col(k[h]) for h in heads]
    q_col = [to_col(q[h]) for h in heads]
    s = [jnp.exp(g_full[:, GDN_HEADS + h:GDN_HEADS + h + 1]) * s_ref[0, h] for h in heads]
    u = [beta_full[:, h:h + 1] * (v[h] - jnp.sum(s[h] * k_col[h], axis=0, keepdims=True))
         for h in heads]
    s = [s[h] + k_col[h] * u[h] for h in heads]
    o = [jnp.sum(s[h] * q_col[h], axis=0, keepdims=True) for h in heads]
    for h in heads:
        snew_ref[0, h] = s[h]
        zh = z[:, h * GDN_DV:(h + 1) * GDN_DV]
        ya_ref[0, :, h * GDN_DV:(h + 1) * GDN_DV] = _rms(o[h], gn) * _silu(zh)


def _gdn_step(qkv, z, ba, conv_state, state, conv_w, alog_row, dt_row, gdn_norm):
    b = qkv.shape[0]
    qkv3 = qkv.reshape(b, 1, QKV_WIDTH)
    z3 = z.reshape(b, 1, Z_WIDTH)
    ba3 = ba.reshape(b, 1, LANES)
    return pl.pallas_call(
        _gdn_step_kernel,
        grid=(b,),
        in_specs=[pl.BlockSpec((1, 1, QKV_WIDTH), lambda i: (i, 0, 0)),
                  pl.BlockSpec((1, 1, Z_WIDTH), lambda i: (i, 0, 0)),
                  pl.BlockSpec((1, 1, LANES), lambda i: (i, 0, 0)),
                  pl.BlockSpec((1, CONV_W - 1, QKV_WIDTH), lambda i: (i, 0, 0)),
                  pl.BlockSpec((1, GDN_HEADS, GDN_DK, GDN_DV), lambda i: (i, 0, 0, 0)),
                  _full_spec(conv_w.shape), _full_spec(alog_row.shape),
                  _full_spec(dt_row.shape), _full_spec(gdn_norm.shape)],
        out_specs=[pl.BlockSpec((1, 1, Z_WIDTH), lambda i: (i, 0, 0)),
                   pl.BlockSpec((1, GDN_HEADS, GDN_DK, GDN_DV), lambda i: (i, 0, 0, 0)),
                   pl.BlockSpec((1, CONV_W - 1, QKV_WIDTH), lambda i: (i, 0, 0))],
        out_shape=[jax.ShapeDtypeStruct((b, 1, Z_WIDTH), F32),
                   jax.ShapeDtypeStruct((b, GDN_HEADS, GDN_DK, GDN_DV), F32),
                   jax.ShapeDtypeStruct((b, CONV_W - 1, QKV_WIDTH), F32)],
        compiler_params=_params("parallel"),
        name="gdn_step",
    )(qkv3, z3, ba3, conv_state, state, conv_w, alog_row, dt_row, gdn_norm)


def _mla_queries_latent(x, cos2, sin2, g_mix, w_qd, w_c, w_kr, g_qa, w_uq, g_kva,
                        g_qn, g_qr, g_kr):
    xn = _rms(x, g_mix).astype(BF16)
    qa = _rms(_dot(xn, w_qd), g_qa).astype(BF16)
    q = _dot(qa, w_uq)
    qn, qr = [], []
    for h in range(MLA_HEADS):
        qn.append(_rms(q[:, h * QK_PAD:h * QK_PAD + NOPE_DIM], g_qn))
        qr.append(_rope(_rms(q[:, h * QK_PAD + NOPE_DIM:(h + 1) * QK_PAD], g_qr, ROPE_DIM),
                        cos2, sin2))
    c = _rms(_dot(xn, w_c), g_kva)
    kr = _rope(_rms(_dot(xn, w_kr), g_kr, ROPE_DIM), cos2, sin2)
    return qn, qr, c, kr


def _mla_pre_kernel(x_ref, cos_ref, sin_ref, gmix_ref, wqd_ref, wc_ref, wkr_ref, gqa_ref,
                    wuq_ref, gkva_ref, gqn_ref, gqr_ref, gkr_ref, wuk_ref, wuv_ref, gkn_ref,
                    q_ref, k_ref, v_ref, lat_ref):
    qn, qr, c, kr = _mla_queries_latent(
        x_ref[0], cos_ref[...], sin_ref[...], gmix_ref[...], wqd_ref[...], wc_ref[...],
        wkr_ref[...], gqa_ref[...], wuq_ref[...], gkva_ref[...], gqn_ref[...] * ATTN_Q_SCALE,
        gqr_ref[...] * ATTN_Q_SCALE, gkr_ref[...])
    lat_ref[0, :, 0:KV_LORA] = c
    lat_ref[0, :, KV_LORA:LATENT_WIDTH] = kr[:, 0:ROPE_DIM]
    c_bf = c.astype(BF16)
    kr_bf = kr.astype(BF16)
    kfull = _dot(c_bf, wuk_ref[...])
    v_ref[0, 0] = _dot_nt(wuv_ref[...], c_bf).astype(BF16)
    gkn = gkn_ref[...]
    for h in range(MLA_HEADS):
        q_ref[0, :, h * QK_PAD:h * QK_PAD + NOPE_DIM] = qn[h].astype(BF16)
        q_ref[0, :, h * QK_PAD + NOPE_DIM:(h + 1) * QK_PAD] = qr[h].astype(BF16)
        kn = _rms(kfull[:, h * NOPE_DIM:(h + 1) * NOPE_DIM], gkn)
        k_ref[0, :, h * QK_PAD:h * QK_PAD + NOPE_DIM] = kn.astype(BF16)
        k_ref[0, :, h * QK_PAD + NOPE_DIM:(h + 1) * QK_PAD] = kr_bf


def _mla_pre(x, cos2, sin2, mw):
    b, t, d = x.shape
    tm = min(ATTN_TILE, t)
    consts = [mw["g_mix"], mw["w_qd"], mw["w_c"], mw["w_kr"], mw["g_qa"], mw["w_uq"],
              mw["g_kva"], mw["g_qn"], mw["g_qr"], mw["g_kr"], mw["w_uk"], mw["w_uv_t"],
              mw["g_kn"]]
    return pl.pallas_call(
        _mla_pre_kernel,
        grid=(b, t // tm),
        in_specs=[pl.BlockSpec((1, tm, d), lambda i, j: (i, j, 0)),
                  pl.BlockSpec((tm, LANES), lambda i, j: (j, 0)),
                  pl.BlockSpec((tm, LANES), lambda i, j: (j, 0))]
                 + [_full_spec(a.shape) for a in consts],
        out_specs=[pl.BlockSpec((1, tm, MLA_HEADS * QK_PAD), lambda i, j: (i, j, 0)),
                   pl.BlockSpec((1, tm, MLA_HEADS * QK_PAD), lambda i, j: (i, j, 0)),
                   pl.BlockSpec((1, 1, MLA_HEADS * V_DIM, tm), lambda i, j: (i, j, 0, 0)),
                   pl.BlockSpec((1, tm, LATENT_WIDTH), lambda i, j: (i, j, 0))],
        out_shape=[jax.ShapeDtypeStruct((b, t, MLA_HEADS * QK_PAD), BF16),
                   jax.ShapeDtypeStruct((b, t, MLA_HEADS * QK_PAD), BF16),
                   jax.ShapeDtypeStruct((b, t // tm, MLA_HEADS * V_DIM, tm), BF16),
                   jax.ShapeDtypeStruct((b, t, LATENT_WIDTH), F32)],
        compiler_params=_params("parallel", "parallel"),
        name="mla_pre",
    )(x, cos2, sin2, *consts)


def _mla_pre_sample_kernel(x_ref, cos_ref, sin_ref, gmix_ref, wqd_ref, wc_ref, wkr_ref, gqa_ref,
                           wuq_ref, gkva_ref, gqn_ref, gqr_ref, gkr_ref, wuk_ref, gkn_ref,
                           qa_ref, qr_ref, lat_ref):
    qn, qr, c, kr = _mla_queries_latent(
        x_ref[...], cos_ref[...], sin_ref[...], gmix_ref[...], wqd_ref[...], wc_ref[...],
        wkr_ref[...], gqa_ref[...], wuq_ref[...], gkva_ref[...], gqn_ref[...], gqr_ref[...],
        gkr_ref[...])
    lat_ref[:, 0:KV_LORA] = c
    lat_ref[:, KV_LORA:LATENT_WIDTH] = kr[:, 0:ROPE_DIM]
    gkn = gkn_ref[...]
    for h in range(MLA_HEADS):
        hi, lo = _split2(qn[h] * gkn)
        w_h = wuk_ref[:, h * NOPE_DIM:(h + 1) * NOPE_DIM]
        qa_ref[h] = _dot_nt(hi, w_h) + _dot_nt(lo, w_h)
        qr_ref[h] = qr[h]


def _mla_pre_sample(x2, cos2, sin2, mw):
    m = x2.shape[0]
    consts = [mw["g_mix"], mw["w_qd"], mw["w_c"], mw["w_kr"], mw["g_qa"], mw["w_uq"],
              mw["g_kva"], mw["g_qn"], mw["g_qr"], mw["g_kr"], mw["w_uk"], mw["g_kn"]]
    args = [x2, cos2, sin2] + consts
    return pl.pallas_call(
        _mla_pre_sample_kernel,
        grid=(1,),
        in_specs=[_full_spec(a.shape) for a in args],
        out_specs=[_full_spec((MLA_HEADS, m, KV_LORA)), _full_spec((MLA_HEADS, m, LANES)),
                   _full_spec((m, LATENT_WIDTH))],
        out_shape=[jax.ShapeDtypeStruct((MLA_HEADS, m, KV_LORA), F32),
                   jax.ShapeDtypeStruct((MLA_HEADS, m, LANES), F32),
                   jax.ShapeDtypeStruct((m, LATENT_WIDTH), F32)],
        compiler_params=_params("arbitrary"),
        name="mla_pre_sample",
    )(*args)


ATTN_HEADS_PER_STEP = 8


def _attn_kernel(q_ref, k_ref, vt_ref, o_ref, *, tq):
    qi = pl.program_id(2)
    hs = range(ATTN_HEADS_PER_STEP)
    q_t = [q_ref[0, :, g * QK_PAD:(g + 1) * QK_PAD].astype(F32).T.astype(BF16) for g in hs]
    key = lax.broadcasted_iota(jnp.int32, (tq, tq), 0)
    qry = lax.broadcasted_iota(jnp.int32, (tq, tq), 1)

    def raw_scores(j):
        start = pl.multiple_of(j * tq, tq)
        return tuple(_dot(k_ref[0, pl.ds(start, tq), g * QK_PAD:(g + 1) * QK_PAD], q_t[g])
                     for g in hs)

    def consume(j, s_raw, stats, diagonal):
        vt = [vt_ref[0, j, g * V_DIM:(g + 1) * V_DIM, :] for g in hs]
        s = list(s_raw)
        if diagonal:
            s = [jnp.where(key <= qry, s[g], -jnp.inf) for g in hs]
        m_new = [jnp.maximum(stats[g][0], jnp.max(s[g], axis=0, keepdims=True)) for g in hs]
        p = [jnp.exp2(s[g] - m_new[g]) for g in hs]
        pv = [_dot(vt[g], p[g].astype(BF16)) for g in hs]
        out = []
        for g in hs:
            m, l, acc = stats[g]
            alpha = jnp.exp2(m - m_new[g])
            out.append((m_new[g], alpha * l + jnp.sum(p[g], axis=0, keepdims=True),
                        alpha * acc + pv[g]))
        return tuple(out)

    init = tuple((jnp.full((1, tq), -jnp.inf, F32), jnp.zeros((1, tq), F32),
                  jnp.zeros((V_DIM, tq), F32)) for _ in hs)
    stats = lax.fori_loop(0, qi, lambda j, st: consume(j, raw_scores(j), st, False), init)
    stats = consume(qi, raw_scores(qi), stats, True)
    for g in hs:
        _, l, acc = stats[g]
        o_ref[0, :, g * V_DIM:(g + 1) * V_DIM] = (acc / l).T.astype(BF16)


def _mla_prompt_attention(q_cat, k_cat, v_t):
    b, t, _ = q_cat.shape
    tq = min(ATTN_TILE, t)
    hps = ATTN_HEADS_PER_STEP
    return pl.pallas_call(
        functools.partial(_attn_kernel, tq=tq),
        grid=(b, MLA_HEADS // hps, t // tq),
        in_specs=[pl.BlockSpec((1, tq, hps * QK_PAD), lambda i, h, j: (i, j, h)),
                  pl.BlockSpec((1, t, hps * QK_PAD), lambda i, h, j: (i, 0, h)),
                  pl.BlockSpec((1, t // tq, hps * V_DIM, tq), lambda i, h, j: (i, 0, h, 0))],
        out_specs=pl.BlockSpec((1, tq, hps * V_DIM), lambda i, h, j: (i, j, h)),
        out_shape=jax.ShapeDtypeStruct((b, t, MLA_HEADS * V_DIM), BF16),
        compiler_params=_params("parallel", "parallel", "arbitrary"),
        name="mla_prompt_attention",
    )(q_cat, k_cat, v_t)


def _dec_attn_kernel(pt_ref, *refs, pps):
    del pt_ref
    page_refs = refs[:pps]
    (wukt_ref, qa_ref, qr_ref, lnew_ref, ctx_ref,
     lhs_ref, qr16_ref, m_ref, l_ref, acc_ref) = refs[pps:]
    b = pl.program_id(0)
    j = pl.program_id(1)
    nj = pl.num_programs(1)
    n_k = MLA_HEADS * NOPE_DIM

    @pl.when((b == 0) & (j == 0))
    def _():
        lhs_ref[0:n_k, :] = wukt_ref[...]

    @pl.when(j == 0)
    def _():
        qa = jnp.concatenate([qa_ref[h, 0] for h in range(MLA_HEADS)]
                             + [jnp.zeros((8, KV_LORA), F32)], axis=0)
        lhs_ref[n_k:n_k + 16, :] = qa.astype(BF16)
        qr = jnp.concatenate([qr_ref[h, 0] for h in range(MLA_HEADS)]
                             + [jnp.zeros((8, LANES), F32)], axis=0)
        qr16_ref[...] = qr.astype(BF16)
        m_ref[...] = jnp.full(m_ref.shape, -jnp.inf, F32)
        l_ref[...] = jnp.zeros(l_ref.shape, F32)
        acc_ref[...] = jnp.zeros(acc_ref.shape, F32)

    def scores(res, sr):
        kt = res[0:n_k]
        ss = jnp.sum((kt * kt).reshape(MLA_HEADS, NOPE_DIM, res.shape[1]), axis=1)
        sn = res[n_k:n_k + MLA_HEADS]
        return (sn * lax.rsqrt(ss * (1.0 / NOPE_DIM) + EPS) + sr[0:MLA_HEADS]) * MLA_SCALE

    c_tok, s_parts = [], []
    for i in range(0, pps, 2):
        tile = jnp.concatenate([page_refs[i][...], page_refs[i + 1][...]], axis=1)
        c_bf = tile[0:KV_LORA].astype(BF16)
        kr_bf = tile[KV_LORA:LATENT_WIDTH].astype(BF16)
        s_parts.append(scores(_dot(lhs_ref[...], c_bf),
                              _dot(qr16_ref[:, 0:ROPE_DIM], kr_bf)))
        c_tok.append(tile[0:KV_LORA].T.astype(BF16))
    s = jnp.concatenate(s_parts, axis=1)
    m_old = m_ref[...]
    m_new = jnp.maximum(m_old, jnp.max(s, axis=-1, keepdims=True))
    alpha = jnp.exp(m_old - m_new)
    p = jnp.exp(s - m_new)
    l_ref[...] = alpha * l_ref[...] + jnp.sum(p, axis=-1, keepdims=True)
    p_bf = p.astype(BF16)
    n_tok = c_tok[0].shape[0]
    acc = alpha * acc_ref[...]
    for t, c_t in enumerate(c_tok):
        acc = acc + _dot(p_bf[:, t * n_tok:(t + 1) * n_tok], c_t)
    acc_ref[...] = acc
    m_ref[...] = m_new

    @pl.when(j == nj - 1)
    def _():
        ln = lnew_ref[0]
        c_new = ln[:, 0:KV_LORA]
        c8 = jnp.broadcast_to(c_new, (8, KV_LORA)).astype(BF16)
        kr8 = jnp.broadcast_to(ln[:, KV_LORA:LATENT_WIDTH], (8, ROPE_DIM)).astype(BF16)
        s_new = scores(_dot_nt(lhs_ref[...], c8),
                       _dot_nt(qr16_ref[:, 0:ROPE_DIM], kr8))[:, 0:1]
        m_old = m_ref[...]
        m_new = jnp.maximum(m_old, s_new)
        alpha = jnp.exp(m_old - m_new)
        p_new = jnp.exp(s_new - m_new)
        ctx = (alpha * acc_ref[...] + p_new * c_new) / (alpha * l_ref[...] + p_new)
        for h in range(MLA_HEADS):
            ctx_ref[h, 0] = ctx[h:h + 1]


def _dec_pages_per_step(n_pages):
    pps = DEC_PAGES_PER_STEP
    while n_pages % pps:
        pps //= 2
    assert pps >= 2
    return pps


def _dec_scratch():
    return [pltpu.VMEM((MLA_HEADS * NOPE_DIM + 16, KV_LORA), BF16),
            pltpu.VMEM((16, LANES), BF16),
            pltpu.VMEM((MLA_HEADS, 1), F32),
            pltpu.VMEM((MLA_HEADS, 1), F32),
            pltpu.VMEM((MLA_HEADS, KV_LORA), F32)]


def _mla_decode_attention(cache, layer, page_table, wuk_t, qa, qr, lat_new):
    bsz, n_pages = page_table.shape
    page = cache.shape[3]
    pps = _dec_pages_per_step(n_pages)
    qa4 = qa.reshape(MLA_HEADS, bsz, 1, KV_LORA)
    qr4 = qr.reshape(MLA_HEADS, bsz, 1, LANES)
    ln3 = lat_new.reshape(bsz, 1, LATENT_WIDTH)

    def page_spec(i):
        return pl.BlockSpec((pl.Squeezed(), pl.Squeezed(), LATENT_WIDTH, page),
                            lambda b, j, pt: (layer, pt[b, j * pps + i], 0, 0))

    grid_spec = pltpu.PrefetchScalarGridSpec(
        num_scalar_prefetch=1,
        grid=(bsz, n_pages // pps),
        in_specs=[page_spec(i) for i in range(pps)] + [
            pl.BlockSpec(wuk_t.shape, lambda b, j, pt: (0, 0)),
            pl.BlockSpec((MLA_HEADS, 1, 1, KV_LORA), lambda b, j, pt: (0, b, 0, 0)),
            pl.BlockSpec((MLA_HEADS, 1, 1, LANES), lambda b, j, pt: (0, b, 0, 0)),
            pl.BlockSpec((1, 1, LATENT_WIDTH), lambda b, j, pt: (b, 0, 0))],
        out_specs=pl.BlockSpec((MLA_HEADS, 1, 1, KV_LORA), lambda b, j, pt: (0, b, 0, 0)),
        scratch_shapes=_dec_scratch())
    ctx = pl.pallas_call(
        functools.partial(_dec_attn_kernel, pps=pps),
        grid_spec=grid_spec,
        out_shape=jax.ShapeDtypeStruct((MLA_HEADS, bsz, 1, KV_LORA), F32),
        compiler_params=_params("arbitrary", "arbitrary"),
        name="mla_decode_attention",
    )(page_table, *([cache] * pps), wuk_t, qa4, qr4, ln3)
    return ctx.reshape(MLA_HEADS, bsz, KV_LORA)


def _dec_post_kernel(ctx_ref, wuv_ref, y_ref):
    for h in range(MLA_HEADS):
        hi, lo = _split2(ctx_ref[h])
        w_h = wuv_ref[:, h * V_DIM:(h + 1) * V_DIM]
        y_ref[:, h * V_DIM:(h + 1) * V_DIM] = (_dot(hi, w_h) + _dot(lo, w_h)).astype(BF16)


def _dec_post(ctx, w_uv):
    bsz = ctx.shape[1]
    return pl.pallas_call(
        _dec_post_kernel,
        grid=(1,),
        in_specs=[_full_spec(ctx.shape), _full_spec(w_uv.shape)],
        out_specs=_full_spec((bsz, MLA_HEADS * V_DIM)),
        out_shape=jax.ShapeDtypeStruct((bsz, MLA_HEADS * V_DIM), BF16),
        compiler_params=_params("arbitrary"),
        name="mla_decode_values",
    )(ctx, w_uv)


def _merge_mlp_kernel(x_ref, ya_ref, yb_ref, gmix_ref, wg_ref, wya_ref, wyb_ref, wo_ref,
                      gmlp_ref, wup_ref, wdn_ref, y_ref):
    tm, d = x_ref.shape
    d_ff = wup_ref.shape[1]
    n_grp = max(1, tm // MERGE_ROWS_PER_GROUP)
    rows = tm // n_grp
    grp = range(n_grp)
    sl = [slice(r * rows, (r + 1) * rows) for r in grp]
    x = [x_ref[sl[r], :] for r in grp]
    xn = [_rms(x[r], gmix_ref[...]).astype(BF16) for r in grp]
    gates = [_sigmoid(_dot(xn[r], wg_ref[...])) for r in grp]
    pa = [_dot(ya_ref[sl[r], :].astype(BF16), wya_ref[...]) for r in grp]
    pb = [_dot(yb_ref[sl[r], :].astype(BF16), wyb_ref[...]) for r in grp]
    mix = [(gates[r][:, :d] * pa[r] + gates[r][:, d:] * pb[r]).astype(BF16) for r in grp]
    x1 = [x[r] + _dot(mix[r], wo_ref[...]) for r in grp]
    hin = [_rms(x1[r], gmlp_ref[...]).astype(BF16) for r in grp]
    y = x1
    for f in range(0, d_ff, MLP_FF_BLOCK):
        hmid = [jnp.maximum(_dot(hin[r], wup_ref[:, f:f + MLP_FF_BLOCK]), 0.0) for r in grp]
        y = [y[r] + _dot((hmid[r] * hmid[r]).astype(BF16), wdn_ref[f:f + MLP_FF_BLOCK, :])
             for r in grp]
    for r in grp:
        y_ref[sl[r], :] = y[r]


MERGE_ROWS_PER_GROUP = 256
MLP_FF_BLOCK = 1024


def _merge_mlp(x2, ya, yb, ew):
    m, d = x2.shape
    tm = min(512, m)
    consts = [ew["g_mix"], ew["w_gates"], ew["w_ya"], ew["w_yb"], ew["w_o"], ew["g_mlp"],
              ew["w_up"], ew["w_down"]]

    def const_spec(a):
        nd = a.ndim
        return pl.BlockSpec(a.shape, lambda i: (0,) * nd, pipeline_mode=pl.Buffered(1))

    return pl.pallas_call(
        _merge_mlp_kernel,
        grid=(m // tm,),
        in_specs=[pl.BlockSpec((tm, d), lambda i: (i, 0)),
                  pl.BlockSpec((tm, ya.shape[1]), lambda i: (i, 0)),
                  pl.BlockSpec((tm, yb.shape[1]), lambda i: (i, 0))]
                 + [const_spec(a) for a in consts],
        out_specs=pl.BlockSpec((tm, d), lambda i: (i, 0)),
        out_shape=jax.ShapeDtypeStruct((m, d), F32),
        compiler_params=_params("parallel"),
        name="merge_mlp",
    )(x2, ya, yb, *consts)


def _rope_tables(pos):
    inv = jnp.power(ROPE_THETA, -jnp.arange(0, ROPE_DIM, 2, dtype=F32) / ROPE_DIM)
    ang = pos.astype(F32)[:, None] * inv[None, :]
    cos, sin = jnp.cos(ang), jnp.sin(ang)
    zeros = jnp.zeros((pos.shape[0], LANES - ROPE_DIM), F32)
    return (jnp.concatenate([cos, cos, zeros], axis=1),
            jnp.concatenate([-sin, sin, zeros], axis=1))


def _pad_lanes(a, width=LANES, offset=0):
    return jnp.zeros((1, width), F32).at[0, offset:offset + a.shape[0]].set(a.astype(F32))


def _layer_weights(l, norm_mix, w_in, conv_w, a_log, dt_bias, gdn_norm, q_a_norm, w_uq,
                   kv_a_norm, w_uk, w_uv, q_norm_nope, q_norm_rope, k_norm_nope, k_norm_rope,
                   w_ya, w_yb, w_o, norm_mlp, w_up, w_down):
    d = w_in.shape[1]
    off_z = QKV_WIDTH
    off_b = off_z + Z_WIDTH
    off_qd = off_b + 2 * GDN_HEADS
    off_kv = off_qd + Q_LORA
    off_g = off_kv + LATENT_WIDTH
    wi = w_in[l]
    g_mix = norm_mix[l].reshape(1, d)
    gw = {
        "g_mix": g_mix,
        "w_qkv": wi[:, :off_z].astype(BF16),
        "w_z": wi[:, off_z:off_b].astype(BF16),
        "w_ba": jnp.pad(wi[:, off_b:off_qd], ((0, 0), (0, LANES - 2 * GDN_HEADS))).astype(BF16),
        "conv_w": conv_w[l],
        "alog_row": _pad_lanes(a_log[l], offset=GDN_HEADS),
        "dt_row": _pad_lanes(dt_bias[l], offset=GDN_HEADS),
        "gdn_norm": gdn_norm[l].reshape(1, GDN_DV),
    }
    wq = w_uq[l].reshape(Q_LORA, MLA_HEADS, NOPE_DIM + ROPE_DIM)
    wq = jnp.pad(wq, ((0, 0), (0, 0), (0, QK_PAD - NOPE_DIM - ROPE_DIM)))
    mw = {
        "g_mix": g_mix,
        "w_qd": wi[:, off_qd:off_kv].astype(BF16),
        "w_c": wi[:, off_kv:off_kv + KV_LORA].astype(BF16),
        "w_kr": jnp.pad(wi[:, off_kv + KV_LORA:off_g], ((0, 0), (0, LANES - ROPE_DIM))).astype(BF16),
        "g_qa": q_a_norm[l].reshape(1, Q_LORA),
        "w_uq": wq.reshape(Q_LORA, MLA_HEADS * QK_PAD).astype(BF16),
        "g_kva": kv_a_norm[l].reshape(1, KV_LORA),
        "g_qn": q_norm_nope[l].reshape(1, NOPE_DIM),
        "g_qr": _pad_lanes(q_norm_rope[l]),
        "g_kr": _pad_lanes(k_norm_rope[l]),
        "g_kn": k_norm_nope[l].reshape(1, NOPE_DIM),
        "w_uk": w_uk[l].astype(BF16),
        "w_uk_t": w_uk[l].T.astype(BF16),
        "w_uv": w_uv[l].astype(BF16),
        "w_uv_t": w_uv[l].T.astype(BF16),
    }
    ew = {
        "g_mix": g_mix,
        "w_gates": wi[:, off_g:].astype(BF16),
        "w_ya": w_ya[l].astype(BF16),
        "w_yb": w_yb[l].astype(BF16),
        "w_o": w_o[l].astype(BF16),
        "g_mlp": norm_mlp[l].reshape(1, d),
        "w_up": w_up[l].astype(BF16),
        "w_down": w_down[l].astype(BF16),
    }
    return gw, mw, ew


def kernel(x_prompt, x_sample, cache_mla, state_gdn, state_conv, page_table, norm_mix, w_in,
           conv_w, a_log, dt_bias, gdn_norm, q_a_norm, w_uq, kv_a_norm, w_uk, w_uv,
           q_norm_nope, q_norm_rope, k_norm_nope, k_norm_rope, w_ya, w_yb, w_o, norm_mlp,
           w_up, w_down):
    depth = w_in.shape[0]
    bp, tp, d = x_prompt.shape
    bs, ts, _ = x_sample.shape
    assert ts == 1 and tp % GDN_CHUNK == 0
    past_len = page_table.shape[1] * cache_mla.shape[2]
    cache_t = jnp.swapaxes(cache_mla, 2, 3)
    cos_p, sin_p = _rope_tables(jnp.arange(tp))
    cos_s, sin_s = _rope_tables(past_len + jnp.zeros((bs,), jnp.int32))

    x_p, x_s = x_prompt, x_sample.reshape(bs, d)
    rows_p, gdn_p, conv_p, rows_s, gdn_s, conv_s = [], [], [], [], [], []
    for l in range(depth):
        gw, mw, ew = _layer_weights(
            l, norm_mix, w_in, conv_w, a_log, dt_bias, gdn_norm, q_a_norm, w_uq, kv_a_norm,
            w_uk, w_uv, q_norm_nope, q_norm_rope, k_norm_nope, k_norm_rope, w_ya, w_yb, w_o,
            norm_mlp, w_up, w_down)
        gdn_consts = (gw["conv_w"], gw["alog_row"], gw["dt_row"], gw["gdn_norm"])

        xp2 = x_p.reshape(bp * tp, d)
        yc, z, ba, cs_p = _gdn_proj_conv(x_p, gw["g_mix"], gw["w_qkv"], gw["w_z"], gw["w_ba"],
                                         gw["conv_w"])
        ya_p, s_p = _gdn_chunked(yc.reshape(bp, tp, QKV_WIDTH), z.reshape(bp, tp, Z_WIDTH),
                                 ba.reshape(bp, tp, LANES), *gdn_consts[1:])
        q_cat, k_cat, v_p, lat_p = _mla_pre(x_p, cos_p, sin_p, mw)
        yb_p = _mla_prompt_attention(q_cat, k_cat, v_p)
        y_p = _merge_mlp(xp2, ya_p.reshape(bp * tp, Z_WIDTH), yb_p.reshape(bp * tp, -1), ew)

        qkv, z, ba = _gdn_proj(x_s, gw["g_mix"], gw["w_qkv"], gw["w_z"], gw["w_ba"])
        ya_s, s_s, cs_s = _gdn_step(qkv, z, ba, state_conv[l], state_gdn[l], *gdn_consts)
        qa, qr, lat_s = _mla_pre_sample(x_s, cos_s, sin_s, mw)
        ctx = _mla_decode_attention(cache_t, l, page_table, mw["w_uk_t"], qa, qr, lat_s)
        yb_s = _dec_post(ctx, mw["w_uv"])
        y_s = _merge_mlp(x_s, ya_s.reshape(bs, Z_WIDTH), yb_s, ew)

        x_p, x_s = y_p.reshape(bp, tp, d), y_s
        rows_p.append(lat_p)
        gdn_p.append(s_p)
        conv_p.append(cs_p)
        rows_s.append(lat_s.reshape(bs, 1, LATENT_WIDTH))
        gdn_s.append(s_s)
        conv_s.append(cs_s)
    return (x_p, x_s.reshape(bs, 1, d), jnp.stack(rows_p), jnp.stack(gdn_p), jnp.stack(conv_p),
            jnp.stack(rows_s), jnp.stack(gdn_s), jnp.stack(conv_s))
```

```python
import functools

import jax
import jax.numpy as jnp
from jax import lax
from jax.experimental import pallas as pl
from jax.experimental.pallas import tpu as pltpu

F32 = jnp.float32
BF16 = jnp.bfloat16

EPS = 1e-6
GDN_HEADS = 8
GDN_DK = 128
GDN_DV = 128
CONV_W = 4
GDN_CHUNK = 64
MLA_HEADS = 8
Q_LORA = 512
KV_LORA = 512
NOPE_DIM = 128
ROPE_DIM = 64
V_DIM = 128
ROPE_THETA = 10000.0
MLA_SCALE = (NOPE_DIM + ROPE_DIM) ** -0.5
LOG2_E = 1.4426950408889634
ATTN_Q_SCALE = MLA_SCALE * LOG2_E
LATENT_WIDTH = KV_LORA + ROPE_DIM
QKV_WIDTH = GDN_HEADS * (2 * GDN_DK + GDN_DV)
Z_WIDTH = GDN_HEADS * GDN_DV

LANES = 128
QK_PAD = 256
VMEM_LIMIT = 56 * 1024 * 1024
DEC_PAGES_PER_STEP = 64
ATTN_TILE = 256


def _dot(a, b):
    return jnp.dot(a, b, preferred_element_type=F32)


def _dot_nt(a, b):
    return lax.dot_general(a, b, (((1,), (1,)), ((), ())), preferred_element_type=F32)


def _dot_tn(a, b):
    return lax.dot_general(a, b, (((0,), (0,)), ((), ())), preferred_element_type=F32)


def _split2(x):
    hi = x.astype(BF16)
    lo = (x - hi.astype(F32)).astype(BF16)
    return hi, lo


def _split3(x):
    hi = x.astype(BF16)
    r = x - hi.astype(F32)
    mid = r.astype(BF16)
    lo = (r - mid.astype(F32)).astype(BF16)
    return hi, mid, lo


def _rms(x, g, n=None):
    n = x.shape[-1] if n is None else n
    ss = jnp.sum(x * x, axis=-1, keepdims=True) * (1.0 / n)
    return x * lax.rsqrt(ss + EPS) * g


def _sigmoid(x):
    return 1.0 / (1.0 + jnp.exp(-x))


def _silu(x):
    h = 0.5 * x
    return h + h * jnp.tanh(h)


def _softplus(x):
    return jnp.maximum(x, 0.0) + jnp.log1p(jnp.exp(-jnp.abs(x)))


def _rope(x, cos2, sin2):
    lane = lax.broadcasted_iota(jnp.int32, x.shape, 1)
    fwd = pltpu.roll(x, LANES - ROPE_DIM // 2, axis=1)
    bwd = pltpu.roll(x, ROPE_DIM // 2, axis=1)
    swapped = jnp.where(lane < ROPE_DIM // 2, fwd, bwd)
    return x * cos2 + swapped * sin2


def _full_spec(shape):
    nd = len(shape)
    return pl.BlockSpec(shape, lambda *_: (0,) * nd)


def _params(*sem):
    return pltpu.CompilerParams(dimension_semantics=sem, vmem_limit_bytes=VMEM_LIMIT)


OFF_Z = QKV_WIDTH
OFF_BA = QKV_WIDTH + Z_WIDTH
GDN_IN_COLS = OFF_BA + LANES


def _gdn_w_spec(w_in_bf):
    return pl.BlockSpec((w_in_bf.shape[0], GDN_IN_COLS), lambda i: (0, 0))


def _gdn_proj_kernel(x_ref, g_ref, w_ref, qkv_ref, z_ref, ba_ref):
    xn = _rms(x_ref[...], g_ref[...]).astype(BF16)
    qkv_ref[...] = _dot(xn, w_ref[:, 0:OFF_Z])
    z_ref[...] = _dot(xn, w_ref[:, OFF_Z:OFF_BA])
    ba_ref[...] = _dot(xn, w_ref[:, OFF_BA:GDN_IN_COLS])


def _gdn_proj(x2, g, w_in_bf):
    m, d = x2.shape
    tm = min(256, m)
    return pl.pallas_call(
        _gdn_proj_kernel,
        grid=(m // tm,),
        in_specs=[pl.BlockSpec((tm, d), lambda i: (i, 0)),
                  _full_spec(g.shape), _gdn_w_spec(w_in_bf)],
        out_specs=[pl.BlockSpec((tm, QKV_WIDTH), lambda i: (i, 0)),
                   pl.BlockSpec((tm, Z_WIDTH), lambda i: (i, 0)),
                   pl.BlockSpec((tm, LANES), lambda i: (i, 0))],
        out_shape=[jax.ShapeDtypeStruct((m, QKV_WIDTH), F32),
                   jax.ShapeDtypeStruct((m, Z_WIDTH), F32),
                   jax.ShapeDtypeStruct((m, LANES), F32)],
        compiler_params=_params("parallel"),
        name="gdn_proj",
    )(x2, g, w_in_bf)


PROJ_CONV_COLS = 512
assert CONV_W == 4


def _gdn_proj_conv_kernel(x_ref, g_ref, w_ref, cw_ref,
                          y_ref, z_ref, ba_ref, cs_ref, xp_ref, *, tiles_per_seq):
    i = pl.program_id(0)
    tm = x_ref.shape[0]

    @pl.when(i % tiles_per_seq == 0)
    def _():
        xp_ref[...] = jnp.zeros((8, QKV_WIDTH), F32)

    xn = _rms(x_ref[...], g_ref[...]).astype(BF16)
    row8 = lax.broadcasted_iota(jnp.int32, (8, PROJ_CONV_COLS), 0)
    for c0 in range(0, QKV_WIDTH, PROJ_CONV_COLS):
        cb = slice(c0, c0 + PROJ_CONV_COLS)
        raw = _dot(xn, w_ref[:, cb])
        halo = xp_ref[:, cb]
        cw = cw_ref[:, cb]
        back2 = pltpu.roll(raw, 2, axis=0)
        x2 = jnp.concatenate([jnp.where(row8 < 2, pltpu.roll(halo, 2, axis=0), back2[0:8]),
                              back2[8:]], axis=0)
        u = raw * cw[2:3] + x2 * cw[0:1]
        u_prev = halo[7:8] * cw[2:3] + halo[5:6] * cw[0:1]
        back1 = pltpu.roll(u, 1, axis=0)
        su = jnp.concatenate([jnp.where(row8 < 1, u_prev, back1[0:8]), back1[8:]], axis=0)
        y_ref[:, cb] = _silu(raw * cw[3:4] + x2 * cw[1:2] + su)
        xp_ref[:, cb] = raw[tm - 8:tm]
    z_ref[...] = _dot(xn, w_ref[:, OFF_Z:OFF_BA])
    ba_ref[...] = _dot(xn, w_ref[:, OFF_BA:GDN_IN_COLS])

    @pl.when(i % tiles_per_seq == tiles_per_seq - 1)
    def _():
        cs_ref[0] = xp_ref[8 - (CONV_W - 1):8, :]


def _gdn_proj_conv(x, g, w_in_bf, conv_w):
    b, t, d = x.shape
    tm = min(256, t)
    assert t % tm == 0
    tiles_per_seq = t // tm
    m = b * t
    return pl.pallas_call(
        functools.partial(_gdn_proj_conv_kernel, tiles_per_seq=tiles_per_seq),
        grid=(m // tm,),
        in_specs=[pl.BlockSpec((tm, d), lambda i: (i, 0)),
                  _full_spec(g.shape), _gdn_w_spec(w_in_bf), _full_spec(conv_w.shape)],
        out_specs=[pl.BlockSpec((tm, QKV_WIDTH), lambda i: (i, 0)),
                   pl.BlockSpec((tm, Z_WIDTH), lambda i: (i, 0)),
                   pl.BlockSpec((tm, LANES), lambda i: (i, 0)),
                   pl.BlockSpec((1, CONV_W - 1, QKV_WIDTH), lambda i: (i // tiles_per_seq, 0, 0))],
        out_shape=[jax.ShapeDtypeStruct((m, QKV_WIDTH), F32),
                   jax.ShapeDtypeStruct((m, Z_WIDTH), F32),
                   jax.ShapeDtypeStruct((m, LANES), F32),
                   jax.ShapeDtypeStruct((b, CONV_W - 1, QKV_WIDTH), F32)],
        scratch_shapes=[pltpu.VMEM((8, QKV_WIDTH), F32)],
        compiler_params=_params("arbitrary"),
        name="gdn_proj_conv",
    )(x.reshape(m, d), g, w_in_bf, conv_w)


def _gate_rows(ba, alog_row, dt_row):
    lane = lax.broadcasted_iota(jnp.int32, ba.shape, 1)
    is_a = (lane >= GDN_HEADS) & (lane < 2 * GDN_HEADS)
    g = jnp.where(is_a, -jnp.exp(alog_row) * _softplus(ba + dt_row), 0.0)
    return _sigmoid(ba), g


def _gdn_chunk_kernel(y_ref, z_ref, ba_ref, alog_ref, dt_ref, gn_ref, ya_ref, s_ref):
    c = pl.program_id(1)
    C = GDN_CHUNK
    TB = y_ref.shape[1]
    n_sub = TB // C

    @pl.when(c == 0)
    def _():
        s_ref[...] = jnp.zeros(s_ref.shape, F32)

    y = y_ref[0]

    beta_full, g_full = _gate_rows(ba_ref[0], alog_ref[...], dt_ref[...])
    rtb =lax.broadcasted_iota(jnp.int32, (TB, TB), 0)
    ctb = lax.broadcasted_iota(jnp.int32, (TB, TB), 1)
    tril = jnp.where((rtb >= ctb) & (rtb // C == ctb // C), 1.0, 0.0).astype(BF16)
    g_col = sum(_dot(tril, part) for part in _split3(g_full))
    r128 = lax.broadcasted_iota(jnp.int32, (LANES, LANES), 0)
    c128 = lax.broadcasted_iota(jnp.int32, (LANES, LANES), 1)
    eye = jnp.where(r128 == c128, 1.0, 0.0).astype(BF16)
    g_rows = sum(_dot_nt(eye, part) for part in _split3(g_col))

    z = z_ref[0]
    gn = gn_ref[...]
    heads = range(GDN_HEADS)
    same_chunk = (rtb // C) == (ctb // C)
    incl = same_chunk & (rtb >= ctb)
    strict = same_chunk & (rtb > ctb)
    chunk_of_row = lax.broadcasted_iota(jnp.int32, (TB, 1), 0) // C
    q, k, gc, eg, g_last, rhs_bf, rhs, decay, kq, kd = ([None] * GDN_HEADS for _ in range(10))
    for h in heads:
        qh = y[:, h * GDN_DK:(h + 1) * GDN_DK]
        kh = y[:, GDN_HEADS * GDN_DK + h * GDN_DK:GDN_HEADS * GDN_DK + (h + 1) * GDN_DK]
        vh = y[:, 2 * GDN_HEADS * GDN_DK + h * GDN_DV:2 * GDN_HEADS * GDN_DK + (h + 1) * GDN_DV]
        q[h] = qh * lax.rsqrt(jnp.sum(qh * qh, axis=-1, keepdims=True) + EPS) * (GDN_DK ** -0.5)
        k[h] = kh * lax.rsqrt(jnp.sum(kh * kh, axis=-1, keepdims=True) + EPS)
        beta = beta_full[:, h:h + 1]
        gc[h] = g_col[:, GDN_HEADS + h:GDN_HEADS + h + 1]
        gr = g_rows[GDN_HEADS + h:GDN_HEADS + h + 1, :]
        decay[h] = jnp.where(incl, jnp.exp(jnp.where(incl, gc[h] - gr, 0.0)), 0.0)
        eg[h] = jnp.exp(gc[h])
        g_last[h] = [gc[h][(n + 1) * C - 1:(n + 1) * C, :] for n in range(n_sub)]
        g_end = g_last[h][0]
        for n in range(1, n_sub):
            g_end = jnp.where(chunk_of_row >= n, g_last[h][n], g_end)
        kd[h] = (k[h] * jnp.exp(g_end - gc[h])).astype(BF16)
        kb = k[h] * beta
        rhs[h] = jnp.concatenate([vh * beta, kb * eg[h]], axis=1)
        rhs_bf[h] = rhs[h].astype(BF16)
        kq[h] = _dot_nt(jnp.concatenate([kb, q[h]], axis=0).astype(BF16),
                        k[h].astype(BF16))
    pw = [jnp.where(strict, kq[h][:TB] * decay[h], 0.0) for h in heads]
    aqk = [(kq[h][TB:] * decay[h]).astype(BF16) for h in heads]
    n_acc = [-pw[h] for h in heads]
    pw_bf = [pw[h].astype(BF16) for h in heads]
    for _ in range(C.bit_length() - 2):
        pw = [_dot(pw_bf[h], pw_bf[h]) for h in heads]
        pw_bf = [pw[h].astype(BF16) for h in heads]
        n_acc = [n_acc[h] + pw[h] + _dot(n_acc[h].astype(BF16), pw_bf[h]) for h in heads]
    sol = [rhs[h] + _dot(n_acc[h].astype(BF16), rhs_bf[h]) for h in heads]
    qg = [q[h] * eg[h] for h in heads]
    s_cur = [s_ref[0, h] for h in heads]
    zero_rows = jnp.zeros((C, GDN_DV), BF16)
    for n in range(n_sub):
        rows = slice(n * C, (n + 1) * C)
        ws = [_dot(jnp.concatenate([sol[h][rows, GDN_DV:], qg[h][rows]], axis=0).astype(BF16),
                   s_cur[h].astype(BF16)) for h in heads]
        u_bf = [(sol[h][rows, :GDN_DV] - ws[h][:C]).astype(BF16) for h in heads]
        u_pad = [jnp.concatenate([zero_rows] * n + [u_bf[h]] + [zero_rows] * (n_sub - 1 - n), axis=0)
                 for h in heads]
        o = [ws[h][C:] + _dot(aqk[h][rows], u_pad[h]) for h in heads]
        s_cur = [jnp.exp(g_last[h][n]) * s_cur[h] + _dot_tn(kd[h][rows], u_bf[h]) for h in heads]
        for h in heads:
            zh = z[rows, h * GDN_DV:(h + 1) * GDN_DV]
            ya_ref[0, rows, h * GDN_DV:(h + 1) * GDN_DV] = (
                _rms(o[h], gn) * _silu(zh)).astype(BF16)
    for h in heads:
        s_ref[0, h] = s_cur[h]


GDN_CHUNKS_PER_STEP = 2


def _gdn_chunked(y, z, ba, alog_row, dt_row, gdn_norm):
    b, t, _ = y.shape
    C = GDN_CHUNK * GDN_CHUNKS_PER_STEP
    assert t % C == 0
    nc = t // C
    return pl.pallas_call(
        _gdn_chunk_kernel,
        grid=(b, nc),
        in_specs=[pl.BlockSpec((1, C, QKV_WIDTH), lambda i, j: (i, j, 0)),
                  pl.BlockSpec((1, C, Z_WIDTH), lambda i, j: (i, j, 0)),
                  pl.BlockSpec((1, C, LANES), lambda i, j: (i, j, 0)),
                  _full_spec(alog_row.shape), _full_spec(dt_row.shape),
                  _full_spec(gdn_norm.shape)],
        out_specs=[pl.BlockSpec((1, C, Z_WIDTH), lambda i, j: (i, j, 0)),
                   pl.BlockSpec((1, GDN_HEADS, GDN_DK, GDN_DV), lambda i, j: (i, 0, 0, 0))],
        out_shape=[jax.ShapeDtypeStruct((b, t, Z_WIDTH), BF16),
                   jax.ShapeDtypeStruct((b, GDN_HEADS, GDN_DK, GDN_DV), F32)],
        compiler_params=_params("parallel", "arbitrary"),
        name="gdn_chunked",
    )(y, z, ba, alog_row, dt_row, gdn_norm)


def _gdn_step_kernel(qkv_ref, z_ref, ba_ref, cs_ref, s_ref, cw_ref, alog_ref, dt_ref, gn_ref,
                     ya_ref, snew_ref, csnew_ref):
    rowv = qkv_ref[0]
    cs = cs_ref[0]
    cw = cw_ref[...]
    y = rowv * cw[CONV_W - 1:CONV_W]
    for i in range(CONV_W - 1):
        y = y + cs[i:i + 1] * cw[i:i + 1]
    y = _silu(y)
    csnew_ref[0, 0:CONV_W - 2, :] = cs[1:CONV_W - 1]
    csnew_ref[0, CONV_W - 2:CONV_W - 1, :] = rowv

    beta_full, g_full = _gate_rows(ba_ref[0], alog_ref[...], dt_ref[...])
    r128 = lax.broadcasted_iota(jnp.int32, (GDN_DK, GDN_DK), 0)
    c128 = lax.broadcasted_iota(jnp.int32, (GDN_DK, GDN_DK), 1)
    eye = r128 == c128
    z = z_ref[0]
    gn = gn_ref[...]

    def to_col(r):
        return jnp.sum(jnp.where(eye, jnp.broadcast_to(r, (GDN_DK, GDN_DK)), 0.0),
                       axis=1, keepdims=True)

    heads = range(GDN_HEADS)
    q, k, v = [], [], []
    for h in heads:
        qh = y[:, h * GDN_DK:(h + 1) * GDN_DK]
        kh = y[:, GDN_HEADS * GDN_DK + h * GDN_DK:GDN_HEADS * GDN_DK + (h + 1) * GDN_DK]
        v.append(y[:, 2 * GDN_HEADS * GDN_DK + h * GDN_DV:2 * GDN_HEADS * GDN_DK + (h + 1) * GDN_DV])
        q.append(qh * lax.rsqrt(jnp.sum(qh * qh, axis=-1, keepdims=True) + EPS) * (GDN_DK ** -0.5))
        k.append(kh * lax.rsqrt(jnp.sum(kh * kh, axis=-1, keepdims=True) + EPS))
    k_col = [to_col(k[h]) for h in heads]
    q_col = [to_col(q[h]) for h in heads]
    s = [jnp.exp(g_full[:, GDN_HEADS + h:GDN_HEADS + h + 1]) * s_ref[0, h] for h in heads]
    u = [beta_full[:, h:h + 1] * (v[h] - jnp.sum(s[h] * k_col[h], axis=0, keepdims=True))
         for h in heads]
    s = [s[h] + k_col[h] * u[h] for h in heads]
    o = [jnp.sum(s[h] * q_col[h], axis=0, keepdims=True) for h in heads]
    for h in heads:
        snew_ref[0, h] = s[h]
        zh = z[:, h * GDN_DV:(h + 1) * GDN_DV]
        ya_ref[0, :, h * GDN_DV:(h + 1) * GDN_DV] = _rms(o[h], gn) * _silu(zh)


def _gdn_step(qkv, z, ba, conv_state, state, conv_w, alog_row, dt_row, gdn_norm):
    b = qkv.shape[0]
    qkv3 = qkv.reshape(b, 1, QKV_WIDTH)
    z3 = z.reshape(b, 1, Z_WIDTH)
    ba3 = ba.reshape(b, 1, LANES)
    return pl.pallas_call(
        _gdn_step_kernel,
        grid=(b,),
        in_specs=[pl.BlockSpec((1, 1, QKV_WIDTH), lambda i: (i, 0, 0)),
                  pl.BlockSpec((1, 1, Z_WIDTH), lambda i: (i, 0, 0)),
                  pl.BlockSpec((1, 1, LANES), lambda i: (i, 0, 0)),
                  pl.BlockSpec((1, CONV_W - 1, QKV_WIDTH), lambda i: (i, 0, 0)),
                  pl.BlockSpec((1, GDN_HEADS, GDN_DK, GDN_DV), lambda i: (i, 0, 0, 0)),
                  _full_spec(conv_w.shape), _full_spec(alog_row.shape),
                  _full_spec(dt_row.shape), _full_spec(gdn_norm.shape)],
        out_specs=[pl.BlockSpec((1, 1, Z_WIDTH), lambda i: (i, 0, 0)),
                   pl.BlockSpec((1, GDN_HEADS, GDN_DK, GDN_DV), lambda i: (i, 0, 0, 0)),
                   pl.BlockSpec((1, CONV_W - 1, QKV_WIDTH), lambda i: (i, 0, 0))],
        out_shape=[jax.ShapeDtypeStruct((b, 1, Z_WIDTH), F32),
                   jax.ShapeDtypeStruct((b, GDN_HEADS, GDN_DK, GDN_DV), F32),
                   jax.ShapeDtypeStruct((b, CONV_W - 1, QKV_WIDTH), F32)],
        compiler_params=_params("parallel"),
        name="gdn_step",
    )(qkv3, z3, ba3, conv_state, state, conv_w, alog_row, dt_row, gdn_norm)


def _mla_queries_latent(x, cos2, sin2, g_mix, w_qd, w_c, w_kr, g_qa, w_uq, g_kva,
                        g_qn, g_qr, g_kr):
    xn = _rms(x, g_mix).astype(BF16)
    qa = _rms(_dot(xn, w_qd), g_qa).astype(BF16)
    q = _dot(qa, w_uq)
    qn, qr = [], []
    for h in range(MLA_HEADS):
        qn.append(_rms(q[:, h * QK_PAD:h * QK_PAD + NOPE_DIM], g_qn))
        qr.append(_rope(_rms(q[:, h * QK_PAD + NOPE_DIM:(h + 1) * QK_PAD], g_qr, ROPE_DIM),
                        cos2, sin2))
    c = _rms(_dot(xn, w_c), g_kva)
    kr = _rope(_rms(_dot(xn, w_kr), g_kr, ROPE_DIM), cos2, sin2)
    return qn, qr, c, kr


def _mla_pre_kernel(x_ref, cos_ref, sin_ref, gmix_ref, wqd_ref, wc_ref, wkr_ref, gqa_ref,
                    wuq_ref, gkva_ref, gqn_ref, gqr_ref, gkr_ref, wuk_ref, wuv_ref, gkn_ref,
                    q_ref, k_ref, v_ref, lat_ref):
    qn, qr, c, kr = _mla_queries_latent(
        x_ref[0], cos_ref[...], sin_ref[...], gmix_ref[...], wqd_ref[...], wc_ref[...],
        wkr_ref[...], gqa_ref[...], wuq_ref[...], gkva_ref[...], gqn_ref[...] * ATTN_Q_SCALE,
        gqr_ref[...] * ATTN_Q_SCALE, gkr_ref[...])
    lat_ref[0, :, 0:KV_LORA] = c
    lat_ref[0, :, KV_LORA:LATENT_WIDTH] = kr[:, 0:ROPE_DIM]
    c_bf = c.astype(BF16)
    kr_bf = kr.astype(BF16)
    kfull = _dot(c_bf, wuk_ref[...])
    v_ref[0, 0] = _dot_nt(wuv_ref[...], c_bf).astype(BF16)
    gkn = gkn_ref[...]
    for h in range(MLA_HEADS):
        q_ref[0, :, h * QK_PAD:h * QK_PAD + NOPE_DIM] = qn[h].astype(BF16)
        q_ref[0, :, h * QK_PAD + NOPE_DIM:(h + 1) * QK_PAD] = qr[h].astype(BF16)
        kn = _rms(kfull[:, h * NOPE_DIM:(h + 1) * NOPE_DIM], gkn)
        k_ref[0, :, h * QK_PAD:h * QK_PAD + NOPE_DIM] = kn.astype(BF16)
        k_ref[0, :, h * QK_PAD + NOPE_DIM:(h + 1) * QK_PAD] = kr_bf


def _mla_pre(x, cos2, sin2, mw):
    b, t, d = x.shape
    tm = min(ATTN_TILE, t)
    consts = [mw["g_mix"], mw["w_qd"], mw["w_c"], mw["w_kr"], mw["g_qa"], mw["w_uq"],
              mw["g_kva"], mw["g_qn"], mw["g_qr"], mw["g_kr"], mw["w_uk"], mw["w_uv_t"],
              mw["g_kn"]]
    return pl.pallas_call(
        _mla_pre_kernel,
        grid=(b, t // tm),
        in_specs=[pl.BlockSpec((1, tm, d), lambda i, j: (i, j, 0)),
                  pl.BlockSpec((tm, LANES), lambda i, j: (j, 0)),
                  pl.BlockSpec((tm, LANES), lambda i, j: (j, 0))]
                 + [_full_spec(a.shape) for a in consts],
        out_specs=[pl.BlockSpec((1, tm, MLA_HEADS * QK_PAD), lambda i, j: (i, j, 0)),
                   pl.BlockSpec((1, tm, MLA_HEADS * QK_PAD), lambda i, j: (i, j, 0)),
                   pl.BlockSpec((1, 1, MLA_HEADS * V_DIM, tm), lambda i, j: (i, j, 0, 0)),
                   pl.BlockSpec((1, tm, LATENT_WIDTH), lambda i, j: (i, j, 0))],
        out_shape=[jax.ShapeDtypeStruct((b, t, MLA_HEADS * QK_PAD), BF16),
                   jax.ShapeDtypeStruct((b, t, MLA_HEADS * QK_PAD), BF16),
                   jax.ShapeDtypeStruct((b, t // tm, MLA_HEADS * V_DIM, tm), BF16),
                   jax.ShapeDtypeStruct((b, t, LATENT_WIDTH), F32)],
        compiler_params=_params("parallel", "parallel"),
        name="mla_pre",
    )(x, cos2, sin2, *consts)


def _mla_pre_sample_kernel(x_ref, cos_ref, sin_ref, gmix_ref, wqd_ref, wc_ref, wkr_ref, gqa_ref,
                           wuq_ref, gkva_ref, gqn_ref, gqr_ref, gkr_ref, wuk_ref, gkn_ref,
                           qa_ref, qr_ref, lat_ref):
    qn, qr, c, kr = _mla_queries_latent(
        x_ref[...], cos_ref[...], sin_ref[...], gmix_ref[...], wqd_ref[...], wc_ref[...],
        wkr_ref[...], gqa_ref[...], wuq_ref[...], gkva_ref[...], gqn_ref[...], gqr_ref[...],
        gkr_ref[...])
    lat_ref[:, 0:KV_LORA] = c
    lat_ref[:, KV_LORA:LATENT_WIDTH] = kr[:, 0:ROPE_DIM]
    gkn = gkn_ref[...]
    for h in range(MLA_HEADS):
        hi, lo = _split2(qn[h] * gkn)
        w_h = wuk_ref[:, h * NOPE_DIM:(h + 1) * NOPE_DIM]
        qa_ref[h] = _dot_nt(hi, w_h) + _dot_nt(lo, w_h)
        qr_ref[h] = qr[h]


def _mla_pre_sample(x2, cos2, sin2, mw):
    m = x2.shape[0]
    consts = [mw["g_mix"], mw["w_qd"], mw["w_c"], mw["w_kr"], mw["g_qa"], mw["w_uq"],
              mw["g_kva"], mw["g_qn"], mw["g_qr"], mw["g_kr"], mw["w_uk"], mw["g_kn"]]
    args = [x2, cos2, sin2] + consts
    return pl.pallas_call(
        _mla_pre_sample_kernel,
        grid=(1,),
        in_specs=[_full_spec(a.shape) for a in args],
        out_specs=[_full_spec((MLA_HEADS, m, KV_LORA)), _full_spec((MLA_HEADS, m, LANES)),
                   _full_spec((m, LATENT_WIDTH))],
        out_shape=[jax.ShapeDtypeStruct((MLA_HEADS, m, KV_LORA), F32),
                   jax.ShapeDtypeStruct((MLA_HEADS, m, LANES), F32),
                   jax.ShapeDtypeStruct((m, LATENT_WIDTH), F32)],
        compiler_params=_params("arbitrary"),
        name="mla_pre_sample",
    )(*args)


ATTN_HEADS_PER_STEP = 8


def _attn_kernel(q_ref, k_ref, vt_ref, o_ref, *, tq):
    qi = pl.program_id(2)
    hs = range(ATTN_HEADS_PER_STEP)
    q_t = [q_ref[0, :, g * QK_PAD:(g + 1) * QK_PAD].astype(F32).T.astype(BF16) for g in hs]
    key = lax.broadcasted_iota(jnp.int32, (tq, tq), 0)
    qry = lax.broadcasted_iota(jnp.int32, (tq, tq), 1)

    def raw_scores(j):
        start = pl.multiple_of(j * tq, tq)
        return tuple(_dot(k_ref[0, pl.ds(start, tq), g * QK_PAD:(g + 1) * QK_PAD], q_t[g])
                     for g in hs)

    def consume(j, s_raw, stats, diagonal):
        vt = [vt_ref[0, j, g * V_DIM:(g + 1) * V_DIM, :] for g in hs]
        s = list(s_raw)
        if diagonal:
            s = [jnp.where(key <= qry, s[g], -jnp.inf) for g in hs]
        m_new = [jnp.maximum(stats[g][0], jnp.max(s[g], axis=0, keepdims=True)) for g in hs]
        p = [jnp.exp2(s[g] - m_new[g]) for g in hs]
        pv = [_dot(vt[g], p[g].astype(BF16)) for g in hs]
        out = []
        for g in hs:
            m, l, acc = stats[g]
            alpha = jnp.exp2(m - m_new[g])
            out.append((m_new[g], alpha * l + jnp.sum(p[g], axis=0, keepdims=True),
                        alpha * acc + pv[g]))
        return tuple(out)

    init = tuple((jnp.full((1, tq), -jnp.inf, F32), jnp.zeros((1, tq), F32),
                  jnp.zeros((V_DIM, tq), F32)) for _ in hs)
    stats = lax.fori_loop(0, qi, lambda j, st: consume(j, raw_scores(j), st, False), init)
    stats = consume(qi, raw_scores(qi), stats, True)
    for g in hs:
        _, l, acc = stats[g]
        o_ref[0, :, g * V_DIM:(g + 1) * V_DIM] = (acc / l).T.astype(BF16)


def _mla_prompt_attention(q_cat, k_cat, v_t):
    b, t, _ = q_cat.shape
    tq = min(ATTN_TILE, t)
    hps = ATTN_HEADS_PER_STEP
    return pl.pallas_call(
        functools.partial(_attn_kernel, tq=tq),
        grid=(b, MLA_HEADS // hps, t // tq),
        in_specs=[pl.BlockSpec((1, tq, hps * QK_PAD), lambda i, h, j: (i, j, h)),
                  pl.BlockSpec((1, t, hps * QK_PAD), lambda i, h, j: (i, 0, h)),
                  pl.BlockSpec((1, t // tq, hps * V_DIM, tq), lambda i, h, j: (i, 0, h, 0))],
        out_specs=pl.BlockSpec((1, tq, hps * V_DIM), lambda i, h, j: (i, j, h)),
        out_shape=jax.ShapeDtypeStruct((b, t, MLA_HEADS * V_DIM), BF16),
        compiler_params=_params("parallel", "parallel", "arbitrary"),
        name="mla_prompt_attention",
    )(q_cat, k_cat, v_t)


def _dec_attn_kernel(pt_ref, *refs, pps):
    del pt_ref
    page_refs = refs[:pps]
    (wukt_ref, qa_ref, qr_ref, lnew_ref, ctx_ref,
     lhs_ref, qr16_ref, m_ref, l_ref, acc_ref) = refs[pps:]
    b = pl.program_id(0)
    j = pl.program_id(1)
    nj = pl.num_programs(1)
    n_k = MLA_HEADS * NOPE_DIM

    @pl.when((b == 0) & (j == 0))
    def _():
        lhs_ref[0:n_k, :] = wukt_ref[...]

    @pl.when(j == 0)
    def _():
        qa = jnp.concatenate([qa_ref[h, 0] for h in range(MLA_HEADS)]
                             + [jnp.zeros((8, KV_LORA), F32)], axis=0)
        lhs_ref[n_k:n_k + 16, :] = qa.astype(BF16)
        qr = jnp.concatenate([qr_ref[h, 0] for h in range(MLA_HEADS)]
                             + [jnp.zeros((8, LANES), F32)], axis=0)
        qr16_ref[...] = qr.astype(BF16)
        m_ref[...] = jnp.full(m_ref.shape, -jnp.inf, F32)
        l_ref[...] = jnp.zeros(l_ref.shape, F32)
        acc_ref[...] = jnp.zeros(acc_ref.shape, F32)

    def scores(res, sr):
        kt = res[0:n_k]
        ss = jnp.sum((kt * kt).reshape(MLA_HEADS, NOPE_DIM, res.shape[1]), axis=1)
        sn = res[n_k:n_k + MLA_HEADS]
        return (sn * lax.rsqrt(ss * (1.0 / NOPE_DIM) + EPS) + sr[0:MLA_HEADS]) * MLA_SCALE

    c_tok, s_parts = [], []
    for i in range(0, pps, 2):
        tile = jnp.concatenate([page_refs[i][...], page_refs[i + 1][...]], axis=1)
        c_bf = tile[0:KV_LORA].astype(BF16)
        kr_bf = tile[KV_LORA:LATENT_WIDTH].astype(BF16)
        s_parts.append(scores(_dot(lhs_ref[...], c_bf),
                              _dot(qr16_ref[:, 0:ROPE_DIM], kr_bf)))
        c_tok.append(tile[0:KV_LORA].T.astype(BF16))
    s = jnp.concatenate(s_parts, axis=1)
    m_old = m_ref[...]
    m_new = jnp.maximum(m_old, jnp.max(s, axis=-1, keepdims=True))
    alpha = jnp.exp(m_old - m_new)
    p = jnp.exp(s - m_new)
    l_ref[...] = alpha * l_ref[...] + jnp.sum(p, axis=-1, keepdims=True)
    p_bf = p.astype(BF16)
    n_tok = c_tok[0].shape[0]
    acc = alpha * acc_ref[...]
    for t, c_t in enumerate(c_tok):
        acc = acc + _dot(p_bf[:, t * n_tok:(t + 1) * n_tok], c_t)
    acc_ref[...] = acc
    m_ref[...] = m_new

    @pl.when(j == nj - 1)
    def _():
        ln = lnew_ref[0]
        c_new = ln[:, 0:KV_LORA]
        c8 = jnp.broadcast_to(c_new, (8, KV_LORA)).astype(BF16)
        kr8 = jnp.broadcast_to(ln[:, KV_LORA:LATENT_WIDTH], (8, ROPE_DIM)).astype(BF16)
        s_new = scores(_dot_nt(lhs_ref[...], c8),
                       _dot_nt(qr16_ref[:, 0:ROPE_DIM], kr8))[:, 0:1]
        m_old = m_ref[...]
        m_new = jnp.maximum(m_old, s_new)
        alpha = jnp.exp(m_old - m_new)
        p_new = jnp.exp(s_new - m_new)
        ctx = (alpha * acc_ref[...] + p_new * c_new) / (alpha * l_ref[...] + p_new)
        for h in range(MLA_HEADS):
            ctx_ref[h, 0] = ctx[h:h + 1]


def _dec_pages_per_step(n_pages):
    pps = DEC_PAGES_PER_STEP
    while n_pages % pps:
        pps //= 2
    assert pps >= 2
    return pps


def _dec_scratch():
    return [pltpu.VMEM((MLA_HEADS * NOPE_DIM + 16, KV_LORA), BF16),
            pltpu.VMEM((16, LANES), BF16),
            pltpu.VMEM((MLA_HEADS, 1), F32),
            pltpu.VMEM((MLA_HEADS, 1), F32),
            pltpu.VMEM((MLA_HEADS, KV_LORA), F32)]


def _mla_decode_attention(cache, layer, page_table, wuk_t, qa, qr, lat_new):
    bsz, n_pages = page_table.shape
    page = cache.shape[3]
    pps = _dec_pages_per_step(n_pages)
    qa4 = qa.reshape(MLA_HEADS, bsz, 1, KV_LORA)
    qr4 = qr.reshape(MLA_HEADS, bsz, 1, LANES)
    ln3 = lat_new.reshape(bsz, 1, LATENT_WIDTH)

    def page_spec(i):
        return pl.BlockSpec((pl.Squeezed(), pl.Squeezed(), LATENT_WIDTH, page),
                            lambda b, j, pt: (layer, pt[b, j * pps + i], 0, 0))

    grid_spec = pltpu.PrefetchScalarGridSpec(
        num_scalar_prefetch=1,
        grid=(bsz, n_pages // pps),
        in_specs=[page_spec(i) for i in range(pps)] + [
            pl.BlockSpec(wuk_t.shape, lambda b, j, pt: (0, 0)),
            pl.BlockSpec((MLA_HEADS, 1, 1, KV_LORA), lambda b, j, pt: (0, b, 0, 0)),
            pl.BlockSpec((MLA_HEADS, 1, 1, LANES), lambda b, j, pt: (0, b, 0, 0)),
            pl.BlockSpec((1, 1, LATENT_WIDTH), lambda b, j, pt: (b, 0, 0))],
        out_specs=pl.BlockSpec((MLA_HEADS, 1, 1, KV_LORA), lambda b, j, pt: (0, b, 0, 0)),
        scratch_shapes=_dec_scratch())
    ctx = pl.pallas_call(
        functools.partial(_dec_attn_kernel, pps=pps),
        grid_spec=grid_spec,
        out_shape=jax.ShapeDtypeStruct((MLA_HEADS, bsz, 1, KV_LORA), F32),
        compiler_params=_params("arbitrary", "arbitrary"),
        name="mla_decode_attention",
    )(page_table, *([cache] * pps), wuk_t, qa4, qr4, ln3)
    return ctx.reshape(MLA_HEADS, bsz, KV_LORA)


def _dec_post_kernel(ctx_ref, wuv_ref, y_ref):
    for h in range(MLA_HEADS):
        hi, lo = _split2(ctx_ref[h])
        w_h = wuv_ref[:, h * V_DIM:(h + 1) * V_DIM]
        y_ref[:, h * V_DIM:(h + 1) * V_DIM] = (_dot(hi, w_h) + _dot(lo, w_h)).astype(BF16)


def _dec_post(ctx, w_uv):
    bsz = ctx.shape[1]
    return pl.pallas_call(
        _dec_post_kernel,
        grid=(1,),
        in_specs=[_full_spec(ctx.shape), _full_spec(w_uv.shape)],
        out_specs=_full_spec((bsz, MLA_HEADS * V_DIM)),
        out_shape=jax.ShapeDtypeStruct((bsz, MLA_HEADS * V_DIM), BF16),
        compiler_params=_params("arbitrary"),
        name="mla_decode_values",
    )(ctx, w_uv)


def _merge_mlp_kernel(x_ref, ya_ref, yb_ref, gmix_ref, wg_ref, wya_ref, wyb_ref, wo_ref,
                      gmlp_ref, wup_ref, wdn_ref, y_ref):
    tm, d = x_ref.shape
    d_ff = wup_ref.shape[1]
    n_grp = max(1, tm // MERGE_ROWS_PER_GROUP)
    rows = tm // n_grp
    grp = range(n_grp)
    sl = [slice(r * rows, (r + 1) * rows) for r in grp]
    x = [x_ref[sl[r], :] for r in grp]
    xn = [_rms(x[r], gmix_ref[...]).astype(BF16) for r in grp]
    gates = [_sigmoid(_dot(xn[r], wg_ref[...])) for r in grp]
    pa = [_dot(ya_ref[sl[r], :].astype(BF16), wya_ref[...]) for r in grp]
    pb = [_dot(yb_ref[sl[r], :].astype(BF16), wyb_ref[...]) for r in grp]
    mix = [(gates[r][:, :d] * pa[r] + gates[r][:, d:] * pb[r]).astype(BF16) for r in grp]
    x1 = [x[r] + _dot(mix[r], wo_ref[...]) for r in grp]
    hin = [_rms(x1[r], gmlp_ref[...]).astype(BF16) for r in grp]
    y = x1
    for f in range(0, d_ff, MLP_FF_BLOCK):
        hmid = [jnp.maximum(_dot(hin[r], wup_ref[:, f:f + MLP_FF_BLOCK]), 0.0) for r in grp]
        y = [y[r] + _dot((hmid[r] * hmid[r]).astype(BF16), wdn_ref[f:f + MLP_FF_BLOCK, :])
             for r in grp]
    for r in grp:
        y_ref[sl[r], :] = y[r]


MERGE_ROWS_PER_GROUP = 256
MLP_FF_BLOCK = 1024


def _merge_mlp(x2, ya, yb, ew):
    m, d = x2.shape
    tm = min(512, m)
    consts = [ew["g_mix"], ew["w_gates"], ew["w_ya"], ew["w_yb"], ew["w_o"], ew["g_mlp"],
              ew["w_up"], ew["w_down"]]

    def const_spec(a):
        nd = a.ndim
        return pl.BlockSpec(a.shape, lambda i: (0,) * nd, pipeline_mode=pl.Buffered(1))

    return pl.pallas_call(
        _merge_mlp_kernel,
        grid=(m // tm,),
        in_specs=[pl.BlockSpec((tm, d), lambda i: (i, 0)),
                  pl.BlockSpec((tm, ya.shape[1]), lambda i: (i, 0)),
                  pl.BlockSpec((tm, yb.shape[1]), lambda i: (i, 0))]
                 + [const_spec(a) for a in consts],
        out_specs=pl.BlockSpec((tm, d), lambda i: (i, 0)),
        out_shape=jax.ShapeDtypeStruct((m, d), F32),
        compiler_params=_params("parallel"),
        name="merge_mlp",
    )(x2, ya, yb, *consts)


def _rope_tables(pos):
    inv = jnp.power(ROPE_THETA, -jnp.arange(0, ROPE_DIM, 2, dtype=F32) / ROPE_DIM)
    ang = pos.astype(F32)[:, None] * inv[None, :]
    cos, sin = jnp.cos(ang), jnp.sin(ang)
    zeros = jnp.zeros((pos.shape[0], LANES - ROPE_DIM), F32)
    return (jnp.concatenate([cos, cos, zeros], axis=1),
            jnp.concatenate([-sin, sin, zeros], axis=1))


def _pad_lanes(a, width=LANES, offset=0):
    return jnp.zeros((1, width), F32).at[0, offset:offset + a.shape[0]].set(a.astype(F32))


def _layer_weights(l, norm_mix, w_in, conv_w, a_log, dt_bias, gdn_norm, q_a_norm, w_uq,
                   kv_a_norm, w_uk, w_uv, q_norm_nope, q_norm_rope, k_norm_nope, k_norm_rope,
                   w_ya, w_yb, w_o, norm_mlp, w_up, w_down):
    d = w_in.shape[1]
    off_z = QKV_WIDTH
    off_b = off_z + Z_WIDTH
    off_qd = off_b + 2 * GDN_HEADS
    off_kv = off_qd + Q_LORA
    off_g = off_kv + LATENT_WIDTH
    wi = w_in[l]
    g_mix = norm_mix[l].reshape(1, d)
    gw = {
        "g_mix": g_mix,
        "w_in": wi.astype(BF16),
        "conv_w": conv_w[l],
        "alog_row": _pad_lanes(a_log[l], offset=GDN_HEADS),
        "dt_row": _pad_lanes(dt_bias[l], offset=GDN_HEADS),
        "gdn_norm": gdn_norm[l].reshape(1, GDN_DV),
    }
    wq = w_uq[l].reshape(Q_LORA, MLA_HEADS, NOPE_DIM + ROPE_DIM)
    wq = jnp.pad(wq, ((0, 0), (0, 0), (0, QK_PAD - NOPE_DIM - ROPE_DIM)))
    mw = {
        "g_mix": g_mix,
        "w_qd": wi[:, off_qd:off_kv].astype(BF16),
        "w_c": wi[:, off_kv:off_kv + KV_LORA].astype(BF16),
        "w_kr": jnp.pad(wi[:, off_kv + KV_LORA:off_g], ((0, 0), (0, LANES - ROPE_DIM))).astype(BF16),
        "g_qa": q_a_norm[l].reshape(1, Q_LORA),
        "w_uq": wq.reshape(Q_LORA, MLA_HEADS * QK_PAD).astype(BF16),
        "g_kva": kv_a_norm[l].reshape(1, KV_LORA),
        "g_qn": q_norm_nope[l].reshape(1, NOPE_DIM),
        "g_qr": _pad_lanes(q_norm_rope[l]),
        "g_kr": _pad_lanes(k_norm_rope[l]),
        "g_kn": k_norm_nope[l].reshape(1, NOPE_DIM),
        "w_uk": w_uk[l].astype(BF16),
        "w_uk_t": w_uk[l].T.astype(BF16),
        "w_uv": w_uv[l].astype(BF16),
        "w_uv_t": w_uv[l].T.astype(BF16),
    }
    ew = {
        "g_mix": g_mix,
        "w_gates": wi[:, off_g:].astype(BF16),
        "w_ya": w_ya[l].astype(BF16),
        "w_yb": w_yb[l].astype(BF16),
        "w_o": w_o[l].astype(BF16),
        "g_mlp": norm_mlp[l].reshape(1, d),
        "w_up": w_up[l].astype(BF16),
        "w_down": w_down[l].astype(BF16),
    }
    return gw, mw, ew


def kernel(x_prompt, x_sample, cache_mla, state_gdn, state_conv, page_table, norm_mix, w_in,
           conv_w, a_log, dt_bias, gdn_norm, q_a_norm, w_uq, kv_a_norm, w_uk, w_uv,
           q_norm_nope, q_norm_rope, k_norm_nope, k_norm_rope, w_ya, w_yb, w_o, norm_mlp,
           w_up, w_down):
    depth = w_in.shape[0]
    bp, tp, d = x_prompt.shape
    bs, ts, _ = x_sample.shape
    assert ts == 1 and tp % GDN_CHUNK == 0
    past_len = page_table.shape[1] * cache_mla.shape[2]
    cache_t = jnp.swapaxes(cache_mla, 2, 3)
    cos_p, sin_p = _rope_tables(jnp.arange(tp))
    cos_s, sin_s = _rope_tables(past_len + jnp.zeros((bs,), jnp.int32))

    x_p, x_s = x_prompt, x_sample.reshape(bs, d)
    rows_p, gdn_p, conv_p, rows_s, gdn_s, conv_s = [], [], [], [], [], []
    for l in range(depth):
        gw, mw, ew = _layer_weights(
            l, norm_mix, w_in, conv_w, a_log, dt_bias, gdn_norm, q_a_norm, w_uq, kv_a_norm,
            w_uk, w_uv, q_norm_nope, q_norm_rope, k_norm_nope, k_norm_rope, w_ya, w_yb, w_o,
            norm_mlp, w_up, w_down)
        gdn_consts = (gw["conv_w"], gw["alog_row"], gw["dt_row"], gw["gdn_norm"])

        xp2 = x_p.reshape(bp * tp, d)
        yc, z, ba, cs_p = _gdn_proj_conv(x_p, gw["g_mix"], gw["w_in"], gw["conv_w"])
        ya_p, s_p = _gdn_chunked(yc.reshape(bp, tp, QKV_WIDTH), z.reshape(bp, tp, Z_WIDTH),
                                 ba.reshape(bp, tp, LANES), *gdn_consts[1:])
        q_cat, k_cat, v_p, lat_p = _mla_pre(x_p, cos_p, sin_p, mw)
        yb_p = _mla_prompt_attention(q_cat, k_cat, v_p)
        y_p = _merge_mlp(xp2, ya_p.reshape(bp * tp, Z_WIDTH), yb_p.reshape(bp * tp, -1), ew)

        qkv, z, ba = _gdn_proj(x_s, gw["g_mix"], gw["w_in"])
        ya_s, s_s, cs_s = _gdn_step(qkv, z, ba, state_conv[l], state_gdn[l], *gdn_consts)
        qa, qr, lat_s = _mla_pre_sample(x_s, cos_s, sin_s, mw)
        ctx = _mla_decode_attention(cache_t, l, page_table, mw["w_uk_t"], qa, qr, lat_s)
        yb_s = _dec_post(ctx, mw["w_uv"])
        y_s = _merge_mlp(x_s, ya_s.reshape(bs, Z_WIDTH), yb_s, ew)

        x_p, x_s = y_p.reshape(bp, tp, d), y_s
        rows_p.append(lat_p)
        gdn_p.append(s_p)
        conv_p.append(cs_p)
        rows_s.append(lat_s.reshape(bs, 1, LATENT_WIDTH))
        gdn_s.append(s_s)
        conv_s.append(cs_s)
    return (x_p, x_s.reshape(bs, 1, d), jnp.stack(rows_p), jnp.stack(gdn_p), jnp.stack(conv_p),
            jnp.stack(rows_s), jnp.stack(gdn_s), jnp.stack(conv_s))
```

```python
import functools

import jax
import jax.numpy as jnp
from jax import lax
from jax.experimental import pallas as pl
from jax.experimental.pallas import tpu as pltpu

F32 = jnp.float32
BF16 = jnp.bfloat16

EPS = 1e-6
GDN_HEADS = 8
GDN_DK = 128
GDN_DV = 128
CONV_W = 4
GDN_CHUNK = 64
MLA_HEADS = 8
Q_LORA = 512
KV_LORA = 512
NOPE_DIM = 128
ROPE_DIM = 64
V_DIM = 128
ROPE_THETA = 10000.0
MLA_SCALE = (NOPE_DIM + ROPE_DIM) ** -0.5
LOG2_E = 1.4426950408889634
ATTN_Q_SCALE = MLA_SCALE * LOG2_E
LATENT_WIDTH = KV_LORA + ROPE_DIM
QKV_WIDTH = GDN_HEADS * (2 * GDN_DK + GDN_DV)
Z_WIDTH = GDN_HEADS * GDN_DV

LANES = 128
QK_PAD = 256
VMEM_LIMIT = 56 * 1024 * 1024
DEC_PAGES_PER_STEP = 64
ATTN_TILE = 256


def _dot(a, b):
    return jnp.dot(a, b, preferred_element_type=F32)


def _dot_nt(a, b):
    return lax.dot_general(a, b, (((1,), (1,)), ((), ())), preferred_element_type=F32)


def _dot_tn(a, b):
    return lax.dot_general(a, b, (((0,), (0,)), ((), ())), preferred_element_type=F32)


def _split2(x):
    hi = x.astype(BF16)
    lo = (x - hi.astype(F32)).astype(BF16)
    return hi, lo


def _split3(x):
    hi = x.astype(BF16)
    r = x - hi.astype(F32)
    mid = r.astype(BF16)
    lo = (r - mid.astype(F32)).astype(BF16)
    return hi, mid, lo


def _rms(x, g, n=None):
    n = x.shape[-1] if n is None else n
    ss = jnp.sum(x * x, axis=-1, keepdims=True) * (1.0 / n)
    return x * lax.rsqrt(ss + EPS) * g


def _sigmoid(x):
    return 1.0 / (1.0 + jnp.exp(-x))


def _silu(x):
    h = 0.5 * x
    return h + h * jnp.tanh(h)


def _softplus(x):
    return jnp.maximum(x, 0.0) + jnp.log1p(jnp.exp(-jnp.abs(x)))


def _rope(x, cos2, sin2):
    lane = lax.broadcasted_iota(jnp.int32, x.shape, 1)
    fwd = pltpu.roll(x, LANES - ROPE_DIM // 2, axis=1)
    bwd = pltpu.roll(x, ROPE_DIM // 2, axis=1)
    swapped = jnp.where(lane < ROPE_DIM // 2, fwd, bwd)
    return x * cos2 + swapped * sin2


def _full_spec(shape):
    nd = len(shape)
    return pl.BlockSpec(shape, lambda *_: (0,) * nd)


def _params(*sem):
    return pltpu.CompilerParams(dimension_semantics=sem, vmem_limit_bytes=VMEM_LIMIT)


OFF_Z = QKV_WIDTH
OFF_BA = QKV_WIDTH + Z_WIDTH
GDN_IN_COLS = OFF_BA + LANES


def _gdn_w_spec(w_in_bf):
    return pl.BlockSpec((w_in_bf.shape[0], GDN_IN_COLS), lambda i: (0, 0))


def _gdn_proj_kernel(x_ref, g_ref, w_ref, qkv_ref, z_ref, ba_ref):
    xn = _rms(x_ref[...], g_ref[...]).astype(BF16)
    qkv_ref[...] = _dot(xn, w_ref[:, 0:OFF_Z])
    z_ref[...] = _dot(xn, w_ref[:, OFF_Z:OFF_BA])
    ba_ref[...] = _dot(xn, w_ref[:, OFF_BA:GDN_IN_COLS])


def _gdn_proj(x2, g, w_in_bf):
    m, d = x2.shape
    tm = min(256, m)
    return pl.pallas_call(
        _gdn_proj_kernel,
        grid=(m // tm,),
        in_specs=[pl.BlockSpec((tm, d), lambda i: (i, 0)),
                  _full_spec(g.shape), _gdn_w_spec(w_in_bf)],
        out_specs=[pl.BlockSpec((tm, QKV_WIDTH), lambda i: (i, 0)),
                   pl.BlockSpec((tm, Z_WIDTH), lambda i: (i, 0)),
                   pl.BlockSpec((tm, LANES), lambda i: (i, 0))],
        out_shape=[jax.ShapeDtypeStruct((m, QKV_WIDTH), F32),
                   jax.ShapeDtypeStruct((m, Z_WIDTH), F32),
                   jax.ShapeDtypeStruct((m, LANES), F32)],
        compiler_params=_params("parallel"),
        name="gdn_proj",
    )(x2, g, w_in_bf)


PROJ_CONV_COLS = 512
assert CONV_W == 4


def _gdn_proj_conv_kernel(x_ref, g_ref, w_ref, cw_ref,
                          y_ref, z_ref, ba_ref, cs_ref, xp_ref, *, tiles_per_seq):
    i = pl.program_id(0)
    tm = x_ref.shape[0]

    @pl.when(i % tiles_per_seq == 0)
    def _():
        xp_ref[...] = jnp.zeros((8, QKV_WIDTH), F32)

    xn = _rms(x_ref[...], g_ref[...]).astype(BF16)
    row8 = lax.broadcasted_iota(jnp.int32, (8, PROJ_CONV_COLS), 0)
    for c0 in range(0, QKV_WIDTH, PROJ_CONV_COLS):
        cb = slice(c0, c0 + PROJ_CONV_COLS)
        raw = _dot(xn, w_ref[:, cb])
        halo = xp_ref[:, cb]
        cw = cw_ref[:, cb]
        back2 = pltpu.roll(raw, 2, axis=0)
        x2 = jnp.concatenate([jnp.where(row8 < 2, pltpu.roll(halo, 2, axis=0), back2[0:8]),
                              back2[8:]], axis=0)
        u = raw * cw[2:3] + x2 * cw[0:1]
        u_prev = halo[7:8] * cw[2:3] + halo[5:6] * cw[0:1]
        back1 = pltpu.roll(u, 1, axis=0)
        su = jnp.concatenate([jnp.where(row8 < 1, u_prev, back1[0:8]), back1[8:]], axis=0)
        y_ref[:, cb] = _silu(raw * cw[3:4] + x2 * cw[1:2] + su)
        xp_ref[:, cb] = raw[tm - 8:tm]
    z_ref[...] = _dot(xn, w_ref[:, OFF_Z:OFF_BA])
    ba_ref[...] = _dot(xn, w_ref[:, OFF_BA:GDN_IN_COLS])

    @pl.when(i % tiles_per_seq == tiles_per_seq - 1)
    def _():
        cs_ref[0] = xp_ref[8 - (CONV_W - 1):8, :]


def _gdn_proj_conv(x, g, w_in_bf, conv_w):
    b, t, d = x.shape
    tm = min(256, t)
    assert t % tm == 0
    tiles_per_seq = t // tm
    m = b * t
    return pl.pallas_call(
        functools.partial(_gdn_proj_conv_kernel, tiles_per_seq=tiles_per_seq),
        grid=(m // tm,),
        in_specs=[pl.BlockSpec((tm, d), lambda i: (i, 0)),
                  _full_spec(g.shape), _gdn_w_spec(w_in_bf), _full_spec(conv_w.shape)],
        out_specs=[pl.BlockSpec((tm, QKV_WIDTH), lambda i: (i, 0)),
                   pl.BlockSpec((tm, Z_WIDTH), lambda i: (i, 0)),
                   pl.BlockSpec((tm, LANES), lambda i: (i, 0)),
                   pl.BlockSpec((1, CONV_W - 1, QKV_WIDTH), lambda i: (i // tiles_per_seq, 0, 0))],
        out_shape=[jax.ShapeDtypeStruct((m, QKV_WIDTH), F32),
                   jax.ShapeDtypeStruct((m, Z_WIDTH), F32),
                   jax.ShapeDtypeStruct((m, LANES), F32),
                   jax.ShapeDtypeStruct((b, CONV_W - 1, QKV_WIDTH), F32)],
        scratch_shapes=[pltpu.VMEM((8, QKV_WIDTH), F32)],
        compiler_params=_params("arbitrary"),
        name="gdn_proj_conv",
    )(x.reshape(m, d), g, w_in_bf, conv_w)


def _gate_rows(ba, alog_row, dt_row):
    lane = lax.broadcasted_iota(jnp.int32, ba.shape, 1)
    is_a = (lane >= GDN_HEADS) & (lane < 2 * GDN_HEADS)
    g = jnp.where(is_a, -jnp.exp(alog_row) * _softplus(ba + dt_row), 0.0)
    return _sigmoid(ba), g


def _gdn_chunk_kernel(y_ref, z_ref, ba_ref, alog_ref, dt_ref, gn_ref, ya_ref, s_ref):
    c = pl.program_id(1)
    C = GDN_CHUNK
    TB = y_ref.shape[1]
    n_sub = TB // C

    @pl.when(c == 0)
    def _():
        s_ref[...] = jnp.zeros(s_ref.shape, F32)

    y = y_ref[0]

    beta_full, g_full = _gate_rows(ba_ref[0], alog_ref[...], dt_ref[...])
    rtb =lax.broadcasted_iota(jnp.int32, (TB, TB), 0)
    ctb = lax.broadcasted_iota(jnp.int32, (TB, TB), 1)
    tril = jnp.where((rtb >= ctb) & (rtb // C == ctb // C), 1.0, 0.0).astype(BF16)
    g_col = sum(_dot(tril, part) for part in _split3(g_full))
    r128 = lax.broadcasted_iota(jnp.int32, (LANES, LANES), 0)
    c128 = lax.broadcasted_iota(jnp.int32, (LANES, LANES), 1)
    eye = jnp.where(r128 == c128, 1.0, 0.0).astype(BF16)
    g_rows = sum(_dot_nt(eye, part) for part in _split3(g_col))

    z = z_ref[0]
    gn = gn_ref[...]
    heads = range(GDN_HEADS)
    same_chunk = (rtb // C) == (ctb // C)
    incl = same_chunk & (rtb >= ctb)
    strict = same_chunk & (rtb > ctb)
    chunk_of_row = lax.broadcasted_iota(jnp.int32, (TB, 1), 0) // C
    q, k, gc, eg, g_last, rhs_bf, rhs, decay, kq, kd = ([None] * GDN_HEADS for _ in range(10))
    for h in heads:
        qh = y[:, h * GDN_DK:(h + 1) * GDN_DK]
        kh = y[:, GDN_HEADS * GDN_DK + h * GDN_DK:GDN_HEADS * GDN_DK + (h + 1) * GDN_DK]
        vh = y[:, 2 * GDN_HEADS * GDN_DK + h * GDN_DV:2 * GDN_HEADS * GDN_DK + (h + 1) * GDN_DV]
        q[h] = qh * lax.rsqrt(jnp.sum(qh * qh, axis=-1, keepdims=True) + EPS) * (GDN_DK ** -0.5)
        k[h] = kh * lax.rsqrt(jnp.sum(kh * kh, axis=-1, keepdims=True) + EPS)
        beta = beta_full[:, h:h + 1]
        gc[h] = g_col[:, GDN_HEADS + h:GDN_HEADS + h + 1]
        gr = g_rows[GDN_HEADS + h:GDN_HEADS + h + 1, :]
        decay[h] = jnp.where(incl, jnp.exp(jnp.where(incl, gc[h] - gr, 0.0)), 0.0)
        eg[h] = jnp.exp(gc[h])
        g_last[h] = [gc[h][(n + 1) * C - 1:(n + 1) * C, :] for n in range(n_sub)]
        g_end = g_last[h][0]
        for n in range(1, n_sub):
            g_end = jnp.where(chunk_of_row >= n, g_last[h][n], g_end)
        kd[h] = (k[h] * jnp.exp(g_end - gc[h])).astype(BF16)
        kb = k[h] * beta
        rhs[h] = jnp.concatenate([vh * beta, kb * eg[h]], axis=1)
        rhs_bf[h] = rhs[h].astype(BF16)
        kq[h] = _dot_nt(jnp.concatenate([kb, q[h]], axis=0).astype(BF16),
                        k[h].astype(BF16))
    pw = [jnp.where(strict, kq[h][:TB] * decay[h], 0.0) for h in heads]
    aqk = [(kq[h][TB:] * decay[h]).astype(BF16) for h in heads]
    n_acc = [-pw[h] for h in heads]
    pw_bf = [pw[h].astype(BF16) for h in heads]
    for _ in range(C.bit_length() - 2):
        pw = [_dot(pw_bf[h], pw_bf[h]) for h in heads]
        pw_bf = [pw[h].astype(BF16) for h in heads]
        n_acc = [n_acc[h] + pw[h] + _dot(n_acc[h].astype(BF16), pw_bf[h]) for h in heads]
    sol = [rhs[h] + _dot(n_acc[h].astype(BF16), rhs_bf[h]) for h in heads]
    qg = [q[h] * eg[h] for h in heads]
    s_cur = [s_ref[0, h] for h in heads]
    zero_rows = jnp.zeros((C, GDN_DV), BF16)
    for n in range(n_sub):
        rows = slice(n * C, (n + 1) * C)
        ws = [_dot(jnp.concatenate([sol[h][rows, GDN_DV:], qg[h][rows]], axis=0).astype(BF16),
                   s_cur[h].astype(BF16)) for h in heads]
        u_bf = [(sol[h][rows, :GDN_DV] - ws[h][:C]).astype(BF16) for h in heads]
        u_pad = [jnp.concatenate([zero_rows] * n + [u_bf[h]] + [zero_rows] * (n_sub - 1 - n), axis=0)
                 for h in heads]
        o = [ws[h][C:] + _dot(aqk[h][rows], u_pad[h]) for h in heads]
        s_cur = [jnp.exp(g_last[h][n]) * s_cur[h] + _dot_tn(kd[h][rows], u_bf[h]) for h in heads]
        for h in heads:
            zh = z[rows, h * GDN_DV:(h + 1) * GDN_DV]
            ya_ref[0, rows, h * GDN_DV:(h + 1) * GDN_DV] = (
                _rms(o[h], gn) * _silu(zh)).astype(BF16)
    for h in heads:
        s_ref[0, h] = s_cur[h]


GDN_CHUNKS_PER_STEP = 2


def _gdn_chunked(y, z, ba, alog_row, dt_row, gdn_norm):
    b, t, _ = y.shape
    C = GDN_CHUNK * GDN_CHUNKS_PER_STEP
    assert t % C == 0
    nc = t // C
    return pl.pallas_call(
        _gdn_chunk_kernel,
        grid=(b, nc),
        in_specs=[pl.BlockSpec((1, C, QKV_WIDTH), lambda i, j: (i, j, 0)),
                  pl.BlockSpec((1, C, Z_WIDTH), lambda i, j: (i, j, 0)),
                  pl.BlockSpec((1, C, LANES), lambda i, j: (i, j, 0)),
                  _full_spec(alog_row.shape), _full_spec(dt_row.shape),
                  _full_spec(gdn_norm.shape)],
        out_specs=[pl.BlockSpec((1, C, Z_WIDTH), lambda i, j: (i, j, 0)),
                   pl.BlockSpec((1, GDN_HEADS, GDN_DK, GDN_DV), lambda i, j: (i, 0, 0, 0))],
        out_shape=[jax.ShapeDtypeStruct((b, t, Z_WIDTH), BF16),
                   jax.ShapeDtypeStruct((b, GDN_HEADS, GDN_DK, GDN_DV), F32)],
        compiler_params=_params("parallel", "arbitrary"),
        name="gdn_chunked",
    )(y, z, ba, alog_row, dt_row, gdn_norm)


def _gdn_step_kernel(qkv_ref, z_ref, ba_ref, cs_ref, s_ref, cw_ref, alog_ref, dt_ref, gn_ref,
                     ya_ref, snew_ref, csnew_ref):
    rowv = qkv_ref[0]
    cs = cs_ref[0]
    cw = cw_ref[...]
    y = rowv * cw[CONV_W - 1:CONV_W]
    for i in range(CONV_W - 1):
        y = y + cs[i:i + 1] * cw[i:i + 1]
    y = _silu(y)
    csnew_ref[0, 0:CONV_W - 2, :] = cs[1:CONV_W - 1]
    csnew_ref[0, CONV_W - 2:CONV_W - 1, :] = rowv

    beta_full, g_full = _gate_rows(ba_ref[0], alog_ref[...], dt_ref[...])
    r128 = lax.broadcasted_iota(jnp.int32, (GDN_DK, GDN_DK), 0)
    c128 = lax.broadcasted_iota(jnp.int32, (GDN_DK, GDN_DK), 1)
    eye = r128 == c128
    z = z_ref[0]
    gn = gn_ref[...]

    def to_col(r):
        return jnp.sum(jnp.where(eye, jnp.broadcast_to(r, (GDN_DK, GDN_DK)), 0.0),
                       axis=1, keepdims=True)

    heads = range(GDN_HEADS)
    q, k, v = [], [], []
    for h in heads:
        qh = y[:, h * GDN_DK:(h + 1) * GDN_DK]
        kh = y[:, GDN_HEADS * GDN_DK + h * GDN_DK:GDN_HEADS * GDN_DK + (h + 1) * GDN_DK]
        v.append(y[:, 2 * GDN_HEADS * GDN_DK + h * GDN_DV:2 * GDN_HEADS * GDN_DK + (h + 1) * GDN_DV])
        q.append(qh * lax.rsqrt(jnp.sum(qh * qh, axis=-1, keepdims=True) + EPS) * (GDN_DK ** -0.5))
        k.append(kh * lax.rsqrt(jnp.sum(kh * kh, axis=-1, keepdims=True) + EPS))
    k_col = [to_col(k[h]) for h in heads]
    q_col = [to_col(q[h]) for h in heads]
    s = [jnp.exp(g_full[:, GDN_HEADS + h:GDN_HEADS + h + 1]) * s_ref[0, h] for h in heads]
    u = [beta_full[:, h:h + 1] * (v[h] - jnp.sum(s[h] * k_col[h], axis=0, keepdims=True))
         for h in heads]
    s = [s[h] + k_col[h] * u[h] for h in heads]
    o = [jnp.sum(s[h] * q_col[h], axis=0, keepdims=True) for h in heads]
    for h in heads:
        snew_ref[0, h] = s[h]
        zh = z[:, h * GDN_DV:(h + 1) * GDN_DV]
        ya_ref[0, :, h * GDN_DV:(h + 1) * GDN_DV] = _rms(o[h], gn) * _silu(zh)


def _gdn_step(qkv, z, ba, conv_state, state, conv_w, alog_row, dt_row, gdn_norm):
    b = qkv.shape[0]
    qkv3 = qkv.reshape(b, 1, QKV_WIDTH)
    z3 = z.reshape(b, 1, Z_WIDTH)
    ba3 = ba.reshape(b, 1, LANES)
    return pl.pallas_call(
        _gdn_step_kernel,
        grid=(b,),
        in_specs=[pl.BlockSpec((1, 1, QKV_WIDTH), lambda i: (i, 0, 0)),
                  pl.BlockSpec((1, 1, Z_WIDTH), lambda i: (i, 0, 0)),
                  pl.BlockSpec((1, 1, LANES), lambda i: (i, 0, 0)),
                  pl.BlockSpec((1, CONV_W - 1, QKV_WIDTH), lambda i: (i, 0, 0)),
                  pl.BlockSpec((1, GDN_HEADS, GDN_DK, GDN_DV), lambda i: (i, 0, 0, 0)),
                  _full_spec(conv_w.shape), _full_spec(alog_row.shape),
                  _full_spec(dt_row.shape), _full_spec(gdn_norm.shape)],
        out_specs=[pl.BlockSpec((1, 1, Z_WIDTH), lambda i: (i, 0, 0)),
                   pl.BlockSpec((1, GDN_HEADS, GDN_DK, GDN_DV), lambda i: (i, 0, 0, 0)),
                   pl.BlockSpec((1, CONV_W - 1, QKV_WIDTH), lambda i: (i, 0, 0))],
        out_shape=[jax.ShapeDtypeStruct((b, 1, Z_WIDTH), F32),
                   jax.ShapeDtypeStruct((b, GDN_HEADS, GDN_DK, GDN_DV), F32),
                   jax.ShapeDtypeStruct((b, CONV_W - 1, QKV_WIDTH), F32)],
        compiler_params=_params("parallel"),
        name="gdn_step",
    )(qkv3, z3, ba3, conv_state, state, conv_w, alog_row, dt_row, gdn_norm)


def _mla_queries_latent(x, cos2, sin2, g_mix, w_down, g_qa, w_uq, g_kva, g_qn, g_qr, g_kr):
    xn = _rms(x, g_mix).astype(BF16)
    qa = _rms(_dot(xn, w_down[:, 0:Q_LORA]), g_qa).astype(BF16)
    q = _dot(qa, w_uq)
    qn, qr = [], []
    for h in range(MLA_HEADS):
        qn.append(_rms(q[:, h * QK_PAD:h * QK_PAD + NOPE_DIM], g_qn))
        qr.append(_rope(_rms(q[:, h * QK_PAD + NOPE_DIM:(h + 1) * QK_PAD], g_qr, ROPE_DIM),
                        cos2, sin2))
    c = _rms(_dot(xn, w_down[:, Q_LORA:Q_LORA + KV_LORA]), g_kva)
    kr = _rope(_rms(_dot(xn, w_down[:, Q_LORA + KV_LORA:]), g_kr, ROPE_DIM), cos2, sin2)
    return qn, qr, c, kr


def _mla_pre_kernel(x_ref, cos_ref, sin_ref, gmix_ref, wdown_ref, gqa_ref,
                    wuq_ref, gkva_ref, gqn_ref, gqr_ref, gkr_ref, wuk_ref, wuv_ref, gkn_ref,
                    q_ref, k_ref, v_ref, lat_ref):
    qn, qr, c, kr = _mla_queries_latent(
        x_ref[0], cos_ref[...], sin_ref[...], gmix_ref[...], wdown_ref[...],
        gqa_ref[...], wuq_ref[...], gkva_ref[...], gqn_ref[...] * ATTN_Q_SCALE,
        gqr_ref[...] * ATTN_Q_SCALE, gkr_ref[...])
    lat_ref[0, :, 0:KV_LORA] = c
    lat_ref[0, :, KV_LORA:LATENT_WIDTH] = kr[:, 0:ROPE_DIM]
    c_bf = c.astype(BF16)
    kr_bf = kr.astype(BF16)
    kfull = _dot(c_bf, wuk_ref[...])
    v_ref[0, 0] = _dot_nt(wuv_ref[...], c_bf).astype(BF16)
    gkn = gkn_ref[...]
    for h in range(MLA_HEADS):
        q_ref[0, :, h * QK_PAD:h * QK_PAD + NOPE_DIM] = qn[h].astype(BF16)
        q_ref[0, :, h * QK_PAD + NOPE_DIM:(h + 1) * QK_PAD] = qr[h].astype(BF16)
        kn = _rms(kfull[:, h * NOPE_DIM:(h + 1) * NOPE_DIM], gkn)
        k_ref[0, :, h * QK_PAD:h * QK_PAD + NOPE_DIM] = kn.astype(BF16)
        k_ref[0, :, h * QK_PAD + NOPE_DIM:(h + 1) * QK_PAD] = kr_bf


def _mla_pre(x, cos2, sin2, mw):
    b, t, d = x.shape
    tm = min(ATTN_TILE, t)
    consts = [mw["g_mix"], mw["w_down"], mw["g_qa"], mw["w_uq"],
              mw["g_kva"], mw["g_qn"], mw["g_qr"], mw["g_kr"], mw["w_uk"], mw["w_uv_t"],
              mw["g_kn"]]
    return pl.pallas_call(
        _mla_pre_kernel,
        grid=(b, t // tm),
        in_specs=[pl.BlockSpec((1, tm, d), lambda i, j: (i, j, 0)),
                  pl.BlockSpec((tm, LANES), lambda i, j: (j, 0)),
                  pl.BlockSpec((tm, LANES), lambda i, j: (j, 0))]
                 + [_full_spec(a.shape) for a in consts],
        out_specs=[pl.BlockSpec((1, tm, MLA_HEADS * QK_PAD), lambda i, j: (i, j, 0)),
                   pl.BlockSpec((1, tm, MLA_HEADS * QK_PAD), lambda i, j: (i, j, 0)),
                   pl.BlockSpec((1, 1, MLA_HEADS * V_DIM, tm), lambda i, j: (i, j, 0, 0)),
                   pl.BlockSpec((1, tm, LATENT_WIDTH), lambda i, j: (i, j, 0))],
        out_shape=[jax.ShapeDtypeStruct((b, t, MLA_HEADS * QK_PAD), BF16),
                   jax.ShapeDtypeStruct((b, t, MLA_HEADS * QK_PAD), BF16),
                   jax.ShapeDtypeStruct((b, t // tm, MLA_HEADS * V_DIM, tm), BF16),
                   jax.ShapeDtypeStruct((b, t, LATENT_WIDTH), F32)],
        compiler_params=_params("parallel", "parallel"),
        name="mla_pre",
    )(x, cos2, sin2, *consts)


def _mla_pre_sample_kernel(x_ref, cos_ref, sin_ref, gmix_ref, wdown_ref, gqa_ref,
                           wuq_ref, gkva_ref, gqn_ref, gqr_ref, gkr_ref, wuk_ref, gkn_ref,
                           qa_ref, qr_ref, lat_ref):
    qn, qr, c, kr = _mla_queries_latent(
        x_ref[...], cos_ref[...], sin_ref[...], gmix_ref[...], wdown_ref[...],
        gqa_ref[...], wuq_ref[...], gkva_ref[...], gqn_ref[...], gqr_ref[...],
        gkr_ref[...])
    lat_ref[:, 0:KV_LORA] = c
    lat_ref[:, KV_LORA:LATENT_WIDTH] = kr[:, 0:ROPE_DIM]
    gkn = gkn_ref[...]
    for h in range(MLA_HEADS):
        hi, lo = _split2(qn[h] * gkn)
        w_h = wuk_ref[:, h * NOPE_DIM:(h + 1) * NOPE_DIM]
        qa_ref[h] = _dot_nt(hi, w_h) + _dot_nt(lo, w_h)
        qr_ref[h] = qr[h]


def _mla_pre_sample(x2, cos2, sin2, mw):
    m = x2.shape[0]
    consts = [mw["g_mix"], mw["w_down"], mw["g_qa"], mw["w_uq"],
              mw["g_kva"], mw["g_qn"], mw["g_qr"], mw["g_kr"], mw["w_uk"], mw["g_kn"]]
    args = [x2, cos2, sin2] + consts
    return pl.pallas_call(
        _mla_pre_sample_kernel,
        grid=(1,),
        in_specs=[_full_spec(a.shape) for a in args],
        out_specs=[_full_spec((MLA_HEADS, m, KV_LORA)), _full_spec((MLA_HEADS, m, LANES)),
                   _full_spec((m, LATENT_WIDTH))],
        out_shape=[jax.ShapeDtypeStruct((MLA_HEADS, m, KV_LORA), F32),
                   jax.ShapeDtypeStruct((MLA_HEADS, m, LANES), F32),
                   jax.ShapeDtypeStruct((m, LATENT_WIDTH), F32)],
        compiler_params=_params("arbitrary"),
        name="mla_pre_sample",
    )(*args)


ATTN_HEADS_PER_STEP = 8


def _attn_kernel(q_ref, k_ref, vt_ref, o_ref, *, tq):
    qi = pl.program_id(2)
    hs = range(ATTN_HEADS_PER_STEP)
    q_t = [q_ref[0, :, g * QK_PAD:(g + 1) * QK_PAD].astype(F32).T.astype(BF16) for g in hs]
    key = lax.broadcasted_iota(jnp.int32, (tq, tq), 0)
    qry = lax.broadcasted_iota(jnp.int32, (tq, tq), 1)

    def raw_scores(j):
        start = pl.multiple_of(j * tq, tq)
        return tuple(_dot(k_ref[0, pl.ds(start, tq), g * QK_PAD:(g + 1) * QK_PAD], q_t[g])
                     for g in hs)

    def consume(j, s_raw, stats, diagonal):
        vt = [vt_ref[0, j, g * V_DIM:(g + 1) * V_DIM, :] for g in hs]
        s = list(s_raw)
        if diagonal:
            s = [jnp.where(key <= qry, s[g], -jnp.inf) for g in hs]
        m_new = [jnp.maximum(stats[g][0], jnp.max(s[g], axis=0, keepdims=True)) for g in hs]
        p = [jnp.exp2(s[g] - m_new[g]) for g in hs]
        pv = [_dot(vt[g], p[g].astype(BF16)) for g in hs]
        out = []
        for g in hs:
            m, l, acc = stats[g]
            alpha = jnp.exp2(m - m_new[g])
            out.append((m_new[g], alpha * l + jnp.sum(p[g], axis=0, keepdims=True),
                        alpha * acc + pv[g]))
        return tuple(out)

    init = tuple((jnp.full((1, tq), -jnp.inf, F32), jnp.zeros((1, tq), F32),
                  jnp.zeros((V_DIM, tq), F32)) for _ in hs)
    stats = lax.fori_loop(0, qi, lambda j, st: consume(j, raw_scores(j), st, False), init)
    stats = consume(qi, raw_scores(qi), stats, True)
    for g in hs:
        _, l, acc = stats[g]
        o_ref[0, :, g * V_DIM:(g + 1) * V_DIM] = (acc / l).T.astype(BF16)


def _mla_prompt_attention(q_cat, k_cat, v_t):
    b, t, _ = q_cat.shape
    tq = min(ATTN_TILE, t)
    hps = ATTN_HEADS_PER_STEP
    return pl.pallas_call(
        functools.partial(_attn_kernel, tq=tq),
        grid=(b, MLA_HEADS // hps, t // tq),
        in_specs=[pl.BlockSpec((1, tq, hps * QK_PAD), lambda i, h, j: (i, j, h)),
                  pl.BlockSpec((1, t, hps * QK_PAD), lambda i, h, j: (i, 0, h)),
                  pl.BlockSpec((1, t // tq, hps * V_DIM, tq), lambda i, h, j: (i, 0, h, 0))],
        out_specs=pl.BlockSpec((1, tq, hps * V_DIM), lambda i, h, j: (i, j, h)),
        out_shape=jax.ShapeDtypeStruct((b, t, MLA_HEADS * V_DIM), BF16),
        compiler_params=_params("parallel", "parallel", "arbitrary"),
        name="mla_prompt_attention",
    )(q_cat, k_cat, v_t)


def _dec_attn_kernel(pt_ref, *refs, pps):
    del pt_ref
    page_refs = refs[:pps]
    (wukt_ref, qa_ref, qr_ref, lnew_ref, ctx_ref,
     lhs_ref, qr16_ref, m_ref, l_ref, acc_ref) = refs[pps:]
    b = pl.program_id(0)
    j = pl.program_id(1)
    nj = pl.num_programs(1)
    n_k = MLA_HEADS * NOPE_DIM

    @pl.when((b == 0) & (j == 0))
    def _():
        lhs_ref[0:n_k, :] = wukt_ref[...]

    @pl.when(j == 0)
    def _():
        qa = jnp.concatenate([qa_ref[h, 0] for h in range(MLA_HEADS)]
                             + [jnp.zeros((8, KV_LORA), F32)], axis=0)
        lhs_ref[n_k:n_k + 16, :] = qa.astype(BF16)
        qr = jnp.concatenate([qr_ref[h, 0] for h in range(MLA_HEADS)]
                             + [jnp.zeros((8, LANES), F32)], axis=0)
        qr16_ref[...] = qr.astype(BF16)
        m_ref[...] = jnp.full(m_ref.shape, -jnp.inf, F32)
        l_ref[...] = jnp.zeros(l_ref.shape, F32)
        acc_ref[...] = jnp.zeros(acc_ref.shape, F32)

    def scores(res, sr):
        kt = res[0:n_k]
        ss = jnp.sum((kt * kt).reshape(MLA_HEADS, NOPE_DIM, res.shape[1]), axis=1)
        sn = res[n_k:n_k + MLA_HEADS]
        return (sn * lax.rsqrt(ss * (1.0 / NOPE_DIM) + EPS) + sr[0:MLA_HEADS]) * MLA_SCALE

    c_tok, s_parts = [], []
    for i in range(0, pps, 2):
        tile = jnp.concatenate([page_refs[i][...], page_refs[i + 1][...]], axis=1)
        c_bf = tile[0:KV_LORA].astype(BF16)
        kr_bf = tile[KV_LORA:LATENT_WIDTH].astype(BF16)
        s_parts.append(scores(_dot(lhs_ref[...], c_bf),
                              _dot(qr16_ref[:, 0:ROPE_DIM], kr_bf)))
        c_tok.append(tile[0:KV_LORA].T.astype(BF16))
    s = jnp.concatenate(s_parts, axis=1)
    m_old = m_ref[...]
    m_new = jnp.maximum(m_old, jnp.max(s, axis=-1, keepdims=True))
    alpha = jnp.exp(m_old - m_new)
    p = jnp.exp(s - m_new)
    l_ref[...] = alpha * l_ref[...] + jnp.sum(p, axis=-1, keepdims=True)
    p_bf = p.astype(BF16)
    n_tok = c_tok[0].shape[0]
    acc = alpha * acc_ref[...]
    for t, c_t in enumerate(c_tok):
        acc = acc + _dot(p_bf[:, t * n_tok:(t + 1) * n_tok], c_t)
    acc_ref[...] = acc
    m_ref[...] = m_new

    @pl.when(j == nj - 1)
    def _():
        ln = lnew_ref[0]
        c_new = ln[:, 0:KV_LORA]
        c8 = jnp.broadcast_to(c_new, (8, KV_LORA)).astype(BF16)
        kr8 = jnp.broadcast_to(ln[:, KV_LORA:LATENT_WIDTH], (8, ROPE_DIM)).astype(BF16)
        s_new = scores(_dot_nt(lhs_ref[...], c8),
                       _dot_nt(qr16_ref[:, 0:ROPE_DIM], kr8))[:, 0:1]
        m_old = m_ref[...]
        m_new = jnp.maximum(m_old, s_new)
        alpha = jnp.exp(m_old - m_new)
        p_new = jnp.exp(s_new - m_new)
        ctx = (alpha * acc_ref[...] + p_new * c_new) / (alpha * l_ref[...] + p_new)
        for h in range(MLA_HEADS):
            ctx_ref[h, 0] = ctx[h:h + 1]


def _dec_pages_per_step(n_pages):
    pps = DEC_PAGES_PER_STEP
    while n_pages % pps:
        pps //= 2
    assert pps >= 2
    return pps


def _dec_scratch():
    return [pltpu.VMEM((MLA_HEADS * NOPE_DIM + 16, KV_LORA), BF16),
            pltpu.VMEM((16, LANES), BF16),
            pltpu.VMEM((MLA_HEADS, 1), F32),
            pltpu.VMEM((MLA_HEADS, 1), F32),
            pltpu.VMEM((MLA_HEADS, KV_LORA), F32)]


def _mla_decode_attention(cache, layer, page_table, wuk_t, qa, qr, lat_new):
    bsz, n_pages = page_table.shape
    page = cache.shape[3]
    pps = _dec_pages_per_step(n_pages)
    qa4 = qa.reshape(MLA_HEADS, bsz, 1, KV_LORA)
    qr4 = qr.reshape(MLA_HEADS, bsz, 1, LANES)
    ln3 = lat_new.reshape(bsz, 1, LATENT_WIDTH)

    def page_spec(i):
        return pl.BlockSpec((pl.Squeezed(), pl.Squeezed(), LATENT_WIDTH, page),
                            lambda b, j, pt: (layer, pt[b, j * pps + i], 0, 0))

    grid_spec = pltpu.PrefetchScalarGridSpec(
        num_scalar_prefetch=1,
        grid=(bsz, n_pages // pps),
        in_specs=[page_spec(i) for i in range(pps)] + [
            pl.BlockSpec(wuk_t.shape, lambda b, j, pt: (0, 0)),
            pl.BlockSpec((MLA_HEADS, 1, 1, KV_LORA), lambda b, j, pt: (0, b, 0, 0)),
            pl.BlockSpec((MLA_HEADS, 1, 1, LANES), lambda b, j, pt: (0, b, 0, 0)),
            pl.BlockSpec((1, 1, LATENT_WIDTH), lambda b, j, pt: (b, 0, 0))],
        out_specs=pl.BlockSpec((MLA_HEADS, 1, 1, KV_LORA), lambda b, j, pt: (0, b, 0, 0)),
        scratch_shapes=_dec_scratch())
    ctx = pl.pallas_call(
        functools.partial(_dec_attn_kernel, pps=pps),
        grid_spec=grid_spec,
        out_shape=jax.ShapeDtypeStruct((MLA_HEADS, bsz, 1, KV_LORA), F32),
        compiler_params=_params("arbitrary", "arbitrary"),
        name="mla_decode_attention",
    )(page_table, *([cache] * pps), wuk_t, qa4, qr4, ln3)
    return ctx.reshape(MLA_HEADS, bsz, KV_LORA)


def _dec_post_kernel(ctx_ref, wuv_ref, y_ref):
    for h in range(MLA_HEADS):
        hi, lo = _split2(ctx_ref[h])
        w_h = wuv_ref[:, h * V_DIM:(h + 1) * V_DIM]
        y_ref[:, h * V_DIM:(h + 1) * V_DIM] = (_dot(hi, w_h) + _dot(lo, w_h)).astype(BF16)


def _dec_post(ctx, w_uv):
    bsz = ctx.shape[1]
    return pl.pallas_call(
        _dec_post_kernel,
        grid=(1,),
        in_specs=[_full_spec(ctx.shape), _full_spec(w_uv.shape)],
        out_specs=_full_spec((bsz, MLA_HEADS * V_DIM)),
        out_shape=jax.ShapeDtypeStruct((bsz, MLA_HEADS * V_DIM), BF16),
        compiler_params=_params("arbitrary"),
        name="mla_decode_values",
    )(ctx, w_uv)


def _merge_mlp_kernel(x_ref, ya_ref, yb_ref, gmix_ref, wg_ref, wya_ref, wyb_ref, wo_ref,
                      gmlp_ref, wup_ref, wdn_ref, y_ref):
    tm, d = x_ref.shape
    d_ff = wup_ref.shape[1]
    n_grp = max(1, tm // MERGE_ROWS_PER_GROUP)
    rows = tm // n_grp
    grp = range(n_grp)
    sl = [slice(r * rows, (r + 1) * rows) for r in grp]
    x = [x_ref[sl[r], :] for r in grp]
    xn = [_rms(x[r], gmix_ref[...]).astype(BF16) for r in grp]
    gates = [_sigmoid(_dot(xn[r], wg_ref[...])) for r in grp]
    pa = [_dot(ya_ref[sl[r], :].astype(BF16), wya_ref[...]) for r in grp]
    pb = [_dot(yb_ref[sl[r], :].astype(BF16), wyb_ref[...]) for r in grp]
    mix = [(gates[r][:, :d] * pa[r] + gates[r][:, d:] * pb[r]).astype(BF16) for r in grp]
    x1 = [x[r] + _dot(mix[r], wo_ref[...]) for r in grp]
    hin = [_rms(x1[r], gmlp_ref[...]).astype(BF16) for r in grp]
    y = x1
    for f in range(0, d_ff, MLP_FF_BLOCK):
        hmid = [jnp.maximum(_dot(hin[r], wup_ref[:, f:f + MLP_FF_BLOCK]), 0.0) for r in grp]
        y = [y[r] + _dot((hmid[r] * hmid[r]).astype(BF16), wdn_ref[f:f + MLP_FF_BLOCK, :])
             for r in grp]
    for r in grp:
        y_ref[sl[r], :] = y[r]


MERGE_ROWS_PER_GROUP = 256
MLP_FF_BLOCK = 1024


def _merge_mlp(x2, ya, yb, ew):
    m, d = x2.shape
    tm = min(512, m)
    consts = [ew["g_mix"], ew["w_gates"], ew["w_ya"], ew["w_yb"], ew["w_o"], ew["g_mlp"],
              ew["w_up"], ew["w_down"]]

    def const_spec(a):
        nd = a.ndim
        return pl.BlockSpec(a.shape, lambda i: (0,) * nd, pipeline_mode=pl.Buffered(1))

    return pl.pallas_call(
        _merge_mlp_kernel,
        grid=(m // tm,),
        in_specs=[pl.BlockSpec((tm, d), lambda i: (i, 0)),
                  pl.BlockSpec((tm, ya.shape[1]), lambda i: (i, 0)),
                  pl.BlockSpec((tm, yb.shape[1]), lambda i: (i, 0))]
                 + [const_spec(a) for a in consts],
        out_specs=pl.BlockSpec((tm, d), lambda i: (i, 0)),
        out_shape=jax.ShapeDtypeStruct((m, d), F32),
        compiler_params=_params("parallel"),
        name="merge_mlp",
    )(x2, ya, yb, *consts)


def _rope_tables(pos):
    inv = jnp.power(ROPE_THETA, -jnp.arange(0, ROPE_DIM, 2, dtype=F32) / ROPE_DIM)
    ang = pos.astype(F32)[:, None] * inv[None, :]
    cos, sin = jnp.cos(ang), jnp.sin(ang)
    zeros = jnp.zeros((pos.shape[0], LANES - ROPE_DIM), F32)
    return (jnp.concatenate([cos, cos, zeros], axis=1),
            jnp.concatenate([-sin, sin, zeros], axis=1))


def _pad_lanes(a, width=LANES, offset=0):
    return jnp.zeros((1, width), F32).at[0, offset:offset + a.shape[0]].set(a.astype(F32))


def _layer_weights(l, norm_mix, w_in, conv_w, a_log, dt_bias, gdn_norm, q_a_norm, w_uq,
                   kv_a_norm, w_uk, w_uv, q_norm_nope, q_norm_rope, k_norm_nope, k_norm_rope,
                   w_ya, w_yb, w_o, norm_mlp, w_up, w_down):
    d = w_in.shape[1]
    off_z = QKV_WIDTH
    off_b = off_z + Z_WIDTH
    off_qd = off_b + 2 * GDN_HEADS
    off_kv = off_qd + Q_LORA
    off_g = off_kv + LATENT_WIDTH
    wi = w_in[l]
    g_mix = norm_mix[l].reshape(1, d)
    gw = {
        "g_mix": g_mix,
        "w_in": wi.astype(BF16),
        "conv_w": conv_w[l],
        "alog_row": _pad_lanes(a_log[l], offset=GDN_HEADS),
        "dt_row": _pad_lanes(dt_bias[l], offset=GDN_HEADS),
        "gdn_norm": gdn_norm[l].reshape(1, GDN_DV),
    }
    wq = w_uq[l].reshape(Q_LORA, MLA_HEADS, NOPE_DIM + ROPE_DIM)
    wq = jnp.pad(wq, ((0, 0), (0, 0), (0, QK_PAD - NOPE_DIM - ROPE_DIM)))
    mw = {
        "g_mix": g_mix,
        "w_down": jnp.pad(wi[:, off_qd:off_g], ((0, 0), (0, LANES - ROPE_DIM))).astype(BF16),
        "g_qa": q_a_norm[l].reshape(1, Q_LORA),
        "w_uq": wq.reshape(Q_LORA, MLA_HEADS * QK_PAD).astype(BF16),
        "g_kva": kv_a_norm[l].reshape(1, KV_LORA),
        "g_qn": q_norm_nope[l].reshape(1, NOPE_DIM),
        "g_qr": _pad_lanes(q_norm_rope[l]),
        "g_kr": _pad_lanes(k_norm_rope[l]),
        "g_kn": k_norm_nope[l].reshape(1, NOPE_DIM),
        "w_uk": w_uk[l].astype(BF16),
        "w_uk_t": w_uk[l].T.astype(BF16),
        "w_uv": w_uv[l].astype(BF16),
        "w_uv_t": w_uv[l].T.astype(BF16),
    }
    ew = {
        "g_mix": g_mix,
        "w_gates": wi[:, off_g:].astype(BF16),
        "w_ya": w_ya[l].astype(BF16),
        "w_yb": w_yb[l].astype(BF16),
        "w_o": w_o[l].astype(BF16),
        "g_mlp": norm_mlp[l].reshape(1, d),
        "w_up": w_up[l].astype(BF16),
        "w_down": w_down[l].astype(BF16),
    }
    return gw, mw, ew


def kernel(x_prompt, x_sample, cache_mla, state_gdn, state_conv, page_table, norm_mix, w_in,
           conv_w, a_log, dt_bias, gdn_norm, q_a_norm, w_uq, kv_a_norm, w_uk, w_uv,
           q_norm_nope, q_norm_rope, k_norm_nope, k_norm_rope, w_ya, w_yb, w_o, norm_mlp,
           w_up, w_down):
    depth = w_in.shape[0]
    bp, tp, d = x_prompt.shape
    bs, ts, _ = x_sample.shape
    assert ts == 1 and tp % GDN_CHUNK == 0
    past_len = page_table.shape[1] * cache_mla.shape[2]
    cache_t = jnp.swapaxes(cache_mla, 2, 3)
    cos_p, sin_p = _rope_tables(jnp.arange(tp))
    cos_s, sin_s = _rope_tables(past_len + jnp.zeros((bs,), jnp.int32))

    x_p, x_s = x_prompt, x_sample.reshape(bs, d)
    rows_p, gdn_p, conv_p, rows_s, gdn_s, conv_s = [], [], [], [], [], []
    for l in range(depth):
        gw, mw, ew = _layer_weights(
            l, norm_mix, w_in, conv_w, a_log, dt_bias, gdn_norm, q_a_norm, w_uq, kv_a_norm,
            w_uk, w_uv, q_norm_nope, q_norm_rope, k_norm_nope, k_norm_rope, w_ya, w_yb, w_o,
            norm_mlp, w_up, w_down)
        gdn_consts = (gw["conv_w"], gw["alog_row"], gw["dt_row"], gw["gdn_norm"])

        xp2 = x_p.reshape(bp * tp, d)
        yc, z, ba, cs_p = _gdn_proj_conv(x_p, gw["g_mix"], gw["w_in"], gw["conv_w"])
        ya_p, s_p = _gdn_chunked(yc.reshape(bp, tp, QKV_WIDTH), z.reshape(bp, tp, Z_WIDTH),
                                 ba.reshape(bp, tp, LANES), *gdn_consts[1:])
        q_cat, k_cat, v_p, lat_p = _mla_pre(x_p, cos_p, sin_p, mw)
        yb_p = _mla_prompt_attention(q_cat, k_cat, v_p)
        y_p = _merge_mlp(xp2, ya_p.reshape(bp * tp, Z_WIDTH), yb_p.reshape(bp * tp, -1), ew)

        qkv, z, ba = _gdn_proj(x_s, gw["g_mix"], gw["w_in"])
        ya_s, s_s, cs_s = _gdn_step(qkv, z, ba, state_conv[l], state_gdn[l], *gdn_consts)
        qa, qr, lat_s = _mla_pre_sample(x_s, cos_s, sin_s, mw)
        ctx = _mla_decode_attention(cache_t, l, page_table, mw["w_uk_t"], qa, qr, lat_s)
        yb_s = _dec_post(ctx, mw["w_uv"])
        y_s = _merge_mlp(x_s, ya_s.reshape(bs, Z_WIDTH), yb_s, ew)

        x_p, x_s = y_p.reshape(bp, tp, d), y_s
        rows_p.append(lat_p)
        gdn_p.append(s_p)
        conv_p.append(cs_p)
        rows_s.append(lat_s.reshape(bs, 1, LATENT_WIDTH))
        gdn_s.append(s_s)
        conv_s.append(cs_s)
    return (x_p, x_s.reshape(bs, 1, d), jnp.stack(rows_p), jnp.stack(gdn_p), jnp.stack(conv_p),
            jnp.stack(rows_s), jnp.stack(gdn_s), jnp.stack(conv_s))
```

```python
import functools

import jax
import jax.numpy as jnp
from jax import lax
from jax.experimental import pallas as pl
from jax.experimental.pallas import tpu as pltpu

F32 = jnp.float32
BF16 = jnp.bfloat16

EPS = 1e-6
GDN_HEADS = 8
GDN_DK = 128
GDN_DV = 128
CONV_W = 4
GDN_CHUNK = 64
MLA_HEADS = 8
Q_LORA = 512
KV_LORA = 512
NOPE_DIM = 128
ROPE_DIM = 64
V_DIM = 128
ROPE_THETA = 10000.0
MLA_SCALE = (NOPE_DIM + ROPE_DIM) ** -0.5
LOG2_E = 1.4426950408889634
ATTN_Q_SCALE = MLA_SCALE * LOG2_E
LATENT_WIDTH = KV_LORA + ROPE_DIM
QKV_WIDTH = GDN_HEADS * (2 * GDN_DK + GDN_DV)
Z_WIDTH = GDN_HEADS * GDN_DV

LANES = 128
QK_PAD = 256
VMEM_LIMIT = 56 * 1024 * 1024
DEC_PAGES_PER_STEP = 64
ATTN_TILE = 256


def _dot(a, b):
    return jnp.dot(a, b, preferred_element_type=F32)


def _dot_nt(a, b):
    return lax.dot_general(a, b, (((1,), (1,)), ((), ())), preferred_element_type=F32)


def _dot_tn(a, b):
    return lax.dot_general(a, b, (((0,), (0,)), ((), ())), preferred_element_type=F32)


def _split2(x):
    hi = x.astype(BF16)
    lo = (x - hi.astype(F32)).astype(BF16)
    return hi, lo


def _split3(x):
    hi = x.astype(BF16)
    r = x - hi.astype(F32)
    mid = r.astype(BF16)
    lo = (r - mid.astype(F32)).astype(BF16)
    return hi, mid, lo


def _rms(x, g, n=None):
    n = x.shape[-1] if n is None else n
    ss = jnp.sum(x * x, axis=-1, keepdims=True) * (1.0 / n)
    return x * lax.rsqrt(ss + EPS) * g


def _sigmoid(x):
    return 1.0 / (1.0 + jnp.exp(-x))


def _silu(x):
    h = 0.5 * x
    return h + h * jnp.tanh(h)


def _softplus(x):
    return jnp.maximum(x, 0.0) + jnp.log1p(jnp.exp(-jnp.abs(x)))


def _rope(x, cos2, sin2):
    lane = lax.broadcasted_iota(jnp.int32, x.shape, 1)
    fwd = pltpu.roll(x, LANES - ROPE_DIM // 2, axis=1)
    bwd = pltpu.roll(x, ROPE_DIM // 2, axis=1)
    swapped = jnp.where(lane < ROPE_DIM // 2, fwd, bwd)
    return x * cos2 + swapped * sin2


def _full_spec(shape):
    nd = len(shape)
    return pl.BlockSpec(shape, lambda *_: (0,) * nd)


def _params(*sem):
    return pltpu.CompilerParams(dimension_semantics=sem, vmem_limit_bytes=VMEM_LIMIT)


OFF_Z = QKV_WIDTH
OFF_BA = QKV_WIDTH + Z_WIDTH
GDN_IN_COLS = OFF_BA + LANES


def _gdn_w_spec(w_in_bf):
    return pl.BlockSpec((w_in_bf.shape[0], GDN_IN_COLS), lambda i: (0, 0))


def _gdn_proj_kernel(x_ref, g_ref, w_ref, qkv_ref, z_ref, ba_ref):
    xn = _rms(x_ref[...], g_ref[...]).astype(BF16)
    qkv_ref[...] = _dot(xn, w_ref[:, 0:OFF_Z])
    z_ref[...] = _dot(xn, w_ref[:, OFF_Z:OFF_BA])
    ba_ref[...] = _dot(xn, w_ref[:, OFF_BA:GDN_IN_COLS])


def _gdn_proj(x2, g, w_in_bf):
    m, d = x2.shape
    tm = min(256, m)
    return pl.pallas_call(
        _gdn_proj_kernel,
        grid=(m // tm,),
        in_specs=[pl.BlockSpec((tm, d), lambda i: (i, 0)),
                  _full_spec(g.shape), _gdn_w_spec(w_in_bf)],
        out_specs=[pl.BlockSpec((tm, QKV_WIDTH), lambda i: (i, 0)),
                   pl.BlockSpec((tm, Z_WIDTH), lambda i: (i, 0)),
                   pl.BlockSpec((tm, LANES), lambda i: (i, 0))],
        out_shape=[jax.ShapeDtypeStruct((m, QKV_WIDTH), F32),
                   jax.ShapeDtypeStruct((m, Z_WIDTH), F32),
                   jax.ShapeDtypeStruct((m, LANES), F32)],
        compiler_params=_params("parallel"),
        name="gdn_proj",
    )(x2, g, w_in_bf)


PROJ_CONV_COLS = 512
assert CONV_W == 4


def _gdn_proj_conv_kernel(x_ref, g_ref, w_ref, cw_ref,
                          y_ref, z_ref, ba_ref, cs_ref, xp_ref, *, tiles_per_seq):
    i = pl.program_id(0)
    tm = x_ref.shape[0]

    @pl.when(i % tiles_per_seq == 0)
    def _():
        xp_ref[...] = jnp.zeros((8, QKV_WIDTH), F32)

    xn = _rms(x_ref[...], g_ref[...]).astype(BF16)
    row8 = lax.broadcasted_iota(jnp.int32, (8, PROJ_CONV_COLS), 0)
    for c0 in range(0, QKV_WIDTH, PROJ_CONV_COLS):
        cb = slice(c0, c0 + PROJ_CONV_COLS)
        raw = _dot(xn, w_ref[:, cb])
        halo = xp_ref[:, cb]
        cw = cw_ref[:, cb]
        back2 = pltpu.roll(raw, 2, axis=0)
        x2 = jnp.concatenate([jnp.where(row8 < 2, pltpu.roll(halo, 2, axis=0), back2[0:8]),
                              back2[8:]], axis=0)
        u = raw * cw[2:3] + x2 * cw[0:1]
        u_prev = halo[7:8] * cw[2:3] + halo[5:6] * cw[0:1]
        back1 = pltpu.roll(u, 1, axis=0)
        su = jnp.concatenate([jnp.where(row8 < 1, u_prev, back1[0:8]), back1[8:]], axis=0)
        y_ref[:, cb] = _silu(raw * cw[3:4] + x2 * cw[1:2] + su)
        xp_ref[:, cb] = raw[tm - 8:tm]
    z_ref[...] = _dot(xn, w_ref[:, OFF_Z:OFF_BA])
    ba_ref[...] = _dot(xn, w_ref[:, OFF_BA:GDN_IN_COLS])

    @pl.when(i % tiles_per_seq == tiles_per_seq - 1)
    def _():
        cs_ref[0] = xp_ref[8 - (CONV_W - 1):8, :]


def _gdn_proj_conv(x, g, w_in_bf, conv_w):
    b, t, d = x.shape
    tm = min(256, t)
    assert t % tm == 0
    tiles_per_seq = t // tm
    m = b * t
    return pl.pallas_call(
        functools.partial(_gdn_proj_conv_kernel, tiles_per_seq=tiles_per_seq),
        grid=(m // tm,),
        in_specs=[pl.BlockSpec((tm, d), lambda i: (i, 0)),
                  _full_spec(g.shape), _gdn_w_spec(w_in_bf), _full_spec(conv_w.shape)],
        out_specs=[pl.BlockSpec((tm, QKV_WIDTH), lambda i: (i, 0)),
                   pl.BlockSpec((tm, Z_WIDTH), lambda i: (i, 0)),
                   pl.BlockSpec((tm, LANES), lambda i: (i, 0)),
                   pl.BlockSpec((1, CONV_W - 1, QKV_WIDTH), lambda i: (i // tiles_per_seq, 0, 0))],
        out_shape=[jax.ShapeDtypeStruct((m, QKV_WIDTH), F32),
                   jax.ShapeDtypeStruct((m, Z_WIDTH), F32),
                   jax.ShapeDtypeStruct((m, LANES), F32),
                   jax.ShapeDtypeStruct((b, CONV_W - 1, QKV_WIDTH), F32)],
        scratch_shapes=[pltpu.VMEM((8, QKV_WIDTH), F32)],
        compiler_params=_params("arbitrary"),
        name="gdn_proj_conv",
    )(x.reshape(m, d), g, w_in_bf, conv_w)


def _gate_rows(ba, alog_row, dt_row):
    lane = lax.broadcasted_iota(jnp.int32, ba.shape, 1)
    is_a = (lane >= GDN_HEADS) & (lane < 2 * GDN_HEADS)
    g = jnp.where(is_a, -jnp.exp(alog_row) * _softplus(ba + dt_row), 0.0)
    return _sigmoid(ba), g


def _gdn_chunk_kernel(y_ref, z_ref, ba_ref, alog_ref, dt_ref, gn_ref, ya_ref, s_ref):
    c = pl.program_id(1)
    C = GDN_CHUNK
    TB = y_ref.shape[1]
    n_sub = TB // C

    @pl.when(c == 0)
    def _():
        s_ref[...] = jnp.zeros(s_ref.shape, F32)

    y = y_ref[0]

    beta_full, g_full = _gate_rows(ba_ref[0], alog_ref[...], dt_ref[...])
    rtb = lax.broadcasted_iota(jnp.int32, (TB, TB), 0)
    ctb = lax.broadcasted_iota(jnp.int32, (TB, TB), 1)
    tril = jnp.where((rtb >= ctb) & (rtb // C == ctb // C), 1.0, 0.0).astype(BF16)
    g_col = sum(_dot(tril, part) for part in _split3(g_full))
    r128 = lax.broadcasted_iota(jnp.int32, (LANES, LANES), 0)
    c128 = lax.broadcasted_iota(jnp.int32, (LANES, LANES), 1)
    eye = jnp.where(r128 == c128, 1.0, 0.0).astype(BF16)
    g_rows = sum(_dot_nt(eye, part) for part in _split3(g_col))

    z = z_ref[0]
    gn = gn_ref[...]
    heads = range(GDN_HEADS)
    SB = min(TB, C * GDN_CHUNKS_PER_BLOCK)
    n_blk = TB // SB
    per_blk = SB // C
    rsb = lax.broadcasted_iota(jnp.int32, (SB, SB), 0)
    csb = lax.broadcasted_iota(jnp.int32, (SB, SB), 1)
    same_chunk = (rsb // C) == (csb // C)
    incl = same_chunk & (rsb >= csb)
    strict = same_chunk & (rsb > csb)
    chunk_of_row = lax.broadcasted_iota(jnp.int32, (SB, 1), 0) // C
    units = [(b, h) for b in range(n_blk) for h in heads]
    q, k, gc, eg, g_last, rhs_bf, rhs, decay, kq, kd = ({} for _ in range(10))
    for u in units:
        b, h = u
        r = slice(b * SB, (b + 1) * SB)
        qh = y[r, h * GDN_DK:(h + 1) * GDN_DK]
        kh = y[r, GDN_HEADS * GDN_DK + h * GDN_DK:GDN_HEADS * GDN_DK + (h + 1) * GDN_DK]
        vh = y[r, 2 * GDN_HEADS * GDN_DK + h * GDN_DV:2 * GDN_HEADS * GDN_DK + (h + 1) * GDN_DV]
        q[u] = qh * lax.rsqrt(jnp.sum(qh * qh, axis=-1, keepdims=True) + EPS) * (GDN_DK ** -0.5)
        k[u] = kh * lax.rsqrt(jnp.sum(kh * kh, axis=-1, keepdims=True) + EPS)
        beta = beta_full[r, h:h + 1]
        gc[u] = g_col[r, GDN_HEADS + h:GDN_HEADS + h + 1]
        gr = g_rows[GDN_HEADS + h:GDN_HEADS + h + 1, r]
        decay[u] = jnp.where(incl, jnp.exp(jnp.where(incl, gc[u] - gr, 0.0)), 0.0)
        eg[u] = jnp.exp(gc[u])
        g_last[u] = [gc[u][(n + 1) * C - 1:(n + 1) * C, :] for n in range(per_blk)]
        g_end = g_last[u][0]
        for n in range(1, per_blk):
            g_end = jnp.where(chunk_of_row >= n, g_last[u][n], g_end)
        kd[u] = (k[u] * jnp.exp(g_end - gc[u])).astype(BF16)
        kb = k[u] * beta
        rhs[u] = jnp.concatenate([vh * beta, kb * eg[u]], axis=1)
        rhs_bf[u] = rhs[u].astype(BF16)
        kq[u] = _dot_nt(jnp.concatenate([kb, q[u]], axis=0).astype(BF16),
                        k[u].astype(BF16))
    pw = {u: jnp.where(strict, kq[u][:SB] * decay[u], 0.0) for u in units}
    aqk = {u: (kq[u][SB:] * decay[u]).astype(BF16) for u in units}
    n_acc = {u: -pw[u] for u in units}
    pw_bf = {u: pw[u].astype(BF16) for u in units}
    for _ in range(C.bit_length() - 2):
        pw = {u: _dot(pw_bf[u], pw_bf[u]) for u in units}
        pw_bf = {u: pw[u].astype(BF16) for u in units}
        n_acc = {u: n_acc[u] + pw[u] + _dot(n_acc[u].astype(BF16), pw_bf[u]) for u in units}
    sol = {u: rhs[u] + _dot(n_acc[u].astype(BF16), rhs_bf[u]) for u in units}
    qg = {u: q[u] * eg[u] for u in units}
    s_cur = [s_ref[0, h] for h in heads]
    zero_rows = jnp.zeros((C, GDN_DV), BF16)
    for n in range(n_sub):
        b, ln = divmod(n, per_blk)
        rows = slice(ln * C, (ln + 1) * C)
        tok = slice(n * C, (n + 1) * C)
        ws = [_dot(jnp.concatenate([sol[(b, h)][rows, GDN_DV:], qg[(b, h)][rows]],
                                   axis=0).astype(BF16),
                   s_cur[h].astype(BF16)) for h in heads]
        u_bf = [(sol[(b, h)][rows, :GDN_DV] - ws[h][:C]).astype(BF16) for h in heads]
        u_pad = [jnp.concatenate([zero_rows] * ln + [u_bf[h]] + [zero_rows] * (per_blk - 1 - ln),
                                 axis=0) for h in heads]
        o = [ws[h][C:] + _dot(aqk[(b, h)][rows], u_pad[h]) for h in heads]
        s_cur = [jnp.exp(g_last[(b, h)][ln]) * s_cur[h] + _dot_tn(kd[(b, h)][rows], u_bf[h])
                 for h in heads]
        for h in heads:
            zh = z[tok, h * GDN_DV:(h + 1) * GDN_DV]
            ya_ref[0, tok, h * GDN_DV:(h + 1) * GDN_DV] = (
                _rms(o[h], gn) * _silu(zh)).astype(BF16)
    for h in heads:
        s_ref[0, h] = s_cur[h]


GDN_CHUNKS_PER_STEP = 4
GDN_CHUNKS_PER_BLOCK = 2


def _gdn_chunked(y, z, ba, alog_row, dt_row, gdn_norm):
    b, t, _ = y.shape
    C = GDN_CHUNK * GDN_CHUNKS_PER_STEP
    assert t % C == 0
    nc = t // C
    return pl.pallas_call(
        _gdn_chunk_kernel,
        grid=(b, nc),
        in_specs=[pl.BlockSpec((1, C, QKV_WIDTH), lambda i, j: (i, j, 0)),
                  pl.BlockSpec((1, C, Z_WIDTH), lambda i, j: (i, j, 0)),
                  pl.BlockSpec((1, C, LANES), lambda i, j: (i, j, 0)),
                  _full_spec(alog_row.shape), _full_spec(dt_row.shape),
                  _full_spec(gdn_norm.shape)],
        out_specs=[pl.BlockSpec((1, C, Z_WIDTH), lambda i, j: (i, j, 0)),
                   pl.BlockSpec((1, GDN_HEADS, GDN_DK, GDN_DV), lambda i, j: (i, 0, 0, 0))],
        out_shape=[jax.ShapeDtypeStruct((b, t, Z_WIDTH), BF16),
                   jax.ShapeDtypeStruct((b, GDN_HEADS, GDN_DK, GDN_DV), F32)],
        compiler_params=_params("parallel", "arbitrary"),
        name="gdn_chunked",
    )(y, z, ba, alog_row, dt_row, gdn_norm)


def _gdn_step_kernel(qkv_ref, z_ref, ba_ref, cs_ref, s_ref, cw_ref, alog_ref, dt_ref, gn_ref,
                     ya_ref, snew_ref, csnew_ref):
    rowv = qkv_ref[0]
    cs = cs_ref[0]
    cw = cw_ref[...]
    y = rowv * cw[CONV_W - 1:CONV_W]
    for i in range(CONV_W - 1):
        y = y + cs[i:i + 1] * cw[i:i + 1]
    y = _silu(y)
    csnew_ref[0, 0:CONV_W - 2, :] = cs[1:CONV_W - 1]
    csnew_ref[0, CONV_W - 2:CONV_W - 1, :] = rowv

    beta_full, g_full = _gate_rows(ba_ref[0], alog_ref[...], dt_ref[...])
    r128 = lax.broadcasted_iota(jnp.int32, (GDN_DK, GDN_DK), 0)
    c128 = lax.broadcasted_iota(jnp.int32, (GDN_DK, GDN_DK), 1)
    eye = r128 == c128
    z = z_ref[0]
    gn = gn_ref[...]

    def to_col(r):
        return jnp.sum(jnp.where(eye, jnp.broadcast_to(r, (GDN_DK, GDN_DK)), 0.0),
                       axis=1, keepdims=True)

    heads = range(GDN_HEADS)
    q, k, v = [], [], []
    for h in heads:
        qh = y[:, h * GDN_DK:(h + 1) * GDN_DK]
        kh = y[:, GDN_HEADS * GDN_DK + h * GDN_DK:GDN_HEADS * GDN_DK + (h + 1) * GDN_DK]
        v.append(y[:, 2 * GDN_HEADS * GDN_DK + h * GDN_DV:2 * GDN_HEADS * GDN_DK + (h + 1) * GDN_DV])
        q.append(qh * lax.rsqrt(jnp.sum(qh * qh, axis=-1, keepdims=True) + EPS) * (GDN_DK ** -0.5))
        k.append(kh * lax.rsqrt(jnp.sum(kh * kh, axis=-1, keepdims=True) + EPS))
    k_col = [to_col(k[h]) for h in heads]
    q_col = [to_col(q[h]) for h in heads]
    s = [jnp.exp(g_full[:, GDN_HEADS + h:GDN_HEADS + h + 1]) * s_ref[0, h] for h in heads]
    u = [beta_full[:, h:h + 1] * (v[h] - jnp.sum(s[h] * k_col[h], axis=0, keepdims=True))
         for h in heads]
    s = [s[h] + k_col[h] * u[h] for h in heads]
    o = [jnp.sum(s[h] * q_col[h], axis=0, keepdims=True) for h in heads]
    for h in heads:
        snew_ref[0, h] = s[h]
        zh = z[:, h * GDN_DV:(h + 1) * GDN_DV]
        ya_ref[0, :, h * GDN_DV:(h + 1) * GDN_DV] = _rms(o[h], gn) * _silu(zh)


def _gdn_step(qkv, z, ba, conv_state, state, conv_w, alog_row, dt_row, gdn_norm):
    b = qkv.shape[0]
    qkv3 = qkv.reshape(b, 1, QKV_WIDTH)
    z3 = z.reshape(b, 1, Z_WIDTH)
    ba3 = ba.reshape(b, 1, LANES)
    return pl.pallas_call(
        _gdn_step_kernel,
        grid=(b,),
        in_specs=[pl.BlockSpec((1, 1, QKV_WIDTH), lambda i: (i, 0, 0)),
                  pl.BlockSpec((1, 1, Z_WIDTH), lambda i: (i, 0, 0)),
                  pl.BlockSpec((1, 1, LANES), lambda i: (i, 0, 0)),
                  pl.BlockSpec((1, CONV_W - 1, QKV_WIDTH), lambda i: (i, 0, 0)),
                  pl.BlockSpec((1, GDN_HEADS, GDN_DK, GDN_DV), lambda i: (i, 0, 0, 0)),
                  _full_spec(conv_w.shape), _full_spec(alog_row.shape),
                  _full_spec(dt_row.shape), _full_spec(gdn_norm.shape)],
        out_specs=[pl.BlockSpec((1, 1, Z_WIDTH), lambda i: (i, 0, 0)),
                   pl.BlockSpec((1, GDN_HEADS, GDN_DK, GDN_DV), lambda i: (i, 0, 0, 0)),
                   pl.BlockSpec((1, CONV_W - 1, QKV_WIDTH), lambda i: (i, 0, 0))],
        out_shape=[jax.ShapeDtypeStruct((b, 1, Z_WIDTH), F32),
                   jax.ShapeDtypeStruct((b, GDN_HEADS, GDN_DK, GDN_DV), F32),
                   jax.ShapeDtypeStruct((b, CONV_W - 1, QKV_WIDTH), F32)],
        compiler_params=_params("parallel"),
        name="gdn_step",
    )(qkv3, z3, ba3, conv_state, state, conv_w, alog_row, dt_row, gdn_norm)


def _mla_queries_latent(x, cos2, sin2, g_mix, w_down, g_qa, w_uq, g_kva, g_qn, g_qr, g_kr):
    xn = _rms(x, g_mix).astype(BF16)
    qa = _rms(_dot(xn, w_down[:, 0:Q_LORA]), g_qa).astype(BF16)
    q = _dot(qa, w_uq)
    qn, qr = [], []
    for h in range(MLA_HEADS):
        qn.append(_rms(q[:, h * QK_PAD:h * QK_PAD + NOPE_DIM], g_qn))
        qr.append(_rope(_rms(q[:, h * QK_PAD + NOPE_DIM:(h + 1) * QK_PAD], g_qr, ROPE_DIM),
                        cos2, sin2))
    c = _rms(_dot(xn, w_down[:, Q_LORA:Q_LORA + KV_LORA]), g_kva)
    kr = _rope(_rms(_dot(xn, w_down[:, Q_LORA + KV_LORA:]), g_kr, ROPE_DIM), cos2, sin2)
    return qn, qr, c, kr


def _mla_pre_kernel(x_ref, cos_ref, sin_ref, gmix_ref, wdown_ref, gqa_ref,
                    wuq_ref, gkva_ref, gqn_ref, gqr_ref, gkr_ref, wuk_ref, wuv_ref, gkn_ref,
                    q_ref, k_ref, v_ref, lat_ref):
    qn, qr, c, kr = _mla_queries_latent(
        x_ref[0], cos_ref[...], sin_ref[...], gmix_ref[...], wdown_ref[...],
        gqa_ref[...], wuq_ref[...], gkva_ref[...], gqn_ref[...] * ATTN_Q_SCALE,
        gqr_ref[...] * ATTN_Q_SCALE, gkr_ref[...])
    lat_ref[0, :, 0:KV_LORA] = c
    lat_ref[0, :, KV_LORA:LATENT_WIDTH] = kr[:, 0:ROPE_DIM]
    c_bf = c.astype(BF16)
    kr_bf = kr.astype(BF16)
    kfull = _dot(c_bf, wuk_ref[...])
    v_ref[0, 0] = _dot_nt(wuv_ref[...], c_bf).astype(BF16)
    gkn = gkn_ref[...]
    for h in range(MLA_HEADS):
        q_ref[0, :, h * QK_PAD:h * QK_PAD + NOPE_DIM] = qn[h].astype(BF16)
        q_ref[0, :, h * QK_PAD + NOPE_DIM:(h + 1) * QK_PAD] = qr[h].astype(BF16)
        kn = _rms(kfull[:, h * NOPE_DIM:(h + 1) * NOPE_DIM], gkn)
        k_ref[0, :, h * QK_PAD:h * QK_PAD + NOPE_DIM] = kn.astype(BF16)
        k_ref[0, :, h * QK_PAD + NOPE_DIM:(h + 1) * QK_PAD] = kr_bf


def _mla_pre(x, cos2, sin2, mw):
    b, t, d = x.shape
    tm = min(ATTN_TILE, t)
    consts = [mw["g_mix"], mw["w_down"], mw["g_qa"], mw["w_uq"],
              mw["g_kva"], mw["g_qn"], mw["g_qr"], mw["g_kr"], mw["w_uk"], mw["w_uv_t"],
              mw["g_kn"]]
    return pl.pallas_call(
        _mla_pre_kernel,
        grid=(b, t // tm),
        in_specs=[pl.BlockSpec((1, tm, d), lambda i, j: (i, j, 0)),
                  pl.BlockSpec((tm, LANES), lambda i, j: (j, 0)),
                  pl.BlockSpec((tm, LANES), lambda i, j: (j, 0))]
                 + [_full_spec(a.shape) for a in consts],
        out_specs=[pl.BlockSpec((1, tm, MLA_HEADS * QK_PAD), lambda i, j: (i, j, 0)),
                   pl.BlockSpec((1, tm, MLA_HEADS * QK_PAD), lambda i, j: (i, j, 0)),
                   pl.BlockSpec((1, 1, MLA_HEADS * V_DIM, tm), lambda i, j: (i, j, 0, 0)),
                   pl.BlockSpec((1, tm, LATENT_WIDTH), lambda i, j: (i, j, 0))],
        out_shape=[jax.ShapeDtypeStruct((b, t, MLA_HEADS * QK_PAD), BF16),
                   jax.ShapeDtypeStruct((b, t, MLA_HEADS * QK_PAD), BF16),
                   jax.ShapeDtypeStruct((b, t // tm, MLA_HEADS * V_DIM, tm), BF16),
                   jax.ShapeDtypeStruct((b, t, LATENT_WIDTH), F32)],
        compiler_params=_params("parallel", "parallel"),
        name="mla_pre",
    )(x, cos2, sin2, *consts)


def _mla_pre_sample_kernel(x_ref, cos_ref, sin_ref, gmix_ref, wdown_ref, gqa_ref,
                           wuq_ref, gkva_ref, gqn_ref, gqr_ref, gkr_ref, wuk_ref, gkn_ref,
                           qa_ref, qr_ref, lat_ref):
    qn, qr, c, kr = _mla_queries_latent(
        x_ref[...], cos_ref[...], sin_ref[...], gmix_ref[...], wdown_ref[...],
        gqa_ref[...], wuq_ref[...], gkva_ref[...], gqn_ref[...], gqr_ref[...],
        gkr_ref[...])
    lat_ref[:, 0:KV_LORA] = c
    lat_ref[:, KV_LORA:LATENT_WIDTH] = kr[:, 0:ROPE_DIM]
    gkn = gkn_ref[...]
    for h in range(MLA_HEADS):
        hi, lo = _split2(qn[h] * gkn)
        w_h = wuk_ref[:, h * NOPE_DIM:(h + 1) * NOPE_DIM]
        qa_ref[h] = _dot_nt(hi, w_h) + _dot_nt(lo, w_h)
        qr_ref[h] = qr[h]


def _mla_pre_sample(x2, cos2, sin2, mw):
    m = x2.shape[0]
    consts = [mw["g_mix"], mw["w_down"], mw["g_qa"], mw["w_uq"],
              mw["g_kva"], mw["g_qn"], mw["g_qr"], mw["g_kr"], mw["w_uk"], mw["g_kn"]]
    args = [x2, cos2, sin2] + consts
    return pl.pallas_call(
        _mla_pre_sample_kernel,
        grid=(1,),
        in_specs=[_full_spec(a.shape) for a in args],
        out_specs=[_full_spec((MLA_HEADS, m, KV_LORA)), _full_spec((MLA_HEADS, m, LANES)),
                   _full_spec((m, LATENT_WIDTH))],
        out_shape=[jax.ShapeDtypeStruct((MLA_HEADS, m, KV_LORA), F32),
                   jax.ShapeDtypeStruct((MLA_HEADS, m, LANES), F32),
                   jax.ShapeDtypeStruct((m, LATENT_WIDTH), F32)],
        compiler_params=_params("arbitrary"),
        name="mla_pre_sample",
    )(*args)


ATTN_HEADS_PER_STEP = 8


def _attn_kernel(q_ref, k_ref, vt_ref, o_ref, *, tq):
    qi = pl.program_id(2)
    hs = range(ATTN_HEADS_PER_STEP)
    q_t = [q_ref[0, :, g * QK_PAD:(g + 1) * QK_PAD].astype(F32).T.astype(BF16) for g in hs]
    key = lax.broadcasted_iota(jnp.int32, (tq, tq), 0)
    qry = lax.broadcasted_iota(jnp.int32, (tq, tq), 1)

    def raw_scores(j):
        start = pl.multiple_of(j * tq, tq)
        return tuple(_dot(k_ref[0, pl.ds(start, tq), g * QK_PAD:(g + 1) * QK_PAD], q_t[g])
                     for g in hs)

    def consume(j, s_raw, stats, diagonal):
        vt = [vt_ref[0, j, g * V_DIM:(g + 1) * V_DIM, :] for g in hs]
        s = list(s_raw)
        if diagonal:
            s = [jnp.where(key <= qry, s[g], -jnp.inf) for g in hs]
        m_new = [jnp.maximum(stats[g][0], jnp.max(s[g], axis=0, keepdims=True)) for g in hs]
        p = [jnp.exp2(s[g] - m_new[g]) for g in hs]
        pv = [_dot(vt[g], p[g].astype(BF16)) for g in hs]
        out = []
        for g in hs:
            m, l, acc = stats[g]
            alpha = jnp.exp2(m - m_new[g])
            out.append((m_new[g], alpha * l + jnp.sum(p[g], axis=0, keepdims=True),
                        alpha * acc + pv[g]))
        return tuple(out)

    init = tuple((jnp.full((1, tq), -jnp.inf, F32), jnp.zeros((1, tq), F32),
                  jnp.zeros((V_DIM, tq), F32)) for _ in hs)
    stats = lax.fori_loop(0, qi, lambda j, st: consume(j, raw_scores(j), st, False), init)
    stats = consume(qi, raw_scores(qi), stats, True)
    for g in hs:
        _, l, acc = stats[g]
        o_ref[0, :, g * V_DIM:(g + 1) * V_DIM] = (acc / l).T.astype(BF16)


def _mla_prompt_attention(q_cat, k_cat, v_t):
    b, t, _ = q_cat.shape
    tq = min(ATTN_TILE, t)
    hps = ATTN_HEADS_PER_STEP
    return pl.pallas_call(
        functools.partial(_attn_kernel, tq=tq),
        grid=(b, MLA_HEADS // hps, t // tq),
        in_specs=[pl.BlockSpec((1, tq, hps * QK_PAD), lambda i, h, j: (i, j, h)),
                  pl.BlockSpec((1, t, hps * QK_PAD), lambda i, h, j: (i, 0, h)),
                  pl.BlockSpec((1, t // tq, hps * V_DIM, tq), lambda i, h, j: (i, 0, h, 0))],
        out_specs=pl.BlockSpec((1, tq, hps * V_DIM), lambda i, h, j: (i, j, h)),
        out_shape=jax.ShapeDtypeStruct((b, t, MLA_HEADS * V_DIM), BF16),
        compiler_params=_params("parallel", "parallel", "arbitrary"),
        name="mla_prompt_attention",
    )(q_cat, k_cat, v_t)


def _dec_attn_kernel(pt_ref, *refs, pps):
    del pt_ref
    page_refs = refs[:pps]
    (wukt_ref, qa_ref, qr_ref, lnew_ref, ctx_ref,
     lhs_ref, qr16_ref, m_ref, l_ref, acc_ref) = refs[pps:]
    b = pl.program_id(0)
    j = pl.program_id(1)
    nj = pl.num_programs(1)
    n_k = MLA_HEADS * NOPE_DIM

    @pl.when((b == 0) & (j == 0))
    def _():
        lhs_ref[0:n_k, :] = wukt_ref[...]

    @pl.when(j == 0)
    def _():
        qa = jnp.concatenate([qa_ref[h, 0] for h in range(MLA_HEADS)]
                             + [jnp.zeros((8, KV_LORA), F32)], axis=0)
        lhs_ref[n_k:n_k + 16, :] = qa.astype(BF16)
        qr = jnp.concatenate([qr_ref[h, 0] for h in range(MLA_HEADS)]
                             + [jnp.zeros((8, LANES), F32)], axis=0)
        qr16_ref[...] = qr.astype(BF16)
        m_ref[...] = jnp.full(m_ref.shape, -jnp.inf, F32)
        l_ref[...] = jnp.zeros(l_ref.shape, F32)
        acc_ref[...] = jnp.zeros(acc_ref.shape, F32)

    def scores(res, sr):
        kt = res[0:n_k]
        ss = jnp.sum((kt * kt).reshape(MLA_HEADS, NOPE_DIM, res.shape[1]), axis=1)
        sn = res[n_k:n_k + MLA_HEADS]
        return (sn * lax.rsqrt(ss * (1.0 / NOPE_DIM) + EPS) + sr[0:MLA_HEADS]) * MLA_SCALE

    c_tok, s_parts = [], []
    for i in range(0, pps, 2):
        tile = jnp.concatenate([page_refs[i][...], page_refs[i + 1][...]], axis=1)
        c_bf = tile[0:KV_LORA].astype(BF16)
        kr_bf = tile[KV_LORA:LATENT_WIDTH].astype(BF16)
        s_parts.append(scores(_dot(lhs_ref[...], c_bf),
                              _dot(qr16_ref[:, 0:ROPE_DIM], kr_bf)))
        c_tok.append(tile[0:KV_LORA].T.astype(BF16))
    s = jnp.concatenate(s_parts, axis=1)
    m_old = m_ref[...]
    m_new = jnp.maximum(m_old, jnp.max(s, axis=-1, keepdims=True))
    alpha = jnp.exp(m_old - m_new)
    p = jnp.exp(s - m_new)
    l_ref[...] = alpha * l_ref[...] + jnp.sum(p, axis=-1, keepdims=True)
    p_bf = p.astype(BF16)
    n_tok = c_tok[0].shape[0]
    acc = alpha * acc_ref[...]
    for t, c_t in enumerate(c_tok):
        acc = acc + _dot(p_bf[:, t * n_tok:(t + 1) * n_tok], c_t)
    acc_ref[...] = acc
    m_ref[...] = m_new

    @pl.when(j == nj - 1)
    def _():
        ln = lnew_ref[0]
        c_new = ln[:, 0:KV_LORA]
        c8 = jnp.broadcast_to(c_new, (8, KV_LORA)).astype(BF16)
        kr8 = jnp.broadcast_to(ln[:, KV_LORA:LATENT_WIDTH], (8, ROPE_DIM)).astype(BF16)
        s_new = scores(_dot_nt(lhs_ref[...], c8),
                       _dot_nt(qr16_ref[:, 0:ROPE_DIM], kr8))[:, 0:1]
        m_old = m_ref[...]
        m_new = jnp.maximum(m_old, s_new)
        alpha = jnp.exp(m_old - m_new)
        p_new = jnp.exp(s_new - m_new)
        ctx = (alpha * acc_ref[...] + p_new * c_new) / (alpha * l_ref[...] + p_new)
        for h in range(MLA_HEADS):
            ctx_ref[h, 0] = ctx[h:h + 1]


def _dec_pages_per_step(n_pages):
    pps = DEC_PAGES_PER_STEP
    while n_pages % pps:
        pps //= 2
    assert pps >= 2
    return pps


def _dec_scratch():
    return [pltpu.VMEM((MLA_HEADS * NOPE_DIM + 16, KV_LORA), BF16),
            pltpu.VMEM((16, LANES), BF16),
            pltpu.VMEM((MLA_HEADS, 1), F32),
            pltpu.VMEM((MLA_HEADS, 1), F32),
            pltpu.VMEM((MLA_HEADS, KV_LORA), F32)]


def _mla_decode_attention(cache, layer, page_table, wuk_t, qa, qr, lat_new):
    bsz, n_pages = page_table.shape
    page = cache.shape[3]
    pps = _dec_pages_per_step(n_pages)
    qa4 = qa.reshape(MLA_HEADS, bsz, 1, KV_LORA)
    qr4 = qr.reshape(MLA_HEADS, bsz, 1, LANES)
    ln3 = lat_new.reshape(bsz, 1, LATENT_WIDTH)

    def page_spec(i):
        return pl.BlockSpec((pl.Squeezed(), pl.Squeezed(), LATENT_WIDTH, page),
                            lambda b, j, pt: (layer, pt[b, j * pps + i], 0, 0))

    grid_spec = pltpu.PrefetchScalarGridSpec(
        num_scalar_prefetch=1,
        grid=(bsz, n_pages // pps),
        in_specs=[page_spec(i) for i in range(pps)] + [
            pl.BlockSpec(wuk_t.shape, lambda b, j, pt: (0, 0)),
            pl.BlockSpec((MLA_HEADS, 1, 1, KV_LORA), lambda b, j, pt: (0, b, 0, 0)),
            pl.BlockSpec((MLA_HEADS, 1, 1, LANES), lambda b, j, pt: (0, b, 0, 0)),
            pl.BlockSpec((1, 1, LATENT_WIDTH), lambda b, j, pt: (b, 0, 0))],
        out_specs=pl.BlockSpec((MLA_HEADS, 1, 1, KV_LORA), lambda b, j, pt: (0, b, 0, 0)),
        scratch_shapes=_dec_scratch())
    ctx = pl.pallas_call(
        functools.partial(_dec_attn_kernel, pps=pps),
        grid_spec=grid_spec,
        out_shape=jax.ShapeDtypeStruct((MLA_HEADS, bsz, 1, KV_LORA), F32),
        compiler_params=_params("arbitrary", "arbitrary"),
        name="mla_decode_attention",
    )(page_table, *([cache] * pps), wuk_t, qa4, qr4, ln3)
    return ctx.reshape(MLA_HEADS, bsz, KV_LORA)


def _dec_post_kernel(ctx_ref, wuv_ref, y_ref):
    for h in range(MLA_HEADS):
        hi, lo = _split2(ctx_ref[h])
        w_h = wuv_ref[:, h * V_DIM:(h + 1) * V_DIM]
        y_ref[:, h * V_DIM:(h + 1) * V_DIM] = (_dot(hi, w_h) + _dot(lo, w_h)).astype(BF16)


def _dec_post(ctx, w_uv):
    bsz = ctx.shape[1]
    return pl.pallas_call(
        _dec_post_kernel,
        grid=(1,),
        in_specs=[_full_spec(ctx.shape), _full_spec(w_uv.shape)],
        out_specs=_full_spec((bsz, MLA_HEADS * V_DIM)),
        out_shape=jax.ShapeDtypeStruct((bsz, MLA_HEADS * V_DIM), BF16),
        compiler_params=_params("arbitrary"),
        name="mla_decode_values",
    )(ctx, w_uv)


def _merge_mlp_kernel(x_ref, ya_ref, yb_ref, gmix_ref, wg_ref, wya_ref, wyb_ref, wo_ref,
                      gmlp_ref, wup_ref, wdn_ref, y_ref):
    tm, d = x_ref.shape
    d_ff = wup_ref.shape[1]
    n_grp = max(1, tm // MERGE_ROWS_PER_GROUP)
    rows = tm // n_grp
    grp = range(n_grp)
    sl = [slice(r * rows, (r + 1) * rows) for r in grp]
    x = [x_ref[sl[r], :] for r in grp]
    xn = [_rms(x[r], gmix_ref[...]).astype(BF16) for r in grp]
    gates = [_sigmoid(_dot(xn[r], wg_ref[...])) for r in grp]
    pa = [_dot(ya_ref[sl[r], :].astype(BF16), wya_ref[...]) for r in grp]
    pb = [_dot(yb_ref[sl[r], :].astype(BF16), wyb_ref[...]) for r in grp]
    mix = [(gates[r][:, :d] * pa[r] + gates[r][:, d:] * pb[r]).astype(BF16) for r in grp]
    x1 = [x[r] + _dot(mix[r], wo_ref[...]) for r in grp]
    hin = [_rms(x1[r], gmlp_ref[...]).astype(BF16) for r in grp]
    y = x1
    for f in range(0, d_ff, MLP_FF_BLOCK):
        hmid = [jnp.maximum(_dot(hin[r], wup_ref[:, f:f + MLP_FF_BLOCK]), 0.0) for r in grp]
        y = [y[r] + _dot((hmid[r] * hmid[r]).astype(BF16), wdn_ref[f:f + MLP_FF_BLOCK, :])
             for r in grp]
    for r in grp:
        y_ref[sl[r], :] = y[r]


MERGE_ROWS_PER_GROUP = 256
MLP_FF_BLOCK = 1024


def _merge_mlp(x2, ya, yb, ew):
    m, d = x2.shape
    tm = min(512, m)
    consts = [ew["g_mix"], ew["w_gates"], ew["w_ya"], ew["w_yb"], ew["w_o"], ew["g_mlp"],
              ew["w_up"], ew["w_down"]]

    def const_spec(a):
        nd = a.ndim
        return pl.BlockSpec(a.shape, lambda i: (0,) * nd, pipeline_mode=pl.Buffered(1))

    return pl.pallas_call(
        _merge_mlp_kernel,
        grid=(m // tm,),
        in_specs=[pl.BlockSpec((tm, d), lambda i: (i, 0)),
                  pl.BlockSpec((tm, ya.shape[1]), lambda i: (i, 0)),
                  pl.BlockSpec((tm, yb.shape[1]), lambda i: (i, 0))]
                 + [const_spec(a) for a in consts],
        out_specs=pl.BlockSpec((tm, d), lambda i: (i, 0)),
        out_shape=jax.ShapeDtypeStruct((m, d), F32),
        compiler_params=_params("parallel"),
        name="merge_mlp",
    )(x2, ya, yb, *consts)


def _rope_tables(pos):
    inv = jnp.power(ROPE_THETA, -jnp.arange(0, ROPE_DIM, 2, dtype=F32) / ROPE_DIM)
    ang = pos.astype(F32)[:, None] * inv[None, :]
    cos, sin = jnp.cos(ang), jnp.sin(ang)
    zeros = jnp.zeros((pos.shape[0], LANES - ROPE_DIM), F32)
    return (jnp.concatenate([cos, cos, zeros], axis=1),
            jnp.concatenate([-sin, sin, zeros], axis=1))


def _pad_lanes(a, width=LANES, offset=0):
    return jnp.zeros((1, width), F32).at[0, offset:offset + a.shape[0]].set(a.astype(F32))


def _layer_weights(l, norm_mix, w_in, conv_w, a_log, dt_bias, gdn_norm, q_a_norm, w_uq,
                   kv_a_norm, w_uk, w_uv, q_norm_nope, q_norm_rope, k_norm_nope, k_norm_rope,
                   w_ya, w_yb, w_o, norm_mlp, w_up, w_down):
    d = w_in.shape[1]
    off_z = QKV_WIDTH
    off_b = off_z + Z_WIDTH
    off_qd = off_b + 2 * GDN_HEADS
    off_kv = off_qd + Q_LORA
    off_g = off_kv + LATENT_WIDTH
    wi = w_in[l]
    g_mix = norm_mix[l].reshape(1, d)
    gw = {
        "g_mix": g_mix,
        "w_in": wi.astype(BF16),
        "conv_w": conv_w[l],
        "alog_row": _pad_lanes(a_log[l], offset=GDN_HEADS),
        "dt_row": _pad_lanes(dt_bias[l], offset=GDN_HEADS),
        "gdn_norm": gdn_norm[l].reshape(1, GDN_DV),
    }
    wq = w_uq[l].reshape(Q_LORA, MLA_HEADS, NOPE_DIM + ROPE_DIM)
    wq = jnp.pad(wq, ((0, 0), (0, 0), (0, QK_PAD - NOPE_DIM - ROPE_DIM)))
    mw = {
        "g_mix": g_mix,
        "w_down": jnp.pad(wi[:, off_qd:off_g], ((0, 0), (0, LANES - ROPE_DIM))).astype(BF16),
        "g_qa": q_a_norm[l].reshape(1, Q_LORA),
        "w_uq": wq.reshape(Q_LORA, MLA_HEADS * QK_PAD).astype(BF16),
        "g_kva": kv_a_norm[l].reshape(1, KV_LORA),
        "g_qn": q_norm_nope[l].reshape(1, NOPE_DIM),
        "g_qr": _pad_lanes(q_norm_rope[l]),
        "g_kr": _pad_lanes(k_norm_rope[l]),
        "g_kn": k_norm_nope[l].reshape(1, NOPE_DIM),
        "w_uk": w_uk[l].astype(BF16),
        "w_uk_t": w_uk[l].T.astype(BF16),
        "w_uv": w_uv[l].astype(BF16),
        "w_uv_t": w_uv[l].T.astype(BF16),
    }
    ew = {
        "g_mix": g_mix,
        "w_gates": wi[:, off_g:].astype(BF16),
        "w_ya": w_ya[l].astype(BF16),
        "w_yb": w_yb[l].astype(BF16),
        "w_o": w_o[l].astype(BF16),
        "g_mlp": norm_mlp[l].reshape(1, d),
        "w_up": w_up[l].astype(BF16),
        "w_down": w_down[l].astype(BF16),
    }
    return gw, mw, ew


def kernel(x_prompt, x_sample, cache_mla, state_gdn, state_conv, page_table, norm_mix, w_in,
           conv_w, a_log, dt_bias, gdn_norm, q_a_norm, w_uq, kv_a_norm, w_uk, w_uv,
           q_norm_nope, q_norm_rope, k_norm_nope, k_norm_rope, w_ya, w_yb, w_o, norm_mlp,
           w_up, w_down):
    depth = w_in.shape[0]
    bp, tp, d = x_prompt.shape
    bs, ts, _ = x_sample.shape
    assert ts == 1 and tp % GDN_CHUNK == 0
    past_len = page_table.shape[1] * cache_mla.shape[2]
    cache_t = jnp.swapaxes(cache_mla, 2, 3)
    cos_p, sin_p = _rope_tables(jnp.arange(tp))
    cos_s, sin_s = _rope_tables(past_len + jnp.zeros((bs,), jnp.int32))

    x_p, x_s = x_prompt, x_sample.reshape(bs, d)
    rows_p, gdn_p, conv_p, rows_s, gdn_s, conv_s = [], [], [], [], [], []
    for l in range(depth):
        gw, mw, ew = _layer_weights(
            l, norm_mix, w_in, conv_w, a_log, dt_bias, gdn_norm, q_a_norm, w_uq, kv_a_norm,
            w_uk, w_uv, q_norm_nope, q_norm_rope, k_norm_nope, k_norm_rope, w_ya, w_yb, w_o,
            norm_mlp, w_up, w_down)
        gdn_consts = (gw["conv_w"], gw["alog_row"], gw["dt_row"], gw["gdn_norm"])

        xp2 = x_p.reshape(bp * tp, d)
        yc, z, ba, cs_p = _gdn_proj_conv(x_p, gw["g_mix"], gw["w_in"], gw["conv_w"])
        ya_p, s_p = _gdn_chunked(yc.reshape(bp, tp, QKV_WIDTH), z.reshape(bp, tp, Z_WIDTH),
                                 ba.reshape(bp, tp, LANES), *gdn_consts[1:])
        q_cat, k_cat, v_p, lat_p = _mla_pre(x_p, cos_p, sin_p, mw)
        yb_p = _mla_prompt_attention(q_cat, k_cat, v_p)
        y_p = _merge_mlp(xp2, ya_p.reshape(bp * tp, Z_WIDTH), yb_p.reshape(bp * tp, -1), ew)

        qkv, z, ba = _gdn_proj(x_s, gw["g_mix"], gw["w_in"])
        ya_s, s_s, cs_s = _gdn_step(qkv, z, ba, state_conv[l], state_gdn[l], *gdn_consts)
        qa, qr, lat_s = _mla_pre_sample(x_s, cos_s, sin_s, mw)
        ctx = _mla_decode_attention(cache_t, l, page_table, mw["w_uk_t"], qa, qr, lat_s)
        yb_s = _dec_post(ctx, mw["w_uv"])
        y_s = _merge_mlp(x_s, ya_s.reshape(bs, Z_WIDTH), yb_s, ew)

        x_p, x_s = y_p.reshape(bp, tp, d), y_s
        rows_p.append(lat_p)
        gdn_p.append(s_p)
        conv_p.append(cs_p)
        rows_s.append(lat_s.reshape(bs, 1, LATENT_WIDTH))
        gdn_s.append(s_s)
        conv_s.append(cs_s)
    return (x_p, x_s.reshape(bs, 1, d), jnp.stack(rows_p), jnp.stack(gdn_p), jnp.stack(conv_p),
            jnp.stack(rows_s), jnp.stack(gdn_s), jnp.stack(conv_s))
```

```python
import functools

import jax
import jax.numpy as jnp
from jax import lax
from jax.experimental import pallas as pl
from jax.experimental.pallas import tpu as pltpu

F32 = jnp.float32
BF16 = jnp.bfloat16

EPS = 1e-6
GDN_HEADS = 8
GDN_DK = 128
GDN_DV = 128
CONV_W = 4
GDN_CHUNK = 64
MLA_HEADS = 8
Q_LORA = 512
KV_LORA = 512
NOPE_DIM = 128
ROPE_DIM = 64
V_DIM = 128
ROPE_THETA = 10000.0
MLA_SCALE = (NOPE_DIM + ROPE_DIM) ** -0.5
LOG2_E = 1.4426950408889634
ATTN_Q_SCALE = MLA_SCALE * LOG2_E
LATENT_WIDTH = KV_LORA + ROPE_DIM
QKV_WIDTH = GDN_HEADS * (2 * GDN_DK + GDN_DV)
Z_WIDTH = GDN_HEADS * GDN_DV

LANES = 128
QK_PAD = 256
VMEM_LIMIT = 56 * 1024 * 1024
DEC_PAGES_PER_STEP = 64
ATTN_TILE = 256
MLA_PRE_TILE = 512


def _dot(a, b):
    return jnp.dot(a, b, preferred_element_type=F32)


def _dot_nt(a, b):
    return lax.dot_general(a, b, (((1,), (1,)), ((), ())), preferred_element_type=F32)


def _dot_tn(a, b):
    return lax.dot_general(a, b, (((0,), (0,)), ((), ())), preferred_element_type=F32)


def _split2(x):
    hi = x.astype(BF16)
    lo = (x - hi.astype(F32)).astype(BF16)
    return hi, lo


def _split3(x):
    hi = x.astype(BF16)
    r = x - hi.astype(F32)
    mid = r.astype(BF16)
    lo = (r - mid.astype(F32)).astype(BF16)
    return hi, mid, lo


def _rms(x, g, n=None):
    n = x.shape[-1] if n is None else n
    ss = jnp.sum(x * x, axis=-1, keepdims=True) * (1.0 / n)
    return x * lax.rsqrt(ss + EPS) * g


def _sigmoid(x):
    return 1.0 / (1.0 + jnp.exp(-x))


def _silu(x):
    h = 0.5 * x
    return h + h * jnp.tanh(h)


def _softplus(x):
    return jnp.maximum(x, 0.0) + jnp.log1p(jnp.exp(-jnp.abs(x)))


def _rope(x, cos2, sin2):
    lane = lax.broadcasted_iota(jnp.int32, x.shape, 1)
    fwd = pltpu.roll(x, LANES - ROPE_DIM // 2, axis=1)
    bwd = pltpu.roll(x, ROPE_DIM // 2, axis=1)
    swapped = jnp.where(lane < ROPE_DIM // 2, fwd, bwd)
    return x * cos2 + swapped * sin2


def _full_spec(shape):
    nd = len(shape)
    return pl.BlockSpec(shape, lambda *_: (0,) * nd)


def _params(*sem):
    return pltpu.CompilerParams(dimension_semantics=sem, vmem_limit_bytes=VMEM_LIMIT)


OFF_Z = QKV_WIDTH
OFF_BA = QKV_WIDTH + Z_WIDTH
GDN_IN_COLS = OFF_BA + LANES


def _gdn_w_spec(w_in_bf):
    return pl.BlockSpec((w_in_bf.shape[0], GDN_IN_COLS), lambda i: (0, 0))


def _gdn_proj_kernel(x_ref, g_ref, w_ref, qkv_ref, z_ref, ba_ref):
    xn = _rms(x_ref[...], g_ref[...]).astype(BF16)
    qkv_ref[...] = _dot(xn, w_ref[:, 0:OFF_Z])
    z_ref[...] = _dot(xn, w_ref[:, OFF_Z:OFF_BA])
    ba_ref[...] = _dot(xn, w_ref[:, OFF_BA:GDN_IN_COLS])


def _gdn_proj(x2, g, w_in_bf):
    m, d = x2.shape
    tm = min(256, m)
    return pl.pallas_call(
        _gdn_proj_kernel,
        grid=(m // tm,),
        in_specs=[pl.BlockSpec((tm, d), lambda i: (i, 0)),
                  _full_spec(g.shape), _gdn_w_spec(w_in_bf)],
        out_specs=[pl.BlockSpec((tm, QKV_WIDTH), lambda i: (i, 0)),
                   pl.BlockSpec((tm, Z_WIDTH), lambda i: (i, 0)),
                   pl.BlockSpec((tm, LANES), lambda i: (i, 0))],
        out_shape=[jax.ShapeDtypeStruct((m, QKV_WIDTH), F32),
                   jax.ShapeDtypeStruct((m, Z_WIDTH), F32),
                   jax.ShapeDtypeStruct((m, LANES), F32)],
        compiler_params=_params("parallel"),
        name="gdn_proj",
    )(x2, g, w_in_bf)


PROJ_CONV_COLS = 512
assert CONV_W == 4


def _gdn_proj_conv_kernel(x_ref, g_ref, w_ref, cw_ref,
                          y_ref, z_ref, ba_ref, cs_ref, xp_ref, *, tiles_per_seq):
    i = pl.program_id(0)
    tm = x_ref.shape[0]

    @pl.when(i % tiles_per_seq == 0)
    def _():
        xp_ref[...] = jnp.zeros((8, QKV_WIDTH), F32)

    xn = _rms(x_ref[...], g_ref[...]).astype(BF16)
    row8 = lax.broadcasted_iota(jnp.int32, (8, PROJ_CONV_COLS), 0)
    for c0 in range(0, QKV_WIDTH, PROJ_CONV_COLS):
        cb = slice(c0, c0 + PROJ_CONV_COLS)
        raw = _dot(xn, w_ref[:, cb])
        halo = xp_ref[:, cb]
        cw = cw_ref[:, cb]
        back2 = pltpu.roll(raw, 2, axis=0)
        x2 = jnp.concatenate([jnp.where(row8 < 2, pltpu.roll(halo, 2, axis=0), back2[0:8]),
                              back2[8:]], axis=0)
        u = raw * cw[2:3] + x2 * cw[0:1]
        u_prev = halo[7:8] * cw[2:3] + halo[5:6] * cw[0:1]
        back1 = pltpu.roll(u, 1, axis=0)
        su = jnp.concatenate([jnp.where(row8 < 1, u_prev, back1[0:8]), back1[8:]], axis=0)
        y_ref[:, cb] = _silu(raw * cw[3:4] + x2 * cw[1:2] + su)
        xp_ref[:, cb] = raw[tm - 8:tm]
    z_ref[...] = _dot(xn, w_ref[:, OFF_Z:OFF_BA])
    ba_ref[...] = _dot(xn, w_ref[:, OFF_BA:GDN_IN_COLS])

    @pl.when(i % tiles_per_seq == tiles_per_seq - 1)
    def _():
        cs_ref[0] = xp_ref[8 - (CONV_W - 1):8, :]


def _gdn_proj_conv(x, g, w_in_bf, conv_w):
    b, t, d = x.shape
    tm = min(512, t)
    assert t % tm == 0
    tiles_per_seq = t // tm
    m = b * t
    return pl.pallas_call(
        functools.partial(_gdn_proj_conv_kernel, tiles_per_seq=tiles_per_seq),
        grid=(m // tm,),
        in_specs=[pl.BlockSpec((tm, d), lambda i: (i, 0)),
                  _full_spec(g.shape), _gdn_w_spec(w_in_bf), _full_spec(conv_w.shape)],
        out_specs=[pl.BlockSpec((tm, QKV_WIDTH), lambda i: (i, 0)),
                   pl.BlockSpec((tm, Z_WIDTH), lambda i: (i, 0)),
                   pl.BlockSpec((tm, LANES), lambda i: (i, 0)),
                   pl.BlockSpec((1, CONV_W - 1, QKV_WIDTH), lambda i: (i // tiles_per_seq, 0, 0))],
        out_shape=[jax.ShapeDtypeStruct((m, QKV_WIDTH), F32),
                   jax.ShapeDtypeStruct((m, Z_WIDTH), F32),
                   jax.ShapeDtypeStruct((m, LANES), F32),
                   jax.ShapeDtypeStruct((b, CONV_W - 1, QKV_WIDTH), F32)],
        scratch_shapes=[pltpu.VMEM((8, QKV_WIDTH), F32)],
        compiler_params=_params("arbitrary"),
        name="gdn_proj_conv",
    )(x.reshape(m, d), g, w_in_bf, conv_w)


def _gate_rows(ba, alog_row, dt_row):
    lane = lax.broadcasted_iota(jnp.int32, ba.shape, 1)
    is_a = (lane >= GDN_HEADS) & (lane < 2 * GDN_HEADS)
    g = jnp.where(is_a, -jnp.exp(alog_row) * _softplus(ba + dt_row), 0.0)
    return _sigmoid(ba), g


def _gdn_chunk_kernel(y_ref, z_ref, ba_ref, alog_ref, dt_ref, gn_ref, ya_ref, s_ref):
    c = pl.program_id(1)
    C = GDN_CHUNK
    TB = y_ref.shape[1]
    n_sub = TB // C

    @pl.when(c == 0)
    def _():
        s_ref[...] = jnp.zeros(s_ref.shape, F32)

    y = y_ref[0]

    beta_full, g_full = _gate_rows(ba_ref[0], alog_ref[...], dt_ref[...])
    rtb = lax.broadcasted_iota(jnp.int32, (TB, TB), 0)
    ctb = lax.broadcasted_iota(jnp.int32, (TB, TB), 1)
    tril = jnp.where((rtb >= ctb) & (rtb // C == ctb // C), 1.0, 0.0).astype(BF16)
    g_col = sum(_dot(tril, part) for part in _split3(g_full))
    r128 = lax.broadcasted_iota(jnp.int32, (LANES, LANES), 0)
    c128 = lax.broadcasted_iota(jnp.int32, (LANES, LANES), 1)
    eye = jnp.where(r128 == c128, 1.0, 0.0).astype(BF16)
    g_rows = sum(_dot_nt(eye, part) for part in _split3(g_col))

    z = z_ref[0]
    gn = gn_ref[...]
    heads = range(GDN_HEADS)
    SB = min(TB, C * GDN_CHUNKS_PER_BLOCK)
    n_blk = TB // SB
    per_blk = SB // C
    rsb = lax.broadcasted_iota(jnp.int32, (SB, SB), 0)
    csb = lax.broadcasted_iota(jnp.int32, (SB, SB), 1)
    same_chunk = (rsb // C) == (csb // C)
    incl = same_chunk & (rsb >= csb)
    strict = same_chunk & (rsb > csb)
    chunk_of_row = lax.broadcasted_iota(jnp.int32, (SB, 1), 0) // C
    units = [(b, h) for b in range(n_blk) for h in heads]
    q, k, gc, eg, g_last, rhs_bf, rhs, decay, kq, kd = ({} for _ in range(10))
    for u in units:
        b, h = u
        r = slice(b * SB, (b + 1) * SB)
        qh = y[r, h * GDN_DK:(h + 1) * GDN_DK]
        kh = y[r, GDN_HEADS * GDN_DK + h * GDN_DK:GDN_HEADS * GDN_DK + (h + 1) * GDN_DK]
        vh = y[r, 2 * GDN_HEADS * GDN_DK + h * GDN_DV:2 * GDN_HEADS * GDN_DK + (h + 1) * GDN_DV]
        q[u] = qh * lax.rsqrt(jnp.sum(qh * qh, axis=-1, keepdims=True) + EPS) * (GDN_DK ** -0.5)
        k[u] = kh * lax.rsqrt(jnp.sum(kh * kh, axis=-1, keepdims=True) + EPS)
        beta = beta_full[r, h:h + 1]
        gc[u] = g_col[r, GDN_HEADS + h:GDN_HEADS + h + 1]
        gr = g_rows[GDN_HEADS + h:GDN_HEADS + h + 1, r]
        decay[u] = jnp.where(incl, jnp.exp(jnp.where(incl, gc[u] - gr, 0.0)), 0.0)
        eg[u] = jnp.exp(gc[u])
        g_last[u] = [gc[u][(n + 1) * C - 1:(n + 1) * C, :] for n in range(per_blk)]
        g_end = g_last[u][0]
        for n in range(1, per_blk):
            g_end = jnp.where(chunk_of_row >= n, g_last[u][n], g_end)
        kd[u] = (k[u] * jnp.exp(g_end - gc[u])).astype(BF16)
        kb = k[u] * beta
        rhs[u] = jnp.concatenate([vh * beta, kb * eg[u]], axis=1)
        rhs_bf[u] = rhs[u].astype(BF16)
        kq[u] = _dot_nt(jnp.concatenate([kb, q[u]], axis=0).astype(BF16),
                        k[u].astype(BF16))
    pw = {u: jnp.where(strict, kq[u][:SB] * decay[u], 0.0) for u in units}
    aqk = {u: (kq[u][SB:] * decay[u]).astype(BF16) for u in units}
    n_acc = {u: -pw[u] for u in units}
    pw_bf = {u: pw[u].astype(BF16) for u in units}
    for _ in range(C.bit_length() - 2):
        pw = {u: _dot(pw_bf[u], pw_bf[u]) for u in units}
        pw_bf = {u: pw[u].astype(BF16) for u in units}
        n_acc = {u: n_acc[u] + pw[u] + _dot(n_acc[u].astype(BF16), pw_bf[u]) for u in units}
    sol = {u: rhs[u] + _dot(n_acc[u].astype(BF16), rhs_bf[u]) for u in units}
    qg = {u: q[u] * eg[u] for u in units}
    s_cur = [s_ref[0, h] for h in heads]
    zero_rows = jnp.zeros((C, GDN_DV), BF16)
    for n in range(n_sub):
        b, ln = divmod(n, per_blk)
        rows = slice(ln * C, (ln + 1) * C)
        tok = slice(n * C, (n + 1) * C)
        ws = [_dot(jnp.concatenate([sol[(b, h)][rows, GDN_DV:], qg[(b, h)][rows]],
                                   axis=0).astype(BF16),
                   s_cur[h].astype(BF16)) for h in heads]
        u_bf = [(sol[(b, h)][rows, :GDN_DV] - ws[h][:C]).astype(BF16) for h in heads]
        u_pad = [jnp.concatenate([zero_rows] * ln + [u_bf[h]] + [zero_rows] * (per_blk - 1 - ln),
                                 axis=0) for h in heads]
        o = [ws[h][C:] + _dot(aqk[(b, h)][rows], u_pad[h]) for h in heads]
        s_cur = [jnp.exp(g_last[(b, h)][ln]) * s_cur[h] + _dot_tn(kd[(b, h)][rows], u_bf[h])
                 for h in heads]
        for h in heads:
            zh = z[tok, h * GDN_DV:(h + 1) * GDN_DV]
            ya_ref[0, tok, h * GDN_DV:(h + 1) * GDN_DV] = (
                _rms(o[h], gn) * _silu(zh)).astype(BF16)
    for h in heads:
        s_ref[0, h] = s_cur[h]


GDN_CHUNKS_PER_STEP = 4
GDN_CHUNKS_PER_BLOCK = 2


def _gdn_chunked(y, z, ba, alog_row, dt_row, gdn_norm):
    b, t, _ = y.shape
    C = GDN_CHUNK * GDN_CHUNKS_PER_STEP
    assert t % C == 0
    nc = t // C
    return pl.pallas_call(
        _gdn_chunk_kernel,
        grid=(b, nc),
        in_specs=[pl.BlockSpec((1, C, QKV_WIDTH), lambda i, j: (i, j, 0)),
                  pl.BlockSpec((1, C, Z_WIDTH), lambda i, j: (i, j, 0)),
                  pl.BlockSpec((1, C, LANES), lambda i, j: (i, j, 0)),
                  _full_spec(alog_row.shape), _full_spec(dt_row.shape),
                  _full_spec(gdn_norm.shape)],
        out_specs=[pl.BlockSpec((1, C, Z_WIDTH), lambda i, j: (i, j, 0)),
                   pl.BlockSpec((1, GDN_HEADS, GDN_DK, GDN_DV), lambda i, j: (i, 0, 0, 0))],
        out_shape=[jax.ShapeDtypeStruct((b, t, Z_WIDTH), BF16),
                   jax.ShapeDtypeStruct((b, GDN_HEADS, GDN_DK, GDN_DV), F32)],
        compiler_params=_params("parallel", "arbitrary"),
        name="gdn_chunked",
    )(y, z, ba, alog_row, dt_row, gdn_norm)


def _gdn_step_kernel(qkv_ref, z_ref, ba_ref, cs_ref, s_ref, cw_ref, alog_ref, dt_ref, gn_ref,
                     ya_ref, snew_ref, csnew_ref):
    rowv = qkv_ref[0]
    cs = cs_ref[0]
    cw = cw_ref[...]
    y = rowv * cw[CONV_W - 1:CONV_W]
    for i in range(CONV_W - 1):
        y = y + cs[i:i + 1] * cw[i:i + 1]
    y = _silu(y)
    csnew_ref[0, 0:CONV_W - 2, :] = cs[1:CONV_W - 1]
    csnew_ref[0, CONV_W - 2:CONV_W - 1, :] = rowv

    beta_full, g_full = _gate_rows(ba_ref[0], alog_ref[...], dt_ref[...])
    r128 = lax.broadcasted_iota(jnp.int32, (GDN_DK, GDN_DK), 0)
    c128 = lax.broadcasted_iota(jnp.int32, (GDN_DK, GDN_DK), 1)
    eye = r128 == c128
    z = z_ref[0]
    gn = gn_ref[...]

    def to_col(r):
        return jnp.sum(jnp.where(eye, jnp.broadcast_to(r, (GDN_DK, GDN_DK)), 0.0),
                       axis=1, keepdims=True)

    heads = range(GDN_HEADS)
    q, k, v = [], [], []
    for h in heads:
        qh = y[:, h * GDN_DK:(h + 1) * GDN_DK]
        kh = y[:, GDN_HEADS * GDN_DK + h * GDN_DK:GDN_HEADS * GDN_DK + (h + 1) * GDN_DK]
        v.append(y[:, 2 * GDN_HEADS * GDN_DK + h * GDN_DV:2 * GDN_HEADS * GDN_DK + (h + 1) * GDN_DV])
        q.append(qh * lax.rsqrt(jnp.sum(qh * qh, axis=-1, keepdims=True) + EPS) * (GDN_DK ** -0.5))
        k.append(kh * lax.rsqrt(jnp.sum(kh * kh, axis=-1, keepdims=True) + EPS))
    k_col = [to_col(k[h]) for h in heads]
    q_col = [to_col(q[h]) for h in heads]
    s = [jnp.exp(g_full[:, GDN_HEADS + h:GDN_HEADS + h + 1]) * s_ref[0, h] for h in heads]
    u = [beta_full[:, h:h + 1] * (v[h] - jnp.sum(s[h] * k_col[h], axis=0, keepdims=True))
         for h in heads]
    s = [s[h] + k_col[h] * u[h] for h in heads]
    o = [jnp.sum(s[h] * q_col[h], axis=0, keepdims=True) for h in heads]
    for h in heads:
        snew_ref[0, h] = s[h]
        zh = z[:, h * GDN_DV:(h + 1) * GDN_DV]
        ya_ref[0, :, h * GDN_DV:(h + 1) * GDN_DV] = _rms(o[h], gn) * _silu(zh)


def _gdn_step(qkv, z, ba, conv_state, state, conv_w, alog_row, dt_row, gdn_norm):
    b = qkv.shape[0]
    qkv3 = qkv.reshape(b, 1, QKV_WIDTH)
    z3 = z.reshape(b, 1, Z_WIDTH)
    ba3 = ba.reshape(b, 1, LANES)
    return pl.pallas_call(
        _gdn_step_kernel,
        grid=(b,),
        in_specs=[pl.BlockSpec((1, 1, QKV_WIDTH), lambda i: (i, 0, 0)),
                  pl.BlockSpec((1, 1, Z_WIDTH), lambda i: (i, 0, 0)),
                  pl.BlockSpec((1, 1, LANES), lambda i: (i, 0, 0)),
                  pl.BlockSpec((1, CONV_W - 1, QKV_WIDTH), lambda i: (i, 0, 0)),
                  pl.BlockSpec((1, GDN_HEADS, GDN_DK, GDN_DV), lambda i: (i, 0, 0, 0)),
                  _full_spec(conv_w.shape), _full_spec(alog_row.shape),
                  _full_spec(dt_row.shape), _full_spec(gdn_norm.shape)],
        out_specs=[pl.BlockSpec((1, 1, Z_WIDTH), lambda i: (i, 0, 0)),
                   pl.BlockSpec((1, GDN_HEADS, GDN_DK, GDN_DV), lambda i: (i, 0, 0, 0)),
                   pl.BlockSpec((1, CONV_W - 1, QKV_WIDTH), lambda i: (i, 0, 0))],
        out_shape=[jax.ShapeDtypeStruct((b, 1, Z_WIDTH), F32),
                   jax.ShapeDtypeStruct((b, GDN_HEADS, GDN_DK, GDN_DV), F32),
                   jax.ShapeDtypeStruct((b, CONV_W - 1, QKV_WIDTH), F32)],
        compiler_params=_params("parallel"),
        name="gdn_step",
    )(qkv3, z3, ba3, conv_state, state, conv_w, alog_row, dt_row, gdn_norm)


def _mla_queries_latent(x, cos2, sin2, g_mix, w_down, g_qa, w_uq, g_kva, g_qn, g_qr, g_kr):
    xn = _rms(x, g_mix).astype(BF16)
    qa = _rms(_dot(xn, w_down[:, 0:Q_LORA]), g_qa).astype(BF16)
    q = _dot(qa, w_uq)
    qn, qr = [], []
    for h in range(MLA_HEADS):
        qn.append(_rms(q[:, h * QK_PAD:h * QK_PAD + NOPE_DIM], g_qn))
        qr.append(_rope(_rms(q[:, h * QK_PAD + NOPE_DIM:(h + 1) * QK_PAD], g_qr, ROPE_DIM),
                        cos2, sin2))
    c = _rms(_dot(xn, w_down[:, Q_LORA:Q_LORA + KV_LORA]), g_kva)
    kr = _rope(_rms(_dot(xn, w_down[:, Q_LORA + KV_LORA:]), g_kr, ROPE_DIM), cos2, sin2)
    return qn, qr, c, kr


def _mla_pre_kernel(x_ref, cos_ref, sin_ref, gmix_ref, wdown_ref, gqa_ref,
                    wuq_ref, gkva_ref, gqn_ref, gqr_ref, gkr_ref, wuk_ref, wuv_ref, gkn_ref,
                    q_ref, k_ref, v_ref, lat_ref):
    qn, qr, c, kr = _mla_queries_latent(
        x_ref[0], cos_ref[...], sin_ref[...], gmix_ref[...], wdown_ref[...],
        gqa_ref[...], wuq_ref[...], gkva_ref[...], gqn_ref[...] * ATTN_Q_SCALE,
        gqr_ref[...] * ATTN_Q_SCALE, gkr_ref[...])
    lat_ref[0, :, 0:KV_LORA] = c
    lat_ref[0, :, KV_LORA:LATENT_WIDTH] = kr[:, 0:ROPE_DIM]
    c_bf = c.astype(BF16)
    kr_bf = kr.astype(BF16)
    kfull = _dot(c_bf, wuk_ref[...])
    v_t = _dot_nt(wuv_ref[...], c_bf).astype(BF16)
    for g in range(v_ref.shape[1]):
        v_ref[0, g] = v_t[:, g * ATTN_TILE:(g + 1) * ATTN_TILE]
    gkn = gkn_ref[...]
    for h in range(MLA_HEADS):
        q_ref[0, :, h * QK_PAD:h * QK_PAD + NOPE_DIM] = qn[h].astype(BF16)
        q_ref[0, :, h * QK_PAD + NOPE_DIM:(h + 1) * QK_PAD] = qr[h].astype(BF16)
        kn = _rms(kfull[:, h * NOPE_DIM:(h + 1) * NOPE_DIM], gkn)
        k_ref[0, :, h * QK_PAD:h * QK_PAD + NOPE_DIM] = kn.astype(BF16)
        k_ref[0, :, h * QK_PAD + NOPE_DIM:(h + 1) * QK_PAD] = kr_bf


def _mla_pre(x, cos2, sin2, mw):
    b, t, d = x.shape
    ta = min(ATTN_TILE, t)
    tm = min(MLA_PRE_TILE, t)
    assert tm % ta == 0 and t % tm == 0
    consts = [mw["g_mix"], mw["w_down"], mw["g_qa"], mw["w_uq"],
              mw["g_kva"], mw["g_qn"], mw["g_qr"], mw["g_kr"], mw["w_uk"], mw["w_uv_t"],
              mw["g_kn"]]
    return pl.pallas_call(
        _mla_pre_kernel,
        grid=(b, t // tm),
        in_specs=[pl.BlockSpec((1, tm, d), lambda i, j: (i, j, 0)),
                  pl.BlockSpec((tm, LANES), lambda i, j: (j, 0)),
                  pl.BlockSpec((tm, LANES), lambda i, j: (j, 0))]
                 + [_full_spec(a.shape) for a in consts],
        out_specs=[pl.BlockSpec((1, tm, MLA_HEADS * QK_PAD), lambda i, j: (i, j, 0)),
                   pl.BlockSpec((1, tm, MLA_HEADS * QK_PAD), lambda i, j: (i, j, 0)),
                   pl.BlockSpec((1, tm // ta, MLA_HEADS * V_DIM, ta), lambda i, j: (i, j, 0, 0)),
                   pl.BlockSpec((1, tm, LATENT_WIDTH), lambda i, j: (i, j, 0))],
        out_shape=[jax.ShapeDtypeStruct((b, t, MLA_HEADS * QK_PAD), BF16),
                   jax.ShapeDtypeStruct((b, t, MLA_HEADS * QK_PAD), BF16),
                   jax.ShapeDtypeStruct((b, t // ta, MLA_HEADS * V_DIM, ta), BF16),
                   jax.ShapeDtypeStruct((b, t, LATENT_WIDTH), F32)],
        compiler_params=_params("parallel", "parallel"),
        name="mla_pre",
    )(x, cos2, sin2, *consts)


def _mla_pre_sample_kernel(x_ref, cos_ref, sin_ref, gmix_ref, wdown_ref, gqa_ref,
                           wuq_ref, gkva_ref, gqn_ref, gqr_ref, gkr_ref, wuk_ref, gkn_ref,
                           qa_ref, qr_ref, lat_ref):
    qn, qr, c, kr = _mla_queries_latent(
        x_ref[...], cos_ref[...], sin_ref[...], gmix_ref[...], wdown_ref[...],
        gqa_ref[...], wuq_ref[...], gkva_ref[...], gqn_ref[...], gqr_ref[...],
        gkr_ref[...])
    lat_ref[:, 0:KV_LORA] = c
    lat_ref[:, KV_LORA:LATENT_WIDTH] = kr[:, 0:ROPE_DIM]
    gkn = gkn_ref[...]
    for h in range(MLA_HEADS):
        hi, lo = _split2(qn[h] * gkn)
        w_h = wuk_ref[:, h * NOPE_DIM:(h + 1) * NOPE_DIM]
        qa_ref[h] = _dot_nt(hi, w_h) + _dot_nt(lo, w_h)
        qr_ref[h] = qr[h]


def _mla_pre_sample(x2, cos2, sin2, mw):
    m = x2.shape[0]
    consts = [mw["g_mix"], mw["w_down"], mw["g_qa"], mw["w_uq"],
              mw["g_kva"], mw["g_qn"], mw["g_qr"], mw["g_kr"], mw["w_uk"], mw["g_kn"]]
    args = [x2, cos2, sin2] + consts
    return pl.pallas_call(
        _mla_pre_sample_kernel,
        grid=(1,),
        in_specs=[_full_spec(a.shape) for a in args],
        out_specs=[_full_spec((MLA_HEADS, m, KV_LORA)), _full_spec((MLA_HEADS, m, LANES)),
                   _full_spec((m, LATENT_WIDTH))],
        out_shape=[jax.ShapeDtypeStruct((MLA_HEADS, m, KV_LORA), F32),
                   jax.ShapeDtypeStruct((MLA_HEADS, m, LANES), F32),
                   jax.ShapeDtypeStruct((m, LATENT_WIDTH), F32)],
        compiler_params=_params("arbitrary"),
        name="mla_pre_sample",
    )(*args)


ATTN_HEADS_PER_STEP = 8


def _attn_kernel(q_ref, k_ref, vt_ref, o_ref, *, tq):
    qi = pl.program_id(2)
    hs = range(ATTN_HEADS_PER_STEP)
    q_t = [q_ref[0, :, g * QK_PAD:(g + 1) * QK_PAD].astype(F32).T.astype(BF16) for g in hs]
    key = lax.broadcasted_iota(jnp.int32, (tq, tq), 0)
    qry = lax.broadcasted_iota(jnp.int32, (tq, tq), 1)

    def raw_scores(j):
        start = pl.multiple_of(j * tq, tq)
        return tuple(_dot(k_ref[0, pl.ds(start, tq), g * QK_PAD:(g + 1) * QK_PAD], q_t[g])
                     for g in hs)

    def consume(j, s_raw, stats, diagonal):
        vt = [vt_ref[0, j, g * V_DIM:(g + 1) * V_DIM, :] for g in hs]
        s = list(s_raw)
        if diagonal:
            s = [jnp.where(key <= qry, s[g], -jnp.inf) for g in hs]
        m_new = [jnp.maximum(stats[g][0], jnp.max(s[g], axis=0, keepdims=True)) for g in hs]
        p = [jnp.exp2(s[g] - m_new[g]) for g in hs]
        pv = [_dot(vt[g], p[g].astype(BF16)) for g in hs]
        out = []
        for g in hs:
            m, l, acc = stats[g]
            alpha = jnp.exp2(m - m_new[g])
            out.append((m_new[g], alpha * l + jnp.sum(p[g], axis=0, keepdims=True),
                        alpha * acc + pv[g]))
        return tuple(out)

    init = tuple((jnp.full((1, tq), -jnp.inf, F32), jnp.zeros((1, tq), F32),
                  jnp.zeros((V_DIM, tq), F32)) for _ in hs)
    stats = lax.fori_loop(0, qi, lambda j, st: consume(j, raw_scores(j), st, False), init)
    stats = consume(qi, raw_scores(qi), stats, True)
    for g in hs:
        _, l, acc = stats[g]
        o_ref[0, :, g * V_DIM:(g + 1) * V_DIM] = (acc / l).T.astype(BF16)


def _mla_prompt_attention(q_cat, k_cat, v_t):
    b, t, _ = q_cat.shape
    tq = min(ATTN_TILE, t)
    hps = ATTN_HEADS_PER_STEP
    return pl.pallas_call(
        functools.partial(_attn_kernel, tq=tq),
        grid=(b, MLA_HEADS // hps, t // tq),
        in_specs=[pl.BlockSpec((1, tq, hps * QK_PAD), lambda i, h, j: (i, j, h)),
                  pl.BlockSpec((1, t, hps * QK_PAD), lambda i, h, j: (i, 0, h)),
                  pl.BlockSpec((1, t // tq, hps * V_DIM, tq), lambda i, h, j: (i, 0, h, 0))],
        out_specs=pl.BlockSpec((1, tq, hps * V_DIM), lambda i, h, j: (i, j, h)),
        out_shape=jax.ShapeDtypeStruct((b, t, MLA_HEADS * V_DIM), BF16),
        compiler_params=_params("parallel", "parallel", "arbitrary"),
        name="mla_prompt_attention",
    )(q_cat, k_cat, v_t)


def _dec_attn_kernel(pt_ref, *refs, pps, nj):
    del pt_ref
    page_refs = refs[:pps]
    (wukt_ref, qa_ref, qr_ref, lnew_ref, ctx_ref,
     lhs_ref, qr16_ref, m_ref, l_ref, acc_ref) = refs[pps:]
    step = pl.program_id(0)
    b = step // nj
    j = step % nj
    n_k = MLA_HEADS * NOPE_DIM

    @pl.when((b == 0) & (j == 0))
    def _():
        lhs_ref[0:n_k, :] = wukt_ref[...]

    @pl.when(j == 0)
    def _():
        qa = jnp.concatenate([qa_ref[h, 0] for h in range(MLA_HEADS)]
                             + [jnp.zeros((8, KV_LORA), F32)], axis=0)
        lhs_ref[n_k:n_k + 16, :] = qa.astype(BF16)
        qr = jnp.concatenate([qr_ref[h, 0] for h in range(MLA_HEADS)]
                             + [jnp.zeros((8, LANES), F32)], axis=0)
        qr16_ref[...] = qr.astype(BF16)
        m_ref[...] = jnp.full(m_ref.shape, -jnp.inf, F32)
        l_ref[...] = jnp.zeros(l_ref.shape, F32)
        acc_ref[...] = jnp.zeros(acc_ref.shape, F32)

    def scores(res, sr):
        kt = res[0:n_k]
        ss = jnp.sum((kt * kt).reshape(MLA_HEADS, NOPE_DIM, res.shape[1]), axis=1)
        sn = res[n_k:n_k + MLA_HEADS]
        return (sn * lax.rsqrt(ss * (1.0 / NOPE_DIM) + EPS) + sr[0:MLA_HEADS]) * MLA_SCALE

    c_tok, s_parts = [], []
    for i in range(0, pps, 2):
        tile = jnp.concatenate([page_refs[i][...], page_refs[i + 1][...]], axis=1)
        c_bf = tile[0:KV_LORA].astype(BF16)
        kr_bf = tile[KV_LORA:LATENT_WIDTH].astype(BF16)
        s_parts.append(scores(_dot(lhs_ref[...], c_bf),
                              _dot(qr16_ref[:, 0:ROPE_DIM], kr_bf)))
        c_tok.append(tile[0:KV_LORA].T.astype(BF16))
    s = jnp.concatenate(s_parts, axis=1)
    m_old = m_ref[...]
    m_new = jnp.maximum(m_old, jnp.max(s, axis=-1, keepdims=True))
    alpha = jnp.exp(m_old - m_new)
    p = jnp.exp(s - m_new)
    l_ref[...] = alpha * l_ref[...] + jnp.sum(p, axis=-1, keepdims=True)
    p_bf = p.astype(BF16)
    n_tok = c_tok[0].shape[0]
    acc = alpha * acc_ref[...]
    for t, c_t in enumerate(c_tok):
        acc = acc + _dot(p_bf[:, t * n_tok:(t + 1) * n_tok], c_t)
    acc_ref[...] = acc
    m_ref[...] = m_new

    @pl.when(j == nj - 1)
    def _():
        ln = lnew_ref[0]
        c_new = ln[:, 0:KV_LORA]
        c8 = jnp.broadcast_to(c_new, (8, KV_LORA)).astype(BF16)
        kr8 = jnp.broadcast_to(ln[:, KV_LORA:LATENT_WIDTH], (8, ROPE_DIM)).astype(BF16)
        s_new = scores(_dot_nt(lhs_ref[...], c8),
                       _dot_nt(qr16_ref[:, 0:ROPE_DIM], kr8))[:, 0:1]
        m_old = m_ref[...]
        m_new = jnp.maximum(m_old, s_new)
        alpha = jnp.exp(m_old - m_new)
        p_new = jnp.exp(s_new - m_new)
        ctx = (alpha * acc_ref[...] + p_new * c_new) / (alpha * l_ref[...] + p_new)
        for h in range(MLA_HEADS):
            ctx_ref[h, 0] = ctx[h:h + 1]


def _dec_pages_per_step(n_pages):
    pps = DEC_PAGES_PER_STEP
    while n_pages % pps:
        pps //= 2
    assert pps >= 2
    return pps


def _dec_scratch():
    return [pltpu.VMEM((MLA_HEADS * NOPE_DIM + 16, KV_LORA), BF16),
            pltpu.VMEM((16, LANES), BF16),
            pltpu.VMEM((MLA_HEADS, 1), F32),
            pltpu.VMEM((MLA_HEADS, 1), F32),
            pltpu.VMEM((MLA_HEADS, KV_LORA), F32)]


def _mla_decode_attention(cache, layer, page_table, wuk_t, qa, qr, lat_new):
    bsz, n_pages = page_table.shape
    page = cache.shape[3]
    pps = _dec_pages_per_step(n_pages)
    qa4 = qa.reshape(MLA_HEADS, bsz, 1, KV_LORA)
    qr4 = qr.reshape(MLA_HEADS, bsz, 1, LANES)
    ln3 = lat_new.reshape(bsz, 1, LATENT_WIDTH)

    nj = n_pages // pps
    def page_spec(i):
        return pl.BlockSpec((pl.Squeezed(), pl.Squeezed(), LATENT_WIDTH, page),
                            lambda s, pt: (layer, pt[s * pps + i], 0, 0))

    grid_spec = pltpu.PrefetchScalarGridSpec(
        num_scalar_prefetch=1,
        grid=(bsz * nj,),
        in_specs=[page_spec(i) for i in range(pps)] + [
            pl.BlockSpec(wuk_t.shape, lambda s, pt: (0, 0)),
            pl.BlockSpec((MLA_HEADS, 1, 1, KV_LORA), lambda s, pt: (0, s // nj, 0, 0)),
            pl.BlockSpec((MLA_HEADS, 1, 1, LANES), lambda s, pt: (0, s // nj, 0, 0)),
            pl.BlockSpec((1, 1, LATENT_WIDTH), lambda s, pt: (s // nj, 0, 0))],
        out_specs=pl.BlockSpec((MLA_HEADS, 1, 1, KV_LORA), lambda s, pt: (0, s // nj, 0, 0)),
        scratch_shapes=_dec_scratch())
    ctx = pl.pallas_call(
        functools.partial(_dec_attn_kernel, pps=pps, nj=nj),
        grid_spec=grid_spec,
        out_shape=jax.ShapeDtypeStruct((MLA_HEADS, bsz, 1, KV_LORA), F32),
        compiler_params=_params("arbitrary"),
        name="mla_decode_attention",
    )(page_table.reshape(bsz * n_pages), *([cache] * pps), wuk_t, qa4, qr4, ln3)
    return ctx.reshape(MLA_HEADS, bsz, KV_LORA)


def _dec_post_kernel(ctx_ref, wuv_ref, y_ref):
    for h in range(MLA_HEADS):
        hi, lo = _split2(ctx_ref[h])
        w_h = wuv_ref[:, h * V_DIM:(h + 1) * V_DIM]
        y_ref[:, h * V_DIM:(h + 1) * V_DIM] = (_dot(hi, w_h) + _dot(lo, w_h)).astype(BF16)


def _dec_post(ctx, w_uv):
    bsz = ctx.shape[1]
    return pl.pallas_call(
        _dec_post_kernel,
        grid=(1,),
        in_specs=[_full_spec(ctx.shape), _full_spec(w_uv.shape)],
        out_specs=_full_spec((bsz, MLA_HEADS * V_DIM)),
        out_shape=jax.ShapeDtypeStruct((bsz, MLA_HEADS * V_DIM), BF16),
        compiler_params=_params("arbitrary"),
        name="mla_decode_values",
    )(ctx, w_uv)


def _merge_mlp_kernel(x_ref, ya_ref, yb_ref, gmix_ref, wg_ref, wya_ref, wyb_ref, wo_ref,
                      gmlp_ref, wup_ref, wdn_ref, y_ref):
    tm, d = x_ref.shape
    d_ff = wup_ref.shape[1]
    n_grp = max(1, tm // MERGE_ROWS_PER_GROUP)
    rows = tm // n_grp
    grp = range(n_grp)
    sl = [slice(r * rows, (r + 1) * rows) for r in grp]
    x = [x_ref[sl[r], :] for r in grp]
    xn = [_rms(x[r], gmix_ref[...]).astype(BF16) for r in grp]
    gates = [_sigmoid(_dot(xn[r], wg_ref[...])) for r in grp]
    pa = [_dot(ya_ref[sl[r], :].astype(BF16), wya_ref[...]) for r in grp]
    pb = [_dot(yb_ref[sl[r], :].astype(BF16), wyb_ref[...]) for r in grp]
    mix = [(gates[r][:, :d] * pa[r] + gates[r][:, d:] * pb[r]).astype(BF16) for r in grp]
    x1 = [x[r] + _dot(mix[r], wo_ref[...]) for r in grp]
    hin = [_rms(x1[r], gmlp_ref[...]).astype(BF16) for r in grp]
    y = x1
    for f in range(0, d_ff, MLP_FF_BLOCK):
        hmid = [jnp.maximum(_dot(hin[r], wup_ref[:, f:f + MLP_FF_BLOCK]), 0.0) for r in grp]
        y = [y[r] + _dot((hmid[r] * hmid[r]).astype(BF16), wdn_ref[f:f + MLP_FF_BLOCK, :])
             for r in grp]
    for r in grp:
        y_ref[sl[r], :] = y[r]


MERGE_ROWS_PER_GROUP = 256
MLP_FF_BLOCK = 1024


def _merge_mlp(x2, ya, yb, ew):
    m, d = x2.shape
    tm = min(512, m)
    consts = [ew["g_mix"], ew["w_gates"], ew["w_ya"], ew["w_yb"], ew["w_o"], ew["g_mlp"],
              ew["w_up"], ew["w_down"]]

    def const_spec(a):
        nd = a.ndim
        return pl.BlockSpec(a.shape, lambda i: (0,) * nd, pipeline_mode=pl.Buffered(1))

    return pl.pallas_call(
        _merge_mlp_kernel,
        grid=(m // tm,),
        in_specs=[pl.BlockSpec((tm, d), lambda i: (i, 0)),
                  pl.BlockSpec((tm, ya.shape[1]), lambda i: (i, 0)),
                  pl.BlockSpec((tm, yb.shape[1]), lambda i: (i, 0))]
                 + [const_spec(a) for a in consts],
        out_specs=pl.BlockSpec((tm, d), lambda i: (i, 0)),
        out_shape=jax.ShapeDtypeStruct((m, d), F32),
        compiler_params=_params("parallel"),
        name="merge_mlp",
    )(x2, ya, yb, *consts)


def _rope_tables(pos):
    inv = jnp.power(ROPE_THETA, -jnp.arange(0, ROPE_DIM, 2, dtype=F32) / ROPE_DIM)
    ang = pos.astype(F32)[:, None] * inv[None, :]
    cos, sin = jnp.cos(ang), jnp.sin(ang)
    zeros = jnp.zeros((pos.shape[0], LANES - ROPE_DIM), F32)
    return (jnp.concatenate([cos, cos, zeros], axis=1),
            jnp.concatenate([-sin, sin, zeros], axis=1))


def _pad_lanes(a, width=LANES, offset=0):
    return jnp.zeros((1, width), F32).at[0, offset:offset + a.shape[0]].set(a.astype(F32))


def _layer_weights(l, norm_mix, w_in, conv_w, a_log, dt_bias, gdn_norm, q_a_norm, w_uq,
                   kv_a_norm, w_uk, w_uv, q_norm_nope, q_norm_rope, k_norm_nope, k_norm_rope,
                   w_ya, w_yb, w_o, norm_mlp, w_up, w_down):
    d = w_in.shape[1]
    off_z = QKV_WIDTH
    off_b = off_z + Z_WIDTH
    off_qd = off_b + 2 * GDN_HEADS
    off_kv = off_qd + Q_LORA
    off_g = off_kv + LATENT_WIDTH
    wi = w_in[l]
    g_mix = norm_mix[l].reshape(1, d)
    gw = {
        "g_mix": g_mix,
        "w_in": wi.astype(BF16),
        "conv_w": conv_w[l],
        "alog_row": _pad_lanes(a_log[l], offset=GDN_HEADS),
        "dt_row": _pad_lanes(dt_bias[l], offset=GDN_HEADS),
        "gdn_norm": gdn_norm[l].reshape(1, GDN_DV),
    }
    wq = w_uq[l].reshape(Q_LORA, MLA_HEADS, NOPE_DIM + ROPE_DIM)
    wq = jnp.pad(wq, ((0, 0), (0, 0), (0, QK_PAD - NOPE_DIM - ROPE_DIM)))
    mw = {
        "g_mix": g_mix,
        "w_down": jnp.pad(wi[:, off_qd:off_g], ((0, 0), (0, LANES - ROPE_DIM))).astype(BF16),
        "g_qa": q_a_norm[l].reshape(1, Q_LORA),
        "w_uq": wq.reshape(Q_LORA, MLA_HEADS * QK_PAD).astype(BF16),
        "g_kva": kv_a_norm[l].reshape(1, KV_LORA),
        "g_qn": q_norm_nope[l].reshape(1, NOPE_DIM),
        "g_qr": _pad_lanes(q_norm_rope[l]),
        "g_kr": _pad_lanes(k_norm_rope[l]),
        "g_kn": k_norm_nope[l].reshape(1, NOPE_DIM),
        "w_uk": w_uk[l].astype(BF16),
        "w_uk_t": w_uk[l].T.astype(BF16),
        "w_uv": w_uv[l].astype(BF16),
        "w_uv_t": w_uv[l].T.astype(BF16),
    }
    ew = {
        "g_mix": g_mix,
        "w_gates": wi[:, off_g:].astype(BF16),
        "w_ya": w_ya[l].astype(BF16),
        "w_yb": w_yb[l].astype(BF16),
        "w_o": w_o[l].astype(BF16),
        "g_mlp": norm_mlp[l].reshape(1, d),
        "w_up": w_up[l].astype(BF16),
        "w_down": w_down[l].astype(BF16),
    }
    return gw, mw, ew


def kernel(x_prompt, x_sample, cache_mla, state_gdn, state_conv, page_table, norm_mix, w_in,
           conv_w, a_log, dt_bias, gdn_norm, q_a_norm, w_uq, kv_a_norm, w_uk, w_uv,
           q_norm_nope, q_norm_rope, k_norm_nope, k_norm_rope, w_ya, w_yb, w_o, norm_mlp,
           w_up, w_down):
    depth = w_in.shape[0]
    bp, tp, d = x_prompt.shape
    bs, ts, _ = x_sample.shape
    assert ts == 1 and tp % GDN_CHUNK == 0
    past_len = page_table.shape[1] * cache_mla.shape[2]
    cache_t = jnp.swapaxes(cache_mla, 2, 3)
    cos_p, sin_p = _rope_tables(jnp.arange(tp))
    cos_s, sin_s = _rope_tables(past_len + jnp.zeros((bs,), jnp.int32))

    x_p, x_s = x_prompt, x_sample.reshape(bs, d)
    rows_p, gdn_p, conv_p, rows_s, gdn_s, conv_s = [], [], [], [], [], []
    for l in range(depth):
        gw, mw, ew = _layer_weights(
            l, norm_mix, w_in, conv_w, a_log, dt_bias, gdn_norm, q_a_norm, w_uq, kv_a_norm,
            w_uk, w_uv, q_norm_nope, q_norm_rope, k_norm_nope, k_norm_rope, w_ya, w_yb, w_o,
            norm_mlp, w_up, w_down)
        gdn_consts = (gw["conv_w"], gw["alog_row"], gw["dt_row"], gw["gdn_norm"])

        xp2 = x_p.reshape(bp * tp, d)
        yc, z, ba, cs_p = _gdn_proj_conv(x_p, gw["g_mix"], gw["w_in"], gw["conv_w"])
        ya_p, s_p = _gdn_chunked(yc.reshape(bp, tp, QKV_WIDTH), z.reshape(bp, tp, Z_WIDTH),
                                 ba.reshape(bp, tp, LANES), *gdn_consts[1:])
        q_cat, k_cat, v_p, lat_p = _mla_pre(x_p, cos_p, sin_p, mw)
        yb_p = _mla_prompt_attention(q_cat, k_cat, v_p)
        y_p = _merge_mlp(xp2, ya_p.reshape(bp * tp, Z_WIDTH), yb_p.reshape(bp * tp, -1), ew)

        qkv, z, ba = _gdn_proj(x_s, gw["g_mix"], gw["w_in"])
        ya_s, s_s, cs_s = _gdn_step(qkv, z, ba, state_conv[l], state_gdn[l], *gdn_consts)
        qa, qr, lat_s = _mla_pre_sample(x_s, cos_s, sin_s, mw)
        ctx = _mla_decode_attention(cache_t, l, page_table, mw["w_uk_t"], qa, qr, lat_s)
        yb_s = _dec_post(ctx, mw["w_uv"])
        y_s = _merge_mlp(x_s, ya_s.reshape(bs, Z_WIDTH), yb_s, ew)

        x_p, x_s = y_p.reshape(bp, tp, d), y_s
        rows_p.append(lat_p)
        gdn_p.append(s_p)
        conv_p.append(cs_p)
        rows_s.append(lat_s.reshape(bs, 1, LATENT_WIDTH))
        gdn_s.append(s_s)
        conv_s.append(cs_s)
    return (x_p, x_s.reshape(bs, 1, d), jnp.stack(rows_p), jnp.stack(gdn_p), jnp.stack(conv_p),
            jnp.stack(rows_s), jnp.stack(gdn_s), jnp.stack(conv_s))
```

```python
import functools

import jax
import jax.numpy as jnp
from jax import lax
from jax.experimental import pallas as pl
from jax.experimental.pallas import tpu as pltpu

F32 = jnp.float32
BF16 = jnp.bfloat16

EPS = 1e-6
GDN_HEADS = 8
GDN_DK = 128
GDN_DV = 128
CONV_W = 4
GDN_CHUNK = 64
MLA_HEADS = 8
Q_LORA = 512
KV_LORA = 512
NOPE_DIM = 128
ROPE_DIM = 64
V_DIM = 128
ROPE_THETA = 10000.0
MLA_SCALE = (NOPE_DIM + ROPE_DIM) ** -0.5
LOG2_E = 1.4426950408889634
ATTN_Q_SCALE = MLA_SCALE * LOG2_E
LATENT_WIDTH = KV_LORA + ROPE_DIM
QKV_WIDTH = GDN_HEADS * (2 * GDN_DK + GDN_DV)
Z_WIDTH = GDN_HEADS * GDN_DV

LANES = 128
QK_PAD = 256
VMEM_LIMIT = 56 * 1024 * 1024
DEC_PAGES_PER_STEP = 64
ATTN_TILE = 256
MLA_PRE_TILE = 512


def _dot(a, b):
    return jnp.dot(a, b, preferred_element_type=F32)


def _dot_nt(a, b):
    return lax.dot_general(a, b, (((1,), (1,)), ((), ())), preferred_element_type=F32)


def _dot_tn(a, b):
    return lax.dot_general(a, b, (((0,), (0,)), ((), ())), preferred_element_type=F32)


def _split2(x):
    hi = x.astype(BF16)
    lo = (x - hi.astype(F32)).astype(BF16)
    return hi, lo


def _split3(x):
    hi = x.astype(BF16)
    r = x - hi.astype(F32)
    mid = r.astype(BF16)
    lo = (r - mid.astype(F32)).astype(BF16)
    return hi, mid, lo


def _rms(x, g, n=None):
    n = x.shape[-1] if n is None else n
    ss = jnp.sum(x * x, axis=-1, keepdims=True) * (1.0 / n)
    return x * lax.rsqrt(ss + EPS) * g


def _sigmoid(x):
    return 1.0 / (1.0 + jnp.exp(-x))


def _silu(x):
    h = 0.5 * x
    return h + h * jnp.tanh(h)


def _softplus(x):
    return jnp.maximum(x, 0.0) + jnp.log1p(jnp.exp(-jnp.abs(x)))


def _rope(x, cos2, sin2):
    lane = lax.broadcasted_iota(jnp.int32, x.shape, 1)
    fwd = pltpu.roll(x, LANES - ROPE_DIM // 2, axis=1)
    bwd = pltpu.roll(x, ROPE_DIM // 2, axis=1)
    swapped = jnp.where(lane < ROPE_DIM // 2, fwd, bwd)
    return x * cos2 + swapped * sin2


def _full_spec(shape):
    nd = len(shape)
    return pl.BlockSpec(shape, lambda *_: (0,) * nd)


def _params(*sem):
    return pltpu.CompilerParams(dimension_semantics=sem, vmem_limit_bytes=VMEM_LIMIT)


OFF_Z = QKV_WIDTH
OFF_BA = QKV_WIDTH + Z_WIDTH
GDN_IN_COLS = OFF_BA + LANES


def _gdn_w_spec(w_in_bf):
    return pl.BlockSpec((w_in_bf.shape[0], GDN_IN_COLS), lambda i: (0, 0))


def _gdn_proj_kernel(x_ref, g_ref, w_ref, qkv_ref, z_ref, ba_ref):
    xn = _rms(x_ref[...], g_ref[...]).astype(BF16)
    qkv_ref[...] = _dot(xn, w_ref[:, 0:OFF_Z])
    z_ref[...] = _dot(xn, w_ref[:, OFF_Z:OFF_BA])
    ba_ref[...] = _dot(xn, w_ref[:, OFF_BA:GDN_IN_COLS])


def _gdn_proj(x2, g, w_in_bf):
    m, d = x2.shape
    tm = min(256, m)
    return pl.pallas_call(
        _gdn_proj_kernel,
        grid=(m // tm,),
        in_specs=[pl.BlockSpec((tm, d), lambda i: (i, 0)),
                  _full_spec(g.shape), _gdn_w_spec(w_in_bf)],
        out_specs=[pl.BlockSpec((tm, QKV_WIDTH), lambda i: (i, 0)),
                   pl.BlockSpec((tm, Z_WIDTH), lambda i: (i, 0)),
                   pl.BlockSpec((tm, LANES), lambda i: (i, 0))],
        out_shape=[jax.ShapeDtypeStruct((m, QKV_WIDTH), F32),
                   jax.ShapeDtypeStruct((m, Z_WIDTH), F32),
                   jax.ShapeDtypeStruct((m, LANES), F32)],
        compiler_params=_params("parallel"),
        name="gdn_proj",
    )(x2, g, w_in_bf)


PROJ_CONV_COLS = 512
assert CONV_W == 4


def _gdn_proj_conv_kernel(x_ref, g_ref, w_ref, cw_ref,
                          y_ref, z_ref, ba_ref, cs_ref, xp_ref, *, tiles_per_seq):
    i = pl.program_id(0)
    tm = x_ref.shape[0]

    @pl.when(i % tiles_per_seq == 0)
    def _():
        xp_ref[...] = jnp.zeros((8, QKV_WIDTH), F32)

    xn = _rms(x_ref[...], g_ref[...]).astype(BF16)
    row8 = lax.broadcasted_iota(jnp.int32, (8, PROJ_CONV_COLS), 0)
    for c0 in range(0, QKV_WIDTH, PROJ_CONV_COLS):
        cb = slice(c0, c0 + PROJ_CONV_COLS)
        raw = _dot(xn, w_ref[:, cb])
        halo = xp_ref[:, cb]
        cw = cw_ref[:, cb]
        back2 = pltpu.roll(raw, 2, axis=0)
        x2 = jnp.concatenate([jnp.where(row8 < 2, pltpu.roll(halo, 2, axis=0), back2[0:8]),
                              back2[8:]], axis=0)
        u = raw * cw[2:3] + x2 * cw[0:1]
        u_prev = halo[7:8] * cw[2:3] + halo[5:6] * cw[0:1]
        back1 = pltpu.roll(u, 1, axis=0)
        su = jnp.concatenate([jnp.where(row8 < 1, u_prev, back1[0:8]), back1[8:]], axis=0)
        y_ref[:, cb] = _silu(raw * cw[3:4] + x2 * cw[1:2] + su)
        xp_ref[:, cb] = raw[tm - 8:tm]
    z_ref[...] = _dot(xn, w_ref[:, OFF_Z:OFF_BA])
    ba_ref[...] = _dot(xn, w_ref[:, OFF_BA:GDN_IN_COLS])

    @pl.when(i % tiles_per_seq == tiles_per_seq - 1)
    def _():
        cs_ref[0] = xp_ref[8 - (CONV_W - 1):8, :]


def _gdn_proj_conv(x, g, w_in_bf, conv_w):
    b, t, d = x.shape
    tm = min(512, t)
    assert t % tm == 0
    tiles_per_seq = t // tm
    m = b * t
    return pl.pallas_call(
        functools.partial(_gdn_proj_conv_kernel, tiles_per_seq=tiles_per_seq),
        grid=(m // tm,),
        in_specs=[pl.BlockSpec((tm, d), lambda i: (i, 0)),
                  _full_spec(g.shape), _gdn_w_spec(w_in_bf), _full_spec(conv_w.shape)],
        out_specs=[pl.BlockSpec((tm, QKV_WIDTH), lambda i: (i, 0)),
                   pl.BlockSpec((tm, Z_WIDTH), lambda i: (i, 0)),
                   pl.BlockSpec((tm, LANES), lambda i: (i, 0)),
                   pl.BlockSpec((1, CONV_W - 1, QKV_WIDTH), lambda i: (i // tiles_per_seq, 0, 0))],
        out_shape=[jax.ShapeDtypeStruct((m, QKV_WIDTH), F32),
                   jax.ShapeDtypeStruct((m, Z_WIDTH), F32),
                   jax.ShapeDtypeStruct((m, LANES), F32),
                   jax.ShapeDtypeStruct((b, CONV_W - 1, QKV_WIDTH), F32)],
        scratch_shapes=[pltpu.VMEM((8, QKV_WIDTH), F32)],
        compiler_params=_params("arbitrary"),
        name="gdn_proj_conv",
    )(x.reshape(m, d), g, w_in_bf, conv_w)


def _gate_rows(ba, alog_row, dt_row):
    lane = lax.broadcasted_iota(jnp.int32, ba.shape, 1)
    is_a = (lane >= GDN_HEADS) & (lane < 2 * GDN_HEADS)
    g = jnp.where(is_a, -jnp.exp(alog_row) * _softplus(ba + dt_row), 0.0)
    return _sigmoid(ba), g


def _gdn_chunk_kernel(y_ref, z_ref, ba_ref, alog_ref, dt_ref, gn_ref, ya_ref, s_ref):
    c = pl.program_id(1)
    C = GDN_CHUNK
    TB = y_ref.shape[1]
    n_sub = TB // C

    @pl.when(c == 0)
    def _():
        s_ref[...] = jnp.zeros(s_ref.shape, F32)

    y = y_ref[0]

    beta_full, g_full = _gate_rows(ba_ref[0], alog_ref[...], dt_ref[...])
    rtb = lax.broadcasted_iota(jnp.int32, (TB, TB), 0)
    ctb = lax.broadcasted_iota(jnp.int32, (TB, TB), 1)
    tril = jnp.where((rtb >= ctb) & (rtb // C == ctb // C), 1.0, 0.0).astype(BF16)
    g_col = sum(_dot(tril, part) for part in _split3(g_full))
    r128 = lax.broadcasted_iota(jnp.int32, (LANES, LANES), 0)
    c128 = lax.broadcasted_iota(jnp.int32, (LANES, LANES), 1)
    eye = jnp.where(r128 == c128, 1.0, 0.0).astype(BF16)
    g_rows = sum(_dot_nt(eye, part) for part in _split3(g_col))

    z = z_ref[0]
    gn = gn_ref[...]
    heads = range(GDN_HEADS)
    SB = min(TB, C * GDN_CHUNKS_PER_BLOCK)
    n_blk = TB // SB
    per_blk = SB // C
    rsb = lax.broadcasted_iota(jnp.int32, (SB, SB), 0)
    csb = lax.broadcasted_iota(jnp.int32, (SB, SB), 1)
    same_chunk = (rsb // C) == (csb // C)
    incl = same_chunk & (rsb >= csb)
    strict = same_chunk & (rsb > csb)
    chunk_of_row = lax.broadcasted_iota(jnp.int32, (SB, 1), 0) // C
    units = [(b, h) for b in range(n_blk) for h in heads]
    q, k, gc, eg, g_last, rhs_bf, rhs, decay, kq, kd = ({} for _ in range(10))
    for u in units:
        b, h = u
        r = slice(b * SB, (b + 1) * SB)
        qh = y[r, h * GDN_DK:(h + 1) * GDN_DK]
        kh = y[r, GDN_HEADS * GDN_DK + h * GDN_DK:GDN_HEADS * GDN_DK + (h + 1) * GDN_DK]
        vh = y[r, 2 * GDN_HEADS * GDN_DK + h * GDN_DV:2 * GDN_HEADS * GDN_DK + (h + 1) * GDN_DV]
        q[u] = qh * lax.rsqrt(jnp.sum(qh * qh, axis=-1, keepdims=True) + EPS) * (GDN_DK ** -0.5)
        k[u] = kh * lax.rsqrt(jnp.sum(kh * kh, axis=-1, keepdims=True) + EPS)
        beta = beta_full[r, h:h + 1]
        gc[u] = g_col[r, GDN_HEADS + h:GDN_HEADS + h + 1]
        gr = g_rows[GDN_HEADS + h:GDN_HEADS + h + 1, r]
        decay[u] = jnp.where(incl, jnp.exp(jnp.where(incl, gc[u] - gr, 0.0)), 0.0)
        eg[u] = jnp.exp(gc[u])
        g_last[u] = [gc[u][(n + 1) * C - 1:(n + 1) * C, :] for n in range(per_blk)]
        g_end = g_last[u][0]
        for n in range(1, per_blk):
            g_end = jnp.where(chunk_of_row >= n, g_last[u][n], g_end)
        kd[u] = (k[u] * jnp.exp(g_end - gc[u])).astype(BF16)
        kb = k[u] * beta
        rhs[u] = jnp.concatenate([vh * beta, kb * eg[u]], axis=1)
        rhs_bf[u] = rhs[u].astype(BF16)
        kq[u] = _dot_nt(jnp.concatenate([kb, q[u]], axis=0).astype(BF16),
                        k[u].astype(BF16))
    pw = {u: jnp.where(strict, kq[u][:SB] * decay[u], 0.0) for u in units}
    aqk = {u: (kq[u][SB:] * decay[u]).astype(BF16) for u in units}
    n_acc = {u: -pw[u] for u in units}
    pw_bf = {u: pw[u].astype(BF16) for u in units}
    for _ in range(C.bit_length() - 2):
        pw = {u: _dot(pw_bf[u], pw_bf[u]) for u in units}
        pw_bf = {u: pw[u].astype(BF16) for u in units}
        n_acc = {u: n_acc[u] + pw[u] + _dot(n_acc[u].astype(BF16), pw_bf[u]) for u in units}
    sol = {u: rhs[u] + _dot(n_acc[u].astype(BF16), rhs_bf[u]) for u in units}
    qg = {u: q[u] * eg[u] for u in units}
    s_cur = [s_ref[0, h] for h in heads]
    zero_rows = jnp.zeros((C, GDN_DV), BF16)
    for n in range(n_sub):
        b, ln = divmod(n, per_blk)
        rows = slice(ln * C, (ln + 1) * C)
        tok = slice(n * C, (n + 1) * C)
        ws = [_dot(jnp.concatenate([sol[(b, h)][rows, GDN_DV:], qg[(b, h)][rows]],
                                   axis=0).astype(BF16),
                   s_cur[h].astype(BF16)) for h in heads]
        u_bf = [(sol[(b, h)][rows, :GDN_DV] - ws[h][:C]).astype(BF16) for h in heads]
        u_pad = [jnp.concatenate([zero_rows] * ln + [u_bf[h]] + [zero_rows] * (per_blk - 1 - ln),
                                 axis=0) for h in heads]
        o = [ws[h][C:] + _dot(aqk[(b, h)][rows], u_pad[h]) for h in heads]
        s_cur = [jnp.exp(g_last[(b, h)][ln]) * s_cur[h] + _dot_tn(kd[(b, h)][rows], u_bf[h])
                 for h in heads]
        for h in heads:
            zh = z[tok, h * GDN_DV:(h + 1) * GDN_DV]
            ya_ref[0, tok, h * GDN_DV:(h + 1) * GDN_DV] = (
                _rms(o[h], gn) * _silu(zh)).astype(BF16)
    for h in heads:
        s_ref[0, h] = s_cur[h]


GDN_CHUNKS_PER_STEP = 4
GDN_CHUNKS_PER_BLOCK = 2


def _gdn_chunked(y, z, ba, alog_row, dt_row, gdn_norm):
    b, t, _ = y.shape
    C = GDN_CHUNK * GDN_CHUNKS_PER_STEP
    assert t % C == 0
    nc = t // C
    return pl.pallas_call(
        _gdn_chunk_kernel,
        grid=(b, nc),
        in_specs=[pl.BlockSpec((1, C, QKV_WIDTH), lambda i, j: (i, j, 0)),
                  pl.BlockSpec((1, C, Z_WIDTH), lambda i, j: (i, j, 0)),
                  pl.BlockSpec((1, C, LANES), lambda i, j: (i, j, 0)),
                  _full_spec(alog_row.shape), _full_spec(dt_row.shape),
                  _full_spec(gdn_norm.shape)],
        out_specs=[pl.BlockSpec((1, C, Z_WIDTH), lambda i, j: (i, j, 0)),
                   pl.BlockSpec((1, GDN_HEADS, GDN_DK, GDN_DV), lambda i, j: (i, 0, 0, 0))],
        out_shape=[jax.ShapeDtypeStruct((b, t, Z_WIDTH), BF16),
                   jax.ShapeDtypeStruct((b, GDN_HEADS, GDN_DK, GDN_DV), F32)],
        compiler_params=_params("parallel", "arbitrary"),
        name="gdn_chunked",
    )(y, z, ba, alog_row, dt_row, gdn_norm)


def _gdn_step_kernel(qkv_ref, z_ref, ba_ref, cs_ref, s_ref, cw_ref, alog_ref, dt_ref, gn_ref,
                     ya_ref, snew_ref, csnew_ref):
    cw = cw_ref[...]
    r128 = lax.broadcasted_iota(jnp.int32, (GDN_DK, GDN_DK), 0)
    c128 = lax.broadcasted_iota(jnp.int32, (GDN_DK, GDN_DK), 1)
    eye = r128 == c128
    gn = gn_ref[...]
    heads = range(GDN_HEADS)

    def to_col(r):
        return jnp.sum(jnp.where(eye, jnp.broadcast_to(r, (GDN_DK, GDN_DK)), 0.0),
                       axis=1, keepdims=True)

    for sq in range(qkv_ref.shape[0]):
        rowv = qkv_ref[sq]
        cs = cs_ref[sq]
        y = rowv * cw[CONV_W - 1:CONV_W]
        for i in range(CONV_W - 1):
            y = y + cs[i:i + 1] * cw[i:i + 1]
        y = _silu(y)
        csnew_ref[sq, 0:CONV_W - 2, :] = cs[1:CONV_W - 1]
        csnew_ref[sq, CONV_W - 2:CONV_W - 1, :] = rowv
        beta_full, g_full = _gate_rows(ba_ref[sq], alog_ref[...], dt_ref[...])
        z = z_ref[sq]
        q, k, v = [], [], []
        for h in heads:
            qh = y[:, h * GDN_DK:(h + 1) * GDN_DK]
            kh = y[:, GDN_HEADS * GDN_DK + h * GDN_DK:GDN_HEADS * GDN_DK + (h + 1) * GDN_DK]
            v.append(y[:, 2 * GDN_HEADS * GDN_DK + h * GDN_DV:
                       2 * GDN_HEADS * GDN_DK + (h + 1) * GDN_DV])
            q.append(qh * lax.rsqrt(jnp.sum(qh * qh, axis=-1, keepdims=True) + EPS)
                     * (GDN_DK ** -0.5))
            k.append(kh * lax.rsqrt(jnp.sum(kh * kh, axis=-1, keepdims=True) + EPS))
        k_col = [to_col(k[h]) for h in heads]
        q_col = [to_col(q[h]) for h in heads]
        s = [jnp.exp(g_full[:, GDN_HEADS + h:GDN_HEADS + h + 1]) * s_ref[sq, h] for h in heads]
        u = [beta_full[:, h:h + 1] * (v[h] - jnp.sum(s[h] * k_col[h], axis=0, keepdims=True))
             for h in heads]
        s = [s[h] + k_col[h] * u[h] for h in heads]
        o = [jnp.sum(s[h] * q_col[h], axis=0, keepdims=True) for h in heads]
        for h in heads:
            snew_ref[sq, h] = s[h]
            zh = z[:, h * GDN_DV:(h + 1) * GDN_DV]
            ya_ref[sq, :, h * GDN_DV:(h + 1) * GDN_DV] = _rms(o[h], gn) * _silu(zh)


GDN_STEP_SEQS = 4


def _gdn_step(qkv, z, ba, conv_state, state, conv_w, alog_row, dt_row, gdn_norm):
    b = qkv.shape[0]
    ns = GDN_STEP_SEQS if b % GDN_STEP_SEQS == 0 else 1
    qkv3 = qkv.reshape(b, 1, QKV_WIDTH)
    z3 = z.reshape(b, 1, Z_WIDTH)
    ba3 = ba.reshape(b, 1, LANES)
    return pl.pallas_call(
        _gdn_step_kernel,
        grid=(b // ns,),
        in_specs=[pl.BlockSpec((ns, 1, QKV_WIDTH), lambda i: (i, 0, 0)),
                  pl.BlockSpec((ns, 1, Z_WIDTH), lambda i: (i, 0, 0)),
                  pl.BlockSpec((ns, 1, LANES), lambda i: (i, 0, 0)),
                  pl.BlockSpec((ns, CONV_W - 1, QKV_WIDTH), lambda i: (i, 0, 0)),
                  pl.BlockSpec((ns, GDN_HEADS, GDN_DK, GDN_DV), lambda i: (i, 0, 0, 0)),
                  _full_spec(conv_w.shape), _full_spec(alog_row.shape),
                  _full_spec(dt_row.shape), _full_spec(gdn_norm.shape)],
        out_specs=[pl.BlockSpec((ns, 1, Z_WIDTH), lambda i: (i, 0, 0)),
                   pl.BlockSpec((ns, GDN_HEADS, GDN_DK, GDN_DV), lambda i: (i, 0, 0, 0)),
                   pl.BlockSpec((ns, CONV_W - 1, QKV_WIDTH), lambda i: (i, 0, 0))],
        out_shape=[jax.ShapeDtypeStruct((b, 1, Z_WIDTH), F32),
                   jax.ShapeDtypeStruct((b, GDN_HEADS, GDN_DK, GDN_DV), F32),
                   jax.ShapeDtypeStruct((b, CONV_W - 1, QKV_WIDTH), F32)],
        compiler_params=_params("parallel"),
        name="gdn_step",
    )(qkv3, z3, ba3, conv_state, state, conv_w, alog_row, dt_row, gdn_norm)


def _mla_queries_latent(x, cos2, sin2, g_mix, w_down, g_qa, w_uq, g_kva, g_qn, g_qr, g_kr):
    xn = _rms(x, g_mix).astype(BF16)
    qa = _rms(_dot(xn, w_down[:, 0:Q_LORA]), g_qa).astype(BF16)
    q = _dot(qa, w_uq)
    qn, qr = [], []
    for h in range(MLA_HEADS):
        qn.append(_rms(q[:, h * QK_PAD:h * QK_PAD + NOPE_DIM], g_qn))
        qr.append(_rope(_rms(q[:, h * QK_PAD + NOPE_DIM:(h + 1) * QK_PAD], g_qr, ROPE_DIM),
                        cos2, sin2))
    c = _rms(_dot(xn, w_down[:, Q_LORA:Q_LORA + KV_LORA]), g_kva)
    kr = _rope(_rms(_dot(xn, w_down[:, Q_LORA + KV_LORA:]), g_kr, ROPE_DIM), cos2, sin2)
    return qn, qr, c, kr


def _mla_pre_kernel(x_ref, cos_ref, sin_ref, gmix_ref, wdown_ref, gqa_ref,
                    wuq_ref, gkva_ref, gqn_ref, gqr_ref, gkr_ref, wuk_ref, wuv_ref, gkn_ref,
                    q_ref, k_ref, v_ref, lat_ref):
    qn, qr, c, kr = _mla_queries_latent(
        x_ref[0], cos_ref[...], sin_ref[...], gmix_ref[...], wdown_ref[...],
        gqa_ref[...], wuq_ref[...], gkva_ref[...], gqn_ref[...] * ATTN_Q_SCALE,
        gqr_ref[...] * ATTN_Q_SCALE, gkr_ref[...])
    lat_ref[0, :, 0:KV_LORA] = c
    lat_ref[0, :, KV_LORA:LATENT_WIDTH] = kr[:, 0:ROPE_DIM]
    c_bf = c.astype(BF16)
    kr_bf = kr.astype(BF16)
    kfull = _dot(c_bf, wuk_ref[...])
    v_t = _dot_nt(wuv_ref[...], c_bf).astype(BF16)
    for g in range(v_ref.shape[1]):
        v_ref[0, g] = v_t[:, g * ATTN_TILE:(g + 1) * ATTN_TILE]
    gkn = gkn_ref[...]
    for h in range(MLA_HEADS):
        q_ref[0, :, h * QK_PAD:h * QK_PAD + NOPE_DIM] = qn[h].astype(BF16)
        q_ref[0, :, h * QK_PAD + NOPE_DIM:(h + 1) * QK_PAD] = qr[h].astype(BF16)
        kn = _rms(kfull[:, h * NOPE_DIM:(h + 1) * NOPE_DIM], gkn)
        k_ref[0, :, h * QK_PAD:h * QK_PAD + NOPE_DIM] = kn.astype(BF16)
        k_ref[0, :, h * QK_PAD + NOPE_DIM:(h + 1) * QK_PAD] = kr_bf


def _mla_pre(x, cos2, sin2, mw):
    b, t, d = x.shape
    ta = min(ATTN_TILE, t)
    tm = min(MLA_PRE_TILE, t)
    assert tm % ta == 0 and t % tm == 0
    consts = [mw["g_mix"], mw["w_down"], mw["g_qa"], mw["w_uq"],
              mw["g_kva"], mw["g_qn"], mw["g_qr"], mw["g_kr"], mw["w_uk"], mw["w_uv_t"],
              mw["g_kn"]]
    return pl.pallas_call(
        _mla_pre_kernel,
        grid=(b, t // tm),
        in_specs=[pl.BlockSpec((1, tm, d), lambda i, j: (i, j, 0)),
                  pl.BlockSpec((tm, LANES), lambda i, j: (j, 0)),
                  pl.BlockSpec((tm, LANES), lambda i, j: (j, 0))]
                 + [_full_spec(a.shape) for a in consts],
        out_specs=[pl.BlockSpec((1, tm, MLA_HEADS * QK_PAD), lambda i, j: (i, j, 0)),
                   pl.BlockSpec((1, tm, MLA_HEADS * QK_PAD), lambda i, j: (i, j, 0)),
                   pl.BlockSpec((1, tm // ta, MLA_HEADS * V_DIM, ta), lambda i, j: (i, j, 0, 0)),
                   pl.BlockSpec((1, tm, LATENT_WIDTH), lambda i, j: (i, j, 0))],
        out_shape=[jax.ShapeDtypeStruct((b, t, MLA_HEADS * QK_PAD), BF16),
                   jax.ShapeDtypeStruct((b, t, MLA_HEADS * QK_PAD), BF16),
                   jax.ShapeDtypeStruct((b, t // ta, MLA_HEADS * V_DIM, ta), BF16),
                   jax.ShapeDtypeStruct((b, t, LATENT_WIDTH), F32)],
        compiler_params=_params("parallel", "parallel"),
        name="mla_pre",
    )(x, cos2, sin2, *consts)


def _mla_pre_sample_kernel(x_ref, cos_ref, sin_ref, gmix_ref, wdown_ref, gqa_ref,
                           wuq_ref, gkva_ref, gqn_ref, gqr_ref, gkr_ref, wuk_ref, gkn_ref,
                           qa_ref, qr_ref, lat_ref):
    qn, qr, c, kr = _mla_queries_latent(
        x_ref[...], cos_ref[...], sin_ref[...], gmix_ref[...], wdown_ref[...],
        gqa_ref[...], wuq_ref[...], gkva_ref[...], gqn_ref[...], gqr_ref[...],
        gkr_ref[...])
    lat_ref[:, 0:KV_LORA] = c
    lat_ref[:, KV_LORA:LATENT_WIDTH] = kr[:, 0:ROPE_DIM]
    gkn = gkn_ref[...]
    for h in range(MLA_HEADS):
        hi, lo = _split2(qn[h] * gkn)
        w_h = wuk_ref[:, h * NOPE_DIM:(h + 1) * NOPE_DIM]
        qa_ref[h] = _dot_nt(hi, w_h) + _dot_nt(lo, w_h)
        qr_ref[h] = qr[h]


def _mla_pre_sample(x2, cos2, sin2, mw):
    m = x2.shape[0]
    consts = [mw["g_mix"], mw["w_down"], mw["g_qa"], mw["w_uq"],
              mw["g_kva"], mw["g_qn"], mw["g_qr"], mw["g_kr"], mw["w_uk"], mw["g_kn"]]
    args = [x2, cos2, sin2] + consts
    return pl.pallas_call(
        _mla_pre_sample_kernel,
        grid=(1,),
        in_specs=[_full_spec(a.shape) for a in args],
        out_specs=[_full_spec((MLA_HEADS, m, KV_LORA)), _full_spec((MLA_HEADS, m, LANES)),
                   _full_spec((m, LATENT_WIDTH))],
        out_shape=[jax.ShapeDtypeStruct((MLA_HEADS, m, KV_LORA), F32),
                   jax.ShapeDtypeStruct((MLA_HEADS, m, LANES), F32),
                   jax.ShapeDtypeStruct((m, LATENT_WIDTH), F32)],
        compiler_params=_params("arbitrary"),
        name="mla_pre_sample",
    )(*args)


ATTN_HEADS_PER_STEP = 8


def _attn_kernel(q_ref, k_ref, vt_ref, o_ref, *, tq):
    qi = pl.program_id(2)
    hs = range(ATTN_HEADS_PER_STEP)
    q_t = [q_ref[0, :, g * QK_PAD:(g + 1) * QK_PAD].astype(F32).T.astype(BF16) for g in hs]
    key = lax.broadcasted_iota(jnp.int32, (tq, tq), 0)
    qry = lax.broadcasted_iota(jnp.int32, (tq, tq), 1)

    def raw_scores(j):
        start = pl.multiple_of(j * tq, tq)
        return tuple(_dot(k_ref[0, pl.ds(start, tq), g * QK_PAD:(g + 1) * QK_PAD], q_t[g])
                     for g in hs)

    def consume(j, s_raw, stats, diagonal):
        vt = [vt_ref[0, j, g * V_DIM:(g + 1) * V_DIM, :] for g in hs]
        s = list(s_raw)
        if diagonal:
            s = [jnp.where(key <= qry, s[g], -jnp.inf) for g in hs]
        m_new = [jnp.maximum(stats[g][0], jnp.max(s[g], axis=0, keepdims=True)) for g in hs]
        p = [jnp.exp2(s[g] - m_new[g]) for g in hs]
        pv = [_dot(vt[g], p[g].astype(BF16)) for g in hs]
        out = []
        for g in hs:
            m, l, acc = stats[g]
            alpha = jnp.exp2(m - m_new[g])
            out.append((m_new[g], alpha * l + jnp.sum(p[g], axis=0, keepdims=True),
                        alpha * acc + pv[g]))
        return tuple(out)

    init = tuple((jnp.full((1, tq), -jnp.inf, F32), jnp.zeros((1, tq), F32),
                  jnp.zeros((V_DIM, tq), F32)) for _ in hs)
    stats = lax.fori_loop(0, qi, lambda j, st: consume(j, raw_scores(j), st, False), init)
    stats = consume(qi, raw_scores(qi), stats, True)
    for g in hs:
        _, l, acc = stats[g]
        o_ref[0, :, g * V_DIM:(g + 1) * V_DIM] = (acc / l).T.astype(BF16)


def _mla_prompt_attention(q_cat, k_cat, v_t):
    b, t, _ = q_cat.shape
    tq = min(ATTN_TILE, t)
    hps = ATTN_HEADS_PER_STEP
    return pl.pallas_call(
        functools.partial(_attn_kernel, tq=tq),
        grid=(b, MLA_HEADS // hps, t // tq),
        in_specs=[pl.BlockSpec((1, tq, hps * QK_PAD), lambda i, h, j: (i, j, h)),
                  pl.BlockSpec((1, t, hps * QK_PAD), lambda i, h, j: (i, 0, h)),
                  pl.BlockSpec((1, t // tq, hps * V_DIM, tq), lambda i, h, j: (i, 0, h, 0))],
        out_specs=pl.BlockSpec((1, tq, hps * V_DIM), lambda i, h, j: (i, j, h)),
        out_shape=jax.ShapeDtypeStruct((b, t, MLA_HEADS * V_DIM), BF16),
        compiler_params=_params("parallel", "parallel", "arbitrary"),
        name="mla_prompt_attention",
    )(q_cat, k_cat, v_t)


def _dec_attn_kernel(pt_ref, *refs, pps, nj):
    del pt_ref
    page_refs = refs[:pps]
    (wukt_ref, qa_ref, qr_ref, lnew_ref, ctx_ref,
     lhs_ref, qr16_ref, m_ref, l_ref, acc_ref) = refs[pps:]
    step = pl.program_id(0)
    b = step // nj
    j = step % nj
    n_k = MLA_HEADS * NOPE_DIM

    @pl.when((b == 0) & (j == 0))
    def _():
        lhs_ref[0:n_k, :] = wukt_ref[...]

    @pl.when(j == 0)
    def _():
        qa = jnp.concatenate([qa_ref[h, 0] for h in range(MLA_HEADS)]
                             + [jnp.zeros((8, KV_LORA), F32)], axis=0)
        lhs_ref[n_k:n_k + 16, :] = qa.astype(BF16)
        qr = jnp.concatenate([qr_ref[h, 0] for h in range(MLA_HEADS)]
                             + [jnp.zeros((8, LANES), F32)], axis=0)
        qr16_ref[...] = qr.astype(BF16)
        m_ref[...] = jnp.full(m_ref.shape, -jnp.inf, F32)
        l_ref[...] = jnp.zeros(l_ref.shape, F32)
        acc_ref[...] = jnp.zeros(acc_ref.shape, F32)

    def scores(res, sr):
        kt = res[0:n_k]
        ss = jnp.sum((kt * kt).reshape(MLA_HEADS, NOPE_DIM, res.shape[1]), axis=1)
        sn = res[n_k:n_k + MLA_HEADS]
        return (sn * lax.rsqrt(ss * (1.0 / NOPE_DIM) + EPS) + sr[0:MLA_HEADS]) * MLA_SCALE

    c_tok, s_parts = [], []
    for i in range(0, pps, 2):
        tile = jnp.concatenate([page_refs[i][...], page_refs[i + 1][...]], axis=1)
        c_bf = tile[0:KV_LORA].astype(BF16)
        kr_bf = tile[KV_LORA:LATENT_WIDTH].astype(BF16)
        s_parts.append(scores(_dot(lhs_ref[...], c_bf),
                              _dot(qr16_ref[:, 0:ROPE_DIM], kr_bf)))
        c_tok.append(tile[0:KV_LORA].T.astype(BF16))
    s = jnp.concatenate(s_parts, axis=1)
    m_old = m_ref[...]
    m_new = jnp.maximum(m_old, jnp.max(s, axis=-1, keepdims=True))
    alpha = jnp.exp(m_old - m_new)
    p = jnp.exp(s - m_new)
    l_ref[...] = alpha * l_ref[...] + jnp.sum(p, axis=-1, keepdims=True)
    p_bf = p.astype(BF16)
    n_tok = c_tok[0].shape[0]
    acc = alpha * acc_ref[...]
    for t, c_t in enumerate(c_tok):
        acc = acc + _dot(p_bf[:, t * n_tok:(t + 1) * n_tok], c_t)
    acc_ref[...] = acc
    m_ref[...] = m_new

    @pl.when(j == nj - 1)
    def _():
        ln = lnew_ref[0]
        c_new = ln[:, 0:KV_LORA]
        c8 = jnp.broadcast_to(c_new, (8, KV_LORA)).astype(BF16)
        kr8 = jnp.broadcast_to(ln[:, KV_LORA:LATENT_WIDTH], (8, ROPE_DIM)).astype(BF16)
        s_new = scores(_dot_nt(lhs_ref[...], c8),
                       _dot_nt(qr16_ref[:, 0:ROPE_DIM], kr8))[:, 0:1]
        m_old = m_ref[...]
        m_new = jnp.maximum(m_old, s_new)
        alpha = jnp.exp(m_old - m_new)
        p_new = jnp.exp(s_new - m_new)
        ctx = (alpha * acc_ref[...] + p_new * c_new) / (alpha * l_ref[...] + p_new)
        for h in range(MLA_HEADS):
            ctx_ref[h, 0] = ctx[h:h + 1]


def _dec_pages_per_step(n_pages):
    pps = DEC_PAGES_PER_STEP
    while n_pages % pps:
        pps //= 2
    assert pps >= 2
    return pps


def _dec_scratch():
    return [pltpu.VMEM((MLA_HEADS * NOPE_DIM + 16, KV_LORA), BF16),
            pltpu.VMEM((16, LANES), BF16),
            pltpu.VMEM((MLA_HEADS, 1), F32),
            pltpu.VMEM((MLA_HEADS, 1), F32),
            pltpu.VMEM((MLA_HEADS, KV_LORA), F32)]


def _mla_decode_attention(cache, layer, page_table, wuk_t, qa, qr, lat_new):
    bsz, n_pages = page_table.shape
    page = cache.shape[3]
    pps = _dec_pages_per_step(n_pages)
    qa4 = qa.reshape(MLA_HEADS, bsz, 1, KV_LORA)
    qr4 = qr.reshape(MLA_HEADS, bsz, 1, LANES)
    ln3 = lat_new.reshape(bsz, 1, LATENT_WIDTH)

    nj = n_pages // pps
    def page_spec(i):
        return pl.BlockSpec((pl.Squeezed(), pl.Squeezed(), LATENT_WIDTH, page),
                            lambda s, pt: (layer, pt[s * pps + i], 0, 0))

    grid_spec = pltpu.PrefetchScalarGridSpec(
        num_scalar_prefetch=1,
        grid=(bsz * nj,),
        in_specs=[page_spec(i) for i in range(pps)] + [
            pl.BlockSpec(wuk_t.shape, lambda s, pt: (0, 0)),
            pl.BlockSpec((MLA_HEADS, 1, 1, KV_LORA), lambda s, pt: (0, s // nj, 0, 0)),
            pl.BlockSpec((MLA_HEADS, 1, 1, LANES), lambda s, pt: (0, s // nj, 0, 0)),
            pl.BlockSpec((1, 1, LATENT_WIDTH), lambda s, pt: (s // nj, 0, 0))],
        out_specs=pl.BlockSpec((MLA_HEADS, 1, 1, KV_LORA), lambda s, pt: (0, s // nj, 0, 0)),
        scratch_shapes=_dec_scratch())
    ctx = pl.pallas_call(
        functools.partial(_dec_attn_kernel, pps=pps, nj=nj),
        grid_spec=grid_spec,
        out_shape=jax.ShapeDtypeStruct((MLA_HEADS, bsz, 1, KV_LORA), F32),
        compiler_params=_params("arbitrary"),
        name="mla_decode_attention",
    )(page_table.reshape(bsz * n_pages), *([cache] * pps), wuk_t, qa4, qr4, ln3)
    return ctx.reshape(MLA_HEADS, bsz, KV_LORA)


def _dec_post_kernel(ctx_ref, wuv_ref, y_ref):
    for h in range(MLA_HEADS):
        hi, lo = _split2(ctx_ref[h])
        w_h = wuv_ref[:, h * V_DIM:(h + 1) * V_DIM]
        y_ref[:, h * V_DIM:(h + 1) * V_DIM] = (_dot(hi, w_h) + _dot(lo, w_h)).astype(BF16)


def _dec_post(ctx, w_uv):
    bsz = ctx.shape[1]
    return pl.pallas_call(
        _dec_post_kernel,
        grid=(1,),
        in_specs=[_full_spec(ctx.shape), _full_spec(w_uv.shape)],
        out_specs=_full_spec((bsz, MLA_HEADS * V_DIM)),
        out_shape=jax.ShapeDtypeStruct((bsz, MLA_HEADS * V_DIM), BF16),
        compiler_params=_params("arbitrary"),
        name="mla_decode_values",
    )(ctx, w_uv)


def _merge_mlp_kernel(x_ref, ya_ref, yb_ref, gmix_ref, wg_ref, wya_ref, wyb_ref, wo_ref,
                      gmlp_ref, wup_ref, wdn_ref, y_ref):
    tm, d = x_ref.shape
    d_ff = wup_ref.shape[1]
    n_grp = max(1, tm // MERGE_ROWS_PER_GROUP)
    rows = tm // n_grp
    grp = range(n_grp)
    sl = [slice(r * rows, (r + 1) * rows) for r in grp]
    x = [x_ref[sl[r], :] for r in grp]
    xn = [_rms(x[r], gmix_ref[...]).astype(BF16) for r in grp]
    gates = [_sigmoid(_dot(xn[r], wg_ref[...])) for r in grp]
    pa = [_dot(ya_ref[sl[r], :].astype(BF16), wya_ref[...]) for r in grp]
    pb = [_dot(yb_ref[sl[r], :].astype(BF16), wyb_ref[...]) for r in grp]
    mix = [(gates[r][:, :d] * pa[r] + gates[r][:, d:] * pb[r]).astype(BF16) for r in grp]
    x1 = [x[r] + _dot(mix[r], wo_ref[...]) for r in grp]
    hin = [_rms(x1[r], gmlp_ref[...]).astype(BF16) for r in grp]
    y = x1
    for f in range(0, d_ff, MLP_FF_BLOCK):
        hmid = [jnp.maximum(_dot(hin[r], wup_ref[:, f:f + MLP_FF_BLOCK]), 0.0) for r in grp]
        y = [y[r] + _dot((hmid[r] * hmid[r]).astype(BF16), wdn_ref[f:f + MLP_FF_BLOCK, :])
             for r in grp]
    for r in grp:
        y_ref[sl[r], :] = y[r]


MERGE_ROWS_PER_GROUP = 256
MLP_FF_BLOCK = 1024


def _merge_mlp(x2, ya, yb, ew):
    m, d = x2.shape
    tm = min(512, m)
    consts = [ew["g_mix"], ew["w_gates"], ew["w_ya"], ew["w_yb"], ew["w_o"], ew["g_mlp"],
              ew["w_up"], ew["w_down"]]

    def const_spec(a):
        nd = a.ndim
        return pl.BlockSpec(a.shape, lambda i: (0,) * nd, pipeline_mode=pl.Buffered(1))

    return pl.pallas_call(
        _merge_mlp_kernel,
        grid=(m // tm,),
        in_specs=[pl.BlockSpec((tm, d), lambda i: (i, 0)),
                  pl.BlockSpec((tm, ya.shape[1]), lambda i: (i, 0)),
                  pl.BlockSpec((tm, yb.shape[1]), lambda i: (i, 0))]
                 + [const_spec(a) for a in consts],
        out_specs=pl.BlockSpec((tm, d), lambda i: (i, 0)),
        out_shape=jax.ShapeDtypeStruct((m, d), F32),
        compiler_params=_params("parallel"),
        name="merge_mlp",
    )(x2, ya, yb, *consts)


def _rope_tables(pos):
    inv = jnp.power(ROPE_THETA, -jnp.arange(0, ROPE_DIM, 2, dtype=F32) / ROPE_DIM)
    ang = pos.astype(F32)[:, None] * inv[None, :]
    cos, sin = jnp.cos(ang), jnp.sin(ang)
    zeros = jnp.zeros((pos.shape[0], LANES - ROPE_DIM), F32)
    return (jnp.concatenate([cos, cos, zeros], axis=1),
            jnp.concatenate([-sin, sin, zeros], axis=1))


def _pad_lanes(a, width=LANES, offset=0):
    return jnp.zeros((1, width), F32).at[0, offset:offset + a.shape[0]].set(a.astype(F32))


def _layer_weights(l, norm_mix, w_in, conv_w, a_log, dt_bias, gdn_norm, q_a_norm, w_uq,
                   kv_a_norm, w_uk, w_uv, q_norm_nope, q_norm_rope, k_norm_nope, k_norm_rope,
                   w_ya, w_yb, w_o, norm_mlp, w_up, w_down):
    d = w_in.shape[1]
    off_z = QKV_WIDTH
    off_b = off_z + Z_WIDTH
    off_qd = off_b + 2 * GDN_HEADS
    off_kv = off_qd + Q_LORA
    off_g = off_kv + LATENT_WIDTH
    wi = w_in[l]
    g_mix = norm_mix[l].reshape(1, d)
    gw = {
        "g_mix": g_mix,
        "w_in": wi.astype(BF16),
        "conv_w": conv_w[l],
        "alog_row": _pad_lanes(a_log[l], offset=GDN_HEADS),
        "dt_row": _pad_lanes(dt_bias[l], offset=GDN_HEADS),
        "gdn_norm": gdn_norm[l].reshape(1, GDN_DV),
    }
    wq = w_uq[l].reshape(Q_LORA, MLA_HEADS, NOPE_DIM + ROPE_DIM)
    wq = jnp.pad(wq, ((0, 0), (0, 0), (0, QK_PAD - NOPE_DIM - ROPE_DIM)))
    mw = {
        "g_mix": g_mix,
        "w_down": jnp.pad(wi[:, off_qd:off_g], ((0, 0), (0, LANES - ROPE_DIM))).astype(BF16),
        "g_qa": q_a_norm[l].reshape(1, Q_LORA),
        "w_uq": wq.reshape(Q_LORA, MLA_HEADS * QK_PAD).astype(BF16),
        "g_kva": kv_a_norm[l].reshape(1, KV_LORA),
        "g_qn": q_norm_nope[l].reshape(1, NOPE_DIM),
        "g_qr": _pad_lanes(q_norm_rope[l]),
        "g_kr": _pad_lanes(k_norm_rope[l]),
        "g_kn": k_norm_nope[l].reshape(1, NOPE_DIM),
        "w_uk": w_uk[l].astype(BF16),
        "w_uk_t": w_uk[l].T.astype(BF16),
        "w_uv": w_uv[l].astype(BF16),
        "w_uv_t": w_uv[l].T.astype(BF16),
    }
    ew = {
        "g_mix": g_mix,
        "w_gates": wi[:, off_g:].astype(BF16),
        "w_ya": w_ya[l].astype(BF16),
        "w_yb": w_yb[l].astype(BF16),
        "w_o": w_o[l].astype(BF16),
        "g_mlp": norm_mlp[l].reshape(1, d),
        "w_up": w_up[l].astype(BF16),
        "w_down": w_down[l].astype(BF16),
    }
    return gw, mw, ew


def kernel(x_prompt, x_sample, cache_mla, state_gdn, state_conv, page_table, norm_mix, w_in,
           conv_w, a_log, dt_bias, gdn_norm, q_a_norm, w_uq, kv_a_norm, w_uk, w_uv,
           q_norm_nope, q_norm_rope, k_norm_nope, k_norm_rope, w_ya, w_yb, w_o, norm_mlp,
           w_up, w_down):
    depth = w_in.shape[0]
    bp, tp, d = x_prompt.shape
    bs, ts, _ = x_sample.shape
    assert ts == 1 and tp % GDN_CHUNK == 0
    past_len = page_table.shape[1] * cache_mla.shape[2]
    cache_t = jnp.swapaxes(cache_mla, 2, 3)
    cos_p, sin_p = _rope_tables(jnp.arange(tp))
    cos_s, sin_s = _rope_tables(past_len + jnp.zeros((bs,), jnp.int32))

    x_p, x_s = x_prompt, x_sample.reshape(bs, d)
    rows_p, gdn_p, conv_p, rows_s, gdn_s, conv_s = [], [], [], [], [], []
    for l in range(depth):
        gw, mw, ew = _layer_weights(
            l, norm_mix, w_in, conv_w, a_log, dt_bias, gdn_norm, q_a_norm, w_uq, kv_a_norm,
            w_uk, w_uv, q_norm_nope, q_norm_rope, k_norm_nope, k_norm_rope, w_ya, w_yb, w_o,
            norm_mlp, w_up, w_down)
        gdn_consts = (gw["conv_w"], gw["alog_row"], gw["dt_row"], gw["gdn_norm"])

        xp2 = x_p.reshape(bp * tp, d)
        yc, z, ba, cs_p = _gdn_proj_conv(x_p, gw["g_mix"], gw["w_in"], gw["conv_w"])
        ya_p, s_p = _gdn_chunked(yc.reshape(bp, tp, QKV_WIDTH), z.reshape(bp, tp, Z_WIDTH),
                                 ba.reshape(bp, tp, LANES), *gdn_consts[1:])
        q_cat, k_cat, v_p, lat_p = _mla_pre(x_p, cos_p, sin_p, mw)
        yb_p = _mla_prompt_attention(q_cat, k_cat, v_p)
        y_p = _merge_mlp(xp2, ya_p.reshape(bp * tp, Z_WIDTH), yb_p.reshape(bp * tp, -1), ew)

        qkv, z, ba = _gdn_proj(x_s, gw["g_mix"], gw["w_in"])
        ya_s, s_s, cs_s = _gdn_step(qkv, z, ba, state_conv[l], state_gdn[l], *gdn_consts)
        qa, qr, lat_s = _mla_pre_sample(x_s, cos_s, sin_s, mw)
        ctx = _mla_decode_attention(cache_t, l, page_table, mw["w_uk_t"], qa, qr, lat_s)
        yb_s = _dec_post(ctx, mw["w_uv"])
        y_s = _merge_mlp(x_s, ya_s.reshape(bs, Z_WIDTH), yb_s, ew)

        x_p, x_s = y_p.reshape(bp, tp, d), y_s
        rows_p.append(lat_p)
        gdn_p.append(s_p)
        conv_p.append(cs_p)
        rows_s.append(lat_s.reshape(bs, 1, LATENT_WIDTH))
        gdn_s.append(s_s)
        conv_s.append(cs_s)
    return (x_p, x_s.reshape(bs, 1, d), jnp.stack(rows_p), jnp.stack(gdn_p), jnp.stack(conv_p),
            jnp.stack(rows_s), jnp.stack(gdn_s), jnp.stack(conv_s))
```

```python
import functools

import jax
import jax.numpy as jnp
from jax import lax
from jax.experimental import pallas as pl
from jax.experimental.pallas import tpu as pltpu

F32 = jnp.float32
BF16 = jnp.bfloat16

EPS = 1e-6
GDN_HEADS = 8
GDN_DK = 128
GDN_DV = 128
CONV_W = 4
GDN_CHUNK = 64
MLA_HEADS = 8
Q_LORA = 512
KV_LORA = 512
NOPE_DIM = 128
ROPE_DIM = 64
V_DIM = 128
ROPE_THETA = 10000.0
MLA_SCALE = (NOPE_DIM + ROPE_DIM) ** -0.5
LOG2_E = 1.4426950408889634
ATTN_Q_SCALE = MLA_SCALE * LOG2_E
LATENT_WIDTH = KV_LORA + ROPE_DIM
QKV_WIDTH = GDN_HEADS * (2 * GDN_DK + GDN_DV)
Z_WIDTH = GDN_HEADS * GDN_DV

LANES = 128
QK_PAD = 256
VMEM_LIMIT = 56 * 1024 * 1024
DEC_PAGES_PER_STEP = 64
ATTN_TILE = 256
MLA_PRE_TILE = 512


def _dot(a, b):
    return jnp.dot(a, b, preferred_element_type=F32)


def _dot_nt(a, b):
    return lax.dot_general(a, b, (((1,), (1,)), ((), ())), preferred_element_type=F32)


def _dot_tn(a, b):
    return lax.dot_general(a, b, (((0,), (0,)), ((), ())), preferred_element_type=F32)


def _split2(x):
    hi = x.astype(BF16)
    lo = (x - hi.astype(F32)).astype(BF16)
    return hi, lo


def _split3(x):
    hi = x.astype(BF16)
    r = x - hi.astype(F32)
    mid = r.astype(BF16)
    lo = (r - mid.astype(F32)).astype(BF16)
    return hi, mid, lo


def _rms(x, g, n=None):
    n = x.shape[-1] if n is None else n
    ss = jnp.sum(x * x, axis=-1, keepdims=True) * (1.0 / n)
    return x * lax.rsqrt(ss + EPS) * g


def _sigmoid(x):
    return 1.0 / (1.0 + jnp.exp(-x))


def _silu(x):
    h = 0.5 * x
    return h + h * jnp.tanh(h)


def _softplus(x):
    return jnp.maximum(x, 0.0) + jnp.log1p(jnp.exp(-jnp.abs(x)))


def _rope(x, cos2, sin2):
    lane = lax.broadcasted_iota(jnp.int32, x.shape, 1)
    fwd = pltpu.roll(x, LANES - ROPE_DIM // 2, axis=1)
    bwd = pltpu.roll(x, ROPE_DIM // 2, axis=1)
    swapped = jnp.where(lane < ROPE_DIM // 2, fwd, bwd)
    return x * cos2 + swapped * sin2


def _full_spec(shape):
    nd = len(shape)
    return pl.BlockSpec(shape, lambda *_: (0,) * nd)


def _params(*sem):
    return pltpu.CompilerParams(dimension_semantics=sem, vmem_limit_bytes=VMEM_LIMIT)


OFF_Z = QKV_WIDTH
OFF_BA = QKV_WIDTH + Z_WIDTH
GDN_IN_COLS = OFF_BA + LANES


def _gdn_w_spec(w_in_bf):
    return pl.BlockSpec((w_in_bf.shape[0], GDN_IN_COLS), lambda i: (0, 0))


def _gdn_proj_kernel(x_ref, g_ref, w_ref, qkv_ref, z_ref, ba_ref):
    xn = _rms(x_ref[...], g_ref[...]).astype(BF16)
    qkv_ref[...] = _dot(xn, w_ref[:, 0:OFF_Z])
    z_ref[...] = _dot(xn, w_ref[:, OFF_Z:OFF_BA])
    ba_ref[...] = _dot(xn, w_ref[:, OFF_BA:GDN_IN_COLS])


def _gdn_proj(x2, g, w_in_bf):
    m, d = x2.shape
    tm = min(256, m)
    return pl.pallas_call(
        _gdn_proj_kernel,
        grid=(m // tm,),
        in_specs=[pl.BlockSpec((tm, d), lambda i: (i, 0)),
                  _full_spec(g.shape), _gdn_w_spec(w_in_bf)],
        out_specs=[pl.BlockSpec((tm, QKV_WIDTH), lambda i: (i, 0)),
                   pl.BlockSpec((tm, Z_WIDTH), lambda i: (i, 0)),
                   pl.BlockSpec((tm, LANES), lambda i: (i, 0))],
        out_shape=[jax.ShapeDtypeStruct((m, QKV_WIDTH), F32),
                   jax.ShapeDtypeStruct((m, Z_WIDTH), F32),
                   jax.ShapeDtypeStruct((m, LANES), F32)],
        compiler_params=_params("parallel"),
        name="gdn_proj",
    )(x2, g, w_in_bf)


PROJ_CONV_COLS = 512
assert CONV_W == 4


def _gdn_proj_conv_kernel(x_ref, g_ref, w_ref, cw_ref,
                          y_ref, z_ref, ba_ref, cs_ref, xp_ref, *, tiles_per_seq):
    i = pl.program_id(0)
    tm = x_ref.shape[0]

    @pl.when(i % tiles_per_seq == 0)
    def _():
        xp_ref[...] = jnp.zeros((8, QKV_WIDTH), F32)

    xn = _rms(x_ref[...], g_ref[...]).astype(BF16)
    row8 = lax.broadcasted_iota(jnp.int32, (8, PROJ_CONV_COLS), 0)
    for c0 in range(0, QKV_WIDTH, PROJ_CONV_COLS):
        cb = slice(c0, c0 + PROJ_CONV_COLS)
        raw = _dot(xn, w_ref[:, cb])
        halo = xp_ref[:, cb]
        cw = cw_ref[:, cb]
        back2 = pltpu.roll(raw, 2, axis=0)
        x2 = jnp.concatenate([jnp.where(row8 < 2, pltpu.roll(halo, 2, axis=0), back2[0:8]),
                              back2[8:]], axis=0)
        u = raw * cw[2:3] + x2 * cw[0:1]
        u_prev = halo[7:8] * cw[2:3] + halo[5:6] * cw[0:1]
        back1 = pltpu.roll(u, 1, axis=0)
        su = jnp.concatenate([jnp.where(row8 < 1, u_prev, back1[0:8]), back1[8:]], axis=0)
        y_ref[:, cb] = _silu(raw * cw[3:4] + x2 * cw[1:2] + su)
        xp_ref[:, cb] = raw[tm - 8:tm]
    z_ref[...] = _dot(xn, w_ref[:, OFF_Z:OFF_BA])
    ba_ref[...] = _dot(xn, w_ref[:, OFF_BA:GDN_IN_COLS])

    @pl.when(i % tiles_per_seq == tiles_per_seq - 1)
    def _():
        cs_ref[0] = xp_ref[8 - (CONV_W - 1):8, :]


def _gdn_proj_conv(x, g, w_in_bf, conv_w):
    b, t, d = x.shape
    tm = min(512, t)
    assert t % tm == 0
    tiles_per_seq = t // tm
    m = b * t
    return pl.pallas_call(
        functools.partial(_gdn_proj_conv_kernel, tiles_per_seq=tiles_per_seq),
        grid=(m // tm,),
        in_specs=[pl.BlockSpec((tm, d), lambda i: (i, 0)),
                  _full_spec(g.shape), _gdn_w_spec(w_in_bf), _full_spec(conv_w.shape)],
        out_specs=[pl.BlockSpec((tm, QKV_WIDTH), lambda i: (i, 0)),
                   pl.BlockSpec((tm, Z_WIDTH), lambda i: (i, 0)),
                   pl.BlockSpec((tm, LANES), lambda i: (i, 0)),
                   pl.BlockSpec((1, CONV_W - 1, QKV_WIDTH), lambda i: (i // tiles_per_seq, 0, 0))],
        out_shape=[jax.ShapeDtypeStruct((m, QKV_WIDTH), F32),
                   jax.ShapeDtypeStruct((m, Z_WIDTH), F32),
                   jax.ShapeDtypeStruct((m, LANES), F32),
                   jax.ShapeDtypeStruct((b, CONV_W - 1, QKV_WIDTH), F32)],
        scratch_shapes=[pltpu.VMEM((8, QKV_WIDTH), F32)],
        compiler_params=_params("arbitrary"),
        name="gdn_proj_conv",
    )(x.reshape(m, d), g, w_in_bf, conv_w)


def _gate_rows(ba, alog_row, dt_row):
    lane = lax.broadcasted_iota(jnp.int32, ba.shape, 1)
    is_a = (lane >= GDN_HEADS) & (lane < 2 * GDN_HEADS)
    g = jnp.where(is_a, -jnp.exp(alog_row) * _softplus(ba + dt_row), 0.0)
    return _sigmoid(ba), g


def _gdn_chunk_kernel(y_ref, z_ref, ba_ref, alog_ref, dt_ref, gn_ref, ya_ref, s_ref):
    c = pl.program_id(1)
    C = GDN_CHUNK
    TB = y_ref.shape[1]
    n_sub = TB // C

    @pl.when(c == 0)
    def _():
        s_ref[...] = jnp.zeros(s_ref.shape, F32)

    y = y_ref[0]

    beta_full, g_full = _gate_rows(ba_ref[0], alog_ref[...], dt_ref[...])
    rtb = lax.broadcasted_iota(jnp.int32, (TB, TB), 0)
    ctb = lax.broadcasted_iota(jnp.int32, (TB, TB), 1)
    tril = jnp.where((rtb >= ctb) & (rtb // C == ctb // C), 1.0, 0.0).astype(BF16)
    g_col = sum(_dot(tril, part) for part in _split3(g_full))
    r128 = lax.broadcasted_iota(jnp.int32, (LANES, LANES), 0)
    c128 = lax.broadcasted_iota(jnp.int32, (LANES, LANES), 1)
    eye = jnp.where(r128 == c128, 1.0, 0.0).astype(BF16)
    g_rows = sum(_dot_nt(eye, part) for part in _split3(g_col))

    z = z_ref[0]
    gn = gn_ref[...]
    heads = range(GDN_HEADS)
    SB = min(TB, C * GDN_CHUNKS_PER_BLOCK)
    n_blk = TB // SB
    per_blk = SB // C
    rsb = lax.broadcasted_iota(jnp.int32, (SB, SB), 0)
    csb = lax.broadcasted_iota(jnp.int32, (SB, SB), 1)
    same_chunk = (rsb // C) == (csb // C)
    incl = same_chunk & (rsb >= csb)
    strict = same_chunk & (rsb > csb)
    chunk_of_row = lax.broadcasted_iota(jnp.int32, (SB, 1), 0) // C
    units = [(b, h) for b in range(n_blk) for h in heads]
    q, k, gc, eg, g_last, rhs_bf, rhs, decay, kq, kd = ({} for _ in range(10))
    for u in units:
        b, h = u
        r = slice(b * SB, (b + 1) * SB)
        qh = y[r, h * GDN_DK:(h + 1) * GDN_DK]
        kh = y[r, GDN_HEADS * GDN_DK + h * GDN_DK:GDN_HEADS * GDN_DK + (h + 1) * GDN_DK]
        vh = y[r, 2 * GDN_HEADS * GDN_DK + h * GDN_DV:2 * GDN_HEADS * GDN_DK + (h + 1) * GDN_DV]
        q[u] = qh * lax.rsqrt(jnp.sum(qh * qh, axis=-1, keepdims=True) + EPS) * (GDN_DK ** -0.5)
        k[u] = kh * lax.rsqrt(jnp.sum(kh * kh, axis=-1, keepdims=True) + EPS)
        beta = beta_full[r, h:h + 1]
        gc[u] = g_col[r, GDN_HEADS + h:GDN_HEADS + h + 1]
        gr = g_rows[GDN_HEADS + h:GDN_HEADS + h + 1, r]
        decay[u] = jnp.where(incl, jnp.exp(jnp.where(incl, gc[u] - gr, 0.0)), 0.0)
        eg[u] = jnp.exp(gc[u])
        g_last[u] = [gc[u][(n + 1) * C - 1:(n + 1) * C, :] for n in range(per_blk)]
        g_end = g_last[u][0]
        for n in range(1, per_blk):
            g_end = jnp.where(chunk_of_row >= n, g_last[u][n], g_end)
        kd[u] = (k[u] * jnp.exp(g_end - gc[u])).astype(BF16)
        kb = k[u] * beta
        rhs[u] = jnp.concatenate([vh * beta, kb * eg[u]], axis=1)
        rhs_bf[u] = rhs[u].astype(BF16)
        kq[u] = _dot_nt(jnp.concatenate([kb, q[u]], axis=0).astype(BF16),
                        k[u].astype(BF16))
    pw = {u: jnp.where(strict, kq[u][:SB] * decay[u], 0.0) for u in units}
    aqk = {u: (kq[u][SB:] * decay[u]).astype(BF16) for u in units}
    n_acc = {u: -pw[u] for u in units}
    pw_bf = {u: pw[u].astype(BF16) for u in units}
    for _ in range(C.bit_length() - 2):
        pw = {u: _dot(pw_bf[u], pw_bf[u]) for u in units}
        pw_bf = {u: pw[u].astype(BF16) for u in units}
        n_acc = {u: n_acc[u] + pw[u] + _dot(n_acc[u].astype(BF16), pw_bf[u]) for u in units}
    sol = {u: rhs[u] + _dot(n_acc[u].astype(BF16), rhs_bf[u]) for u in units}
    qg = {u: q[u] * eg[u] for u in units}
    s_cur = [s_ref[0, h] for h in heads]
    zero_rows = jnp.zeros((C, GDN_DV), BF16)
    for n in range(n_sub):
        b, ln = divmod(n, per_blk)
        rows = slice(ln * C, (ln + 1) * C)
        tok = slice(n * C, (n + 1) * C)
        ws = [_dot(jnp.concatenate([sol[(b, h)][rows, GDN_DV:], qg[(b, h)][rows]],
                                   axis=0).astype(BF16),
                   s_cur[h].astype(BF16)) for h in heads]
        u_bf = [(sol[(b, h)][rows, :GDN_DV] - ws[h][:C]).astype(BF16) for h in heads]
        u_pad = [jnp.concatenate([zero_rows] * ln + [u_bf[h]] + [zero_rows] * (per_blk - 1 - ln),
                                 axis=0) for h in heads]
        o = [ws[h][C:] + _dot(aqk[(b, h)][rows], u_pad[h]) for h in heads]
        s_cur = [jnp.exp(g_last[(b, h)][ln]) * s_cur[h] + _dot_tn(kd[(b, h)][rows], u_bf[h])
                 for h in heads]
        for h in heads:
            zh = z[tok, h * GDN_DV:(h + 1) * GDN_DV]
            ya_ref[0, tok, h * GDN_DV:(h + 1) * GDN_DV] = (
                _rms(o[h], gn) * _silu(zh)).astype(BF16)
    for h in heads:
        s_ref[0, h] = s_cur[h]


GDN_CHUNKS_PER_STEP = 4
GDN_CHUNKS_PER_BLOCK = 2


def _gdn_chunked(y, z, ba, alog_row, dt_row, gdn_norm):
    b, t, _ = y.shape
    C = GDN_CHUNK * GDN_CHUNKS_PER_STEP
    assert t % C == 0
    nc = t // C
    return pl.pallas_call(
        _gdn_chunk_kernel,
        grid=(b, nc),
        in_specs=[pl.BlockSpec((1, C, QKV_WIDTH), lambda i, j: (i, j, 0)),
                  pl.BlockSpec((1, C, Z_WIDTH), lambda i, j: (i, j, 0)),
                  pl.BlockSpec((1, C, LANES), lambda i, j: (i, j, 0)),
                  _full_spec(alog_row.shape), _full_spec(dt_row.shape),
                  _full_spec(gdn_norm.shape)],
        out_specs=[pl.BlockSpec((1, C, Z_WIDTH), lambda i, j: (i, j, 0)),
                   pl.BlockSpec((1, GDN_HEADS, GDN_DK, GDN_DV), lambda i, j: (i, 0, 0, 0))],
        out_shape=[jax.ShapeDtypeStruct((b, t, Z_WIDTH), BF16),
                   jax.ShapeDtypeStruct((b, GDN_HEADS, GDN_DK, GDN_DV), F32)],
        compiler_params=_params("parallel", "arbitrary"),
        name="gdn_chunked",
    )(y, z, ba, alog_row, dt_row, gdn_norm)


def _gdn_step_kernel(qkv_ref, z_ref, ba_ref, cs_ref, s_ref, cw_ref, alog_ref, dt_ref, gn_ref,
                     ya_ref, snew_ref, csnew_ref):
    cw = cw_ref[...]
    r128 = lax.broadcasted_iota(jnp.int32, (GDN_DK, GDN_DK), 0)
    c128 = lax.broadcasted_iota(jnp.int32, (GDN_DK, GDN_DK), 1)
    eye = r128 == c128
    gn = gn_ref[...]
    heads = range(GDN_HEADS)

    def to_col(r):
        return jnp.sum(jnp.where(eye, jnp.broadcast_to(r, (GDN_DK, GDN_DK)), 0.0),
                       axis=1, keepdims=True)

    for sq in range(qkv_ref.shape[0]):
        rowv = qkv_ref[sq]
        cs = cs_ref[sq]
        y = rowv * cw[CONV_W - 1:CONV_W]
        for i in range(CONV_W - 1):
            y = y + cs[i:i + 1] * cw[i:i + 1]
        y = _silu(y)
        csnew_ref[sq, 0:CONV_W - 2, :] = cs[1:CONV_W - 1]
        csnew_ref[sq, CONV_W - 2:CONV_W - 1, :] = rowv
        beta_full, g_full = _gate_rows(ba_ref[sq], alog_ref[...], dt_ref[...])
        z = z_ref[sq]
        q, k, v = [], [], []
        for h in heads:
            qh = y[:, h * GDN_DK:(h + 1) * GDN_DK]
            kh = y[:, GDN_HEADS * GDN_DK + h * GDN_DK:GDN_HEADS * GDN_DK + (h + 1) * GDN_DK]
            v.append(y[:, 2 * GDN_HEADS * GDN_DK + h * GDN_DV:
                       2 * GDN_HEADS * GDN_DK + (h + 1) * GDN_DV])
            q.append(qh * lax.rsqrt(jnp.sum(qh * qh, axis=-1, keepdims=True) + EPS)
                     * (GDN_DK ** -0.5))
            k.append(kh * lax.rsqrt(jnp.sum(kh * kh, axis=-1, keepdims=True) + EPS))
        k_col = [to_col(k[h]) for h in heads]
        q_col = [to_col(q[h]) for h in heads]
        s = [jnp.exp(g_full[:, GDN_HEADS + h:GDN_HEADS + h + 1]) * s_ref[sq, h] for h in heads]
        u = [beta_full[:, h:h + 1] * (v[h] - jnp.sum(s[h] * k_col[h], axis=0, keepdims=True))
             for h in heads]
        s = [s[h] + k_col[h] * u[h] for h in heads]
        o = [jnp.sum(s[h] * q_col[h], axis=0, keepdims=True) for h in heads]
        for h in heads:
            snew_ref[sq, h] = s[h]
            zh = z[:, h * GDN_DV:(h + 1) * GDN_DV]
            ya_ref[sq, :, h * GDN_DV:(h + 1) * GDN_DV] = _rms(o[h], gn) * _silu(zh)


GDN_STEP_SEQS = 8


def _gdn_step(qkv, z, ba, conv_state, state, conv_w, alog_row, dt_row, gdn_norm):
    b = qkv.shape[0]
    ns = GDN_STEP_SEQS if b % GDN_STEP_SEQS == 0 else 1
    qkv3 = qkv.reshape(b, 1, QKV_WIDTH)
    z3 = z.reshape(b, 1, Z_WIDTH)
    ba3 = ba.reshape(b, 1, LANES)
    return pl.pallas_call(
        _gdn_step_kernel,
        grid=(b // ns,),
        in_specs=[pl.BlockSpec((ns, 1, QKV_WIDTH), lambda i: (i, 0, 0)),
                  pl.BlockSpec((ns, 1, Z_WIDTH), lambda i: (i, 0, 0)),
                  pl.BlockSpec((ns, 1, LANES), lambda i: (i, 0, 0)),
                  pl.BlockSpec((ns, CONV_W - 1, QKV_WIDTH), lambda i: (i, 0, 0)),
                  pl.BlockSpec((ns, GDN_HEADS, GDN_DK, GDN_DV), lambda i: (i, 0, 0, 0)),
                  _full_spec(conv_w.shape), _full_spec(alog_row.shape),
                  _full_spec(dt_row.shape), _full_spec(gdn_norm.shape)],
        out_specs=[pl.BlockSpec((ns, 1, Z_WIDTH), lambda i: (i, 0, 0)),
                   pl.BlockSpec((ns, GDN_HEADS, GDN_DK, GDN_DV), lambda i: (i, 0, 0, 0)),
                   pl.BlockSpec((ns, CONV_W - 1, QKV_WIDTH), lambda i: (i, 0, 0))],
        out_shape=[jax.ShapeDtypeStruct((b, 1, Z_WIDTH), F32),
                   jax.ShapeDtypeStruct((b, GDN_HEADS, GDN_DK, GDN_DV), F32),
                   jax.ShapeDtypeStruct((b, CONV_W - 1, QKV_WIDTH), F32)],
        compiler_params=_params("parallel"),
        name="gdn_step",
    )(qkv3, z3, ba3, conv_state, state, conv_w, alog_row, dt_row, gdn_norm)


def _mla_queries_latent(x, cos2, sin2, g_mix, w_down, g_qa, w_uq, g_kva, g_qn, g_qr, g_kr):
    xn = _rms(x, g_mix).astype(BF16)
    qa = _rms(_dot(xn, w_down[:, 0:Q_LORA]), g_qa).astype(BF16)
    q = _dot(qa, w_uq)
    qn, qr = [], []
    for h in range(MLA_HEADS):
        qn.append(_rms(q[:, h * QK_PAD:h * QK_PAD + NOPE_DIM], g_qn))
        qr.append(_rope(_rms(q[:, h * QK_PAD + NOPE_DIM:(h + 1) * QK_PAD], g_qr, ROPE_DIM),
                        cos2, sin2))
    c = _rms(_dot(xn, w_down[:, Q_LORA:Q_LORA + KV_LORA]), g_kva)
    kr = _rope(_rms(_dot(xn, w_down[:, Q_LORA + KV_LORA:]), g_kr, ROPE_DIM), cos2, sin2)
    return qn, qr, c, kr


def _mla_pre_kernel(x_ref, cos_ref, sin_ref, gmix_ref, wdown_ref, gqa_ref,
                    wuq_ref, gkva_ref, gqn_ref, gqr_ref, gkr_ref, wuk_ref, wuv_ref, gkn_ref,
                    q_ref, k_ref, v_ref, lat_ref):
    qn, qr, c, kr = _mla_queries_latent(
        x_ref[0], cos_ref[...], sin_ref[...], gmix_ref[...], wdown_ref[...],
        gqa_ref[...], wuq_ref[...], gkva_ref[...], gqn_ref[...] * ATTN_Q_SCALE,
        gqr_ref[...] * ATTN_Q_SCALE, gkr_ref[...])
    lat_ref[0, :, 0:KV_LORA] = c
    lat_ref[0, :, KV_LORA:LATENT_WIDTH] = kr[:, 0:ROPE_DIM]
    c_bf = c.astype(BF16)
    kr_bf = kr.astype(BF16)
    kfull = _dot(c_bf, wuk_ref[...])
    v_t = _dot_nt(wuv_ref[...], c_bf).astype(BF16)
    for g in range(v_ref.shape[1]):
        v_ref[0, g] = v_t[:, g * ATTN_TILE:(g + 1) * ATTN_TILE]
    gkn = gkn_ref[...]
    for h in range(MLA_HEADS):
        q_ref[0, :, h * QK_PAD:h * QK_PAD + NOPE_DIM] = qn[h].astype(BF16)
        q_ref[0, :, h * QK_PAD + NOPE_DIM:(h + 1) * QK_PAD] = qr[h].astype(BF16)
        kn = _rms(kfull[:, h * NOPE_DIM:(h + 1) * NOPE_DIM], gkn)
        k_ref[0, :, h * QK_PAD:h * QK_PAD + NOPE_DIM] = kn.astype(BF16)
        k_ref[0, :, h * QK_PAD + NOPE_DIM:(h + 1) * QK_PAD] = kr_bf


def _mla_pre(x, cos2, sin2, mw):
    b, t, d = x.shape
    ta = min(ATTN_TILE, t)
    tm = min(MLA_PRE_TILE, t)
    assert tm % ta == 0 and t % tm == 0
    consts = [mw["g_mix"], mw["w_down"], mw["g_qa"], mw["w_uq"],
              mw["g_kva"], mw["g_qn"], mw["g_qr"], mw["g_kr"], mw["w_uk"], mw["w_uv_t"],
              mw["g_kn"]]
    return pl.pallas_call(
        _mla_pre_kernel,
        grid=(b, t // tm),
        in_specs=[pl.BlockSpec((1, tm, d), lambda i, j: (i, j, 0)),
                  pl.BlockSpec((tm, LANES), lambda i, j: (j, 0)),
                  pl.BlockSpec((tm, LANES), lambda i, j: (j, 0))]
                 + [_full_spec(a.shape) for a in consts],
        out_specs=[pl.BlockSpec((1, tm, MLA_HEADS * QK_PAD), lambda i, j: (i, j, 0)),
                   pl.BlockSpec((1, tm, MLA_HEADS * QK_PAD), lambda i, j: (i, j, 0)),
                   pl.BlockSpec((1, tm // ta, MLA_HEADS * V_DIM, ta), lambda i, j: (i, j, 0, 0)),
                   pl.BlockSpec((1, tm, LATENT_WIDTH), lambda i, j: (i, j, 0))],
        out_shape=[jax.ShapeDtypeStruct((b, t, MLA_HEADS * QK_PAD), BF16),
                   jax.ShapeDtypeStruct((b, t, MLA_HEADS * QK_PAD), BF16),
                   jax.ShapeDtypeStruct((b, t // ta, MLA_HEADS * V_DIM, ta), BF16),
                   jax.ShapeDtypeStruct((b, t, LATENT_WIDTH), F32)],
        compiler_params=_params("parallel", "parallel"),
        name="mla_pre",
    )(x, cos2, sin2, *consts)


def _mla_pre_sample_kernel(x_ref, cos_ref, sin_ref, gmix_ref, wdown_ref, gqa_ref,
                           wuq_ref, gkva_ref, gqn_ref, gqr_ref, gkr_ref, wuk_ref, gkn_ref,
                           qa_ref, qr_ref, lat_ref):
    qn, qr, c, kr = _mla_queries_latent(
        x_ref[...], cos_ref[...], sin_ref[...], gmix_ref[...], wdown_ref[...],
        gqa_ref[...], wuq_ref[...], gkva_ref[...], gqn_ref[...], gqr_ref[...],
        gkr_ref[...])
    lat_ref[:, 0:KV_LORA] = c
    lat_ref[:, KV_LORA:LATENT_WIDTH] = kr[:, 0:ROPE_DIM]
    gkn = gkn_ref[...]
    for h in range(MLA_HEADS):
        hi, lo = _split2(qn[h] * gkn)
        w_h = wuk_ref[:, h * NOPE_DIM:(h + 1) * NOPE_DIM]
        qa_ref[h] = _dot_nt(hi, w_h) + _dot_nt(lo, w_h)
        qr_ref[h] = qr[h]


def _mla_pre_sample(x2, cos2, sin2, mw):
    m = x2.shape[0]
    consts = [mw["g_mix"], mw["w_down"], mw["g_qa"], mw["w_uq"],
              mw["g_kva"], mw["g_qn"], mw["g_qr"], mw["g_kr"], mw["w_uk"], mw["g_kn"]]
    args = [x2, cos2, sin2] + consts
    return pl.pallas_call(
        _mla_pre_sample_kernel,
        grid=(1,),
        in_specs=[_full_spec(a.shape) for a in args],
        out_specs=[_full_spec((MLA_HEADS, m, KV_LORA)), _full_spec((MLA_HEADS, m, LANES)),
                   _full_spec((m, LATENT_WIDTH))],
        out_shape=[jax.ShapeDtypeStruct((MLA_HEADS, m, KV_LORA), F32),
                   jax.ShapeDtypeStruct((MLA_HEADS, m, LANES), F32),
                   jax.ShapeDtypeStruct((m, LATENT_WIDTH), F32)],
        compiler_params=_params("arbitrary"),
        name="mla_pre_sample",
    )(*args)


ATTN_HEADS_PER_STEP = 8


def _attn_kernel(q_ref, k_ref, vt_ref, o_ref, *, tq):
    qi = pl.program_id(2)
    hs = range(ATTN_HEADS_PER_STEP)
    q_t = [q_ref[0, :, g * QK_PAD:(g + 1) * QK_PAD].astype(F32).T.astype(BF16) for g in hs]
    key = lax.broadcasted_iota(jnp.int32, (tq, tq), 0)
    qry = lax.broadcasted_iota(jnp.int32, (tq, tq), 1)

    def raw_scores(j):
        start = pl.multiple_of(j * tq, tq)
        return tuple(_dot(k_ref[0, pl.ds(start, tq), g * QK_PAD:(g + 1) * QK_PAD], q_t[g])
                     for g in hs)

    def consume(j, s_raw, stats, diagonal):
        vt = [vt_ref[0, j, g * V_DIM:(g + 1) * V_DIM, :] for g in hs]
        s = list(s_raw)
        if diagonal:
            s = [jnp.where(key <= qry, s[g], -jnp.inf) for g in hs]
        m_new = [jnp.maximum(stats[g][0], jnp.max(s[g], axis=0, keepdims=True)) for g in hs]
        p = [jnp.exp2(s[g] - m_new[g]).astype(BF16) for g in hs]
        pv = [_dot(jnp.concatenate([vt[g], ones_rows], axis=0), p[g]) for g in hs]
        out = []
        for g in hs:
            m, acc = stats[g]
            out.append((m_new[g], jnp.exp2(m - m_new[g]) * acc + pv[g]))
        return tuple(out)

    ones_rows = jnp.ones((16, tq), BF16)
    init = tuple((jnp.full((1, tq), -jnp.inf, F32), jnp.zeros((V_DIM + 16, tq), F32))
                 for _ in hs)
    stats = lax.fori_loop(0, qi, lambda j, st: consume(j, raw_scores(j), st, False), init)
    stats = consume(qi, raw_scores(qi), stats, True)
    for g in hs:
        _, acc = stats[g]
        o_ref[0, :, g * V_DIM:(g + 1) * V_DIM] = (
            acc[0:V_DIM] / acc[V_DIM:V_DIM + 1]).T.astype(BF16)


def _mla_prompt_attention(q_cat, k_cat, v_t):
    b, t, _ = q_cat.shape
    tq = min(ATTN_TILE, t)
    hps = ATTN_HEADS_PER_STEP
    return pl.pallas_call(
        functools.partial(_attn_kernel, tq=tq),
        grid=(b, MLA_HEADS // hps, t // tq),
        in_specs=[pl.BlockSpec((1, tq, hps * QK_PAD), lambda i, h, j: (i, j, h)),
                  pl.BlockSpec((1, t, hps * QK_PAD), lambda i, h, j: (i, 0, h)),
                  pl.BlockSpec((1, t // tq, hps * V_DIM, tq), lambda i, h, j: (i, 0, h, 0))],
        out_specs=pl.BlockSpec((1, tq, hps * V_DIM), lambda i, h, j: (i, j, h)),
        out_shape=jax.ShapeDtypeStruct((b, t, MLA_HEADS * V_DIM), BF16),
        compiler_params=_params("parallel", "parallel", "arbitrary"),
        name="mla_prompt_attention",
    )(q_cat, k_cat, v_t)


def _dec_attn_kernel(pt_ref, *refs, pps, nj):
    del pt_ref
    page_refs = refs[:pps]
    (wukt_ref, qa_ref, qr_ref, lnew_ref, ctx_ref,
     lhs_ref, qr16_ref, m_ref, l_ref, acc_ref) = refs[pps:]
    step = pl.program_id(0)
    b = step // nj
    j = step % nj
    n_k = MLA_HEADS * NOPE_DIM

    @pl.when((b == 0) & (j == 0))
    def _():
        lhs_ref[0:n_k, :] = wukt_ref[...]

    @pl.when(j == 0)
    def _():
        qa = jnp.concatenate([qa_ref[h, 0] for h in range(MLA_HEADS)]
                             + [jnp.zeros((8, KV_LORA), F32)], axis=0)
        lhs_ref[n_k:n_k + 16, :] = qa.astype(BF16)
        qr = jnp.concatenate([qr_ref[h, 0] for h in range(MLA_HEADS)]
                             + [jnp.zeros((8, LANES), F32)], axis=0)
        qr16_ref[...] = qr.astype(BF16)
        m_ref[...] = jnp.full(m_ref.shape, -jnp.inf, F32)
        l_ref[...] = jnp.zeros(l_ref.shape, F32)
        acc_ref[...] = jnp.zeros(acc_ref.shape, F32)

    def scores(res, sr):
        kt = res[0:n_k]
        ss = jnp.sum((kt * kt).reshape(MLA_HEADS, NOPE_DIM, res.shape[1]), axis=1)
        sn = res[n_k:n_k + MLA_HEADS]
        return (sn * lax.rsqrt(ss * (1.0 / NOPE_DIM) + EPS) + sr[0:MLA_HEADS]) * MLA_SCALE

    c_tok, s_parts = [], []
    for i in range(0, pps, 2):
        tile = jnp.concatenate([page_refs[i][...], page_refs[i + 1][...]], axis=1)
        c_bf = tile[0:KV_LORA].astype(BF16)
        kr_bf = tile[KV_LORA:LATENT_WIDTH].astype(BF16)
        s_parts.append(scores(_dot(lhs_ref[...], c_bf),
                              _dot(qr16_ref[:, 0:ROPE_DIM], kr_bf)))
        c_tok.append(tile[0:KV_LORA].T.astype(BF16))
    s = jnp.concatenate(s_parts, axis=1)
    m_old = m_ref[...]
    m_new = jnp.maximum(m_old, jnp.max(s, axis=-1, keepdims=True))
    alpha = jnp.exp(m_old - m_new)
    p = jnp.exp(s - m_new)
    l_ref[...] = alpha * l_ref[...] + jnp.sum(p, axis=-1, keepdims=True)
    p_bf = p.astype(BF16)
    n_tok = c_tok[0].shape[0]
    acc = alpha * acc_ref[...]
    for t, c_t in enumerate(c_tok):
        acc = acc + _dot(p_bf[:, t * n_tok:(t + 1) * n_tok], c_t)
    acc_ref[...] = acc
    m_ref[...] = m_new

    @pl.when(j == nj - 1)
    def _():
        ln = lnew_ref[0]
        c_new = ln[:, 0:KV_LORA]
        c8 = jnp.broadcast_to(c_new, (8, KV_LORA)).astype(BF16)
        kr8 = jnp.broadcast_to(ln[:, KV_LORA:LATENT_WIDTH], (8, ROPE_DIM)).astype(BF16)
        s_new = scores(_dot_nt(lhs_ref[...], c8),
                       _dot_nt(qr16_ref[:, 0:ROPE_DIM], kr8))[:, 0:1]
        m_old = m_ref[...]
        m_new = jnp.maximum(m_old, s_new)
        alpha = jnp.exp(m_old - m_new)
        p_new = jnp.exp(s_new - m_new)
        ctx = (alpha * acc_ref[...] + p_new * c_new) / (alpha * l_ref[...] + p_new)
        for h in range(MLA_HEADS):
            ctx_ref[h, 0] = ctx[h:h + 1]


def _dec_pages_per_step(n_pages):
    pps = DEC_PAGES_PER_STEP
    while n_pages % pps:
        pps //= 2
    assert pps >= 2
    return pps


def _dec_scratch():
    return [pltpu.VMEM((MLA_HEADS * NOPE_DIM + 16, KV_LORA), BF16),
            pltpu.VMEM((16, LANES), BF16),
            pltpu.VMEM((MLA_HEADS, 1), F32),
            pltpu.VMEM((MLA_HEADS, 1), F32),
            pltpu.VMEM((MLA_HEADS, KV_LORA), F32)]


def _mla_decode_attention(cache, layer, page_table, wuk_t, qa, qr, lat_new):
    bsz, n_pages = page_table.shape
    page = cache.shape[3]
    pps = _dec_pages_per_step(n_pages)
    qa4 = qa.reshape(MLA_HEADS, bsz, 1, KV_LORA)
    qr4 = qr.reshape(MLA_HEADS, bsz, 1, LANES)
    ln3 = lat_new.reshape(bsz, 1, LATENT_WIDTH)

    nj = n_pages // pps
    def page_spec(i):
        return pl.BlockSpec((pl.Squeezed(), pl.Squeezed(), LATENT_WIDTH, page),
                            lambda s, pt: (layer, pt[s * pps + i], 0, 0))

    grid_spec = pltpu.PrefetchScalarGridSpec(
        num_scalar_prefetch=1,
        grid=(bsz * nj,),
        in_specs=[page_spec(i) for i in range(pps)] + [
            pl.BlockSpec(wuk_t.shape, lambda s, pt: (0, 0)),
            pl.BlockSpec((MLA_HEADS, 1, 1, KV_LORA), lambda s, pt: (0, s // nj, 0, 0)),
            pl.BlockSpec((MLA_HEADS, 1, 1, LANES), lambda s, pt: (0, s // nj, 0, 0)),
            pl.BlockSpec((1, 1, LATENT_WIDTH), lambda s, pt: (s // nj, 0, 0))],
        out_specs=pl.BlockSpec((MLA_HEADS, 1, 1, KV_LORA), lambda s, pt: (0, s // nj, 0, 0)),
        scratch_shapes=_dec_scratch())
    ctx = pl.pallas_call(
        functools.partial(_dec_attn_kernel, pps=pps, nj=nj),
        grid_spec=grid_spec,
        out_shape=jax.ShapeDtypeStruct((MLA_HEADS, bsz, 1, KV_LORA), F32),
        compiler_params=_params("arbitrary"),
        name="mla_decode_attention",
    )(page_table.reshape(bsz * n_pages), *([cache] * pps), wuk_t, qa4, qr4, ln3)
    return ctx.reshape(MLA_HEADS, bsz, KV_LORA)


def _dec_post_kernel(ctx_ref, wuv_ref, y_ref):
    for h in range(MLA_HEADS):
        hi, lo = _split2(ctx_ref[h])
        w_h = wuv_ref[:, h * V_DIM:(h + 1) * V_DIM]
        y_ref[:, h * V_DIM:(h + 1) * V_DIM] = (_dot(hi, w_h) + _dot(lo, w_h)).astype(BF16)


def _dec_post(ctx, w_uv):
    bsz = ctx.shape[1]
    return pl.pallas_call(
        _dec_post_kernel,
        grid=(1,),
        in_specs=[_full_spec(ctx.shape), _full_spec(w_uv.shape)],
        out_specs=_full_spec((bsz, MLA_HEADS * V_DIM)),
        out_shape=jax.ShapeDtypeStruct((bsz, MLA_HEADS * V_DIM), BF16),
        compiler_params=_params("arbitrary"),
        name="mla_decode_values",
    )(ctx, w_uv)


def _merge_mlp_kernel(x_ref, ya_ref, yb_ref, gmix_ref, wg_ref, wya_ref, wyb_ref, wo_ref,
                      gmlp_ref, wup_ref, wdn_ref, y_ref):
    tm, d = x_ref.shape
    d_ff = wup_ref.shape[1]
    n_grp = max(1, tm // MERGE_ROWS_PER_GROUP)
    rows = tm // n_grp
    grp = range(n_grp)
    sl = [slice(r * rows, (r + 1) * rows) for r in grp]
    x = [x_ref[sl[r], :] for r in grp]
    xn = [_rms(x[r], gmix_ref[...]).astype(BF16) for r in grp]
    gates = [_sigmoid(_dot(xn[r], wg_ref[...])) for r in grp]
    pa = [_dot(ya_ref[sl[r], :].astype(BF16), wya_ref[...]) for r in grp]
    pb = [_dot(yb_ref[sl[r], :].astype(BF16), wyb_ref[...]) for r in grp]
    mix = [(gates[r][:, :d] * pa[r] + gates[r][:, d:] * pb[r]).astype(BF16) for r in grp]
    x1 = [x[r] + _dot(mix[r], wo_ref[...]) for r in grp]
    hin = [_rms(x1[r], gmlp_ref[...]).astype(BF16) for r in grp]
    y = x1
    for f in range(0, d_ff, MLP_FF_BLOCK):
        hmid = [jnp.maximum(_dot(hin[r], wup_ref[:, f:f + MLP_FF_BLOCK]), 0.0) for r in grp]
        y = [y[r] + _dot((hmid[r] * hmid[r]).astype(BF16), wdn_ref[f:f + MLP_FF_BLOCK, :])
             for r in grp]
    for r in grp:
        y_ref[sl[r], :] = y[r]


MERGE_ROWS_PER_GROUP = 256
MLP_FF_BLOCK = 1024


def _merge_mlp(x2, ya, yb, ew):
    m, d = x2.shape
    tm = min(512, m)
    consts = [ew["g_mix"], ew["w_gates"], ew["w_ya"], ew["w_yb"], ew["w_o"], ew["g_mlp"],
              ew["w_up"], ew["w_down"]]

    def const_spec(a):
        nd = a.ndim
        return pl.BlockSpec(a.shape, lambda i: (0,) * nd, pipeline_mode=pl.Buffered(1))

    return pl.pallas_call(
        _merge_mlp_kernel,
        grid=(m // tm,),
        in_specs=[pl.BlockSpec((tm, d), lambda i: (i, 0)),
                  pl.BlockSpec((tm, ya.shape[1]), lambda i: (i, 0)),
                  pl.BlockSpec((tm, yb.shape[1]), lambda i: (i, 0))]
                 + [const_spec(a) for a in consts],
        out_specs=pl.BlockSpec((tm, d), lambda i: (i, 0)),
        out_shape=jax.ShapeDtypeStruct((m, d), F32),
        compiler_params=_params("parallel"),
        name="merge_mlp",
    )(x2, ya, yb, *consts)


def _rope_tables(pos):
    inv = jnp.power(ROPE_THETA, -jnp.arange(0, ROPE_DIM, 2, dtype=F32) / ROPE_DIM)
    ang = pos.astype(F32)[:, None] * inv[None, :]
    cos, sin = jnp.cos(ang), jnp.sin(ang)
    zeros = jnp.zeros((pos.shape[0], LANES - ROPE_DIM), F32)
    return (jnp.concatenate([cos, cos, zeros], axis=1),
            jnp.concatenate([-sin, sin, zeros], axis=1))


def _pad_lanes(a, width=LANES, offset=0):
    return jnp.zeros((1, width), F32).at[0, offset:offset + a.shape[0]].set(a.astype(F32))


def _layer_weights(l, norm_mix, w_in, conv_w, a_log, dt_bias, gdn_norm, q_a_norm, w_uq,
                   kv_a_norm, w_uk, w_uv, q_norm_nope, q_norm_rope, k_norm_nope, k_norm_rope,
                   w_ya, w_yb, w_o, norm_mlp, w_up, w_down):
    d = w_in.shape[1]
    off_z = QKV_WIDTH
    off_b = off_z + Z_WIDTH
    off_qd = off_b + 2 * GDN_HEADS
    off_kv = off_qd + Q_LORA
    off_g = off_kv + LATENT_WIDTH
    wi = w_in[l]
    g_mix = norm_mix[l].reshape(1, d)
    gw = {
        "g_mix": g_mix,
        "w_in": wi.astype(BF16),
        "conv_w": conv_w[l],
        "alog_row": _pad_lanes(a_log[l], offset=GDN_HEADS),
        "dt_row": _pad_lanes(dt_bias[l], offset=GDN_HEADS),
        "gdn_norm": gdn_norm[l].reshape(1, GDN_DV),
    }
    wq = w_uq[l].reshape(Q_LORA, MLA_HEADS, NOPE_DIM + ROPE_DIM)
    wq = jnp.pad(wq, ((0, 0), (0, 0), (0, QK_PAD - NOPE_DIM - ROPE_DIM)))
    mw = {
        "g_mix": g_mix,
        "w_down": jnp.pad(wi[:, off_qd:off_g], ((0, 0), (0, LANES - ROPE_DIM))).astype(BF16),
        "g_qa": q_a_norm[l].reshape(1, Q_LORA),
        "w_uq": wq.reshape(Q_LORA, MLA_HEADS * QK_PAD).astype(BF16),
        "g_kva": kv_a_norm[l].reshape(1, KV_LORA),
        "g_qn": q_norm_nope[l].reshape(1, NOPE_DIM),
        "g_qr": _pad_lanes(q_norm_rope[l]),
        "g_kr": _pad_lanes(k_norm_rope[l]),
        "g_kn": k_norm_nope[l].reshape(1, NOPE_DIM),
        "w_uk": w_uk[l].astype(BF16),
        "w_uk_t": w_uk[l].T.astype(BF16),
        "w_uv": w_uv[l].astype(BF16),
        "w_uv_t": w_uv[l].T.astype(BF16),
    }
    ew = {
        "g_mix": g_mix,
        "w_gates": wi[:, off_g:].astype(BF16),
        "w_ya": w_ya[l].astype(BF16),
        "w_yb": w_yb[l].astype(BF16),
        "w_o": w_o[l].astype(BF16),
        "g_mlp": norm_mlp[l].reshape(1, d),
        "w_up": w_up[l].astype(BF16),
        "w_down": w_down[l].astype(BF16),
    }
    return gw, mw, ew


def kernel(x_prompt, x_sample, cache_mla, state_gdn, state_conv, page_table, norm_mix, w_in,
           conv_w, a_log, dt_bias, gdn_norm, q_a_norm, w_uq, kv_a_norm, w_uk, w_uv,
           q_norm_nope, q_norm_rope, k_norm_nope, k_norm_rope, w_ya, w_yb, w_o, norm_mlp,
           w_up, w_down):
    depth = w_in.shape[0]
    bp, tp, d = x_prompt.shape
    bs, ts, _ = x_sample.shape
    assert ts == 1 and tp % GDN_CHUNK == 0
    past_len = page_table.shape[1] * cache_mla.shape[2]
    cache_t = jnp.swapaxes(cache_mla, 2, 3)
    cos_p, sin_p = _rope_tables(jnp.arange(tp))
    cos_s, sin_s = _rope_tables(past_len + jnp.zeros((bs,), jnp.int32))

    x_p, x_s = x_prompt, x_sample.reshape(bs, d)
    rows_p, gdn_p, conv_p, rows_s, gdn_s, conv_s = [], [], [], [], [], []
    for l in range(depth):
        gw, mw, ew = _layer_weights(
            l, norm_mix, w_in, conv_w, a_log, dt_bias, gdn_norm, q_a_norm, w_uq, kv_a_norm,
            w_uk, w_uv, q_norm_nope, q_norm_rope, k_norm_nope, k_norm_rope, w_ya, w_yb, w_o,
            norm_mlp, w_up, w_down)
        gdn_consts = (gw["conv_w"], gw["alog_row"], gw["dt_row"], gw["gdn_norm"])

        xp2 = x_p.reshape(bp * tp, d)
        yc, z, ba, cs_p = _gdn_proj_conv(x_p, gw["g_mix"], gw["w_in"], gw["conv_w"])
        ya_p, s_p = _gdn_chunked(yc.reshape(bp, tp, QKV_WIDTH), z.reshape(bp, tp, Z_WIDTH),
                                 ba.reshape(bp, tp, LANES), *gdn_consts[1:])
        q_cat, k_cat, v_p, lat_p = _mla_pre(x_p, cos_p, sin_p, mw)
        yb_p = _mla_prompt_attention(q_cat, k_cat, v_p)
        y_p = _merge_mlp(xp2, ya_p.reshape(bp * tp, Z_WIDTH), yb_p.reshape(bp * tp, -1), ew)

        qkv, z, ba = _gdn_proj(x_s, gw["g_mix"], gw["w_in"])
        ya_s, s_s, cs_s = _gdn_step(qkv, z, ba, state_conv[l], state_gdn[l], *gdn_consts)
        qa, qr, lat_s = _mla_pre_sample(x_s, cos_s, sin_s, mw)
        ctx = _mla_decode_attention(cache_t, l, page_table, mw["w_uk_t"], qa, qr, lat_s)
        yb_s = _dec_post(ctx, mw["w_uv"])
        y_s = _merge_mlp(x_s, ya_s.reshape(bs, Z_WIDTH), yb_s, ew)

        x_p, x_s = y_p.reshape(bp, tp, d), y_s
        rows_p.append(lat_p)
        gdn_p.append(s_p)
        conv_p.append(cs_p)
        rows_s.append(lat_s.reshape(bs, 1, LATENT_WIDTH))
        gdn_s.append(s_s)
        conv_s.append(cs_s)
    return (x_p, x_s.reshape(bs, 1, d), jnp.stack(rows_p), jnp.stack(gdn_p), jnp.stack(conv_p),
            jnp.stack(rows_s), jnp.stack(gdn_s), jnp.stack(conv_s))
```

```python
import functools

import jax
import jax.numpy as jnp
from jax import lax
from jax.experimental import pallas as pl
from jax.experimental.pallas import tpu as pltpu

F32 = jnp.float32
BF16 = jnp.bfloat16

EPS = 1e-6
GDN_HEADS = 8
GDN_DK = 128
GDN_DV = 128
CONV_W = 4
GDN_CHUNK = 64
MLA_HEADS = 8
Q_LORA = 512
KV_LORA = 512
NOPE_DIM = 128
ROPE_DIM = 64
V_DIM = 128
ROPE_THETA = 10000.0
MLA_SCALE = (NOPE_DIM + ROPE_DIM) ** -0.5
LOG2_E = 1.4426950408889634
ATTN_Q_SCALE = MLA_SCALE * LOG2_E
LATENT_WIDTH = KV_LORA + ROPE_DIM
QKV_WIDTH = GDN_HEADS * (2 * GDN_DK + GDN_DV)
Z_WIDTH = GDN_HEADS * GDN_DV

LANES = 128
QK_PAD = 256
VMEM_LIMIT = 56 * 1024 * 1024
DEC_PAGES_PER_STEP = 64
ATTN_TILE = 256
MLA_PRE_TILE = 512


def _dot(a, b):
    return jnp.dot(a, b, preferred_element_type=F32)


def _dot_nt(a, b):
    return lax.dot_general(a, b, (((1,), (1,)), ((), ())), preferred_element_type=F32)


def _dot_tn(a, b):
    return lax.dot_general(a, b, (((0,), (0,)), ((), ())), preferred_element_type=F32)


def _split2(x):
    hi = x.astype(BF16)
    lo = (x - hi.astype(F32)).astype(BF16)
    return hi, lo


def _split3(x):
    hi = x.astype(BF16)
    r = x - hi.astype(F32)
    mid = r.astype(BF16)
    lo = (r - mid.astype(F32)).astype(BF16)
    return hi, mid, lo


def _rms(x, g, n=None):
    n = x.shape[-1] if n is None else n
    ss = jnp.sum(x * x, axis=-1, keepdims=True) * (1.0 / n)
    return x * lax.rsqrt(ss + EPS) * g


def _sigmoid(x):
    return 1.0 / (1.0 + jnp.exp(-x))


def _silu_of_half(h):
    return h + h * jnp.tanh(h)


def _silu(x):
    return _silu_of_half(0.5 * x)


def _softplus(x):
    return jnp.maximum(x, 0.0) + jnp.log1p(jnp.exp(-jnp.abs(x)))


def _rope(x, cos2, sin2):
    lane = lax.broadcasted_iota(jnp.int32, x.shape, 1)
    fwd = pltpu.roll(x, LANES - ROPE_DIM // 2, axis=1)
    bwd = pltpu.roll(x, ROPE_DIM // 2, axis=1)
    swapped = jnp.where(lane < ROPE_DIM // 2, fwd, bwd)
    return x * cos2 + swapped * sin2


def _full_spec(shape):
    nd = len(shape)
    return pl.BlockSpec(shape, lambda *_: (0,) * nd)


def _params(*sem):
    return pltpu.CompilerParams(dimension_semantics=sem, vmem_limit_bytes=VMEM_LIMIT)


OFF_Z = QKV_WIDTH
OFF_BA = QKV_WIDTH + Z_WIDTH
GDN_IN_COLS = OFF_BA + LANES


def _gdn_w_spec(w_in_bf):
    return pl.BlockSpec((w_in_bf.shape[0], GDN_IN_COLS), lambda i: (0, 0))


def _gdn_proj_kernel(x_ref, g_ref, w_ref, qkv_ref, z_ref, ba_ref):
    xn = _rms(x_ref[...], g_ref[...]).astype(BF16)
    qkv_ref[...] = _dot(xn, w_ref[:, 0:OFF_Z])
    z_ref[...] = _dot(xn, w_ref[:, OFF_Z:OFF_BA])
    ba_ref[...] = _dot(xn, w_ref[:, OFF_BA:GDN_IN_COLS])


def _gdn_proj(x2, g, w_in_bf):
    m, d = x2.shape
    tm = min(256, m)
    return pl.pallas_call(
        _gdn_proj_kernel,
        grid=(m // tm,),
        in_specs=[pl.BlockSpec((tm, d), lambda i: (i, 0)),
                  _full_spec(g.shape), _gdn_w_spec(w_in_bf)],
        out_specs=[pl.BlockSpec((tm, QKV_WIDTH), lambda i: (i, 0)),
                   pl.BlockSpec((tm, Z_WIDTH), lambda i: (i, 0)),
                   pl.BlockSpec((tm, LANES), lambda i: (i, 0))],
        out_shape=[jax.ShapeDtypeStruct((m, QKV_WIDTH), F32),
                   jax.ShapeDtypeStruct((m, Z_WIDTH), F32),
                   jax.ShapeDtypeStruct((m, LANES), F32)],
        compiler_params=_params("parallel"),
        name="gdn_proj",
    )(x2, g, w_in_bf)


PROJ_CONV_COLS = 512
assert CONV_W == 4


def _gdn_proj_conv_kernel(x_ref, g_ref, w_ref, cw_ref,
                          y_ref, z_ref, ba_ref, cs_ref, xp_ref, *, tiles_per_seq):
    i = pl.program_id(0)
    tm = x_ref.shape[0]

    @pl.when(i % tiles_per_seq == 0)
    def _():
        xp_ref[...] = jnp.zeros((8, QKV_WIDTH), F32)

    xn = _rms(x_ref[...], g_ref[...]).astype(BF16)
    row8 = lax.broadcasted_iota(jnp.int32, (8, PROJ_CONV_COLS), 0)
    for c0 in range(0, QKV_WIDTH, PROJ_CONV_COLS):
        cb = slice(c0, c0 + PROJ_CONV_COLS)
        raw = _dot(xn, w_ref[:, cb])
        halo = xp_ref[:, cb]
        cw = cw_ref[:, cb] * 0.5
        back2 = pltpu.roll(raw, 2, axis=0)
        x2 = jnp.concatenate([jnp.where(row8 < 2, pltpu.roll(halo, 2, axis=0), back2[0:8]),
                              back2[8:]], axis=0)
        u = raw * cw[2:3] + x2 * cw[0:1]
        u_prev = halo[7:8] * cw[2:3] + halo[5:6] * cw[0:1]
        back1 = pltpu.roll(u, 1, axis=0)
        su = jnp.concatenate([jnp.where(row8 < 1, u_prev, back1[0:8]), back1[8:]], axis=0)
        y_ref[:, cb] = _silu_of_half(raw * cw[3:4] + x2 * cw[1:2] + su)
        xp_ref[:, cb] = raw[tm - 8:tm]
    z_ref[...] = _dot(xn, w_ref[:, OFF_Z:OFF_BA])
    ba_ref[...] = _dot(xn, w_ref[:, OFF_BA:GDN_IN_COLS])

    @pl.when(i % tiles_per_seq == tiles_per_seq - 1)
    def _():
        cs_ref[0] = xp_ref[8 - (CONV_W - 1):8, :]


def _gdn_proj_conv(x, g, w_in_bf, conv_w):
    b, t, d = x.shape
    tm = min(512, t)
    assert t % tm == 0
    tiles_per_seq = t // tm
    m = b * t
    return pl.pallas_call(
        functools.partial(_gdn_proj_conv_kernel, tiles_per_seq=tiles_per_seq),
        grid=(m // tm,),
        in_specs=[pl.BlockSpec((tm, d), lambda i: (i, 0)),
                  _full_spec(g.shape), _gdn_w_spec(w_in_bf), _full_spec(conv_w.shape)],
        out_specs=[pl.BlockSpec((tm, QKV_WIDTH), lambda i: (i, 0)),
                   pl.BlockSpec((tm, Z_WIDTH), lambda i: (i, 0)),
                   pl.BlockSpec((tm, LANES), lambda i: (i, 0)),
                   pl.BlockSpec((1, CONV_W - 1, QKV_WIDTH), lambda i: (i // tiles_per_seq, 0, 0))],
        out_shape=[jax.ShapeDtypeStruct((m, QKV_WIDTH), F32),
                   jax.ShapeDtypeStruct((m, Z_WIDTH), F32),
                   jax.ShapeDtypeStruct((m, LANES), F32),
                   jax.ShapeDtypeStruct((b, CONV_W - 1, QKV_WIDTH), F32)],
        scratch_shapes=[pltpu.VMEM((8, QKV_WIDTH), F32)],
        compiler_params=_params("arbitrary"),
        name="gdn_proj_conv",
    )(x.reshape(m, d), g, w_in_bf, conv_w)


def _gate_rows(ba, alog_row, dt_row):
    lane = lax.broadcasted_iota(jnp.int32, ba.shape, 1)
    is_a = (lane >= GDN_HEADS) & (lane < 2 * GDN_HEADS)
    g = jnp.where(is_a, -jnp.exp(alog_row) * _softplus(ba + dt_row), 0.0)
    return _sigmoid(ba), g


def _gdn_chunk_kernel(y_ref, z_ref, ba_ref, alog_ref, dt_ref, gn_ref, ya_ref, s_ref):
    c = pl.program_id(1)
    C = GDN_CHUNK
    TB = y_ref.shape[1]
    n_sub = TB // C

    @pl.when(c == 0)
    def _():
        s_ref[...] = jnp.zeros(s_ref.shape, F32)

    y = y_ref[0]

    beta_full, g_full = _gate_rows(ba_ref[0], alog_ref[...], dt_ref[...])
    rtb = lax.broadcasted_iota(jnp.int32, (TB, TB), 0)
    ctb = lax.broadcasted_iota(jnp.int32, (TB, TB), 1)
    tril = jnp.where((rtb >= ctb) & (rtb // C == ctb // C), 1.0, 0.0).astype(BF16)
    g_col = sum(_dot(tril, part) for part in _split3(g_full))
    r128 = lax.broadcasted_iota(jnp.int32, (LANES, LANES), 0)
    c128 = lax.broadcasted_iota(jnp.int32, (LANES, LANES), 1)
    eye = jnp.where(r128 == c128, 1.0, 0.0).astype(BF16)
    g_rows = sum(_dot_nt(eye, part) for part in _split3(g_col))

    z = z_ref[0]
    gn = gn_ref[...]
    heads = range(GDN_HEADS)
    SB = min(TB, C * GDN_CHUNKS_PER_BLOCK)
    n_blk = TB // SB
    per_blk = SB // C
    rsb = lax.broadcasted_iota(jnp.int32, (SB, SB), 0)
    csb = lax.broadcasted_iota(jnp.int32, (SB, SB), 1)
    same_chunk = (rsb // C) == (csb // C)
    incl = same_chunk & (rsb >= csb)
    strict = same_chunk & (rsb > csb)
    chunk_of_row = lax.broadcasted_iota(jnp.int32, (SB, 1), 0) // C
    units = [(b, h) for b in range(n_blk) for h in heads]
    q, k, gc, eg, g_last, rhs_bf, rhs, decay, kq, kd = ({} for _ in range(10))
    for u in units:
        b, h = u
        r = slice(b * SB, (b + 1) * SB)
        qh = y[r, h * GDN_DK:(h + 1) * GDN_DK]
        kh = y[r, GDN_HEADS * GDN_DK + h * GDN_DK:GDN_HEADS * GDN_DK + (h + 1) * GDN_DK]
        vh = y[r, 2 * GDN_HEADS * GDN_DK + h * GDN_DV:2 * GDN_HEADS * GDN_DK + (h + 1) * GDN_DV]
        q[u] = qh * lax.rsqrt(jnp.sum(qh * qh, axis=-1, keepdims=True) + EPS) * (GDN_DK ** -0.5)
        k[u] = kh * lax.rsqrt(jnp.sum(kh * kh, axis=-1, keepdims=True) + EPS)
        beta = beta_full[r, h:h + 1]
        gc[u] = g_col[r, GDN_HEADS + h:GDN_HEADS + h + 1]
        gr = g_rows[GDN_HEADS + h:GDN_HEADS + h + 1, r]
        decay[u] = jnp.where(incl, jnp.exp(jnp.where(incl, gc[u] - gr, 0.0)), 0.0)
        eg[u] = jnp.exp(gc[u])
        g_last[u] = [gc[u][(n + 1) * C - 1:(n + 1) * C, :] for n in range(per_blk)]
        g_end = g_last[u][0]
        for n in range(1, per_blk):
            g_end = jnp.where(chunk_of_row >= n, g_last[u][n], g_end)
        kd[u] = (k[u] * jnp.exp(g_end - gc[u])).astype(BF16)
        kb = k[u] * beta
        rhs[u] = jnp.concatenate([vh * beta, kb * eg[u]], axis=1)
        rhs_bf[u] = rhs[u].astype(BF16)
        kq[u] = _dot_nt(jnp.concatenate([kb, q[u]], axis=0).astype(BF16),
                        k[u].astype(BF16))
    pw = {u: jnp.where(strict, kq[u][:SB] * decay[u], 0.0) for u in units}
    aqk = {u: (kq[u][SB:] * decay[u]).astype(BF16) for u in units}
    n_acc = {u: -pw[u] for u in units}
    pw_bf = {u: pw[u].astype(BF16) for u in units}
    for _ in range(C.bit_length() - 2):
        pw = {u: _dot(pw_bf[u], pw_bf[u]) for u in units}
        pw_bf = {u: pw[u].astype(BF16) for u in units}
        n_acc = {u: n_acc[u] + pw[u] + _dot(n_acc[u].astype(BF16), pw_bf[u]) for u in units}
    sol = {u: rhs[u] + _dot(n_acc[u].astype(BF16), rhs_bf[u]) for u in units}
    qg = {u: q[u] * eg[u] for u in units}
    s_cur = [s_ref[0, h] for h in heads]
    zero_rows = jnp.zeros((C, GDN_DV), BF16)
    for n in range(n_sub):
        b, ln = divmod(n, per_blk)
        rows = slice(ln * C, (ln + 1) * C)
        tok = slice(n * C, (n + 1) * C)
        ws = [_dot(jnp.concatenate([sol[(b, h)][rows, GDN_DV:], qg[(b, h)][rows]],
                                   axis=0).astype(BF16),
                   s_cur[h].astype(BF16)) for h in heads]
        u_bf = [(sol[(b, h)][rows, :GDN_DV] - ws[h][:C]).astype(BF16) for h in heads]
        u_pad = [jnp.concatenate([zero_rows] * ln + [u_bf[h]] + [zero_rows] * (per_blk - 1 - ln),
                                 axis=0) for h in heads]
        o = [ws[h][C:] + _dot(aqk[(b, h)][rows], u_pad[h]) for h in heads]
        s_cur = [jnp.exp(g_last[(b, h)][ln]) * s_cur[h] + _dot_tn(kd[(b, h)][rows], u_bf[h])
                 for h in heads]
        for h in heads:
            zh = z[tok, h * GDN_DV:(h + 1) * GDN_DV]
            ya_ref[0, tok, h * GDN_DV:(h + 1) * GDN_DV] = (
                _rms(o[h], gn) * _silu(zh)).astype(BF16)
    for h in heads:
        s_ref[0, h] = s_cur[h]


GDN_CHUNKS_PER_STEP = 4
GDN_CHUNKS_PER_BLOCK = 2


def _gdn_chunked(y, z, ba, alog_row, dt_row, gdn_norm):
    b, t, _ = y.shape
    C = GDN_CHUNK * GDN_CHUNKS_PER_STEP
    assert t % C == 0
    nc = t // C
    return pl.pallas_call(
        _gdn_chunk_kernel,
        grid=(b, nc),
        in_specs=[pl.BlockSpec((1, C, QKV_WIDTH), lambda i, j: (i, j, 0)),
                  pl.BlockSpec((1, C, Z_WIDTH), lambda i, j: (i, j, 0)),
                  pl.BlockSpec((1, C, LANES), lambda i, j: (i, j, 0)),
                  _full_spec(alog_row.shape), _full_spec(dt_row.shape),
                  _full_spec(gdn_norm.shape)],
        out_specs=[pl.BlockSpec((1, C, Z_WIDTH), lambda i, j: (i, j, 0)),
                   pl.BlockSpec((1, GDN_HEADS, GDN_DK, GDN_DV), lambda i, j: (i, 0, 0, 0))],
        out_shape=[jax.ShapeDtypeStruct((b, t, Z_WIDTH), BF16),
                   jax.ShapeDtypeStruct((b, GDN_HEADS, GDN_DK, GDN_DV), F32)],
        compiler_params=_params("parallel", "arbitrary"),
        name="gdn_chunked",
    )(y, z, ba, alog_row, dt_row, gdn_norm)


def _gdn_step_kernel(qkv_ref, z_ref, ba_ref, cs_ref, s_ref, cw_ref, alog_ref, dt_ref, gn_ref,
                     ya_ref, snew_ref, csnew_ref):
    cw = cw_ref[...]
    r128 = lax.broadcasted_iota(jnp.int32, (GDN_DK, GDN_DK), 0)
    c128 = lax.broadcasted_iota(jnp.int32, (GDN_DK, GDN_DK), 1)
    eye = r128 == c128
    gn = gn_ref[...]
    heads = range(GDN_HEADS)

    def to_col(r):
        return jnp.sum(jnp.where(eye, jnp.broadcast_to(r, (GDN_DK, GDN_DK)), 0.0),
                       axis=1, keepdims=True)

    for sq in range(qkv_ref.shape[0]):
        rowv = qkv_ref[sq]
        cs = cs_ref[sq]
        y = rowv * cw[CONV_W - 1:CONV_W]
        for i in range(CONV_W - 1):
            y = y + cs[i:i + 1] * cw[i:i + 1]
        y = _silu(y)
        csnew_ref[sq, 0:CONV_W - 2, :] = cs[1:CONV_W - 1]
        csnew_ref[sq, CONV_W - 2:CONV_W - 1, :] = rowv
        beta_full, g_full = _gate_rows(ba_ref[sq], alog_ref[...], dt_ref[...])
        z = z_ref[sq]
        q, k, v = [], [], []
        for h in heads:
            qh = y[:, h * GDN_DK:(h + 1) * GDN_DK]
            kh = y[:, GDN_HEADS * GDN_DK + h * GDN_DK:GDN_HEADS * GDN_DK + (h + 1) * GDN_DK]
            v.append(y[:, 2 * GDN_HEADS * GDN_DK + h * GDN_DV:
                       2 * GDN_HEADS * GDN_DK + (h + 1) * GDN_DV])
            q.append(qh * lax.rsqrt(jnp.sum(qh * qh, axis=-1, keepdims=True) + EPS)
                     * (GDN_DK ** -0.5))
            k.append(kh * lax.rsqrt(jnp.sum(kh * kh, axis=-1, keepdims=True) + EPS))
        k_col = [to_col(k[h]) for h in heads]
        q_col = [to_col(q[h]) for h in heads]
        s = [jnp.exp(g_full[:, GDN_HEADS + h:GDN_HEADS + h + 1]) * s_ref[sq, h] for h in heads]
        u = [beta_full[:, h:h + 1] * (v[h] - jnp.sum(s[h] * k_col[h], axis=0, keepdims=True))
             for h in heads]
        s = [s[h] + k_col[h] * u[h] for h in heads]
        o = [jnp.sum(s[h] * q_col[h], axis=0, keepdims=True) for h in heads]
        for h in heads:
            snew_ref[sq, h] = s[h]
            zh = z[:, h * GDN_DV:(h + 1) * GDN_DV]
            ya_ref[sq, :, h * GDN_DV:(h + 1) * GDN_DV] = _rms(o[h], gn) * _silu(zh)


GDN_STEP_SEQS = 8


def _gdn_step(qkv, z, ba, conv_state, state, conv_w, alog_row, dt_row, gdn_norm):
    b = qkv.shape[0]
    ns = GDN_STEP_SEQS if b % GDN_STEP_SEQS == 0 else 1
    qkv3 = qkv.reshape(b, 1, QKV_WIDTH)
    z3 = z.reshape(b, 1, Z_WIDTH)
    ba3 = ba.reshape(b, 1, LANES)
    return pl.pallas_call(
        _gdn_step_kernel,
        grid=(b // ns,),
        in_specs=[pl.BlockSpec((ns, 1, QKV_WIDTH), lambda i: (i, 0, 0)),
                  pl.BlockSpec((ns, 1, Z_WIDTH), lambda i: (i, 0, 0)),
                  pl.BlockSpec((ns, 1, LANES), lambda i: (i, 0, 0)),
                  pl.BlockSpec((ns, CONV_W - 1, QKV_WIDTH), lambda i: (i, 0, 0)),
                  pl.BlockSpec((ns, GDN_HEADS, GDN_DK, GDN_DV), lambda i: (i, 0, 0, 0)),
                  _full_spec(conv_w.shape), _full_spec(alog_row.shape),
                  _full_spec(dt_row.shape), _full_spec(gdn_norm.shape)],
        out_specs=[pl.BlockSpec((ns, 1, Z_WIDTH), lambda i: (i, 0, 0)),
                   pl.BlockSpec((ns, GDN_HEADS, GDN_DK, GDN_DV), lambda i: (i, 0, 0, 0)),
                   pl.BlockSpec((ns, CONV_W - 1, QKV_WIDTH), lambda i: (i, 0, 0))],
        out_shape=[jax.ShapeDtypeStruct((b, 1, Z_WIDTH), F32),
                   jax.ShapeDtypeStruct((b, GDN_HEADS, GDN_DK, GDN_DV), F32),
                   jax.ShapeDtypeStruct((b, CONV_W - 1, QKV_WIDTH), F32)],
        compiler_params=_params("parallel"),
        name="gdn_step",
    )(qkv3, z3, ba3, conv_state, state, conv_w, alog_row, dt_row, gdn_norm)


def _mla_queries_latent(x, cos2, sin2, g_mix, w_down, g_qa, w_uq, g_kva, g_qn, g_qr, g_kr):
    xn = _rms(x, g_mix).astype(BF16)
    qa = _rms(_dot(xn, w_down[:, 0:Q_LORA]), g_qa).astype(BF16)
    q = _dot(qa, w_uq)
    qn, qr = [], []
    for h in range(MLA_HEADS):
        qn.append(_rms(q[:, h * QK_PAD:h * QK_PAD + NOPE_DIM], g_qn))
        qr.append(_rope(_rms(q[:, h * QK_PAD + NOPE_DIM:(h + 1) * QK_PAD], g_qr, ROPE_DIM),
                        cos2, sin2))
    c = _rms(_dot(xn, w_down[:, Q_LORA:Q_LORA + KV_LORA]), g_kva)
    kr = _rope(_rms(_dot(xn, w_down[:, Q_LORA + KV_LORA:]), g_kr, ROPE_DIM), cos2, sin2)
    return qn, qr, c, kr


def _mla_pre_kernel(x_ref, cos_ref, sin_ref, gmix_ref, wdown_ref, gqa_ref,
                    wuq_ref, gkva_ref, gqn_ref, gqr_ref, gkr_ref, wuk_ref, wuv_ref, gkn_ref,
                    q_ref, k_ref, v_ref, lat_ref):
    qn, qr, c, kr = _mla_queries_latent(
        x_ref[0], cos_ref[...], sin_ref[...], gmix_ref[...], wdown_ref[...],
        gqa_ref[...], wuq_ref[...], gkva_ref[...], gqn_ref[...] * ATTN_Q_SCALE,
        gqr_ref[...] * ATTN_Q_SCALE, gkr_ref[...])
    lat_ref[0, :, 0:KV_LORA] = c
    lat_ref[0, :, KV_LORA:LATENT_WIDTH] = kr[:, 0:ROPE_DIM]
    c_bf = c.astype(BF16)
    kr_bf = kr.astype(BF16)
    kfull = _dot(c_bf, wuk_ref[...])
    v_t = _dot_nt(wuv_ref[...], c_bf).astype(BF16)
    for g in range(v_ref.shape[1]):
        v_ref[0, g] = v_t[:, g * ATTN_TILE:(g + 1) * ATTN_TILE]
    gkn = gkn_ref[...]
    for h in range(MLA_HEADS):
        q_ref[0, :, h * QK_PAD:h * QK_PAD + NOPE_DIM] = qn[h].astype(BF16)
        q_ref[0, :, h * QK_PAD + NOPE_DIM:(h + 1) * QK_PAD] = qr[h].astype(BF16)
        kn = _rms(kfull[:, h * NOPE_DIM:(h + 1) * NOPE_DIM], gkn)
        k_ref[0, :, h * QK_PAD:h * QK_PAD + NOPE_DIM] = kn.astype(BF16)
        k_ref[0, :, h * QK_PAD + NOPE_DIM:(h + 1) * QK_PAD] = kr_bf


def _mla_pre(x, cos2, sin2, mw):
    b, t, d = x.shape
    ta = min(ATTN_TILE, t)
    tm = min(MLA_PRE_TILE, t)
    assert tm % ta == 0 and t % tm == 0
    consts = [mw["g_mix"], mw["w_down"], mw["g_qa"], mw["w_uq"],
              mw["g_kva"], mw["g_qn"], mw["g_qr"], mw["g_kr"], mw["w_uk"], mw["w_uv_t"],
              mw["g_kn"]]
    return pl.pallas_call(
        _mla_pre_kernel,
        grid=(b, t // tm),
        in_specs=[pl.BlockSpec((1, tm, d), lambda i, j: (i, j, 0)),
                  pl.BlockSpec((tm, LANES), lambda i, j: (j, 0)),
                  pl.BlockSpec((tm, LANES), lambda i, j: (j, 0))]
                 + [_full_spec(a.shape) for a in consts],
        out_specs=[pl.BlockSpec((1, tm, MLA_HEADS * QK_PAD), lambda i, j: (i, j, 0)),
                   pl.BlockSpec((1, tm, MLA_HEADS * QK_PAD), lambda i, j: (i, j, 0)),
                   pl.BlockSpec((1, tm // ta, MLA_HEADS * V_DIM, ta), lambda i, j: (i, j, 0, 0)),
                   pl.BlockSpec((1, tm, LATENT_WIDTH), lambda i, j: (i, j, 0))],
        out_shape=[jax.ShapeDtypeStruct((b, t, MLA_HEADS * QK_PAD), BF16),
                   jax.ShapeDtypeStruct((b, t, MLA_HEADS * QK_PAD), BF16),
                   jax.ShapeDtypeStruct((b, t // ta, MLA_HEADS * V_DIM, ta), BF16),
                   jax.ShapeDtypeStruct((b, t, LATENT_WIDTH), F32)],
        compiler_params=_params("parallel", "parallel"),
        name="mla_pre",
    )(x, cos2, sin2, *consts)


def _mla_pre_sample_kernel(x_ref, cos_ref, sin_ref, gmix_ref, wdown_ref, gqa_ref,
                           wuq_ref, gkva_ref, gqn_ref, gqr_ref, gkr_ref, wuk_ref, gkn_ref,
                           qa_ref, qr_ref, lat_ref):
    qn, qr, c, kr = _mla_queries_latent(
        x_ref[...], cos_ref[...], sin_ref[...], gmix_ref[...], wdown_ref[...],
        gqa_ref[...], wuq_ref[...], gkva_ref[...], gqn_ref[...], gqr_ref[...],
        gkr_ref[...])
    lat_ref[:, 0:KV_LORA] = c
    lat_ref[:, KV_LORA:LATENT_WIDTH] = kr[:, 0:ROPE_DIM]
    gkn = gkn_ref[...]
    for h in range(MLA_HEADS):
        hi, lo = _split2(qn[h] * gkn)
        w_h = wuk_ref[:, h * NOPE_DIM:(h + 1) * NOPE_DIM]
        qa_ref[h] = _dot_nt(hi, w_h) + _dot_nt(lo, w_h)
        qr_ref[h] = qr[h]


def _mla_pre_sample(x2, cos2, sin2, mw):
    m = x2.shape[0]
    consts = [mw["g_mix"], mw["w_down"], mw["g_qa"], mw["w_uq"],
              mw["g_kva"], mw["g_qn"], mw["g_qr"], mw["g_kr"], mw["w_uk"], mw["g_kn"]]
    args = [x2, cos2, sin2] + consts
    return pl.pallas_call(
        _mla_pre_sample_kernel,
        grid=(1,),
        in_specs=[_full_spec(a.shape) for a in args],
        out_specs=[_full_spec((MLA_HEADS, m, KV_LORA)), _full_spec((MLA_HEADS, m, LANES)),
                   _full_spec((m, LATENT_WIDTH))],
        out_shape=[jax.ShapeDtypeStruct((MLA_HEADS, m, KV_LORA), F32),
                   jax.ShapeDtypeStruct((MLA_HEADS, m, LANES), F32),
                   jax.ShapeDtypeStruct((m, LATENT_WIDTH), F32)],
        compiler_params=_params("arbitrary"),
        name="mla_pre_sample",
    )(*args)


ATTN_HEADS_PER_STEP = 8


def _attn_kernel(q_ref, k_ref, vt_ref, o_ref, *, tq):
    qi = pl.program_id(2)
    hs = range(ATTN_HEADS_PER_STEP)
    q_t = [q_ref[0, :, g * QK_PAD:(g + 1) * QK_PAD].astype(F32).T.astype(BF16) for g in hs]
    key = lax.broadcasted_iota(jnp.int32, (tq, tq), 0)
    qry = lax.broadcasted_iota(jnp.int32, (tq, tq), 1)

    def raw_scores(j):
        start = pl.multiple_of(j * tq, tq)
        return tuple(_dot(k_ref[0, pl.ds(start, tq), g * QK_PAD:(g + 1) * QK_PAD], q_t[g])
                     for g in hs)

    def consume(j, s_raw, stats, diagonal):
        vt = [vt_ref[0, j, g * V_DIM:(g + 1) * V_DIM, :] for g in hs]
        s = list(s_raw)
        if diagonal:
            s = [jnp.where(key <= qry, s[g], -jnp.inf) for g in hs]
        m_new = [jnp.maximum(stats[g][0], jnp.max(s[g], axis=0, keepdims=True)) for g in hs]
        p = [jnp.exp2(s[g] - m_new[g]).astype(BF16) for g in hs]
        pv = [_dot(jnp.concatenate([vt[g], ones_rows], axis=0), p[g]) for g in hs]
        out = []
        for g in hs:
            m, acc = stats[g]
            out.append((m_new[g], jnp.exp2(m - m_new[g]) * acc + pv[g]))
        return tuple(out)

    ones_rows = jnp.ones((16, tq), BF16)
    init = tuple((jnp.full((1, tq), -jnp.inf, F32), jnp.zeros((V_DIM + 16, tq), F32))
                 for _ in hs)
    stats = lax.fori_loop(0, qi, lambda j, st: consume(j, raw_scores(j), st, False), init)
    stats = consume(qi, raw_scores(qi), stats, True)
    for g in hs:
        _, acc = stats[g]
        o_ref[0, :, g * V_DIM:(g + 1) * V_DIM] = (
            acc[0:V_DIM] / acc[V_DIM:V_DIM + 1]).T.astype(BF16)


def _mla_prompt_attention(q_cat, k_cat, v_t):
    b, t, _ = q_cat.shape
    tq = min(ATTN_TILE, t)
    hps = ATTN_HEADS_PER_STEP
    return pl.pallas_call(
        functools.partial(_attn_kernel, tq=tq),
        grid=(b, MLA_HEADS // hps, t // tq),
        in_specs=[pl.BlockSpec((1, tq, hps * QK_PAD), lambda i, h, j: (i, j, h)),
                  pl.BlockSpec((1, t, hps * QK_PAD), lambda i, h, j: (i, 0, h)),
                  pl.BlockSpec((1, t // tq, hps * V_DIM, tq), lambda i, h, j: (i, 0, h, 0))],
        out_specs=pl.BlockSpec((1, tq, hps * V_DIM), lambda i, h, j: (i, j, h)),
        out_shape=jax.ShapeDtypeStruct((b, t, MLA_HEADS * V_DIM), BF16),
        compiler_params=_params("parallel", "parallel", "arbitrary"),
        name="mla_prompt_attention",
    )(q_cat, k_cat, v_t)


def _dec_attn_kernel(pt_ref, *refs, pps, nj):
    del pt_ref
    page_refs = refs[:pps]
    (wukt_ref, qa_ref, qr_ref, lnew_ref, ctx_ref,
     lhs_ref, qr16_ref, m_ref, l_ref, acc_ref) = refs[pps:]
    step = pl.program_id(0)
    b = step // nj
    j = step % nj
    n_k = MLA_HEADS * NOPE_DIM

    @pl.when((b == 0) & (j == 0))
    def _():
        lhs_ref[0:n_k, :] = wukt_ref[...]

    @pl.when(j == 0)
    def _():
        qa = jnp.concatenate([qa_ref[h, 0] for h in range(MLA_HEADS)]
                             + [jnp.zeros((8, KV_LORA), F32)], axis=0)
        lhs_ref[n_k:n_k + 16, :] = qa.astype(BF16)
        qr = jnp.concatenate([qr_ref[h, 0] for h in range(MLA_HEADS)]
                             + [jnp.zeros((8, LANES), F32)], axis=0)
        qr16_ref[...] = qr.astype(BF16)
        m_ref[...] = jnp.full(m_ref.shape, -jnp.inf, F32)
        l_ref[...] = jnp.zeros(l_ref.shape, F32)
        acc_ref[...] = jnp.zeros(acc_ref.shape, F32)

    def scores(res, sr):
        kt = res[0:n_k]
        ss = jnp.sum((kt * kt).reshape(MLA_HEADS, NOPE_DIM, res.shape[1]), axis=1)
        sn = res[n_k:n_k + MLA_HEADS]
        return (sn * lax.rsqrt(ss * (1.0 / NOPE_DIM) + EPS) + sr[0:MLA_HEADS]) * MLA_SCALE

    c_tok, s_parts = [], []
    for i in range(0, pps, 2):
        tile = jnp.concatenate([page_refs[i][...], page_refs[i + 1][...]], axis=1)
        c_bf = tile[0:KV_LORA].astype(BF16)
        kr_bf = tile[KV_LORA:LATENT_WIDTH].astype(BF16)
        s_parts.append(scores(_dot(lhs_ref[...], c_bf),
                              _dot(qr16_ref[:, 0:ROPE_DIM], kr_bf)))
        c_tok.append(tile[0:KV_LORA].T.astype(BF16))
    s = jnp.concatenate(s_parts, axis=1)
    m_old = m_ref[...]
    m_new = jnp.maximum(m_old, jnp.max(s, axis=-1, keepdims=True))
    alpha = jnp.exp(m_old - m_new)
    p = jnp.exp(s - m_new)
    l_ref[...] = alpha * l_ref[...] + jnp.sum(p, axis=-1, keepdims=True)
    p_bf = p.astype(BF16)
    n_tok = c_tok[0].shape[0]
    acc = alpha * acc_ref[...]
    for t, c_t in enumerate(c_tok):
        acc = acc + _dot(p_bf[:, t * n_tok:(t + 1) * n_tok], c_t)
    acc_ref[...] = acc
    m_ref[...] = m_new

    @pl.when(j == nj - 1)
    def _():
        ln = lnew_ref[0]
        c_new = ln[:, 0:KV_LORA]
        c8 = jnp.broadcast_to(c_new, (8, KV_LORA)).astype(BF16)
        kr8 = jnp.broadcast_to(ln[:, KV_LORA:LATENT_WIDTH], (8, ROPE_DIM)).astype(BF16)
        s_new = scores(_dot_nt(lhs_ref[...], c8),
                       _dot_nt(qr16_ref[:, 0:ROPE_DIM], kr8))[:, 0:1]
        m_old = m_ref[...]
        m_new = jnp.maximum(m_old, s_new)
        alpha = jnp.exp(m_old - m_new)
        p_new = jnp.exp(s_new - m_new)
        ctx = (alpha * acc_ref[...] + p_new * c_new) / (alpha * l_ref[...] + p_new)
        for h in range(MLA_HEADS):
            ctx_ref[h, 0] = ctx[h:h + 1]


def _dec_pages_per_step(n_pages):
    pps = DEC_PAGES_PER_STEP
    while n_pages % pps:
        pps //= 2
    assert pps >= 2
    return pps


def _dec_scratch():
    return [pltpu.VMEM((MLA_HEADS * NOPE_DIM + 16, KV_LORA), BF16),
            pltpu.VMEM((16, LANES), BF16),
            pltpu.VMEM((MLA_HEADS, 1), F32),
            pltpu.VMEM((MLA_HEADS, 1), F32),
            pltpu.VMEM((MLA_HEADS, KV_LORA), F32)]


def _mla_decode_attention(cache, layer, page_table, wuk_t, qa, qr, lat_new):
    bsz, n_pages = page_table.shape
    page = cache.shape[3]
    pps = _dec_pages_per_step(n_pages)
    qa4 = qa.reshape(MLA_HEADS, bsz, 1, KV_LORA)
    qr4 = qr.reshape(MLA_HEADS, bsz, 1, LANES)
    ln3 = lat_new.reshape(bsz, 1, LATENT_WIDTH)

    nj = n_pages // pps
    def page_spec(i):
        return pl.BlockSpec((pl.Squeezed(), pl.Squeezed(), LATENT_WIDTH, page),
                            lambda s, pt: (layer, pt[s * pps + i], 0, 0))

    grid_spec = pltpu.PrefetchScalarGridSpec(
        num_scalar_prefetch=1,
        grid=(bsz * nj,),
        in_specs=[page_spec(i) for i in range(pps)] + [
            pl.BlockSpec(wuk_t.shape, lambda s, pt: (0, 0)),
            pl.BlockSpec((MLA_HEADS, 1, 1, KV_LORA), lambda s, pt: (0, s // nj, 0, 0)),
            pl.BlockSpec((MLA_HEADS, 1, 1, LANES), lambda s, pt: (0, s // nj, 0, 0)),
            pl.BlockSpec((1, 1, LATENT_WIDTH), lambda s, pt: (s // nj, 0, 0))],
        out_specs=pl.BlockSpec((MLA_HEADS, 1, 1, KV_LORA), lambda s, pt: (0, s // nj, 0, 0)),
        scratch_shapes=_dec_scratch())
    ctx = pl.pallas_call(
        functools.partial(_dec_attn_kernel, pps=pps, nj=nj),
        grid_spec=grid_spec,
        out_shape=jax.ShapeDtypeStruct((MLA_HEADS, bsz, 1, KV_LORA), F32),
        compiler_params=_params("arbitrary"),
        name="mla_decode_attention",
    )(page_table.reshape(bsz * n_pages), *([cache] * pps), wuk_t, qa4, qr4, ln3)
    return ctx.reshape(MLA_HEADS, bsz, KV_LORA)


def _dec_post_kernel(ctx_ref, wuv_ref, y_ref):
    for h in range(MLA_HEADS):
        hi, lo = _split2(ctx_ref[h])
        w_h = wuv_ref[:, h * V_DIM:(h + 1) * V_DIM]
        y_ref[:, h * V_DIM:(h + 1) * V_DIM] = (_dot(hi, w_h) + _dot(lo, w_h)).astype(BF16)


def _dec_post(ctx, w_uv):
    bsz = ctx.shape[1]
    return pl.pallas_call(
        _dec_post_kernel,
        grid=(1,),
        in_specs=[_full_spec(ctx.shape), _full_spec(w_uv.shape)],
        out_specs=_full_spec((bsz, MLA_HEADS * V_DIM)),
        out_shape=jax.ShapeDtypeStruct((bsz, MLA_HEADS * V_DIM), BF16),
        compiler_params=_params("arbitrary"),
        name="mla_decode_values",
    )(ctx, w_uv)


def _merge_mlp_kernel(x_ref, ya_ref, yb_ref, gmix_ref, wg_ref, wya_ref, wyb_ref, wo_ref,
                      gmlp_ref, wup_ref, wdn_ref, y_ref):
    tm, d = x_ref.shape
    d_ff = wup_ref.shape[1]
    n_grp = max(1, tm // MERGE_ROWS_PER_GROUP)
    rows = tm // n_grp
    grp = range(n_grp)
    sl = [slice(r * rows, (r + 1) * rows) for r in grp]
    x = [x_ref[sl[r], :] for r in grp]
    xn = [_rms(x[r], gmix_ref[...]).astype(BF16) for r in grp]
    gates = [_sigmoid(_dot(xn[r], wg_ref[...])) for r in grp]
    pa = [_dot(ya_ref[sl[r], :].astype(BF16), wya_ref[...]) for r in grp]
    pb = [_dot(yb_ref[sl[r], :].astype(BF16), wyb_ref[...]) for r in grp]
    mix = [(gates[r][:, :d] * pa[r] + gates[r][:, d:] * pb[r]).astype(BF16) for r in grp]
    x1 = [x[r] + _dot(mix[r], wo_ref[...]) for r in grp]
    hin = [_rms(x1[r], gmlp_ref[...]).astype(BF16) for r in grp]
    y = x1
    for f in range(0, d_ff, MLP_FF_BLOCK):
        hmid = [jnp.maximum(_dot(hin[r], wup_ref[:, f:f + MLP_FF_BLOCK]), 0.0) for r in grp]
        y = [y[r] + _dot((hmid[r] * hmid[r]).astype(BF16), wdn_ref[f:f + MLP_FF_BLOCK, :])
             for r in grp]
    for r in grp:
        y_ref[sl[r], :] = y[r]


MERGE_ROWS_PER_GROUP = 256
MLP_FF_BLOCK = 1024


def _merge_mlp(x2, ya, yb, ew):
    m, d = x2.shape
    tm = min(512, m)
    consts = [ew["g_mix"], ew["w_gates"], ew["w_ya"], ew["w_yb"], ew["w_o"], ew["g_mlp"],
              ew["w_up"], ew["w_down"]]

    def const_spec(a):
        nd = a.ndim
        return pl.BlockSpec(a.shape, lambda i: (0,) * nd, pipeline_mode=pl.Buffered(1))

    return pl.pallas_call(
        _merge_mlp_kernel,
        grid=(m // tm,),
        in_specs=[pl.BlockSpec((tm, d), lambda i: (i, 0)),
                  pl.BlockSpec((tm, ya.shape[1]), lambda i: (i, 0)),
                  pl.BlockSpec((tm, yb.shape[1]), lambda i: (i, 0))]
                 + [const_spec(a) for a in consts],
        out_specs=pl.BlockSpec((tm, d), lambda i: (i, 0)),
        out_shape=jax.ShapeDtypeStruct((m, d), F32),
        compiler_params=_params("parallel"),
        name="merge_mlp",
    )(x2, ya, yb, *consts)


def _rope_tables(pos):
    inv = jnp.power(ROPE_THETA, -jnp.arange(0, ROPE_DIM, 2, dtype=F32) / ROPE_DIM)
    ang = pos.astype(F32)[:, None] * inv[None, :]
    cos, sin = jnp.cos(ang), jnp.sin(ang)
    zeros = jnp.zeros((pos.shape[0], LANES - ROPE_DIM), F32)
    return (jnp.concatenate([cos, cos, zeros], axis=1),
            jnp.concatenate([-sin, sin, zeros], axis=1))


def _pad_lanes(a, width=LANES, offset=0):
    return jnp.zeros((1, width), F32).at[0, offset:offset + a.shape[0]].set(a.astype(F32))


def _layer_weights(l, norm_mix, w_in, conv_w, a_log, dt_bias, gdn_norm, q_a_norm, w_uq,
                   kv_a_norm, w_uk, w_uv, q_norm_nope, q_norm_rope, k_norm_nope, k_norm_rope,
                   w_ya, w_yb, w_o, norm_mlp, w_up, w_down):
    d = w_in.shape[1]
    off_z = QKV_WIDTH
    off_b = off_z + Z_WIDTH
    off_qd = off_b + 2 * GDN_HEADS
    off_kv = off_qd + Q_LORA
    off_g = off_kv + LATENT_WIDTH
    wi = w_in[l]
    g_mix = norm_mix[l].reshape(1, d)
    gw = {
        "g_mix": g_mix,
        "w_in": wi.astype(BF16),
        "conv_w": conv_w[l],
        "alog_row": _pad_lanes(a_log[l], offset=GDN_HEADS),
        "dt_row": _pad_lanes(dt_bias[l], offset=GDN_HEADS),
        "gdn_norm": gdn_norm[l].reshape(1, GDN_DV),
    }
    wq = w_uq[l].reshape(Q_LORA, MLA_HEADS, NOPE_DIM + ROPE_DIM)
    wq = jnp.pad(wq, ((0, 0), (0, 0), (0, QK_PAD - NOPE_DIM - ROPE_DIM)))
    mw = {
        "g_mix": g_mix,
        "w_down": jnp.pad(wi[:, off_qd:off_g], ((0, 0), (0, LANES - ROPE_DIM))).astype(BF16),
        "g_qa": q_a_norm[l].reshape(1, Q_LORA),
        "w_uq": wq.reshape(Q_LORA, MLA_HEADS * QK_PAD).astype(BF16),
        "g_kva": kv_a_norm[l].reshape(1, KV_LORA),
        "g_qn": q_norm_nope[l].reshape(1, NOPE_DIM),
        "g_qr": _pad_lanes(q_norm_rope[l]),
        "g_kr": _pad_lanes(k_norm_rope[l]),
        "g_kn": k_norm_nope[l].reshape(1, NOPE_DIM),
        "w_uk": w_uk[l].astype(BF16),
        "w_uk_t": w_uk[l].T.astype(BF16),
        "w_uv": w_uv[l].astype(BF16),
        "w_uv_t": w_uv[l].T.astype(BF16),
    }
    ew = {
        "g_mix": g_mix,
        "w_gates": wi[:, off_g:].astype(BF16),
        "w_ya": w_ya[l].astype(BF16),
        "w_yb": w_yb[l].astype(BF16),
        "w_o": w_o[l].astype(BF16),
        "g_mlp": norm_mlp[l].reshape(1, d),
        "w_up": w_up[l].astype(BF16),
        "w_down": w_down[l].astype(BF16),
    }
    return gw, mw, ew


def kernel(x_prompt, x_sample, cache_mla, state_gdn, state_conv, page_table, norm_mix, w_in,
           conv_w, a_log, dt_bias, gdn_norm, q_a_norm, w_uq, kv_a_norm, w_uk, w_uv,
           q_norm_nope, q_norm_rope, k_norm_nope, k_norm_rope, w_ya, w_yb, w_o, norm_mlp,
           w_up, w_down):
    depth = w_in.shape[0]
    bp, tp, d = x_prompt.shape
    bs, ts, _ = x_sample.shape
    assert ts == 1 and tp % GDN_CHUNK == 0
    past_len = page_table.shape[1] * cache_mla.shape[2]
    cache_t = jnp.swapaxes(cache_mla, 2, 3)
    cos_p, sin_p = _rope_tables(jnp.arange(tp))
    cos_s, sin_s = _rope_tables(past_len + jnp.zeros((bs,), jnp.int32))

    x_p, x_s = x_prompt, x_sample.reshape(bs, d)
    rows_p, gdn_p, conv_p, rows_s, gdn_s, conv_s = [], [], [], [], [], []
    for l in range(depth):
        gw, mw, ew = _layer_weights(
            l, norm_mix, w_in, conv_w, a_log, dt_bias, gdn_norm, q_a_norm, w_uq, kv_a_norm,
            w_uk, w_uv, q_norm_nope, q_norm_rope, k_norm_nope, k_norm_rope, w_ya, w_yb, w_o,
            norm_mlp, w_up, w_down)
        gdn_consts = (gw["conv_w"], gw["alog_row"], gw["dt_row"], gw["gdn_norm"])

        xp2 = x_p.reshape(bp * tp, d)
        yc, z, ba, cs_p = _gdn_proj_conv(x_p, gw["g_mix"], gw["w_in"], gw["conv_w"])
        ya_p, s_p = _gdn_chunked(yc.reshape(bp, tp, QKV_WIDTH), z.reshape(bp, tp, Z_WIDTH),
                                 ba.reshape(bp, tp, LANES), *gdn_consts[1:])
        q_cat, k_cat, v_p, lat_p = _mla_pre(x_p, cos_p, sin_p, mw)
        yb_p = _mla_prompt_attention(q_cat, k_cat, v_p)
        y_p = _merge_mlp(xp2, ya_p.reshape(bp * tp, Z_WIDTH), yb_p.reshape(bp * tp, -1), ew)

        qkv, z, ba = _gdn_proj(x_s, gw["g_mix"], gw["w_in"])
        ya_s, s_s, cs_s = _gdn_step(qkv, z, ba, state_conv[l], state_gdn[l], *gdn_consts)
        qa, qr, lat_s = _mla_pre_sample(x_s, cos_s, sin_s, mw)
        ctx = _mla_decode_attention(cache_t, l, page_table, mw["w_uk_t"], qa, qr, lat_s)
        yb_s = _dec_post(ctx, mw["w_uv"])
        y_s = _merge_mlp(x_s, ya_s.reshape(bs, Z_WIDTH), yb_s, ew)

        x_p, x_s = y_p.reshape(bp, tp, d), y_s
        rows_p.append(lat_p)
        gdn_p.append(s_p)
        conv_p.append(cs_p)
        rows_s.append(lat_s.reshape(bs, 1, LATENT_WIDTH))
        gdn_s.append(s_s)
        conv_s.append(cs_s)
    return (x_p, x_s.reshape(bs, 1, d), jnp.stack(rows_p), jnp.stack(gdn_p), jnp.stack(conv_p),
            jnp.stack(rows_s), jnp.stack(gdn_s), jnp.stack(conv_s))
```
